```python
import math
import jax
import jax.numpy as jnp
from jax import lax
import numpy as np

D_MODEL = 1024
BATCH = 2
SEQ = 8192
DEPTH = 2
DEC_BATCH = 128
DEC_SEQ = 4
PAST_LEN = 16384
PAGE_SIZE = 128

N_MIXERS = 2
N_HGRN_LAYERS = (DEPTH + 1) // 2
N_SWA_LAYERS = DEPTH // 2
HG_DK = 128
HG_HEADS = D_MODEL // HG_DK
HG_DV = D_MODEL // HG_HEADS
HG_KEY_W = HG_HEADS * HG_DK
HG_VAL_W = HG_HEADS * HG_DV
HG_CHUNK = 64
SW_HEAD_DIM = 64
SW_Q_HEADS = D_MODEL // SW_HEAD_DIM
SW_KV_HEADS = 4
SW_GROUP = SW_Q_HEADS // SW_KV_HEADS
WINDOW = 128
ATT_BLOCK = 128
SW_SCALE = SW_HEAD_DIM ** -0.5
REL_BUCKETS = 32
REL_MAX_DIST = 128
N_EXPERTS = 32
TOP_K = 4
D_EXPERT = D_MODEL
SWIGLU_LIMIT = 7.0
SWIGLU_ALPHA = 1.702
MOE_BLOCK = 128
NORM_EPS = 1e-5

kernel_name = 'hybrid_hgrn2_swa_sink_moe_adaln_step'


def rmsnorm(x, g):
    xf = x.astype(jnp.float32)
    y = xf * lax.rsqrt(jnp.mean(xf * xf, axis=-1, keepdims=True) + NORM_EPS)
    return (y * g.astype(jnp.float32)).astype(x.dtype)


def ada_mod(c, w, b):
    m = jax.nn.silu(c) @ w + b
    return jnp.split(m[:, None, :], 6, axis=-1)


def rel_bucket(dist):
    n = jnp.maximum(dist, 0)
    max_exact = REL_BUCKETS // 2
    large = max_exact + (jnp.log(jnp.maximum(n, 1).astype(jnp.float32) / max_exact)
                         / math.log(REL_MAX_DIST / max_exact)
                         * (REL_BUCKETS - max_exact)).astype(jnp.int32)
    large = jnp.minimum(large, REL_BUCKETS - 1)
    return jnp.where(n < max_exact, n, large)


def rel_bias_for(dist, rel_bias):
    bias = rel_bias[rel_bucket(dist)].astype(jnp.float32)
    bias = jnp.transpose(bias, (2, 0, 1))
    return bias.reshape(SW_KV_HEADS, SW_GROUP, dist.shape[0], dist.shape[1])


def sink_softmax(s, sink):
    m = jnp.maximum(jnp.max(s, axis=-1, keepdims=True), sink)
    e = jnp.exp(s - m)
    return e / (jnp.sum(e, axis=-1, keepdims=True) + jnp.exp(sink - m))


def gla_chunk(S, q, k, v, logf):
    L = q.shape[1]
    b = jnp.cumsum(logf, axis=1)
    o_inter = jnp.einsum('blhk,bhkv->blhv', q * jnp.exp(b), S)
    causal = jnp.tril(jnp.ones((L, L), dtype=bool))
    diff = b[:, :, None] - b[:, None, :]
    decay = jnp.exp(jnp.where(causal[None, :, :, None, None], diff, -jnp.inf))
    att = jnp.einsum('bthk,bshk,btshk->bhts', q, k, decay)
    o_intra = jnp.einsum('bhts,bshv->bthv', att, v)
    b_last = b[:, -1]
    k_dec = k * jnp.exp(b_last[:, None] - b)
    S_new = jnp.exp(b_last)[..., None] * S + jnp.einsum('blhk,blhv->bhkv', k_dec, v)
    return o_inter + o_intra, S_new


def hgrn2_mixer(h, S0, w_in, lb, o_gain, w_out):
    B, T, _ = h.shape
    z = h @ w_in
    qz, fz, iz, gz = jnp.split(z, [HG_KEY_W, 2 * HG_KEY_W, 2 * HG_KEY_W + HG_VAL_W], axis=-1)
    fz = fz.astype(jnp.float32)
    lb = lb.astype(jnp.float32)
    logf = jnp.logaddexp(jnp.log(lb), jnp.log1p(-lb) + jax.nn.log_sigmoid(fz))
    k = (1.0 - lb) * jax.nn.sigmoid(-fz)
    q = jax.nn.silu(qz.astype(jnp.float32))
    v = iz.astype(jnp.float32)
    L = HG_CHUNK if T % HG_CHUNK == 0 else T
    n = T // L

    def to_chunks(a, d):
        return jnp.swapaxes(a.reshape(B, n, L, HG_HEADS, d), 0, 1)

    def step(S, xs):
        qc, kc, vc, lc = xs
        o, S = gla_chunk(S, qc, kc, vc, lc)
        return S, o

    S_fin, o = lax.scan(step, S0.astype(jnp.float32),
                        (to_chunks(q, HG_DK), to_chunks(k, HG_DK),
                         to_chunks(v, HG_DV), to_chunks(logf, HG_DK)))
    o = jnp.swapaxes(o, 0, 1).reshape(B, T, HG_HEADS, HG_DV)
    o = rmsnorm(o, o_gain) * jax.nn.silu(gz.astype(jnp.float32)).reshape(B, T, HG_HEADS, HG_DV)
    out = o.reshape(B, T, HG_VAL_W).astype(h.dtype) @ w_out
    return out, S_fin.astype(h.dtype)


def swa_project(h, w_in, q_gain, k_gain):
    B, T, _ = h.shape
    z = h @ w_in
    q, k, v = jnp.split(z, [SW_Q_HEADS * SW_HEAD_DIM, (SW_Q_HEADS + SW_KV_HEADS) * SW_HEAD_DIM], axis=-1)
    q = rmsnorm(q.reshape(B, T, SW_Q_HEADS, SW_HEAD_DIM), q_gain)
    k = rmsnorm(k.reshape(B, T, SW_KV_HEADS, SW_HEAD_DIM), k_gain)
    v = v.reshape(B, T, SW_KV_HEADS, SW_HEAD_DIM)
    return q, k, v


def swa_banded(q, k, v, sinks, rel_bias):
    B, T = q.shape[:2]
    nb = T // ATT_BLOCK
    qb = q.reshape(B, nb, ATT_BLOCK, SW_KV_HEADS, SW_GROUP, SW_HEAD_DIM)
    kb = k.reshape(B, nb, ATT_BLOCK, SW_KV_HEADS, SW_HEAD_DIM)
    vb = v.reshape(B, nb, ATT_BLOCK, SW_KV_HEADS, SW_HEAD_DIM)
    pad = ((0, 0), (1, 0), (0, 0), (0, 0), (0, 0))
    kk = jnp.concatenate([jnp.pad(kb, pad)[:, :-1], kb], axis=2)
    vv = jnp.concatenate([jnp.pad(vb, pad)[:, :-1], vb], axis=2)
    s = jnp.einsum('bnqkgd,bnskd->bnkgqs', qb, kk).astype(jnp.float32) * SW_SCALE
    qpos = ATT_BLOCK + jnp.arange(ATT_BLOCK)
    kpos = jnp.arange(2 * ATT_BLOCK)
    dist = qpos[:, None] - kpos[None, :]
    valid = (dist >= 0) & (dist <= WINDOW)
    block_valid = (jnp.arange(nb)[:, None] > 0) | (kpos[None, :] >= ATT_BLOCK)
    mask = valid[None, :, :] & block_valid[:, None, :]
    s = s + rel_bias_for(dist, rel_bias)
    s = jnp.where(mask[None, :, None, None], s, -jnp.inf)
    sink = sinks.astype(jnp.float32).reshape(SW_KV_HEADS, SW_GROUP, 1, 1)
    p = sink_softmax(s, sink).astype(vv.dtype)
    o = jnp.einsum('bnkgqs,bnskd->bnqkgd', p, vv)
    return o.reshape(B, T, SW_Q_HEADS * SW_HEAD_DIM)


def swa_with_buffer(q, k_new, v_new, k_buf, v_buf, sinks, rel_bias):
    B, L = q.shape[:2]
    W = k_buf.shape[1]
    kk = jnp.concatenate([k_buf.astype(k_new.dtype), k_new], axis=1)
    vv = jnp.concatenate([v_buf.astype(v_new.dtype), v_new], axis=1)
    qg = q.reshape(B, L, SW_KV_HEADS, SW_GROUP, SW_HEAD_DIM)
    s = jnp.einsum('blkgd,bskd->bkgls', qg, kk).astype(jnp.float32) * SW_SCALE
    qpos = W + jnp.arange(L)
    kpos = jnp.arange(W + L)
    dist = qpos[:, None] - kpos[None, :]
    mask = (dist >= 0) & (dist <= WINDOW)
    s = s + rel_bias_for(dist, rel_bias)
    s = jnp.where(mask[None, None, None], s, -jnp.inf)
    sink = sinks.astype(jnp.float32).reshape(SW_KV_HEADS, SW_GROUP, 1, 1)
    p = sink_softmax(s, sink).astype(vv.dtype)
    o = jnp.einsum('bkgls,bskd->blkgd', p, vv)
    return o.reshape(B, L, SW_Q_HEADS * SW_HEAD_DIM), kk[:, -W:], vv[:, -W:]


def moe_ffn(h, router_w, router_b, w_in, b_in, w_out, b_out):
    N, D = h.shape
    logits = (h @ router_w + router_b).astype(jnp.float32)
    top_v, top_e = lax.top_k(logits, TOP_K)
    gates = jax.nn.softmax(top_v, axis=-1)
    NK = N * TOP_K
    flat_e = top_e.reshape(NK)
    flat_tok = jnp.repeat(jnp.arange(N, dtype=jnp.int32), TOP_K)
    flat_g = gates.reshape(NK)
    order = jnp.argsort(flat_e)
    e_sorted = flat_e[order]
    counts = jnp.zeros((N_EXPERTS,), jnp.int32).at[flat_e].add(1)
    padded = (counts + MOE_BLOCK - 1) // MOE_BLOCK * MOE_BLOCK
    start = jnp.cumsum(counts) - counts
    pend = jnp.cumsum(padded)
    pstart = pend - padded
    slot = pstart[e_sorted] + jnp.arange(NK, dtype=jnp.int32) - start[e_sorted]
    n_blocks = -(-(NK + N_EXPERTS * (MOE_BLOCK - 1)) // MOE_BLOCK)
    n_slots = n_blocks * MOE_BLOCK
    slot_tok = jnp.zeros((n_slots,), jnp.int32).at[slot].set(flat_tok[order])
    slot_gate = jnp.zeros((n_slots,), jnp.float32).at[slot].set(flat_g[order])
    block_start = jnp.arange(n_blocks, dtype=jnp.int32) * MOE_BLOCK
    block_e = jnp.minimum(jnp.sum(block_start[:, None] >= pend[None, :], axis=1), N_EXPERTS - 1)
    xs = h[slot_tok].reshape(n_blocks, MOE_BLOCK, D)

    def expert_block(args):
        xb, e = args
        gu = xb @ w_in[e] + b_in[e]
        gate, up = jnp.split(gu, 2, axis=-1)
        gate = jnp.minimum(gate, SWIGLU_LIMIT)
        up = jnp.clip(up, -SWIGLU_LIMIT, SWIGLU_LIMIT)
        act = gate * jax.nn.sigmoid(SWIGLU_ALPHA * gate) * (up + 1.0)
        return act @ w_out[e] + b_out[e]

    yb = lax.map(expert_block, (xs, block_e))
    y = yb.reshape(n_slots, D) * slot_gate[:, None].astype(yb.dtype)
    return jnp.zeros_like(h).at[slot_tok].add(y.astype(h.dtype))


def run_group(x, c, prompt, hg_state, win_k, win_v, norm1_g, norm2_g, ada_w, ada_b,
              hg_w_in, hg_lb, hg_onorm_g, hg_w_out, sw_w_in, sw_qnorm_g, sw_knorm_g,
              sw_sinks, sw_w_out, rel_bias, router_w, router_b, moe_w_in, moe_b_in,
              moe_w_out, moe_b_out):
    B, T, D = x.shape
    new_s, new_k, new_v = [], [], []
    for layer in range(DEPTH):
        j = layer // N_MIXERS
        sh1, sc1, g1, sh2, sc2, g2 = ada_mod(c, ada_w[layer], ada_b[layer])
        h = rmsnorm(x, norm1_g[layer]) * (1 + sc1) + sh1
        if layer % N_MIXERS == 0:
            s0 = jnp.zeros((B, HG_HEADS, HG_DK, HG_DV), jnp.float32) if prompt else hg_state[j]
            out, s_fin = hgrn2_mixer(h, s0, hg_w_in[j], hg_lb[j], hg_onorm_g[j], hg_w_out[j])
            new_s.append(s_fin)
        else:
            q, k, v = swa_project(h, sw_w_in[j], sw_qnorm_g[j], sw_knorm_g[j])
            if prompt:
                o = swa_banded(q, k, v, sw_sinks[j], rel_bias)
                kb, vb = k[:, -WINDOW:], v[:, -WINDOW:]
            else:
                o, kb, vb = swa_with_buffer(q, k, v, win_k[j], win_v[j], sw_sinks[j], rel_bias)
            out = o @ sw_w_out[j]
            new_k.append(kb)
            new_v.append(vb)
        x = x + g1 * out
        h = rmsnorm(x, norm2_g[layer]) * (1 + sc2) + sh2
        ffn = moe_ffn(h.reshape(B * T, D), router_w[layer], router_b[layer], moe_w_in[layer],
                      moe_b_in[layer], moe_w_out[layer], moe_b_out[layer])
        x = x + g2 * ffn.reshape(B, T, D)
    return x, jnp.stack(new_s), jnp.stack(new_k), jnp.stack(new_v)


def setup_inputs(seed: int = 0) -> dict:
    key = jax.random.key(seed)
    ks = iter(jax.random.split(key, 32))
    D = D_MODEL
    buf = min(WINDOW, PAST_LEN)
    qkv_w = (SW_Q_HEADS + 2 * SW_KV_HEADS) * SW_HEAD_DIM

    def nrm(shape, s):
        return jax.random.normal(next(ks), shape, jnp.float32) * s

    return {
        'x_prompt': nrm((BATCH, SEQ, D), 1.0),
        'x_sample': nrm((DEC_BATCH, DEC_SEQ, D), 1.0),
        'state_hgrn': nrm((N_HGRN_LAYERS, DEC_BATCH, HG_HEADS, HG_DK, HG_DV), 0.5),
        'cache_win_k': nrm((N_SWA_LAYERS, DEC_BATCH, buf, SW_KV_HEADS, SW_HEAD_DIM), 1.0),
        'cache_win_v': nrm((N_SWA_LAYERS, DEC_BATCH, buf, SW_KV_HEADS, SW_HEAD_DIM), 1.0),
        'c_prompt': nrm((BATCH, D), 1.0),
        'c_sample': nrm((DEC_BATCH, D), 1.0),
        'norm1_g': 1.0 + nrm((DEPTH, D), 0.02),
        'norm2_g': 1.0 + nrm((DEPTH, D), 0.02),
        'ada_w': nrm((DEPTH, D, 6 * D), 0.5 * D ** -0.5),
        'ada_b': nrm((DEPTH, 6 * D), 0.02),
        'hg_w_in': nrm((N_HGRN_LAYERS, D, 2 * HG_KEY_W + 2 * HG_VAL_W), D ** -0.5),
        'hg_lb_table': nrm((N_HGRN_LAYERS + 1, HG_KEY_W), 0.5),
        'hg_onorm_g': 1.0 + nrm((N_HGRN_LAYERS, HG_DV), 0.02),
        'hg_w_out': nrm((N_HGRN_LAYERS, HG_VAL_W, D), HG_VAL_W ** -0.5),
        'sw_w_in': nrm((N_SWA_LAYERS, D, qkv_w), D ** -0.5),
        'sw_qnorm_g': 1.0 + nrm((N_SWA_LAYERS, SW_HEAD_DIM), 0.02),
        'sw_knorm_g': 1.0 + nrm((N_SWA_LAYERS, SW_HEAD_DIM), 0.02),
        'sw_sinks': nrm((N_SWA_LAYERS, SW_Q_HEADS), 0.5),
        'sw_w_out': nrm((N_SWA_LAYERS, SW_Q_HEADS * SW_HEAD_DIM, D), (SW_Q_HEADS * SW_HEAD_DIM) ** -0.5),
        'rel_bias': nrm((REL_BUCKETS, SW_Q_HEADS), 0.5),
        'router_w': nrm((DEPTH, D, N_EXPERTS), D ** -0.5),
        'router_b': nrm((DEPTH, N_EXPERTS), 0.01),
        'moe_w_in': nrm((DEPTH, N_EXPERTS, D, 2 * D_EXPERT), D ** -0.5),
        'moe_b_in': nrm((DEPTH, N_EXPERTS, 2 * D_EXPERT), 0.01),
        'moe_w_out': nrm((DEPTH, N_EXPERTS, D_EXPERT, D), D_EXPERT ** -0.5),
        'moe_b_out': nrm((DEPTH, N_EXPERTS, D), 0.01),
    }


def reference(x_prompt, x_sample, state_hgrn, cache_win_k, cache_win_v, c_prompt, c_sample,
              norm1_g, norm2_g, ada_w, ada_b, hg_w_in, hg_lb_table, hg_onorm_g, hg_w_out,
              sw_w_in, sw_qnorm_g, sw_knorm_g, sw_sinks, sw_w_out, rel_bias, router_w,
              router_b, moe_w_in, moe_b_in, moe_w_out, moe_b_out):
    hg_lb = jnp.cumsum(jax.nn.softmax(hg_lb_table.astype(jnp.float32), axis=0), axis=0)[:N_HGRN_LAYERS]
    y_prompt, s_p, k_p, v_p = run_group(
        x_prompt, c_prompt, True, None, None, None, norm1_g, norm2_g, ada_w, ada_b,
        hg_w_in, hg_lb, hg_onorm_g, hg_w_out, sw_w_in, sw_qnorm_g, sw_knorm_g, sw_sinks,
        sw_w_out, rel_bias, router_w, router_b, moe_w_in, moe_b_in, moe_w_out, moe_b_out)
    y_sample, s_s, k_s, v_s = run_group(
        x_sample, c_sample, False, state_hgrn, cache_win_k, cache_win_v, norm1_g, norm2_g,
        ada_w, ada_b, hg_w_in, hg_lb, hg_onorm_g, hg_w_out, sw_w_in, sw_qnorm_g, sw_knorm_g,
        sw_sinks, sw_w_out, rel_bias, router_w, router_b, moe_w_in, moe_b_in, moe_w_out, moe_b_out)
    return (y_prompt, y_sample, s_p, s_s, k_p, v_p, k_s, v_s)
```

```python
import functools
import math

import numpy as np
import jax
import jax.numpy as jnp
from jax import lax
from jax.experimental import pallas as pl
from jax.experimental.pallas import tpu as pltpu

F32 = jnp.float32
BF16 = jnp.bfloat16
I32 = jnp.int32

D_MODEL = 1024
LANES = 128
SUBLANES = 8
D_TILES = D_MODEL // LANES
HG_DK = 128
HG_HEADS = D_MODEL // HG_DK
HG_DV = D_MODEL // HG_HEADS
HG_CHUNK = 64
SW_HEAD_DIM = 64
SW_Q_HEADS = D_MODEL // SW_HEAD_DIM
SW_KV_HEADS = 4
SW_GROUP = SW_Q_HEADS // SW_KV_HEADS
SW_KV_W = SW_KV_HEADS * SW_HEAD_DIM
WINDOW = 128
ATT_BLOCK = 128
SW_SCALE = SW_HEAD_DIM ** -0.5
REL_BUCKETS = 32
REL_MAX_DIST = 128
N_EXPERTS = 32
TOP_K = 4
SWIGLU_LIMIT = 7.0
SWIGLU_ALPHA = 1.702
NORM_EPS = 1e-5
MOE_BLOCK = 256
VMEM_LIMIT = 56 * 1024 * 1024


def _cparams(sem):
    return pltpu.CompilerParams(dimension_semantics=sem, vmem_limit_bytes=VMEM_LIMIT)


def _sigmoid(x):
    return 1.0 / (1.0 + jnp.exp(-x))


def _silu(x):
    return x * _sigmoid(x)


def _dot(a, b):
    return jnp.dot(a, b, preferred_element_type=F32)


def _dot_nt(a, b):
    return lax.dot_general(a, b, (((1,), (1,)), ((), ())), preferred_element_type=F32)


def _dot_tn(a, b):
    return lax.dot_general(a, b, (((0,), (0,)), ((), ())), preferred_element_type=F32)


def _split3(x):
    hi = x.astype(BF16)
    r = x - hi.astype(F32)
    mid = r.astype(BF16)
    lo = (r - mid.astype(F32)).astype(BF16)
    return hi, mid, lo


def _ada_kernel(c_ref, w_ref, b_ref, o_ref):
    s = _silu(c_ref[...]).astype(BF16)
    o_ref[0] = _dot(s, w_ref[0].astype(BF16)) + b_ref[0]


def _ada_mods(c_all, ada_w, ada_b):
    depth, d, n6 = ada_w.shape
    rows = c_all.shape[0]
    tn = 1536
    return pl.pallas_call(
        _ada_kernel,
        grid=(depth, n6 // tn),
        in_specs=[
            pl.BlockSpec((rows, d), lambda l, j: (0, 0)),
            pl.BlockSpec((1, d, tn), lambda l, j: (l, 0, j)),
            pl.BlockSpec((1, 1, tn), lambda l, j: (l, 0, j)),
        ],
        out_specs=pl.BlockSpec((1, rows, tn), lambda l, j: (l, 0, j)),
        out_shape=jax.ShapeDtypeStruct((depth, rows, n6), F32),
        compiler_params=_cparams(("arbitrary", "arbitrary")),
        name="ada_mods",
    )(c_all, ada_w, ada_b.reshape(depth, 1, n6))


class _Geom:
    def __init__(self, batch, seq, dec_batch, dec_seq):
        self.batch, self.seq, self.dec_batch, self.dec_seq = batch, seq, dec_batch, dec_seq
        self.n_prompt = batch * seq
        self.n_sample = dec_batch * dec_seq
        self.nt = self.n_prompt + self.n_sample
        tm = 512
        while seq % tm or self.n_sample % tm:
            tm //= 2
        assert tm >= 8
        self.tm = tm
        self.n_pt = self.n_prompt // tm
        self.n_tiles = self.nt // tm


def _mod_specs(geo, col, tm):
    seq, batch = geo.seq, geo.batch
    n_pt = geo.n_prompt // tm

    def p_map(i, *_):
        return (jnp.minimum(i * tm // seq, batch - 1), 0, col)

    def s_map(i, *_):
        return (jnp.maximum(i - n_pt, 0), col)

    return [pl.BlockSpec((1, 1, D_MODEL), p_map), pl.BlockSpec((tm, D_MODEL), s_map)]


def _pick_mod(i, n_pt, p_ref, s_ref):
    return jnp.where(i >= n_pt, s_ref[...], p_ref[0])


def _norm_mod(x, g, sc, sh):
    ms = jnp.mean(x * x, axis=-1, keepdims=True)
    return x * lax.rsqrt(ms + NORM_EPS) * g * (1.0 + sc) + sh


def _proj_kernel(x_ref, g_ref, shp_ref, shs_ref, scp_ref, scs_ref, w_ref, o_ref, h_scr, *, n_pt):
    i = pl.program_id(0)

    @pl.when(pl.program_id(1) == 0)
    def _():
        sh = _pick_mod(i, n_pt, shp_ref, shs_ref)
        sc = _pick_mod(i, n_pt, scp_ref, scs_ref)
        h_scr[...] = _norm_mod(x_ref[...], g_ref[...], sc, sh).astype(BF16)

    o_ref[...] = _dot(h_scr[...], w_ref[...])


def _proj(geo, x, g, mod_p, mod_s, w_bf16, col_shift, col_scale):
    n_out = w_bf16.shape[1]
    tn = n_out if n_out <= 2048 else 1024
    tm = geo.tm
    return pl.pallas_call(
        functools.partial(_proj_kernel, n_pt=geo.n_pt),
        grid=(geo.n_tiles, n_out // tn),
        in_specs=[
            pl.BlockSpec((tm, D_MODEL), lambda i, j: (i, 0)),
            pl.BlockSpec((1, D_MODEL), lambda i, j: (0, 0)),
            *_mod_specs(geo, col_shift, tm),
            *_mod_specs(geo, col_scale, tm),
            pl.BlockSpec((D_MODEL, tn), lambda i, j: (0, j)),
        ],
        out_specs=pl.BlockSpec((tm, tn), lambda i, j: (i, j)),
        out_shape=jax.ShapeDtypeStruct((geo.nt, n_out), F32),
        scratch_shapes=[pltpu.VMEM((tm, D_MODEL), BF16)],
        compiler_params=_cparams(("arbitrary", "arbitrary")),
        name="norm_mod_proj",
    )(x, g, mod_p, mod_s, mod_p, mod_s, w_bf16)


def _hg_lower_bound(lbt_ref):
    t = lbt_ref[...]
    e = jnp.exp(t - jnp.max(t, axis=0, keepdims=True))
    return e[0:1] / jnp.sum(e, axis=0, keepdims=True)


def _hg_gates(fz, lb):
    e = jnp.exp(-jnp.abs(fz))
    inv = 1.0 / (1.0 + e)
    pos = fz >= 0
    sig = jnp.where(pos, inv, e * inv)
    sig_neg = jnp.where(pos, e * inv, inv)
    logf = jnp.log(lb + (1.0 - lb) * sig)
    return logf, (1.0 - lb) * sig_neg


def _hg_out(o, gz, og):
    ms = jnp.mean(o * o, axis=-1, keepdims=True)
    return o * lax.rsqrt(ms + NORM_EPS) * og * _silu(gz)


def _gla_prompt_kernel(z_ref, lbt_ref, og_ref, o_ref, sfin_ref, st_scr, *, chunk, n_chunks):
    t_step = pl.program_id(1)
    kw = HG_HEADS * HG_DK

    @pl.when(t_step == 0)
    def _():
        st_scr[...] = jnp.zeros_like(st_scr)

    lb = _hg_lower_bound(lbt_ref)
    og = og_ref[...]
    r_i = lax.broadcasted_iota(I32, (chunk, chunk), 0)
    c_i = lax.broadcasted_iota(I32, (chunk, chunk), 1)
    causal = c_i <= r_i
    tri = causal.astype(BF16)
    mid = chunk // 2 - 1

    def body(c, carry):
        rows = pl.ds(pl.multiple_of(c * chunk, chunk), chunk)
        logf, kk = _hg_gates(z_ref[rows, kw:2 * kw], lb)
        hi, md, lo = _split3(logf)
        cs = _dot(tri, jnp.concatenate([hi, md, lo], axis=1))
        b = cs[:, :kw] + cs[:, kw:2 * kw] + cs[:, 2 * kw:]
        b_mid = b[mid:mid + 1]
        b_last = b[chunk - 1:chunk]
        q_hat = _silu(z_ref[rows, 0:kw]) * jnp.exp(b - b_mid)
        k_hat = kk * jnp.exp(b_mid - b)
        q_in = (q_hat * jnp.exp(b_mid)).astype(BF16)
        k_dec = (k_hat * jnp.exp(b_last - b_mid)).astype(BF16)
        q_hat = q_hat.astype(BF16)
        k_hat = k_hat.astype(BF16)
        dec = jnp.exp(b_last)
        for h in range(HG_HEADS):
            cols = slice(h * HG_DK, (h + 1) * HG_DK)
            vcols = slice(2 * kw + h * HG_DV, 2 * kw + (h + 1) * HG_DV)
            gcols = slice(2 * kw + HG_HEADS * HG_DV + h * HG_DV, 2 * kw + HG_HEADS * HG_DV + (h + 1) * HG_DV)
            v = z_ref[rows, vcols].astype(BF16)
            att = jnp.where(causal, _dot_nt(q_hat[:, cols], k_hat[:, cols]), 0.0).astype(BF16)
            st = st_scr[h]
            o = _dot(att, v) + _dot_nt(q_in[:, cols], st.astype(BF16))
            st_scr[h] = st * dec[:, cols] + _dot_tn(v, k_dec[:, cols])
            o_ref[rows, h * HG_DV:(h + 1) * HG_DV] = _hg_out(o, z_ref[rows, gcols], og).astype(o_ref.dtype)
        return carry

    lax.fori_loop(0, n_chunks, body, 0)

    @pl.when(t_step == pl.num_programs(1) - 1)
    def _():
        sfin_ref[0] = st_scr[...]


def _gla_prompt(geo, z, lb_table, o_gain):
    tg = min(256, geo.seq)
    chunk = HG_CHUNK if geo.seq % HG_CHUNK == 0 else geo.seq
    assert tg % chunk == 0 and geo.seq % tg == 0
    nt = geo.seq // tg
    return pl.pallas_call(
        functools.partial(_gla_prompt_kernel, chunk=chunk, n_chunks=tg // chunk),
        grid=(geo.batch, nt),
        in_specs=[
            pl.BlockSpec((tg, 4 * D_MODEL), lambda b, t: (b * nt + t, 0)),
            pl.BlockSpec(lb_table.shape, lambda b, t: (0, 0)),
            pl.BlockSpec((1, HG_DV), lambda b, t: (0, 0)),
        ],
        out_specs=[
            pl.BlockSpec((tg, D_MODEL), lambda b, t: (b * nt + t, 0)),
            pl.BlockSpec((1, HG_HEADS, HG_DV, HG_DK), lambda b, t: (b, 0, 0, 0)),
        ],
        out_shape=[
            jax.ShapeDtypeStruct((geo.n_prompt, D_MODEL), BF16),
            jax.ShapeDtypeStruct((geo.batch, HG_HEADS, HG_DV, HG_DK), F32),
        ],
        scratch_shapes=[pltpu.VMEM((HG_HEADS, HG_DV, HG_DK), F32)],
        compiler_params=_cparams(("arbitrary", "arbitrary")),
        name="gla_prompt",
    )(z, lb_table, o_gain)


def _gla_step_kernel(z_ref, s_ref, lbt_ref, og_ref, o_ref, snew_ref, *, steps, bb):
    kw = HG_HEADS * HG_DK
    pad = SUBLANES
    lb = _hg_lower_bound(lbt_ref)
    og = og_ref[...]
    r_i = lax.broadcasted_iota(I32, (pad, pad), 0)
    c_i = lax.broadcasted_iota(I32, (pad, pad), 1)
    causal = c_i <= r_i
    tri = causal.astype(BF16)
    row_w = lax.broadcasted_iota(I32, (pad, kw), 0)
    live = row_w < steps
    row_k = lax.broadcasted_iota(I32, (pad, HG_DK), 0)
    ones_sel = jnp.where((row_k == steps) | (row_k == steps + 1), 1.0, 0.0).astype(BF16)

    def body(s, carry):
        z = z_ref[s]
        logf, kk = _hg_gates(z[:, kw:2 * kw], lb)
        hi, md, lo = _split3(jnp.where(live, logf, 0.0))
        cs = _dot(tri, jnp.concatenate([hi, md, lo], axis=1))
        b = cs[:, :kw] + cs[:, kw:2 * kw] + cs[:, 2 * kw:]
        b_last = b[steps - 1:steps]
        q_in = (_silu(z[:, 0:kw]) * jnp.exp(b)).astype(BF16)
        k_hat = jnp.where(live, kk * jnp.exp(-b), 0.0).astype(BF16)
        k_dec = jnp.where(live, kk * jnp.exp(b_last - b), 0.0).astype(BF16)
        dec = jnp.exp(b_last)
        d_hi = dec.astype(BF16)
        d_lo = (dec - d_hi.astype(F32)).astype(BF16)
        a_all = jnp.where(row_w == steps, d_hi, jnp.where(row_w == steps + 1, d_lo, k_dec))
        for h in range(HG_HEADS):
            cols = slice(h * HG_DK, (h + 1) * HG_DK)
            vcols = slice(2 * kw + h * HG_DV, 2 * kw + (h + 1) * HG_DV)
            gcols = slice(2 * kw + HG_HEADS * HG_DV + h * HG_DV, 2 * kw + HG_HEADS * HG_DV + (h + 1) * HG_DV)
            v = z[:, vcols].astype(BF16)
            s0 = s_ref[s, h]
            att = jnp.where(causal, _dot_nt(q_in[:, cols], k_hat[:, cols]), 0.0).astype(BF16)
            o = _dot(att, v) + _dot(q_in[:, cols], s0.astype(BF16))
            upd = _dot_tn(a_all[:, cols], jnp.concatenate([v, ones_sel], axis=1))
            snew_ref[s, h] = upd[:, HG_DV:] * s0 + upd[:, :HG_DV]
            o_ref[s, :, h * HG_DV:(h + 1) * HG_DV] = _hg_out(o, z[:, gcols], og)
        return carry

    lax.fori_loop(0, bb, body, 0)


def _seq_major(geo, rows):
    steps, db = geo.dec_seq, geo.dec_batch
    w = rows.shape[-1]
    r = rows.reshape(steps, db, w).transpose(1, 0, 2)
    return jnp.concatenate([r, jnp.zeros((db, SUBLANES - steps, w), rows.dtype)], axis=1)


def _time_major(geo, r):
    steps, db = geo.dec_seq, geo.dec_batch
    return r[:, :steps].transpose(1, 0, 2).reshape(steps * db, r.shape[-1])


def _gla_step(geo, z, state, lb_table, o_gain):
    steps, db = geo.dec_seq, geo.dec_batch
    assert steps + 2 <= SUBLANES
    bb = min(8, db)
    z_s = _seq_major(geo, z[geo.n_prompt:])
    o_s, s_new = pl.pallas_call(
        functools.partial(_gla_step_kernel, steps=steps, bb=bb),
        grid=(db // bb,),
        in_specs=[
            pl.BlockSpec((bb, SUBLANES, 4 * D_MODEL), lambda i: (i, 0, 0)),
            pl.BlockSpec((bb, HG_HEADS, HG_DK, HG_DV), lambda i: (i, 0, 0, 0)),
            pl.BlockSpec(lb_table.shape, lambda i: (0, 0)),
            pl.BlockSpec((1, HG_DV), lambda i: (0, 0)),
        ],
        out_specs=[
            pl.BlockSpec((bb, SUBLANES, D_MODEL), lambda i: (i, 0, 0)),
            pl.BlockSpec((bb, HG_HEADS, HG_DK, HG_DV), lambda i: (i, 0, 0, 0)),
        ],
        out_shape=[
            jax.ShapeDtypeStruct((db, SUBLANES, D_MODEL), F32),
            jax.ShapeDtypeStruct(state.shape, F32),
        ],
        compiler_params=_cparams(("arbitrary",)),
        name="gla_step",
    )(z_s, state, lb_table, o_gain)
    return _time_major(geo, o_s), s_new


def _post_kernel(ap_ref, as_ref, wo_ref, x_ref, gp_ref, gs_ref, n2_ref, shp_ref, shs_ref, scp_ref, scs_ref,
                 rw_ref, rb_ref, x1_ref, h2_ref, te_ref, gt_ref, rk_ref, cnt_ref, carry_scr, *, n_pt, tm):
    i = pl.program_id(0)

    @pl.when(i == 0)
    def _():
        carry_scr[...] = jnp.zeros_like(carry_scr)

    g1 = _pick_mod(i, n_pt, gp_ref, gs_ref)
    a = jnp.where(i >= n_pt, as_ref[...], ap_ref[...])
    x1 = x_ref[...] + g1 * _dot(a, wo_ref[...])
    x1_ref[...] = x1
    sh = _pick_mod(i, n_pt, shp_ref, shs_ref)
    sc = _pick_mod(i, n_pt, scp_ref, scs_ref)
    h2 = _norm_mod(x1, n2_ref[...], sc, sh)
    for j in range(D_TILES):
        h2_ref[:, j, :] = h2[:, j * LANES:(j + 1) * LANES]

    lane = lax.broadcasted_iota(I32, (tm, LANES), 1)
    logits = _dot(h2.astype(BF16), rw_ref[...]) + rb_ref[...]
    work = jnp.where(lane < N_EXPERTS, logits, -jnp.inf)
    vals, idxs, hits = [], [], []
    for _ in range(TOP_K):
        m = jnp.max(work, axis=-1, keepdims=True)
        idx = jnp.min(jnp.where(work == m, lane, LANES), axis=-1, keepdims=True)
        hit = lane == idx
        vals.append(m)
        idxs.append(idx)
        hits.append(hit)
        work = jnp.where(hit, -jnp.inf, work)
    exps = [jnp.exp(v - vals[0]) for v in vals]
    den = exps[0]
    for e in exps[1:]:
        den = den + e
    any_hit = hits[0]
    for hmask in hits[1:]:
        any_hit = any_hit | hmask
    any_f = jnp.where(any_hit, 1.0, 0.0)
    r_i = lax.broadcasted_iota(I32, (tm, tm), 0)
    c_i = lax.broadcasted_iota(I32, (tm, tm), 1)
    before = (c_i < r_i).astype(BF16)
    pos = carry_scr[...] + _dot(before, any_f.astype(BF16))
    te = jnp.zeros((tm, LANES), I32)
    rk = jnp.zeros((tm, LANES), I32)
    gt = jnp.zeros((tm, LANES), F32)
    for k in range(TOP_K):
        rank = jnp.sum(jnp.where(hits[k], pos, 0.0), axis=-1, keepdims=True)
        te = jnp.where(lane == k, idxs[k], te)
        rk = jnp.where(lane == k, rank.astype(I32), rk)
        gt = jnp.where(lane == k, exps[k] / den, gt)
    te_ref[...] = te
    rk_ref[...] = rk
    gt_ref[...] = gt
    carry_scr[...] = carry_scr[...] + jnp.sum(any_f, axis=0, keepdims=True)
    cnt_ref[...] = carry_scr[...]


def _post(geo, a_p, a_s, w_out_bf16, x, norm_g, mod_p, mod_s, col0, router_w, router_b):
    tm = geo.tm
    n_pt = geo.n_pt
    rw = jnp.zeros((D_MODEL, LANES), BF16).at[:, :N_EXPERTS].set(router_w.astype(BF16))
    rb = jnp.zeros((1, LANES), F32).at[0, :N_EXPERTS].set(router_b)
    row = lambda i: (i, 0)
    fixed = lambda i: (0, 0)
    return pl.pallas_call(
        functools.partial(_post_kernel, n_pt=n_pt, tm=tm),
        grid=(geo.n_tiles,),
        in_specs=[
            pl.BlockSpec((tm, D_MODEL), lambda i: (jnp.minimum(i, n_pt - 1), 0)),
            pl.BlockSpec((tm, D_MODEL), lambda i: (jnp.maximum(i - n_pt, 0), 0)),
            pl.BlockSpec((D_MODEL, D_MODEL), fixed),
            pl.BlockSpec((tm, D_MODEL), row),
            *_mod_specs(geo, col0 + 2, tm),
            pl.BlockSpec((1, D_MODEL), fixed),
            *_mod_specs(geo, col0 + 3, tm),
            *_mod_specs(geo, col0 + 4, tm),
            pl.BlockSpec((D_MODEL, LANES), fixed),
            pl.BlockSpec((1, LANES), fixed),
        ],
        out_specs=[
            pl.BlockSpec((tm, D_MODEL), row),
            pl.BlockSpec((tm, D_TILES, LANES), lambda i: (i, 0, 0)),
            pl.BlockSpec((tm, LANES), row),
            pl.BlockSpec((tm, LANES), row),
            pl.BlockSpec((tm, LANES), row),
            pl.BlockSpec((1, LANES), fixed),
        ],
        out_shape=[
            jax.ShapeDtypeStruct((geo.nt, D_MODEL), F32),
            jax.ShapeDtypeStruct((geo.nt, D_TILES, LANES), F32),
            jax.ShapeDtypeStruct((geo.nt, LANES), I32),
            jax.ShapeDtypeStruct((geo.nt, LANES), F32),
            jax.ShapeDtypeStruct((geo.nt, LANES), I32),
            jax.ShapeDtypeStruct((1, LANES), F32),
        ],
        scratch_shapes=[pltpu.VMEM((1, LANES), F32)],
        compiler_params=_cparams(("arbitrary",)),
        name="post_mixer_router",
    )(a_p, a_s.astype(BF16), w_out_bf16, x, mod_p, mod_s, norm_g, mod_p, mod_s, mod_p, mod_s, rw, rb)


def _ffn_kernel(be_ref, nreal_ref, tok_cur_ref, tok_nxt_ref, h_hbm, win_ref, bin_ref, wout_ref, bout_ref,
                y_ref, xs_scr, sem, win_scr, wout_scr, *, bm):
    b = pl.program_id(0)
    n_real = nreal_ref[0]
    slot = b % 2

    def gather(tok_ref, to_slot):
        def issue(i, carry):
            pltpu.make_async_copy(h_hbm.at[tok_ref[0, 0, i]], xs_scr.at[to_slot, i], sem.at[to_slot]).start()
            return carry
        lax.fori_loop(0, bm, issue, 0, unroll=8)

    @pl.when(b == 0)
    def _():
        gather(tok_cur_ref, 0)

    @pl.when(b + 1 < n_real)
    def _():
        gather(tok_nxt_ref, 1 - slot)

    @pl.when(b < n_real)
    def _():
        pltpu.make_async_copy(h_hbm.at[pl.ds(0, bm)], xs_scr.at[slot], sem.at[slot]).wait()
        changed = (b == 0) | (be_ref[b] != be_ref[jnp.maximum(b - 1, 0)])

        @pl.when(changed)
        def _():
            win_scr[...] = win_ref[0].astype(BF16)
            wout_scr[...] = wout_ref[0].astype(BF16)

        x = jnp.concatenate([xs_scr[slot, :, j, :] for j in range(D_TILES)], axis=-1).astype(BF16)
        gu = _dot(x, win_scr[...]) + bin_ref[0]
        gate = jnp.minimum(gu[:, :D_MODEL], SWIGLU_LIMIT)
        up = jnp.clip(gu[:, D_MODEL:], -SWIGLU_LIMIT, SWIGLU_LIMIT)
        act = gate * _sigmoid(SWIGLU_ALPHA * gate) * (up + 1.0)
        y = _dot(act.astype(BF16), wout_scr[...]) + bout_ref[0]
        for j in range(D_TILES):
            y_ref[:, j, :] = y[:, j * LANES:(j + 1) * LANES]

    @pl.when(b >= n_real)
    def _():
        y_ref[...] = jnp.zeros_like(y_ref)


def _ffn(h2, slot_tok, block_e, n_real, w_in, b_in, w_out, b_out, bm):
    n_blocks = slot_tok.shape[0]
    n_e, d, d2 = w_in.shape
    last = n_blocks - 1
    return pl.pallas_call(
        functools.partial(_ffn_kernel, bm=bm),
        grid_spec=pltpu.PrefetchScalarGridSpec(
            num_scalar_prefetch=2,
            grid=(n_blocks,),
            in_specs=[
                pl.BlockSpec((1, 1, bm), lambda b, be, nr: (b, 0, 0), memory_space=pltpu.SMEM),
                pl.BlockSpec((1, 1, bm), lambda b, be, nr: (jnp.minimum(b + 1, last), 0, 0),
                             memory_space=pltpu.SMEM),
                pl.BlockSpec(memory_space=pl.ANY),
                pl.BlockSpec((1, d, d2), lambda b, be, nr: (be[b], 0, 0)),
                pl.BlockSpec((1, 1, d2), lambda b, be, nr: (be[b], 0, 0)),
                pl.BlockSpec((1, d2 // 2, d), lambda b, be, nr: (be[b], 0, 0)),
                pl.BlockSpec((1, 1, d), lambda b, be, nr: (be[b], 0, 0)),
            ],
            out_specs=pl.BlockSpec((bm, D_TILES, LANES), lambda b, be, nr: (b, 0, 0)),
            scratch_shapes=[
                pltpu.VMEM((2, bm, D_TILES, LANES), F32),
                pltpu.SemaphoreType.DMA((2,)),
                pltpu.VMEM((d, d2), BF16),
                pltpu.VMEM((d2 // 2, d), BF16),
            ],
        ),
        out_shape=jax.ShapeDtypeStruct((n_blocks * bm, D_TILES, LANES), F32),
        compiler_params=_cparams(("arbitrary",)),
        name="moe_ffn",
    )(block_e, n_real, slot_tok, slot_tok, h2, w_in, b_in.reshape(n_e, 1, d2), w_out, b_out.reshape(n_e, 1, d))


def _combine_kernel(dst_cur_ref, dst_nxt_ref, y_hbm, x_ref, gp_ref, gs_ref, gate_ref, o_ref, buf, sem,
                    *, n_pt, tc):
    i = pl.program_id(0)
    n = pl.num_programs(0)
    slot = i % 2

    def gather(dst_ref, to_slot):
        for k in range(TOP_K):
            def issue(t, carry):
                pltpu.make_async_copy(y_hbm.at[dst_ref[0, 0, k * tc + t]], buf.at[to_slot, k, t],
                                      sem.at[to_slot]).start()
                return carry
            lax.fori_loop(0, tc, issue, 0, unroll=8)

    @pl.when(i == 0)
    def _():
        gather(dst_cur_ref, 0)

    @pl.when(i + 1 < n)
    def _():
        gather(dst_nxt_ref, 1 - slot)

    for k in range(TOP_K):
        pltpu.make_async_copy(y_hbm.at[pl.ds(0, tc)], buf.at[slot, k], sem.at[slot]).wait()
    g = gate_ref[...]
    ffn = jnp.zeros((tc, D_MODEL), F32)
    for k in range(TOP_K):
        rows = jnp.concatenate([buf[slot, k, :, j, :] for j in range(D_TILES)], axis=-1)
        ffn = ffn + g[:, k:k + 1] * rows
    o_ref[...] = x_ref[...] + _pick_mod(i, n_pt, gp_ref, gs_ref) * ffn


def _combine(geo, y_slots, dest, gates, x1, mod_p, mod_s, col_gate):
    tc = min(128, geo.tm)
    n_tiles = geo.nt // tc
    last = n_tiles - 1
    dst = dest.reshape(n_tiles, tc, TOP_K).transpose(0, 2, 1).reshape(n_tiles, 1, TOP_K * tc)
    row = lambda i: (i, 0)
    return pl.pallas_call(
        functools.partial(_combine_kernel, n_pt=geo.n_prompt // tc, tc=tc),
        grid=(n_tiles,),
        in_specs=[
            pl.BlockSpec((1, 1, TOP_K * tc), lambda i: (i, 0, 0), memory_space=pltpu.SMEM),
            pl.BlockSpec((1, 1, TOP_K * tc), lambda i: (jnp.minimum(i + 1, last), 0, 0), memory_space=pltpu.SMEM),
            pl.BlockSpec(memory_space=pl.ANY),
            pl.BlockSpec((tc, D_MODEL), row),
            *_mod_specs(geo, col_gate, tc),
            pl.BlockSpec((tc, LANES), row),
        ],
        out_specs=pl.BlockSpec((tc, D_MODEL), row),
        out_shape=jax.ShapeDtypeStruct((geo.nt, D_MODEL), F32),
        scratch_shapes=[
            pltpu.VMEM((2, TOP_K, tc, D_TILES, LANES), F32),
            pltpu.SemaphoreType.DMA((2,)),
        ],
        compiler_params=_cparams(("arbitrary",)),
        name="moe_combine",
    )(dst, dst, y_slots, x1, mod_p, mod_s, gates)


def _moe(geo, h2, te_pad, gt_pad, rk_pad, cnt, x1, mod_p, mod_s, col_gate, w_in, b_in, w_out, b_out):
    bm = MOE_BLOCK
    nk = geo.nt * TOP_K
    n_blocks = -(-(nk + N_EXPERTS * (bm - 1)) // bm)
    n_slots = n_blocks * bm
    te = te_pad[:, :TOP_K]
    rk = rk_pad[:, :TOP_K]
    counts = cnt[0, :N_EXPERTS].astype(I32)
    padded = (counts + bm - 1) // bm * bm
    pend = jnp.cumsum(padded)
    pstart = pend - padded
    start = jnp.cumsum(counts) - counts
    dest = jnp.sum(jnp.where(te[..., None] == jnp.arange(N_EXPERTS, dtype=I32), pstart, 0), axis=-1) + rk
    order = jnp.argsort(te.reshape(nk), stable=True).astype(I32)
    block_start = jnp.arange(n_blocks, dtype=I32) * bm
    block_e = jnp.minimum(jnp.sum(block_start[:, None] >= pend[None, :], axis=1), N_EXPERTS - 1).astype(I32)
    slot = jnp.arange(n_slots, dtype=I32)
    e_of_slot = jnp.repeat(block_e, bm)
    r = slot - pstart[e_of_slot]
    valid = (r >= 0) & (r < counts[e_of_slot])
    src = order[jnp.clip(start[e_of_slot] + r, 0, nk - 1)]
    slot_tok = jnp.where(valid, src // TOP_K, 0).reshape(n_blocks, 1, bm)
    n_real = (pend[-1:] // bm).astype(I32)
    y_slots = _ffn(h2, slot_tok, block_e, n_real, w_in, b_in, w_out, b_out, bm)
    return _combine(geo, y_slots, dest, gt_pad, x1, mod_p, mod_s, col_gate)


SW_QW = SW_Q_HEADS * SW_HEAD_DIM
SW_QKW = SW_QW + SW_KV_W
SW_KEYS = 2 * ATT_BLOCK


def _attn_proj_kernel(x_ref, g_ref, shp_ref, shs_ref, scp_ref, scs_ref, w_ref, e_ref, et_ref, qg_ref, kg_ref,
                      q_ref, k_ref, v_ref, *, n_pt):
    i = pl.program_id(0)
    sh = _pick_mod(i, n_pt, shp_ref, shs_ref)
    sc = _pick_mod(i, n_pt, scp_ref, scs_ref)
    h = _norm_mod(x_ref[...], g_ref[...], sc, sh).astype(BF16)
    z = _dot(h, w_ref[...])
    qk = z[:, :SW_QKW]
    sq = qk * qk
    sq_hi = sq.astype(BF16)
    sq_lo = (sq - sq_hi.astype(F32)).astype(BF16)
    ms = (_dot(sq_hi, e_ref[...]) + _dot(sq_lo, e_ref[...])) * (1.0 / SW_HEAD_DIM)
    inv = lax.rsqrt(ms + NORM_EPS)
    inv_hi = inv.astype(BF16)
    inv_lo = (inv - inv_hi.astype(F32)).astype(BF16)
    qk = qk * (_dot(inv_hi, et_ref[...]) + _dot(inv_lo, et_ref[...]))
    q_ref[...] = (qk[:, :SW_QW] * qg_ref[...] * SW_SCALE).astype(q_ref.dtype)
    k_ref[...] = qk[:, SW_QW:] * kg_ref[...]
    v_ref[...] = z[:, SW_QKW:]


def _attn_proj(geo, x, g, mod_p, mod_s, w_bf16, q_gain, k_gain):
    tm = geo.tm
    n_out = w_bf16.shape[1]
    heads = SW_QKW // SW_HEAD_DIM
    member = (np.arange(SW_QKW)[:, None] // SW_HEAD_DIM == np.arange(LANES)[None, :]).astype(np.float32)
    e = jnp.asarray(member, BF16)
    et = jnp.asarray(member.T, BF16)
    assert heads <= LANES
    qg = jnp.tile(q_gain, SW_Q_HEADS).reshape(1, SW_QW)
    kg = jnp.tile(k_gain, SW_KV_HEADS).reshape(1, SW_KV_W)
    row = lambda i: (i, 0)
    fixed = lambda i: (0, 0)
    return pl.pallas_call(
        functools.partial(_attn_proj_kernel, n_pt=geo.n_pt),
        grid=(geo.n_tiles,),
        in_specs=[
            pl.BlockSpec((tm, D_MODEL), row),
            pl.BlockSpec((1, D_MODEL), fixed),
            *_mod_specs(geo, 0, tm),
            *_mod_specs(geo, 1, tm),
            pl.BlockSpec((D_MODEL, n_out), fixed),
            pl.BlockSpec((SW_QKW, LANES), fixed),
            pl.BlockSpec((LANES, SW_QKW), fixed),
            pl.BlockSpec((1, SW_QW), fixed),
            pl.BlockSpec((1, SW_KV_W), fixed),
        ],
        out_specs=[
            pl.BlockSpec((tm, SW_QW), row),
            pl.BlockSpec((tm, SW_KV_W), row),
            pl.BlockSpec((tm, SW_KV_W), row),
        ],
        out_shape=[
            jax.ShapeDtypeStruct((geo.nt, SW_QW), BF16),
            jax.ShapeDtypeStruct((geo.nt, SW_KV_W), F32),
            jax.ShapeDtypeStruct((geo.nt, SW_KV_W), F32),
        ],
        compiler_params=_cparams(("arbitrary",)),
        name="attn_proj_qknorm",
    )(x, g, mod_p, mod_s, mod_p, mod_s, w_bf16, e, et, qg, kg)


def _rel_bucket_np(dist):
    n = np.maximum(dist, 0)
    max_exact = REL_BUCKETS // 2
    ratio = np.log(np.maximum(n, 1).astype(np.float32) / np.float32(max_exact)) / np.float32(
        math.log(REL_MAX_DIST / max_exact))
    large = max_exact + (ratio * np.float32(REL_BUCKETS - max_exact)).astype(np.int32)
    large = np.minimum(large, REL_BUCKETS - 1)
    return np.where(n < max_exact, n, large).astype(np.int32)


def _bucket_table(qpos, kpos, k_ok):
    dist = qpos[:, None] - kpos[None, :]
    ok = (dist >= 0) & (dist <= WINDOW) & k_ok[None, :]
    return np.where(ok, _rel_bucket_np(dist), -1).astype(np.int32)


def _build_bias(bkt_ref, rb_ref, bias_scr, rows):
    bkt = bkt_ref[...]
    base = jnp.where(bkt < 0, -jnp.inf, 0.0)
    for h in range(SW_Q_HEADS):
        bias_scr[h * rows:(h + 1) * rows, :] = base

    def add_bucket(j, carry):
        hit = bkt == j
        for h in range(SW_Q_HEADS):
            sl = slice(h * rows, (h + 1) * rows)
            bias_scr[sl, :] = bias_scr[sl, :] + jnp.where(hit, rb_ref[j, h], 0.0)
        return carry

    lax.fori_loop(0, REL_BUCKETS, add_bucket, 0)


def _sink_softmax_pv(s, sink, vv_g):
    m = jnp.maximum(jnp.max(s, axis=-1, keepdims=True), sink)
    e = jnp.exp(s - m)
    p = e / (jnp.sum(e, axis=-1, keepdims=True) + jnp.exp(sink - m))
    return _dot(p.astype(BF16), vv_g)


def _swa_prompt_kernel(rb_ref, sink_ref, bkt_ref, q_ref, kp_ref, kc_ref, vp_ref, vc_ref, o_ref, bias_scr):
    n = pl.program_id(1)
    blk = ATT_BLOCK

    @pl.when((pl.program_id(0) == 0) & (n == 0))
    def _():
        _build_bias(bkt_ref, rb_ref, bias_scr, blk)

    kk = jnp.concatenate([kp_ref[...], kc_ref[...]], axis=0).astype(BF16)
    vv = jnp.concatenate([vp_ref[...], vc_ref[...]], axis=0).astype(BF16)
    col = lax.broadcasted_iota(I32, (SW_GROUP * blk, SW_KEYS), 1)
    hide_prev = (n == 0) & (col < blk)
    for g in range(SW_KV_HEADS):
        kcols = slice(g * SW_HEAD_DIM, (g + 1) * SW_HEAD_DIM)
        heads = [g * SW_GROUP + j for j in range(SW_GROUP)]
        qs = jnp.concatenate([q_ref[:, hq * SW_HEAD_DIM:(hq + 1) * SW_HEAD_DIM] for hq in heads], axis=0)
        s = _dot_nt(qs, kk[:, kcols]) + bias_scr[heads[0] * blk:(heads[-1] + 1) * blk, :]
        s = jnp.where(hide_prev, -jnp.inf, s)
        sink = jnp.concatenate([jnp.full((blk, 1), sink_ref[hq], F32) for hq in heads], axis=0)
        og = _sink_softmax_pv(s, sink, vv[:, kcols])
        for j, hq in enumerate(heads):
            o_ref[:, hq * SW_HEAD_DIM:(hq + 1) * SW_HEAD_DIM] = og[j * blk:(j + 1) * blk].astype(o_ref.dtype)


def _swa_prompt(geo, q, k, v, rel_bias, sinks):
    blk = ATT_BLOCK
    assert geo.seq % blk == 0 and WINDOW == blk
    nb = geo.seq // blk
    bkt = jnp.asarray(_bucket_table(blk + np.arange(blk), np.arange(2 * blk), np.ones(2 * blk, bool)))
    cur = lambda b, n: (b * nb + n, 0)
    prev = lambda b, n: (b * nb + jnp.maximum(n - 1, 0), 0)
    smem = pl.BlockSpec(memory_space=pltpu.SMEM)
    return pl.pallas_call(
        _swa_prompt_kernel,
        grid=(geo.batch, nb),
        in_specs=[
            smem, smem,
            pl.BlockSpec((blk, SW_KEYS), lambda b, n: (0, 0)),
            pl.BlockSpec((blk, SW_QW), cur),
            pl.BlockSpec((blk, SW_KV_W), prev),
            pl.BlockSpec((blk, SW_KV_W), cur),
            pl.BlockSpec((blk, SW_KV_W), prev),
            pl.BlockSpec((blk, SW_KV_W), cur),
        ],
        out_specs=pl.BlockSpec((blk, SW_QW), cur),
        out_shape=jax.ShapeDtypeStruct((geo.n_prompt, SW_QW), BF16),
        scratch_shapes=[pltpu.VMEM((SW_Q_HEADS * blk, SW_KEYS), F32)],
        compiler_params=_cparams(("arbitrary", "arbitrary")),
        name="swa_prompt",
    )(rel_bias, sinks, bkt, q, k, k, v, v)


def _swa_step_kernel(rb_ref, sink_ref, bkt_ref, q_ref, kn_ref, vn_ref, ck_ref, cv_ref,
                     o_ref, nk_ref, nv_ref, bias_scr, *, steps, bb):
    pad = SUBLANES
    win = WINDOW

    @pl.when(pl.program_id(0) == 0)
    def _():
        _build_bias(bkt_ref, rb_ref, bias_scr, pad)

    fill = jnp.zeros((SW_KEYS - win - pad, SW_KV_W), F32)
    row8 = lax.broadcasted_iota(I32, (pad, SW_KV_W), 0)

    def shifted(cache, new):
        rolled = pltpu.roll(cache, win - steps, 0)
        tail = jnp.where(row8 < pad - steps, rolled[win - pad:], pltpu.roll(new, pad - steps, 0))
        return rolled[:win - pad], tail

    def body(s, carry):
        ck, cv, kn, vn = ck_ref[s], cv_ref[s], kn_ref[s], vn_ref[s]
        kk = jnp.concatenate([ck, kn, fill], axis=0).astype(BF16)
        vv = jnp.concatenate([cv, vn, fill], axis=0).astype(BF16)
        q = q_ref[s]
        for g in range(SW_KV_HEADS):
            kcols = slice(g * SW_HEAD_DIM, (g + 1) * SW_HEAD_DIM)
            heads = [g * SW_GROUP + j for j in range(SW_GROUP)]
            qs = jnp.concatenate([q[:, hq * SW_HEAD_DIM:(hq + 1) * SW_HEAD_DIM] for hq in heads], axis=0)
            sc = _dot_nt(qs.astype(BF16), kk[:, kcols]) + bias_scr[heads[0] * pad:(heads[-1] + 1) * pad, :]
            sink = jnp.concatenate([jnp.full((pad, 1), sink_ref[hq], F32) for hq in heads], axis=0)
            og = _sink_softmax_pv(sc, sink, vv[:, kcols])
            for j, hq in enumerate(heads):
                o_ref[s, :, hq * SW_HEAD_DIM:(hq + 1) * SW_HEAD_DIM] = og[j * pad:(j + 1) * pad]
        head, tail = shifted(ck, kn)
        nk_ref[s, 0:win - pad] = head
        nk_ref[s, win - pad:win] = tail
        head, tail = shifted(cv, vn)
        nv_ref[s, 0:win - pad] = head
        nv_ref[s, win - pad:win] = tail
        return carry

    lax.fori_loop(0, bb, body, 0)


def _swa_step(geo, q, k, v, cache_k, cache_v, rel_bias, sinks):
    steps, db = geo.dec_seq, geo.dec_batch
    win = cache_k.shape[1]
    assert win == WINDOW and steps <= SUBLANES
    bb = min(8, db)
    pad = SUBLANES
    q_s = _seq_major(geo, q[geo.n_prompt:].astype(F32))
    k_s = _seq_major(geo, k[geo.n_prompt:])
    v_s = _seq_major(geo, v[geo.n_prompt:])
    kpos = np.arange(SW_KEYS)
    k_ok = kpos < win + steps
    qpos = win + np.arange(pad)
    bkt = _bucket_table(qpos, kpos, k_ok)
    bkt[steps:] = -1
    smem = pl.BlockSpec(memory_space=pltpu.SMEM)
    blk3 = lambda r, w: pl.BlockSpec((bb, r, w), lambda i: (i, 0, 0))
    o_s, nk, nv = pl.pallas_call(
        functools.partial(_swa_step_kernel, steps=steps, bb=bb),
        grid=(db // bb,),
        in_specs=[
            smem, smem,
            pl.BlockSpec((pad, SW_KEYS), lambda i: (0, 0)),
            blk3(pad, SW_QW), blk3(pad, SW_KV_W), blk3(pad, SW_KV_W),
            blk3(win, SW_KV_W), blk3(win, SW_KV_W),
        ],
        out_specs=[blk3(pad, SW_QW), blk3(win, SW_KV_W), blk3(win, SW_KV_W)],
        out_shape=[
            jax.ShapeDtypeStruct((db, pad, SW_QW), F32),
            jax.ShapeDtypeStruct((db, win, SW_KV_W), F32),
            jax.ShapeDtypeStruct((db, win, SW_KV_W), F32),
        ],
        scratch_shapes=[pltpu.VMEM((SW_Q_HEADS * pad, SW_KEYS), F32)],
        compiler_params=_cparams(("arbitrary",)),
        name="swa_step",
    )(rel_bias, sinks, jnp.asarray(bkt), q_s, k_s, v_s, cache_k, cache_v)
    return _time_major(geo, o_s), nk, nv


def kernel(x_prompt, x_sample, state_hgrn, cache_win_k, cache_win_v, c_prompt, c_sample, norm1_g, norm2_g, ada_w, ada_b, hg_w_in, hg_lb_table, hg_onorm_g, hg_w_out, sw_w_in, sw_qnorm_g, sw_knorm_g, sw_sinks, sw_w_out, rel_bias, router_w, router_b, moe_w_in, moe_b_in, moe_w_out, moe_b_out):
    batch, seq, d = x_prompt.shape
    db, steps, _ = x_sample.shape
    assert d == D_MODEL and ada_w.shape[0] == 2 and hg_w_in.shape[0] == 1 and sw_w_in.shape[0] == 1
    geo = _Geom(batch, seq, db, steps)
    x = jnp.concatenate([x_prompt.reshape(batch * seq, d),
                         x_sample.transpose(1, 0, 2).reshape(steps * db, d)], axis=0)
    n_seq = batch + db
    rows = -(-n_seq // SUBLANES) * SUBLANES
    c_all = jnp.concatenate([c_prompt, c_sample, jnp.zeros((rows - n_seq, d), F32)], axis=0)
    mods = _ada_mods(c_all, ada_w, ada_b)

    def layer_mods(layer):
        mod_p = mods[layer, :batch].reshape(batch, 1, 6 * d)
        mod_s = jnp.tile(mods[layer, batch:n_seq], (steps, 1))
        return mod_p, mod_s

    def moe(layer, a_p, a_s, w_out, x_in, mod_p, mod_s):
        x1, h2, te, gt, rk, cnt = _post(geo, a_p, a_s, w_out.astype(BF16), x_in, norm2_g[layer:layer + 1],
                                        mod_p, mod_s, 0, router_w[layer], router_b[layer])
        return _moe(geo, h2, te, gt, rk, cnt, x1, mod_p, mod_s, 5,
                    moe_w_in[layer], moe_b_in[layer], moe_w_out[layer], moe_b_out[layer])

    mod_p, mod_s = layer_mods(0)
    og = hg_onorm_g[0:1]
    z = _proj(geo, x, norm1_g[0:1], mod_p, mod_s, hg_w_in[0].astype(BF16), 0, 1)
    o_p, st_p = _gla_prompt(geo, z, hg_lb_table, og)
    o_s, st_s = _gla_step(geo, z, state_hgrn[0], hg_lb_table, og)
    x = moe(0, o_p, o_s, hg_w_out[0], x, mod_p, mod_s)

    mod_p, mod_s = layer_mods(1)
    q, k, v = _attn_proj(geo, x, norm1_g[1:2], mod_p, mod_s, sw_w_in[0].astype(BF16), sw_qnorm_g[0], sw_knorm_g[0])
    win = cache_win_k.shape[2]
    a_p = _swa_prompt(geo, q, k, v, rel_bias, sw_sinks[0])
    a_s, nk, nv = _swa_step(geo, q, k, v, cache_win_k[0].reshape(db, win, SW_KV_W),
                            cache_win_v[0].reshape(db, win, SW_KV_W), rel_bias, sw_sinks[0])
    x = moe(1, a_p, a_s, sw_w_out[0], x, mod_p, mod_s)

    y_prompt = x[:geo.n_prompt].reshape(batch, seq, d)
    y_sample = x[geo.n_prompt:].reshape(steps, db, d).transpose(1, 0, 2)
    kv_shape = (1, batch, WINDOW, SW_KV_HEADS, SW_HEAD_DIM)
    k_p = k[:geo.n_prompt].reshape(batch, seq, SW_KV_W)[:, seq - WINDOW:].reshape(kv_shape)
    v_p = v[:geo.n_prompt].reshape(batch, seq, SW_KV_W)[:, seq - WINDOW:].reshape(kv_shape)
    cache_shape = (1, db, win, SW_KV_HEADS, SW_HEAD_DIM)
    return (y_prompt, y_sample, jnp.swapaxes(st_p, -1, -2)[None], st_s[None], k_p, v_p,
            nk.reshape(cache_shape), nv.reshape(cache_shape))
```

```python
import functools
import math

import numpy as np
import jax
import jax.numpy as jnp
from jax import lax
from jax.experimental import pallas as pl
from jax.experimental.pallas import tpu as pltpu

F32 = jnp.float32
BF16 = jnp.bfloat16
I32 = jnp.int32

D_MODEL = 1024
LANES = 128
SUBLANES = 8
D_TILES = D_MODEL // LANES
HG_DK = 128
HG_HEADS = D_MODEL // HG_DK
HG_DV = D_MODEL // HG_HEADS
HG_CHUNK = 64
SW_HEAD_DIM = 64
SW_Q_HEADS = D_MODEL // SW_HEAD_DIM
SW_KV_HEADS = 4
SW_GROUP = SW_Q_HEADS // SW_KV_HEADS
SW_KV_W = SW_KV_HEADS * SW_HEAD_DIM
WINDOW = 128
ATT_BLOCK = 128
SW_SCALE = SW_HEAD_DIM ** -0.5
REL_BUCKETS = 32
REL_MAX_DIST = 128
N_EXPERTS = 32
TOP_K = 4
SWIGLU_LIMIT = 7.0
SWIGLU_ALPHA = 1.702
NORM_EPS = 1e-5
MOE_BLOCK = 256
VMEM_LIMIT = 56 * 1024 * 1024


def _cparams(sem):
    return pltpu.CompilerParams(dimension_semantics=sem, vmem_limit_bytes=VMEM_LIMIT)


def _sigmoid(x):
    return 1.0 / (1.0 + jnp.exp(-x))


def _silu(x):
    return x * _sigmoid(x)


def _dot(a, b):
    return jnp.dot(a, b, preferred_element_type=F32)


def _dot_nt(a, b):
    return lax.dot_general(a, b, (((1,), (1,)), ((), ())), preferred_element_type=F32)


def _dot_tn(a, b):
    return lax.dot_general(a, b, (((0,), (0,)), ((), ())), preferred_element_type=F32)


def _split3(x):
    hi = x.astype(BF16)
    r = x - hi.astype(F32)
    mid = r.astype(BF16)
    lo = (r - mid.astype(F32)).astype(BF16)
    return hi, mid, lo


def _ada_kernel(c_ref, w_ref, b_ref, o_ref):
    s = _silu(c_ref[...]).astype(BF16)
    o_ref[0] = _dot(s, w_ref[0].astype(BF16)) + b_ref[0]


def _ada_mods(c_all, ada_w, ada_b):
    depth, d, n6 = ada_w.shape
    rows = c_all.shape[0]
    tn = 1536
    return pl.pallas_call(
        _ada_kernel,
        grid=(depth, n6 // tn),
        in_specs=[
            pl.BlockSpec((rows, d), lambda l, j: (0, 0)),
            pl.BlockSpec((1, d, tn), lambda l, j: (l, 0, j)),
            pl.BlockSpec((1, 1, tn), lambda l, j: (l, 0, j)),
        ],
        out_specs=pl.BlockSpec((1, rows, tn), lambda l, j: (l, 0, j)),
        out_shape=jax.ShapeDtypeStruct((depth, rows, n6), F32),
        compiler_params=_cparams(("arbitrary", "arbitrary")),
        name="ada_mods",
    )(c_all, ada_w, ada_b.reshape(depth, 1, n6))


class _Geom:
    def __init__(self, batch, seq, dec_batch, dec_seq):
        self.batch, self.seq, self.dec_batch, self.dec_seq = batch, seq, dec_batch, dec_seq
        self.n_prompt = batch * seq
        self.n_sample = dec_batch * dec_seq
        self.nt = self.n_prompt + self.n_sample
        tm = 512
        while seq % tm or self.n_sample % tm:
            tm //= 2
        assert tm >= 8
        self.tm = tm
        self.n_pt = self.n_prompt // tm
        self.n_tiles = self.nt // tm


def _mod_specs(geo, col, tm):
    seq, batch = geo.seq, geo.batch
    n_pt = geo.n_prompt // tm

    def p_map(i, *_):
        return (jnp.minimum(i * tm // seq, batch - 1), 0, col)

    def s_map(i, *_):
        return (jnp.maximum(i - n_pt, 0), col)

    return [pl.BlockSpec((1, 1, D_MODEL), p_map), pl.BlockSpec((tm, D_MODEL), s_map)]


def _pick_mod(i, n_pt, p_ref, s_ref):
    return jnp.where(i >= n_pt, s_ref[...], p_ref[0])


def _norm_mod(x, g, sc, sh):
    ms = jnp.mean(x * x, axis=-1, keepdims=True)
    return x * lax.rsqrt(ms + NORM_EPS) * g * (1.0 + sc) + sh


def _proj_kernel(x_ref, g_ref, shp_ref, shs_ref, scp_ref, scs_ref, w_ref, o_ref, h_scr, *, n_pt):
    i = pl.program_id(0)

    @pl.when(pl.program_id(1) == 0)
    def _():
        sh = _pick_mod(i, n_pt, shp_ref, shs_ref)
        sc = _pick_mod(i, n_pt, scp_ref, scs_ref)
        h_scr[...] = _norm_mod(x_ref[...], g_ref[...], sc, sh).astype(BF16)

    o_ref[...] = _dot(h_scr[...], w_ref[...])


def _proj(geo, x, g, mod_p, mod_s, w_bf16, col_shift, col_scale):
    n_out = w_bf16.shape[1]
    tn = n_out if n_out <= 2048 else 1024
    tm = geo.tm
    return pl.pallas_call(
        functools.partial(_proj_kernel, n_pt=geo.n_pt),
        grid=(geo.n_tiles, n_out // tn),
        in_specs=[
            pl.BlockSpec((tm, D_MODEL), lambda i, j: (i, 0)),
            pl.BlockSpec((1, D_MODEL), lambda i, j: (0, 0)),
            *_mod_specs(geo, col_shift, tm),
            *_mod_specs(geo, col_scale, tm),
            pl.BlockSpec((D_MODEL, tn), lambda i, j: (0, j)),
        ],
        out_specs=pl.BlockSpec((tm, tn), lambda i, j: (i, j)),
        out_shape=jax.ShapeDtypeStruct((geo.nt, n_out), F32),
        scratch_shapes=[pltpu.VMEM((tm, D_MODEL), BF16)],
        compiler_params=_cparams(("arbitrary", "arbitrary")),
        name="norm_mod_proj",
    )(x, g, mod_p, mod_s, mod_p, mod_s, w_bf16)


def _hg_lower_bound(lbt_ref):
    t = lbt_ref[...]
    e = jnp.exp(t - jnp.max(t, axis=0, keepdims=True))
    return e[0:1] / jnp.sum(e, axis=0, keepdims=True)


def _hg_gates(fz, lb):
    e = jnp.exp(-jnp.abs(fz))
    inv = 1.0 / (1.0 + e)
    pos = fz >= 0
    sig = jnp.where(pos, inv, e * inv)
    sig_neg = jnp.where(pos, e * inv, inv)
    logf = jnp.log(lb + (1.0 - lb) * sig)
    return logf, (1.0 - lb) * sig_neg


def _hg_out(o, gz, og):
    ms = jnp.mean(o * o, axis=-1, keepdims=True)
    return o * lax.rsqrt(ms + NORM_EPS) * og * _silu(gz)


def _gla_prompt_kernel(z_ref, lbt_ref, og_ref, o_ref, sfin_ref, st_scr, *, chunk, n_chunks):
    t_step = pl.program_id(1)
    kw = HG_HEADS * HG_DK

    @pl.when(t_step == 0)
    def _():
        st_scr[...] = jnp.zeros_like(st_scr)

    lb = _hg_lower_bound(lbt_ref)
    og = og_ref[...]
    r_i = lax.broadcasted_iota(I32, (chunk, chunk), 0)
    c_i = lax.broadcasted_iota(I32, (chunk, chunk), 1)
    causal = c_i <= r_i
    tri = causal.astype(BF16)
    mid = chunk // 2 - 1

    def body(c, carry):
        rows = pl.ds(pl.multiple_of(c * chunk, chunk), chunk)
        logf, kk = _hg_gates(z_ref[rows, kw:2 * kw], lb)
        hi, md, lo = _split3(logf)
        cs = _dot(tri, jnp.concatenate([hi, md, lo], axis=1))
        b = cs[:, :kw] + cs[:, kw:2 * kw] + cs[:, 2 * kw:]
        b_mid = b[mid:mid + 1]
        b_last = b[chunk - 1:chunk]
        q_hat = _silu(z_ref[rows, 0:kw]) * jnp.exp(b - b_mid)
        k_hat = kk * jnp.exp(b_mid - b)
        q_in = (q_hat * jnp.exp(b_mid)).astype(BF16)
        k_dec = (k_hat * jnp.exp(b_last - b_mid)).astype(BF16)
        q_hat = q_hat.astype(BF16)
        k_hat = k_hat.astype(BF16)
        dec = jnp.exp(b_last)
        for h in range(HG_HEADS):
            cols = slice(h * HG_DK, (h + 1) * HG_DK)
            vcols = slice(2 * kw + h * HG_DV, 2 * kw + (h + 1) * HG_DV)
            gcols = slice(2 * kw + HG_HEADS * HG_DV + h * HG_DV, 2 * kw + HG_HEADS * HG_DV + (h + 1) * HG_DV)
            v = z_ref[rows, vcols].astype(BF16)
            att = jnp.where(causal, _dot_nt(q_hat[:, cols], k_hat[:, cols]), 0.0).astype(BF16)
            st = st_scr[h]
            o = _dot(att, v) + _dot_nt(q_in[:, cols], st.astype(BF16))
            st_scr[h] = st * dec[:, cols] + _dot_tn(v, k_dec[:, cols])
            o_ref[rows, h * HG_DV:(h + 1) * HG_DV] = _hg_out(o, z_ref[rows, gcols], og).astype(o_ref.dtype)
        return carry

    lax.fori_loop(0, n_chunks, body, 0)

    @pl.when(t_step == pl.num_programs(1) - 1)
    def _():
        sfin_ref[0] = st_scr[...]


def _gla_prompt(geo, z, lb_table, o_gain):
    tg = min(256, geo.seq)
    chunk = HG_CHUNK if geo.seq % HG_CHUNK == 0 else geo.seq
    assert tg % chunk == 0 and geo.seq % tg == 0
    nt = geo.seq // tg
    return pl.pallas_call(
        functools.partial(_gla_prompt_kernel, chunk=chunk, n_chunks=tg // chunk),
        grid=(geo.batch, nt),
        in_specs=[
            pl.BlockSpec((tg, 4 * D_MODEL), lambda b, t: (b * nt + t, 0)),
            pl.BlockSpec(lb_table.shape, lambda b, t: (0, 0)),
            pl.BlockSpec((1, HG_DV), lambda b, t: (0, 0)),
        ],
        out_specs=[
            pl.BlockSpec((tg, D_MODEL), lambda b, t: (b * nt + t, 0)),
            pl.BlockSpec((1, HG_HEADS, HG_DV, HG_DK), lambda b, t: (b, 0, 0, 0)),
        ],
        out_shape=[
            jax.ShapeDtypeStruct((geo.n_prompt, D_MODEL), BF16),
            jax.ShapeDtypeStruct((geo.batch, HG_HEADS, HG_DV, HG_DK), F32),
        ],
        scratch_shapes=[pltpu.VMEM((HG_HEADS, HG_DV, HG_DK), F32)],
        compiler_params=_cparams(("arbitrary", "arbitrary")),
        name="gla_prompt",
    )(z, lb_table, o_gain)


def _gla_step_kernel(z_ref, s_ref, lbt_ref, og_ref, o_ref, snew_ref, *, steps, bb):
    kw = HG_HEADS * HG_DK
    pad = SUBLANES
    lb = _hg_lower_bound(lbt_ref)
    og = og_ref[...]
    r_i = lax.broadcasted_iota(I32, (pad, pad), 0)
    c_i = lax.broadcasted_iota(I32, (pad, pad), 1)
    causal = c_i <= r_i
    tri = causal.astype(BF16)
    row_w = lax.broadcasted_iota(I32, (pad, kw), 0)
    live = row_w < steps
    row_k = lax.broadcasted_iota(I32, (pad, HG_DK), 0)
    ones_sel = jnp.where((row_k == steps) | (row_k == steps + 1), 1.0, 0.0).astype(BF16)

    def body(s, carry):
        z = z_ref[s]
        logf, kk = _hg_gates(z[:, kw:2 * kw], lb)
        hi, md, lo = _split3(jnp.where(live, logf, 0.0))
        cs = _dot(tri, jnp.concatenate([hi, md, lo], axis=1))
        b = cs[:, :kw] + cs[:, kw:2 * kw] + cs[:, 2 * kw:]
        b_last = b[steps - 1:steps]
        q_in = (_silu(z[:, 0:kw]) * jnp.exp(b)).astype(BF16)
        k_hat = jnp.where(live, kk * jnp.exp(-b), 0.0).astype(BF16)
        k_dec = jnp.where(live, kk * jnp.exp(b_last - b), 0.0).astype(BF16)
        dec = jnp.exp(b_last)
        d_hi = dec.astype(BF16)
        d_lo = (dec - d_hi.astype(F32)).astype(BF16)
        a_all = jnp.where(row_w == steps, d_hi, jnp.where(row_w == steps + 1, d_lo, k_dec))
        for h in range(HG_HEADS):
            cols = slice(h * HG_DK, (h + 1) * HG_DK)
            vcols = slice(2 * kw + h * HG_DV, 2 * kw + (h + 1) * HG_DV)
            gcols = slice(2 * kw + HG_HEADS * HG_DV + h * HG_DV, 2 * kw + HG_HEADS * HG_DV + (h + 1) * HG_DV)
            v = z[:, vcols].astype(BF16)
            s0 = s_ref[s, h]
            att = jnp.where(causal, _dot_nt(q_in[:, cols], k_hat[:, cols]), 0.0).astype(BF16)
            o = _dot(att, v) + _dot(q_in[:, cols], s0.astype(BF16))
            upd = _dot_tn(a_all[:, cols], jnp.concatenate([v, ones_sel], axis=1))
            snew_ref[s, h] = upd[:, HG_DV:] * s0 + upd[:, :HG_DV]
            o_ref[s, :, h * HG_DV:(h + 1) * HG_DV] = _hg_out(o, z[:, gcols], og)
        return carry

    lax.fori_loop(0, bb, body, 0)


def _seq_major(geo, rows):
    steps, db = geo.dec_seq, geo.dec_batch
    w = rows.shape[-1]
    r = rows.reshape(steps, db, w).transpose(1, 0, 2)
    return jnp.concatenate([r, jnp.zeros((db, SUBLANES - steps, w), rows.dtype)], axis=1)


def _time_major(geo, r):
    steps, db = geo.dec_seq, geo.dec_batch
    return r[:, :steps].transpose(1, 0, 2).reshape(steps * db, r.shape[-1])


def _gla_step(geo, z, state, lb_table, o_gain):
    steps, db = geo.dec_seq, geo.dec_batch
    assert steps + 2 <= SUBLANES
    bb = min(8, db)
    z_s = _seq_major(geo, z[geo.n_prompt:])
    o_s, s_new = pl.pallas_call(
        functools.partial(_gla_step_kernel, steps=steps, bb=bb),
        grid=(db // bb,),
        in_specs=[
            pl.BlockSpec((bb, SUBLANES, 4 * D_MODEL), lambda i: (i, 0, 0)),
            pl.BlockSpec((bb, HG_HEADS, HG_DK, HG_DV), lambda i: (i, 0, 0, 0)),
            pl.BlockSpec(lb_table.shape, lambda i: (0, 0)),
            pl.BlockSpec((1, HG_DV), lambda i: (0, 0)),
        ],
        out_specs=[
            pl.BlockSpec((bb, SUBLANES, D_MODEL), lambda i: (i, 0, 0)),
            pl.BlockSpec((bb, HG_HEADS, HG_DK, HG_DV), lambda i: (i, 0, 0, 0)),
        ],
        out_shape=[
            jax.ShapeDtypeStruct((db, SUBLANES, D_MODEL), F32),
            jax.ShapeDtypeStruct(state.shape, F32),
        ],
        compiler_params=_cparams(("arbitrary",)),
        name="gla_step",
    )(z_s, state, lb_table, o_gain)
    return _time_major(geo, o_s), s_new


def _post_kernel(ap_ref, as_ref, wo_ref, x_ref, gp_ref, gs_ref, n2_ref, shp_ref, shs_ref, scp_ref, scs_ref,
                 rw_ref, rb_ref, x1_ref, h2_ref, te_ref, gt_ref, rk_ref, cnt_ref, carry_scr, *, n_pt, tm):
    i = pl.program_id(0)

    @pl.when(i == 0)
    def _():
        carry_scr[...] = jnp.zeros_like(carry_scr)

    g1 = _pick_mod(i, n_pt, gp_ref, gs_ref)
    a = jnp.where(i >= n_pt, as_ref[...], ap_ref[...])
    x1 = x_ref[...] + g1 * _dot(a, wo_ref[...])
    x1_ref[...] = x1
    sh = _pick_mod(i, n_pt, shp_ref, shs_ref)
    sc = _pick_mod(i, n_pt, scp_ref, scs_ref)
    h2 = _norm_mod(x1, n2_ref[...], sc, sh)
    for j in range(D_TILES):
        h2_ref[:, j, :] = h2[:, j * LANES:(j + 1) * LANES]

    lane = lax.broadcasted_iota(I32, (tm, LANES), 1)
    logits = _dot(h2.astype(BF16), rw_ref[...]) + rb_ref[...]
    work = jnp.where(lane < N_EXPERTS, logits, -jnp.inf)
    vals, idxs, hits = [], [], []
    for _ in range(TOP_K):
        m = jnp.max(work, axis=-1, keepdims=True)
        idx = jnp.min(jnp.where(work == m, lane, LANES), axis=-1, keepdims=True)
        hit = lane == idx
        vals.append(m)
        idxs.append(idx)
        hits.append(hit)
        work = jnp.where(hit, -jnp.inf, work)
    exps = [jnp.exp(v - vals[0]) for v in vals]
    den = exps[0]
    for e in exps[1:]:
        den = den + e
    any_hit = hits[0]
    for hmask in hits[1:]:
        any_hit = any_hit | hmask
    any_f = jnp.where(any_hit, 1.0, 0.0)
    r_i = lax.broadcasted_iota(I32, (tm, tm), 0)
    c_i = lax.broadcasted_iota(I32, (tm, tm), 1)
    before = (c_i < r_i).astype(BF16)
    pos = carry_scr[...] + _dot(before, any_f.astype(BF16))
    te = jnp.zeros((tm, LANES), I32)
    rk = jnp.zeros((tm, LANES), I32)
    gt = jnp.zeros((tm, LANES), F32)
    for k in range(TOP_K):
        rank = jnp.sum(jnp.where(hits[k], pos, 0.0), axis=-1, keepdims=True)
        te = jnp.where(lane == k, idxs[k], te)
        rk = jnp.where(lane == k, rank.astype(I32), rk)
        gt = jnp.where(lane == k, exps[k] / den, gt)
    te_ref[...] = te
    rk_ref[...] = rk
    gt_ref[...] = gt
    carry_scr[...] = carry_scr[...] + jnp.sum(any_f, axis=0, keepdims=True)
    cnt_ref[...] = carry_scr[...]


def _post(geo, a_p, a_s, w_out_bf16, x, norm_g, mod_p, mod_s, col0, router_w, router_b):
    tm = geo.tm
    n_pt = geo.n_pt
    rw = jnp.zeros((D_MODEL, LANES), BF16).at[:, :N_EXPERTS].set(router_w.astype(BF16))
    rb = jnp.zeros((1, LANES), F32).at[0, :N_EXPERTS].set(router_b)
    row = lambda i: (i, 0)
    fixed = lambda i: (0, 0)
    return pl.pallas_call(
        functools.partial(_post_kernel, n_pt=n_pt, tm=tm),
        grid=(geo.n_tiles,),
        in_specs=[
            pl.BlockSpec((tm, D_MODEL), lambda i: (jnp.minimum(i, n_pt - 1), 0)),
            pl.BlockSpec((tm, D_MODEL), lambda i: (jnp.maximum(i - n_pt, 0), 0)),
            pl.BlockSpec((D_MODEL, D_MODEL), fixed),
            pl.BlockSpec((tm, D_MODEL), row),
            *_mod_specs(geo, col0 + 2, tm),
            pl.BlockSpec((1, D_MODEL), fixed),
            *_mod_specs(geo, col0 + 3, tm),
            *_mod_specs(geo, col0 + 4, tm),
            pl.BlockSpec((D_MODEL, LANES), fixed),
            pl.BlockSpec((1, LANES), fixed),
        ],
        out_specs=[
            pl.BlockSpec((tm, D_MODEL), row),
            pl.BlockSpec((tm, D_TILES, LANES), lambda i: (i, 0, 0)),
            pl.BlockSpec((tm, LANES), row),
            pl.BlockSpec((tm, LANES), row),
            pl.BlockSpec((tm, LANES), row),
            pl.BlockSpec((1, LANES), fixed),
        ],
        out_shape=[
            jax.ShapeDtypeStruct((geo.nt, D_MODEL), F32),
            jax.ShapeDtypeStruct((geo.nt, D_TILES, LANES), F32),
            jax.ShapeDtypeStruct((geo.nt, LANES), I32),
            jax.ShapeDtypeStruct((geo.nt, LANES), F32),
            jax.ShapeDtypeStruct((geo.nt, LANES), I32),
            jax.ShapeDtypeStruct((1, LANES), F32),
        ],
        scratch_shapes=[pltpu.VMEM((1, LANES), F32)],
        compiler_params=_cparams(("arbitrary",)),
        name="post_mixer_router",
    )(a_p, a_s.astype(BF16), w_out_bf16, x, mod_p, mod_s, norm_g, mod_p, mod_s, mod_p, mod_s, rw, rb)


def _ffn_kernel(be_ref, nreal_ref, tok_cur_ref, tok_nxt_ref, h_hbm, win_ref, bin_ref, wout_ref, bout_ref,
                y_ref, xs_scr, sem, win_scr, wout_scr, *, bm):
    b = pl.program_id(0)
    n_real = nreal_ref[0]
    slot = b % 2

    def gather(tok_ref, to_slot):
        def issue(i, carry):
            pltpu.make_async_copy(h_hbm.at[tok_ref[0, 0, i]], xs_scr.at[to_slot, i], sem.at[to_slot]).start()
            return carry
        lax.fori_loop(0, bm, issue, 0, unroll=8)

    @pl.when(b == 0)
    def _():
        gather(tok_cur_ref, 0)

    @pl.when(b + 1 < n_real)
    def _():
        gather(tok_nxt_ref, 1 - slot)

    @pl.when(b < n_real)
    def _():
        pltpu.make_async_copy(h_hbm.at[pl.ds(0, bm)], xs_scr.at[slot], sem.at[slot]).wait()
        changed = (b == 0) | (be_ref[b] != be_ref[jnp.maximum(b - 1, 0)])

        @pl.when(changed)
        def _():
            win_scr[...] = win_ref[0].astype(BF16)
            wout_scr[...] = wout_ref[0].astype(BF16)

        x = jnp.concatenate([xs_scr[slot, :, j, :] for j in range(D_TILES)], axis=-1).astype(BF16)
        gu = _dot(x, win_scr[...]) + bin_ref[0]
        gate = jnp.minimum(gu[:, :D_MODEL], SWIGLU_LIMIT)
        up = jnp.clip(gu[:, D_MODEL:], -SWIGLU_LIMIT, SWIGLU_LIMIT)
        act = gate * _sigmoid(SWIGLU_ALPHA * gate) * (up + 1.0)
        y = _dot(act.astype(BF16), wout_scr[...]) + bout_ref[0]
        for j in range(D_TILES):
            y_ref[:, j, :] = y[:, j * LANES:(j + 1) * LANES]

    @pl.when(b >= n_real)
    def _():
        y_ref[...] = jnp.zeros_like(y_ref)


def _ffn(h2, slot_tok, block_e, n_real, w_in, b_in, w_out, b_out, bm):
    n_blocks = slot_tok.shape[0]
    n_e, d, d2 = w_in.shape
    last = n_blocks - 1
    return pl.pallas_call(
        functools.partial(_ffn_kernel, bm=bm),
        grid_spec=pltpu.PrefetchScalarGridSpec(
            num_scalar_prefetch=2,
            grid=(n_blocks,),
            in_specs=[
                pl.BlockSpec((1, 1, bm), lambda b, be, nr: (b, 0, 0), memory_space=pltpu.SMEM),
                pl.BlockSpec((1, 1, bm), lambda b, be, nr: (jnp.minimum(b + 1, last), 0, 0),
                             memory_space=pltpu.SMEM),
                pl.BlockSpec(memory_space=pl.ANY),
                pl.BlockSpec((1, d, d2), lambda b, be, nr: (be[b], 0, 0)),
                pl.BlockSpec((1, 1, d2), lambda b, be, nr: (be[b], 0, 0)),
                pl.BlockSpec((1, d2 // 2, d), lambda b, be, nr: (be[b], 0, 0)),
                pl.BlockSpec((1, 1, d), lambda b, be, nr: (be[b], 0, 0)),
            ],
            out_specs=pl.BlockSpec((bm, D_TILES, LANES), lambda b, be, nr: (b, 0, 0)),
            scratch_shapes=[
                pltpu.VMEM((2, bm, D_TILES, LANES), F32),
                pltpu.SemaphoreType.DMA((2,)),
                pltpu.VMEM((d, d2), BF16),
                pltpu.VMEM((d2 // 2, d), BF16),
            ],
        ),
        out_shape=jax.ShapeDtypeStruct((n_blocks * bm, D_TILES, LANES), F32),
        compiler_params=_cparams(("arbitrary",)),
        name="moe_ffn",
    )(block_e, n_real, slot_tok, slot_tok, h2, w_in, b_in.reshape(n_e, 1, d2), w_out, b_out.reshape(n_e, 1, d))


def _combine_kernel(dst_cur_ref, dst_nxt_ref, y_hbm, x_ref, gp_ref, gs_ref, gate_ref, o_ref, buf, sem,
                    *, n_pt, tc):
    i = pl.program_id(0)
    n = pl.num_programs(0)
    slot = i % 2

    def gather(dst_ref, to_slot):
        for k in range(TOP_K):
            def issue(t, carry):
                pltpu.make_async_copy(y_hbm.at[dst_ref[0, 0, k * tc + t]], buf.at[to_slot, k, t],
                                      sem.at[to_slot]).start()
                return carry
            lax.fori_loop(0, tc, issue, 0, unroll=8)

    @pl.when(i == 0)
    def _():
        gather(dst_cur_ref, 0)

    @pl.when(i + 1 < n)
    def _():
        gather(dst_nxt_ref, 1 - slot)

    for k in range(TOP_K):
        pltpu.make_async_copy(y_hbm.at[pl.ds(0, tc)], buf.at[slot, k], sem.at[slot]).wait()
    g = gate_ref[...]
    ffn = jnp.zeros((tc, D_MODEL), F32)
    for k in range(TOP_K):
        rows = jnp.concatenate([buf[slot, k, :, j, :] for j in range(D_TILES)], axis=-1)
        ffn = ffn + g[:, k:k + 1] * rows
    o_ref[...] = x_ref[...] + _pick_mod(i, n_pt, gp_ref, gs_ref) * ffn


def _combine(geo, y_slots, dest, gates, x1, mod_p, mod_s, col_gate):
    tc = min(128, geo.tm)
    n_tiles = geo.nt // tc
    last = n_tiles - 1
    dst = dest.reshape(n_tiles, tc, TOP_K).transpose(0, 2, 1).reshape(n_tiles, 1, TOP_K * tc)
    row = lambda i: (i, 0)
    return pl.pallas_call(
        functools.partial(_combine_kernel, n_pt=geo.n_prompt // tc, tc=tc),
        grid=(n_tiles,),
        in_specs=[
            pl.BlockSpec((1, 1, TOP_K * tc), lambda i: (i, 0, 0), memory_space=pltpu.SMEM),
            pl.BlockSpec((1, 1, TOP_K * tc), lambda i: (jnp.minimum(i + 1, last), 0, 0), memory_space=pltpu.SMEM),
            pl.BlockSpec(memory_space=pl.ANY),
            pl.BlockSpec((tc, D_MODEL), row),
            *_mod_specs(geo, col_gate, tc),
            pl.BlockSpec((tc, LANES), row),
        ],
        out_specs=pl.BlockSpec((tc, D_MODEL), row),
        out_shape=jax.ShapeDtypeStruct((geo.nt, D_MODEL), F32),
        scratch_shapes=[
            pltpu.VMEM((2, TOP_K, tc, D_TILES, LANES), F32),
            pltpu.SemaphoreType.DMA((2,)),
        ],
        compiler_params=_cparams(("arbitrary",)),
        name="moe_combine",
    )(dst, dst, y_slots, x1, mod_p, mod_s, gates)


def _moe(geo, h2, te_pad, gt_pad, rk_pad, cnt, x1, mod_p, mod_s, col_gate, layer, w_in, b_in, w_out, b_out):
    bm = MOE_BLOCK
    depth, n_e, d, d2 = w_in.shape
    w_in = w_in.reshape(depth * n_e, d, d2)
    b_in = b_in.reshape(depth * n_e, d2)
    w_out = w_out.reshape(depth * n_e, d2 // 2, d)
    b_out = b_out.reshape(depth * n_e, d)
    nk = geo.nt * TOP_K
    n_blocks = -(-(nk + N_EXPERTS * (bm - 1)) // bm)
    n_slots = n_blocks * bm
    te = te_pad[:, :TOP_K]
    rk = rk_pad[:, :TOP_K]
    counts = cnt[0, :N_EXPERTS].astype(I32)
    padded = (counts + bm - 1) // bm * bm
    pend = jnp.cumsum(padded)
    pstart = pend - padded
    start = jnp.cumsum(counts) - counts
    dest = jnp.sum(jnp.where(te[..., None] == jnp.arange(N_EXPERTS, dtype=I32), pstart, 0), axis=-1) + rk
    order = jnp.argsort(te.reshape(nk), stable=True).astype(I32)
    block_start = jnp.arange(n_blocks, dtype=I32) * bm
    block_e = jnp.minimum(jnp.sum(block_start[:, None] >= pend[None, :], axis=1), N_EXPERTS - 1).astype(I32)
    slot = jnp.arange(n_slots, dtype=I32)
    e_of_slot = jnp.repeat(block_e, bm)
    r = slot - pstart[e_of_slot]
    valid = (r >= 0) & (r < counts[e_of_slot])
    src = order[jnp.clip(start[e_of_slot] + r, 0, nk - 1)]
    slot_tok = jnp.where(valid, src // TOP_K, 0).reshape(n_blocks, 1, bm)
    n_real = (pend[-1:] // bm).astype(I32)
    y_slots = _ffn(h2, slot_tok, block_e + layer * n_e, n_real, w_in, b_in, w_out, b_out, bm)
    return _combine(geo, y_slots, dest, gt_pad, x1, mod_p, mod_s, col_gate)


SW_QW = SW_Q_HEADS * SW_HEAD_DIM
SW_QKW = SW_QW + SW_KV_W
SW_KEYS = 2 * ATT_BLOCK


def _attn_proj_kernel(x_ref, g_ref, shp_ref, shs_ref, scp_ref, scs_ref, w_ref, e_ref, et_ref, qg_ref, kg_ref,
                      q_ref, k_ref, v_ref, *, n_pt):
    i = pl.program_id(0)
    sh = _pick_mod(i, n_pt, shp_ref, shs_ref)
    sc = _pick_mod(i, n_pt, scp_ref, scs_ref)
    h = _norm_mod(x_ref[...], g_ref[...], sc, sh).astype(BF16)
    z = _dot(h, w_ref[...])
    qk = z[:, :SW_QKW]
    sq = qk * qk
    sq_hi = sq.astype(BF16)
    sq_lo = (sq - sq_hi.astype(F32)).astype(BF16)
    ms = (_dot(sq_hi, e_ref[...]) + _dot(sq_lo, e_ref[...])) * (1.0 / SW_HEAD_DIM)
    inv = lax.rsqrt(ms + NORM_EPS)
    inv_hi = inv.astype(BF16)
    inv_lo = (inv - inv_hi.astype(F32)).astype(BF16)
    qk = qk * (_dot(inv_hi, et_ref[...]) + _dot(inv_lo, et_ref[...]))
    q_ref[...] = (qk[:, :SW_QW] * qg_ref[...] * SW_SCALE).astype(q_ref.dtype)
    k_ref[...] = qk[:, SW_QW:] * kg_ref[...]
    v_ref[...] = z[:, SW_QKW:]


def _attn_proj(geo, x, g, mod_p, mod_s, w_bf16, q_gain, k_gain):
    tm = geo.tm
    n_out = w_bf16.shape[1]
    heads = SW_QKW // SW_HEAD_DIM
    member = (np.arange(SW_QKW)[:, None] // SW_HEAD_DIM == np.arange(LANES)[None, :]).astype(np.float32)
    e = jnp.asarray(member, BF16)
    et = jnp.asarray(member.T, BF16)
    assert heads <= LANES
    qg = jnp.tile(q_gain, SW_Q_HEADS).reshape(1, SW_QW)
    kg = jnp.tile(k_gain, SW_KV_HEADS).reshape(1, SW_KV_W)
    row = lambda i: (i, 0)
    fixed = lambda i: (0, 0)
    return pl.pallas_call(
        functools.partial(_attn_proj_kernel, n_pt=geo.n_pt),
        grid=(geo.n_tiles,),
        in_specs=[
            pl.BlockSpec((tm, D_MODEL), row),
            pl.BlockSpec((1, D_MODEL), fixed),
            *_mod_specs(geo, 0, tm),
            *_mod_specs(geo, 1, tm),
            pl.BlockSpec((D_MODEL, n_out), fixed),
            pl.BlockSpec((SW_QKW, LANES), fixed),
            pl.BlockSpec((LANES, SW_QKW), fixed),
            pl.BlockSpec((1, SW_QW), fixed),
            pl.BlockSpec((1, SW_KV_W), fixed),
        ],
        out_specs=[
            pl.BlockSpec((tm, SW_QW), row),
            pl.BlockSpec((tm, SW_KV_W), row),
            pl.BlockSpec((tm, SW_KV_W), row),
        ],
        out_shape=[
            jax.ShapeDtypeStruct((geo.nt, SW_QW), BF16),
            jax.ShapeDtypeStruct((geo.nt, SW_KV_W), F32),
            jax.ShapeDtypeStruct((geo.nt, SW_KV_W), F32),
        ],
        compiler_params=_cparams(("arbitrary",)),
        name="attn_proj_qknorm",
    )(x, g, mod_p, mod_s, mod_p, mod_s, w_bf16, e, et, qg, kg)


def _rel_bucket_np(dist):
    n = np.maximum(dist, 0)
    max_exact = REL_BUCKETS // 2
    ratio = np.log(np.maximum(n, 1).astype(np.float32) / np.float32(max_exact)) / np.float32(
        math.log(REL_MAX_DIST / max_exact))
    large = max_exact + (ratio * np.float32(REL_BUCKETS - max_exact)).astype(np.int32)
    large = np.minimum(large, REL_BUCKETS - 1)
    return np.where(n < max_exact, n, large).astype(np.int32)


def _bucket_table(qpos, kpos, k_ok):
    dist = qpos[:, None] - kpos[None, :]
    ok = (dist >= 0) & (dist <= WINDOW) & k_ok[None, :]
    return np.where(ok, _rel_bucket_np(dist), -1).astype(np.int32)


def _build_bias(bkt_ref, rb_ref, bias_scr, rows):
    bkt = bkt_ref[...]
    base = jnp.where(bkt < 0, -jnp.inf, 0.0)
    for h in range(SW_Q_HEADS):
        bias_scr[h * rows:(h + 1) * rows, :] = base

    def add_bucket(j, carry):
        hit = bkt == j
        for h in range(SW_Q_HEADS):
            sl = slice(h * rows, (h + 1) * rows)
            bias_scr[sl, :] = bias_scr[sl, :] + jnp.where(hit, rb_ref[j, h], 0.0)
        return carry

    lax.fori_loop(0, REL_BUCKETS, add_bucket, 0)


def _sink_softmax_pv(s, sink, vv_g):
    m = jnp.maximum(jnp.max(s, axis=-1, keepdims=True), sink)
    e = jnp.exp(s - m)
    p = e / (jnp.sum(e, axis=-1, keepdims=True) + jnp.exp(sink - m))
    return _dot(p.astype(BF16), vv_g)


def _build_bias_t(bkt_ref, rb_ref, bias_scr, blk):
    bkt = bkt_ref[...]
    base = jnp.where(bkt < 0, -jnp.inf, 0.0)
    slots = [(g, j) for g in range(SW_KV_HEADS) for j in range(SW_GROUP)]
    for g, j in slots:
        bias_scr[g, :, j * blk:(j + 1) * blk] = base

    def add_bucket(b, carry):
        hit = bkt == b
        for g, j in slots:
            cols = slice(j * blk, (j + 1) * blk)
            bias_scr[g, :, cols] = bias_scr[g, :, cols] + jnp.where(hit, rb_ref[b, g * SW_GROUP + j], 0.0)
        return carry

    lax.fori_loop(0, REL_BUCKETS, add_bucket, 0)


def _swa_prompt_kernel(rb_ref, sink_ref, bkt_ref, q_ref, kp_ref, kc_ref, vp_ref, vc_ref, o_ref, bias_scr):
    n = pl.program_id(1)
    blk = ATT_BLOCK
    hd = SW_HEAD_DIM

    @pl.when((pl.program_id(0) == 0) & (n == 0))
    def _():
        _build_bias_t(bkt_ref, rb_ref, bias_scr, blk)

    kk = jnp.concatenate([kp_ref[...], kc_ref[...]], axis=0).astype(BF16)
    vv_t = jnp.concatenate([vp_ref[...], vc_ref[...]], axis=0).T.astype(BF16)
    q_t = q_ref[...].astype(F32).T.astype(BF16)
    key_i = lax.broadcasted_iota(I32, (SW_KEYS, SW_GROUP * blk), 0)
    hide_prev = (n == 0) & (key_i < blk)
    zeros_q = jnp.zeros((hd, SW_GROUP * blk), BF16)
    outs = []
    for g in range(SW_KV_HEADS):
        heads = [g * SW_GROUP + j for j in range(SW_GROUP)]
        q_cat = jnp.concatenate([q_t[hq * hd:(hq + 1) * hd, :] for hq in heads], axis=1)
        q_full = jnp.concatenate([q_cat if gg == g else zeros_q for gg in range(SW_KV_HEADS)], axis=0)
        s = _dot(kk, q_full) + bias_scr[g]
        s = jnp.where(hide_prev, -jnp.inf, s)
        sink = jnp.concatenate([jnp.full((1, blk), sink_ref[hq], F32) for hq in heads], axis=1)
        m = jnp.maximum(jnp.max(s, axis=0, keepdims=True), sink)
        e = jnp.exp(s - m)
        den = jnp.sum(e, axis=0, keepdims=True) + jnp.exp(sink - m)
        o_t = _dot(vv_t[g * hd:(g + 1) * hd, :], e.astype(BF16)) * (1.0 / den)
        outs.extend(o_t[:, j * blk:(j + 1) * blk] for j in range(SW_GROUP))
    o_ref[...] = jnp.concatenate(outs, axis=0).T.astype(o_ref.dtype)


def _swa_prompt(geo, q, k, v, rel_bias, sinks):
    blk = ATT_BLOCK
    assert geo.seq % blk == 0 and WINDOW == blk
    nb = geo.seq // blk
    bkt = jnp.asarray(_bucket_table(blk + np.arange(blk), np.arange(2 * blk), np.ones(2 * blk, bool)).T.copy())
    cur = lambda b, n: (b * nb + n, 0)
    prev = lambda b, n: (b * nb + jnp.maximum(n - 1, 0), 0)
    smem = pl.BlockSpec(memory_space=pltpu.SMEM)
    return pl.pallas_call(
        _swa_prompt_kernel,
        grid=(geo.batch, nb),
        in_specs=[
            smem, smem,
            pl.BlockSpec((SW_KEYS, blk), lambda b, n: (0, 0)),
            pl.BlockSpec((blk, SW_QW), cur),
            pl.BlockSpec((blk, SW_KV_W), prev),
            pl.BlockSpec((blk, SW_KV_W), cur),
            pl.BlockSpec((blk, SW_KV_W), prev),
            pl.BlockSpec((blk, SW_KV_W), cur),
        ],
        out_specs=pl.BlockSpec((blk, SW_QW), cur),
        out_shape=jax.ShapeDtypeStruct((geo.n_prompt, SW_QW), BF16),
        scratch_shapes=[pltpu.VMEM((SW_KV_HEADS, SW_KEYS, SW_GROUP * blk), F32)],
        compiler_params=_cparams(("arbitrary", "arbitrary")),
        name="swa_prompt",
    )(rel_bias, sinks, bkt, q, k, k, v, v)


def _swa_step_kernel(rb_ref, sink_ref, bkt_ref, q_ref, kn_ref, vn_ref, ck_ref, cv_ref,
                     o_ref, nk_ref, nv_ref, bias_scr, *, steps, bb):
    pad = SUBLANES
    win = WINDOW

    @pl.when(pl.program_id(0) == 0)
    def _():
        _build_bias(bkt_ref, rb_ref, bias_scr, pad)

    fill = jnp.zeros((SW_KEYS - win - pad, SW_KV_W), F32)
    row8 = lax.broadcasted_iota(I32, (pad, SW_KV_W), 0)

    def shifted(cache, new):
        rolled = pltpu.roll(cache, win - steps, 0)
        tail = jnp.where(row8 < pad - steps, rolled[win - pad:], pltpu.roll(new, pad - steps, 0))
        return rolled[:win - pad], tail

    def body(s, carry):
        ck, cv, kn, vn = ck_ref[s], cv_ref[s], kn_ref[s], vn_ref[s]
        kk = jnp.concatenate([ck, kn, fill], axis=0).astype(BF16)
        vv = jnp.concatenate([cv, vn, fill], axis=0).astype(BF16)
        q = q_ref[s]
        for g in range(SW_KV_HEADS):
            kcols = slice(g * SW_HEAD_DIM, (g + 1) * SW_HEAD_DIM)
            heads = [g * SW_GROUP + j for j in range(SW_GROUP)]
            qs = jnp.concatenate([q[:, hq * SW_HEAD_DIM:(hq + 1) * SW_HEAD_DIM] for hq in heads], axis=0)
            sc = _dot_nt(qs.astype(BF16), kk[:, kcols]) + bias_scr[heads[0] * pad:(heads[-1] + 1) * pad, :]
            sink = jnp.concatenate([jnp.full((pad, 1), sink_ref[hq], F32) for hq in heads], axis=0)
            og = _sink_softmax_pv(sc, sink, vv[:, kcols])
            for j, hq in enumerate(heads):
                o_ref[s, :, hq * SW_HEAD_DIM:(hq + 1) * SW_HEAD_DIM] = og[j * pad:(j + 1) * pad]
        head, tail = shifted(ck, kn)
        nk_ref[s, 0:win - pad] = head
        nk_ref[s, win - pad:win] = tail
        head, tail = shifted(cv, vn)
        nv_ref[s, 0:win - pad] = head
        nv_ref[s, win - pad:win] = tail
        return carry

    lax.fori_loop(0, bb, body, 0)


def _swa_step(geo, q, k, v, cache_k, cache_v, rel_bias, sinks):
    steps, db = geo.dec_seq, geo.dec_batch
    win = cache_k.shape[1]
    assert win == WINDOW and steps <= SUBLANES
    bb = min(8, db)
    pad = SUBLANES
    q_s = _seq_major(geo, q[geo.n_prompt:].astype(F32))
    k_s = _seq_major(geo, k[geo.n_prompt:])
    v_s = _seq_major(geo, v[geo.n_prompt:])
    kpos = np.arange(SW_KEYS)
    k_ok = kpos < win + steps
    qpos = win + np.arange(pad)
    bkt = _bucket_table(qpos, kpos, k_ok)
    bkt[steps:] = -1
    smem = pl.BlockSpec(memory_space=pltpu.SMEM)
    blk3 = lambda r, w: pl.BlockSpec((bb, r, w), lambda i: (i, 0, 0))
    o_s, nk, nv = pl.pallas_call(
        functools.partial(_swa_step_kernel, steps=steps, bb=bb),
        grid=(db // bb,),
        in_specs=[
            smem, smem,
            pl.BlockSpec((pad, SW_KEYS), lambda i: (0, 0)),
            blk3(pad, SW_QW), blk3(pad, SW_KV_W), blk3(pad, SW_KV_W),
            blk3(win, SW_KV_W), blk3(win, SW_KV_W),
        ],
        out_specs=[blk3(pad, SW_QW), blk3(win, SW_KV_W), blk3(win, SW_KV_W)],
        out_shape=[
            jax.ShapeDtypeStruct((db, pad, SW_QW), F32),
            jax.ShapeDtypeStruct((db, win, SW_KV_W), F32),
            jax.ShapeDtypeStruct((db, win, SW_KV_W), F32),
        ],
        scratch_shapes=[pltpu.VMEM((SW_Q_HEADS * pad, SW_KEYS), F32)],
        compiler_params=_cparams(("arbitrary",)),
        name="swa_step",
    )(rel_bias, sinks, jnp.asarray(bkt), q_s, k_s, v_s, cache_k, cache_v)
    return _time_major(geo, o_s), nk, nv


def kernel(x_prompt, x_sample, state_hgrn, cache_win_k, cache_win_v, c_prompt, c_sample, norm1_g, norm2_g, ada_w, ada_b, hg_w_in, hg_lb_table, hg_onorm_g, hg_w_out, sw_w_in, sw_qnorm_g, sw_knorm_g, sw_sinks, sw_w_out, rel_bias, router_w, router_b, moe_w_in, moe_b_in, moe_w_out, moe_b_out):
    batch, seq, d = x_prompt.shape
    db, steps, _ = x_sample.shape
    assert d == D_MODEL and ada_w.shape[0] == 2 and hg_w_in.shape[0] == 1 and sw_w_in.shape[0] == 1
    geo = _Geom(batch, seq, db, steps)
    x = jnp.concatenate([x_prompt.reshape(batch * seq, d),
                         x_sample.transpose(1, 0, 2).reshape(steps * db, d)], axis=0)
    n_seq = batch + db
    rows = -(-n_seq // SUBLANES) * SUBLANES
    c_all = jnp.concatenate([c_prompt, c_sample, jnp.zeros((rows - n_seq, d), F32)], axis=0)
    mods = _ada_mods(c_all, ada_w, ada_b)

    def layer_mods(layer):
        mod_p = mods[layer, :batch].reshape(batch, 1, 6 * d)
        mod_s = jnp.tile(mods[layer, batch:n_seq], (steps, 1))
        return mod_p, mod_s

    def moe(layer, a_p, a_s, w_out, x_in, mod_p, mod_s):
        x1, h2, te, gt, rk, cnt = _post(geo, a_p, a_s, w_out.astype(BF16), x_in, norm2_g[layer:layer + 1],
                                        mod_p, mod_s, 0, router_w[layer], router_b[layer])
        return _moe(geo, h2, te, gt, rk, cnt, x1, mod_p, mod_s, 5,
                    layer, moe_w_in, moe_b_in, moe_w_out, moe_b_out)

    mod_p, mod_s = layer_mods(0)
    og = hg_onorm_g[0:1]
    z = _proj(geo, x, norm1_g[0:1], mod_p, mod_s, hg_w_in[0].astype(BF16), 0, 1)
    o_p, st_p = _gla_prompt(geo, z, hg_lb_table, og)
    o_s, st_s = _gla_step(geo, z, state_hgrn[0], hg_lb_table, og)
    x = moe(0, o_p, o_s, hg_w_out[0], x, mod_p, mod_s)

    mod_p, mod_s = layer_mods(1)
    q, k, v = _attn_proj(geo, x, norm1_g[1:2], mod_p, mod_s, sw_w_in[0].astype(BF16), sw_qnorm_g[0], sw_knorm_g[0])
    win = cache_win_k.shape[2]
    a_p = _swa_prompt(geo, q, k, v, rel_bias, sw_sinks[0])
    a_s, nk, nv = _swa_step(geo, q, k, v, cache_win_k[0].reshape(db, win, SW_KV_W),
                            cache_win_v[0].reshape(db, win, SW_KV_W), rel_bias, sw_sinks[0])
    x = moe(1, a_p, a_s, sw_w_out[0], x, mod_p, mod_s)

    y_prompt = x[:geo.n_prompt].reshape(batch, seq, d)
    y_sample = x[geo.n_prompt:].reshape(steps, db, d).transpose(1, 0, 2)
    kv_shape = (1, batch, WINDOW, SW_KV_HEADS, SW_HEAD_DIM)
    k_p = k[:geo.n_prompt].reshape(batch, seq, SW_KV_W)[:, seq - WINDOW:].reshape(kv_shape)
    v_p = v[:geo.n_prompt].reshape(batch, seq, SW_KV_W)[:, seq - WINDOW:].reshape(kv_shape)
    cache_shape = (1, db, win, SW_KV_HEADS, SW_HEAD_DIM)
    return (y_prompt, y_sample, jnp.swapaxes(st_p, -1, -2)[None], st_s[None], k_p, v_p,
            nk.reshape(cache_shape), nv.reshape(cache_shape))
```

```python
import functools
import math

import numpy as np
import jax
import jax.numpy as jnp
from jax import lax
from jax.experimental import pallas as pl
from jax.experimental.pallas import tpu as pltpu

F32 = jnp.float32
BF16 = jnp.bfloat16
I32 = jnp.int32

D_MODEL = 1024
LANES = 128
SUBLANES = 8
D_TILES = D_MODEL // LANES
HG_DK = 128
HG_HEADS = D_MODEL // HG_DK
HG_DV = D_MODEL // HG_HEADS
HG_CHUNK = 64
SW_HEAD_DIM = 64
SW_Q_HEADS = D_MODEL // SW_HEAD_DIM
SW_KV_HEADS = 4
SW_GROUP = SW_Q_HEADS // SW_KV_HEADS
SW_KV_W = SW_KV_HEADS * SW_HEAD_DIM
WINDOW = 128
ATT_BLOCK = 128
SW_SCALE = SW_HEAD_DIM ** -0.5
REL_BUCKETS = 32
REL_MAX_DIST = 128
N_EXPERTS = 32
TOP_K = 4
SWIGLU_LIMIT = 7.0
SWIGLU_ALPHA = 1.702
NORM_EPS = 1e-5
MOE_BLOCK = 256
ROW_CHUNK = SUBLANES
VMEM_LIMIT = 56 * 1024 * 1024


def _cparams(sem):
    return pltpu.CompilerParams(dimension_semantics=sem, vmem_limit_bytes=VMEM_LIMIT)


def _sigmoid(x):
    return 1.0 / (1.0 + jnp.exp(-x))


def _silu(x):
    return x * _sigmoid(x)


def _dot(a, b):
    return jnp.dot(a, b, preferred_element_type=F32)


def _dot_nt(a, b):
    return lax.dot_general(a, b, (((1,), (1,)), ((), ())), preferred_element_type=F32)


def _dot_tn(a, b):
    return lax.dot_general(a, b, (((0,), (0,)), ((), ())), preferred_element_type=F32)


def _split3(x):
    hi = x.astype(BF16)
    r = x - hi.astype(F32)
    mid = r.astype(BF16)
    lo = (r - mid.astype(F32)).astype(BF16)
    return hi, mid, lo


def _ada_kernel(c_ref, w_ref, b_ref, o_ref):
    s = _silu(c_ref[...]).astype(BF16)
    o_ref[0] = _dot(s, w_ref[0].astype(BF16)) + b_ref[0]


def _ada_mods(c_all, ada_w, ada_b):
    depth, d, n6 = ada_w.shape
    rows = c_all.shape[0]
    tn = 1536
    return pl.pallas_call(
        _ada_kernel,
        grid=(depth, n6 // tn),
        in_specs=[
            pl.BlockSpec((rows, d), lambda l, j: (0, 0)),
            pl.BlockSpec((1, d, tn), lambda l, j: (l, 0, j)),
            pl.BlockSpec((1, 1, tn), lambda l, j: (l, 0, j)),
        ],
        out_specs=pl.BlockSpec((1, rows, tn), lambda l, j: (l, 0, j)),
        out_shape=jax.ShapeDtypeStruct((depth, rows, n6), F32),
        compiler_params=_cparams(("arbitrary", "arbitrary")),
        name="ada_mods",
    )(c_all, ada_w, ada_b.reshape(depth, 1, n6))


class _Geom:
    def __init__(self, batch, seq, dec_batch, dec_seq):
        self.batch, self.seq, self.dec_batch, self.dec_seq = batch, seq, dec_batch, dec_seq
        self.n_prompt = batch * seq
        self.n_sample = dec_batch * dec_seq
        self.nt = self.n_prompt + self.n_sample
        tm = 512
        while seq % tm or self.n_sample % tm:
            tm //= 2
        assert tm >= 8
        self.tm = tm
        self.n_pt = self.n_prompt // tm
        self.n_tiles = self.nt // tm
        self.tm_moe = min(256, tm)


def _mod_specs(geo, col, tm):
    seq, batch = geo.seq, geo.batch
    n_pt = geo.n_prompt // tm

    def p_map(i, *_):
        return (jnp.minimum(i * tm // seq, batch - 1), 0, col)

    def s_map(i, *_):
        return (jnp.maximum(i - n_pt, 0), col)

    return [pl.BlockSpec((1, 1, D_MODEL), p_map), pl.BlockSpec((tm, D_MODEL), s_map)]


def _pick_mod(i, n_pt, p_ref, s_ref):
    return jnp.where(i >= n_pt, s_ref[...], p_ref[0])


def _norm_mod(x, g, sc, sh):
    ms = jnp.mean(x * x, axis=-1, keepdims=True)
    return x * lax.rsqrt(ms + NORM_EPS) * g * (1.0 + sc) + sh


def _proj_kernel(x_ref, g_ref, shp_ref, shs_ref, scp_ref, scs_ref, w_ref, o_ref, h_scr, *, n_pt):
    i = pl.program_id(0)

    @pl.when(pl.program_id(1) == 0)
    def _():
        sh = _pick_mod(i, n_pt, shp_ref, shs_ref)
        sc = _pick_mod(i, n_pt, scp_ref, scs_ref)
        h_scr[...] = _norm_mod(x_ref[...], g_ref[...], sc, sh).astype(BF16)

    o_ref[...] = _dot(h_scr[...], w_ref[...])


def _proj(geo, x, g, mod_p, mod_s, w_bf16, col_shift, col_scale):
    n_out = w_bf16.shape[1]
    tn = n_out if n_out <= 2048 else 1024
    tm = geo.tm
    return pl.pallas_call(
        functools.partial(_proj_kernel, n_pt=geo.n_pt),
        grid=(geo.n_tiles, n_out // tn),
        in_specs=[
            pl.BlockSpec((tm, D_MODEL), lambda i, j: (i, 0)),
            pl.BlockSpec((1, D_MODEL), lambda i, j: (0, 0)),
            *_mod_specs(geo, col_shift, tm),
            *_mod_specs(geo, col_scale, tm),
            pl.BlockSpec((D_MODEL, tn), lambda i, j: (0, j)),
        ],
        out_specs=pl.BlockSpec((tm, tn), lambda i, j: (i, j)),
        out_shape=jax.ShapeDtypeStruct((geo.nt, n_out), F32),
        scratch_shapes=[pltpu.VMEM((tm, D_MODEL), BF16)],
        compiler_params=_cparams(("arbitrary", "arbitrary")),
        name="norm_mod_proj",
    )(x, g, mod_p, mod_s, mod_p, mod_s, w_bf16)


def _hg_lower_bound(lbt_ref):
    t = lbt_ref[...]
    e = jnp.exp(t - jnp.max(t, axis=0, keepdims=True))
    return e[0:1] / jnp.sum(e, axis=0, keepdims=True)


def _hg_gates(fz, lb):
    e = jnp.exp(-jnp.abs(fz))
    inv = 1.0 / (1.0 + e)
    pos = fz >= 0
    sig = jnp.where(pos, inv, e * inv)
    sig_neg = jnp.where(pos, e * inv, inv)
    logf = jnp.log(lb + (1.0 - lb) * sig)
    return logf, (1.0 - lb) * sig_neg


def _hg_out(o, gz, og):
    ms = jnp.mean(o * o, axis=-1, keepdims=True)
    return o * lax.rsqrt(ms + NORM_EPS) * og * _silu(gz)


def _gla_prompt_kernel(z_ref, lbt_ref, og_ref, o_ref, sfin_ref, st_scr, *, chunk, n_chunks):
    t_step = pl.program_id(1)
    kw = HG_HEADS * HG_DK

    @pl.when(t_step == 0)
    def _():
        st_scr[...] = jnp.zeros_like(st_scr)

    lb = _hg_lower_bound(lbt_ref)
    og = og_ref[...]
    r_i = lax.broadcasted_iota(I32, (chunk, chunk), 0)
    c_i = lax.broadcasted_iota(I32, (chunk, chunk), 1)
    causal = c_i <= r_i
    tri = causal.astype(BF16)
    mid = chunk // 2 - 1

    def body(c, carry):
        rows = pl.ds(pl.multiple_of(c * chunk, chunk), chunk)
        logf, kk = _hg_gates(z_ref[rows, kw:2 * kw], lb)
        hi, md, lo = _split3(logf)
        cs = _dot(tri, jnp.concatenate([hi, md, lo], axis=1))
        b = cs[:, :kw] + cs[:, kw:2 * kw] + cs[:, 2 * kw:]
        b_mid = b[mid:mid + 1]
        b_last = b[chunk - 1:chunk]
        q_hat = _silu(z_ref[rows, 0:kw]) * jnp.exp(b - b_mid)
        k_hat = kk * jnp.exp(b_mid - b)
        q_in = (q_hat * jnp.exp(b_mid)).astype(BF16)
        k_dec = (k_hat * jnp.exp(b_last - b_mid)).astype(BF16)
        q_hat = q_hat.astype(BF16)
        k_hat = k_hat.astype(BF16)
        dec = jnp.exp(b_last)
        for h in range(HG_HEADS):
            cols = slice(h * HG_DK, (h + 1) * HG_DK)
            vcols = slice(2 * kw + h * HG_DV, 2 * kw + (h + 1) * HG_DV)
            gcols = slice(2 * kw + HG_HEADS * HG_DV + h * HG_DV, 2 * kw + HG_HEADS * HG_DV + (h + 1) * HG_DV)
            v = z_ref[rows, vcols].astype(BF16)
            att = jnp.where(causal, _dot_nt(q_hat[:, cols], k_hat[:, cols]), 0.0).astype(BF16)
            st = st_scr[h]
            o = _dot(att, v) + _dot_nt(q_in[:, cols], st.astype(BF16))
            st_scr[h] = st * dec[:, cols] + _dot_tn(v, k_dec[:, cols])
            o_ref[rows, h * HG_DV:(h + 1) * HG_DV] = _hg_out(o, z_ref[rows, gcols], og).astype(o_ref.dtype)
        return carry

    lax.fori_loop(0, n_chunks, body, 0)

    @pl.when(t_step == pl.num_programs(1) - 1)
    def _():
        sfin_ref[0] = st_scr[...]


def _gla_prompt(geo, z, lb_table, o_gain):
    tg = min(256, geo.seq)
    chunk = HG_CHUNK if geo.seq % HG_CHUNK == 0 else geo.seq
    assert tg % chunk == 0 and geo.seq % tg == 0
    nt = geo.seq // tg
    return pl.pallas_call(
        functools.partial(_gla_prompt_kernel, chunk=chunk, n_chunks=tg // chunk),
        grid=(geo.batch, nt),
        in_specs=[
            pl.BlockSpec((tg, 4 * D_MODEL), lambda b, t: (b * nt + t, 0)),
            pl.BlockSpec(lb_table.shape, lambda b, t: (0, 0)),
            pl.BlockSpec((1, HG_DV), lambda b, t: (0, 0)),
        ],
        out_specs=[
            pl.BlockSpec((tg, D_MODEL), lambda b, t: (b * nt + t, 0)),
            pl.BlockSpec((1, HG_HEADS, HG_DV, HG_DK), lambda b, t: (b, 0, 0, 0)),
        ],
        out_shape=[
            jax.ShapeDtypeStruct((geo.n_prompt, D_MODEL), BF16),
            jax.ShapeDtypeStruct((geo.batch, HG_HEADS, HG_DV, HG_DK), F32),
        ],
        scratch_shapes=[pltpu.VMEM((HG_HEADS, HG_DV, HG_DK), F32)],
        compiler_params=_cparams(("arbitrary", "arbitrary")),
        name="gla_prompt",
    )(z, lb_table, o_gain)


def _gla_step_kernel(z_ref, s_ref, lbt_ref, og_ref, o_ref, snew_ref, *, steps, bb):
    kw = HG_HEADS * HG_DK
    pad = SUBLANES
    lb = _hg_lower_bound(lbt_ref)
    og = og_ref[...]
    r_i = lax.broadcasted_iota(I32, (pad, pad), 0)
    c_i = lax.broadcasted_iota(I32, (pad, pad), 1)
    causal = c_i <= r_i
    tri = causal.astype(BF16)
    row_w = lax.broadcasted_iota(I32, (pad, kw), 0)
    live = row_w < steps
    row_k = lax.broadcasted_iota(I32, (pad, HG_DK), 0)
    ones_sel = jnp.where((row_k == steps) | (row_k == steps + 1), 1.0, 0.0).astype(BF16)

    def body(s, carry):
        z = z_ref[s]
        logf, kk = _hg_gates(z[:, kw:2 * kw], lb)
        hi, md, lo = _split3(jnp.where(live, logf, 0.0))
        cs = _dot(tri, jnp.concatenate([hi, md, lo], axis=1))
        b = cs[:, :kw] + cs[:, kw:2 * kw] + cs[:, 2 * kw:]
        b_last = b[steps - 1:steps]
        q_in = (_silu(z[:, 0:kw]) * jnp.exp(b)).astype(BF16)
        k_hat = jnp.where(live, kk * jnp.exp(-b), 0.0).astype(BF16)
        k_dec = jnp.where(live, kk * jnp.exp(b_last - b), 0.0).astype(BF16)
        dec = jnp.exp(b_last)
        d_hi = dec.astype(BF16)
        d_lo = (dec - d_hi.astype(F32)).astype(BF16)
        a_all = jnp.where(row_w == steps, d_hi, jnp.where(row_w == steps + 1, d_lo, k_dec))
        for h in range(HG_HEADS):
            cols = slice(h * HG_DK, (h + 1) * HG_DK)
            vcols = slice(2 * kw + h * HG_DV, 2 * kw + (h + 1) * HG_DV)
            gcols = slice(2 * kw + HG_HEADS * HG_DV + h * HG_DV, 2 * kw + HG_HEADS * HG_DV + (h + 1) * HG_DV)
            v = z[:, vcols].astype(BF16)
            s0 = s_ref[s, h]
            att = jnp.where(causal, _dot_nt(q_in[:, cols], k_hat[:, cols]), 0.0).astype(BF16)
            o = _dot(att, v) + _dot(q_in[:, cols], s0.astype(BF16))
            upd = _dot_tn(a_all[:, cols], jnp.concatenate([v, ones_sel], axis=1))
            snew_ref[s, h] = upd[:, HG_DV:] * s0 + upd[:, :HG_DV]
            o_ref[s, :, h * HG_DV:(h + 1) * HG_DV] = _hg_out(o, z[:, gcols], og)
        return carry

    lax.fori_loop(0, bb, body, 0)


def _seq_major(geo, rows):
    steps, db = geo.dec_seq, geo.dec_batch
    w = rows.shape[-1]
    r = rows.reshape(steps, db, w).transpose(1, 0, 2)
    return jnp.concatenate([r, jnp.zeros((db, SUBLANES - steps, w), rows.dtype)], axis=1)


def _time_major(geo, r):
    steps, db = geo.dec_seq, geo.dec_batch
    return r[:, :steps].transpose(1, 0, 2).reshape(steps * db, r.shape[-1])


def _gla_step(geo, z, state, lb_table, o_gain):
    steps, db = geo.dec_seq, geo.dec_batch
    assert steps + 2 <= SUBLANES
    bb = min(8, db)
    z_s = _seq_major(geo, z[geo.n_prompt:])
    o_s, s_new = pl.pallas_call(
        functools.partial(_gla_step_kernel, steps=steps, bb=bb),
        grid=(db // bb,),
        in_specs=[
            pl.BlockSpec((bb, SUBLANES, 4 * D_MODEL), lambda i: (i, 0, 0)),
            pl.BlockSpec((bb, HG_HEADS, HG_DK, HG_DV), lambda i: (i, 0, 0, 0)),
            pl.BlockSpec(lb_table.shape, lambda i: (0, 0)),
            pl.BlockSpec((1, HG_DV), lambda i: (0, 0)),
        ],
        out_specs=[
            pl.BlockSpec((bb, SUBLANES, D_MODEL), lambda i: (i, 0, 0)),
            pl.BlockSpec((bb, HG_HEADS, HG_DK, HG_DV), lambda i: (i, 0, 0, 0)),
        ],
        out_shape=[
            jax.ShapeDtypeStruct((db, SUBLANES, D_MODEL), F32),
            jax.ShapeDtypeStruct(state.shape, F32),
        ],
        compiler_params=_cparams(("arbitrary",)),
        name="gla_step",
    )(z_s, state, lb_table, o_gain)
    return _time_major(geo, o_s), s_new


def _post_kernel(ap_ref, as_ref, wo_ref, x_ref, gp_ref, gs_ref, n2_ref, shp_ref, shs_ref, scp_ref, scs_ref,
                 rw_ref, rb_ref, x1_ref, h2_ref, pos_ref, gt_ref, nch_ref, soff_ref, *, n_pt, tm):
    i = pl.program_id(0)
    g1 = _pick_mod(i, n_pt, gp_ref, gs_ref)
    a = jnp.where(i >= n_pt, as_ref[...], ap_ref[...])
    x1 = x_ref[...] + g1 * _dot(a, wo_ref[...])
    x1_ref[...] = x1
    sh = _pick_mod(i, n_pt, shp_ref, shs_ref)
    sc = _pick_mod(i, n_pt, scp_ref, scs_ref)
    h2 = _norm_mod(x1, n2_ref[...], sc, sh)
    h2_ref[...] = h2.astype(h2_ref.dtype)

    lane = lax.broadcasted_iota(I32, (tm, LANES), 1)
    logits = _dot(h2.astype(BF16), rw_ref[...]) + rb_ref[...]
    work = jnp.where(lane < N_EXPERTS, logits, -jnp.inf)
    vals, idxs, hits = [], [], []
    for _ in range(TOP_K):
        m = jnp.max(work, axis=-1, keepdims=True)
        idx = jnp.min(jnp.where(work == m, lane, LANES), axis=-1, keepdims=True)
        hit = lane == idx
        vals.append(m)
        idxs.append(idx)
        hits.append(hit)
        work = jnp.where(hit, -jnp.inf, work)
    exps = [jnp.exp(v - vals[0]) for v in vals]
    den = exps[0]
    for e in exps[1:]:
        den = den + e
    any_hit = hits[0]
    for hmask in hits[1:]:
        any_hit = any_hit | hmask
    any_f = jnp.where(any_hit, 1.0, 0.0)
    r_i = lax.broadcasted_iota(I32, (tm, tm), 0)
    c_i = lax.broadcasted_iota(I32, (tm, tm), 1)
    before = (c_i < r_i).astype(BF16)
    rank = _dot(before, any_f.astype(BF16))
    n_chunk = jnp.floor((jnp.sum(any_f, axis=0, keepdims=True) + (ROW_CHUNK - 1)) * (1.0 / ROW_CHUNK))
    e_i = lax.broadcasted_iota(I32, (LANES, LANES), 0)
    e_j = lax.broadcasted_iota(I32, (LANES, LANES), 1)
    earlier = (e_i < e_j).astype(BF16)
    seg = _dot(jnp.broadcast_to(n_chunk, (SUBLANES, LANES)).astype(BF16), earlier)[0:1]
    where_to = seg * float(ROW_CHUNK) + rank
    pos = jnp.zeros((tm, LANES), I32)
    gt = jnp.zeros((tm, LANES), F32)
    for k in range(TOP_K):
        p_k = jnp.sum(jnp.where(hits[k], where_to, 0.0), axis=-1, keepdims=True)
        pos = jnp.where(lane == k, p_k.astype(I32), pos)
        gt = jnp.where(lane == k, exps[k] / den, gt)
    pos_ref[...] = pos
    gt_ref[...] = gt
    nch_ref[0] = n_chunk.astype(I32)
    soff_ref[0] = seg.astype(I32)


def _post(geo, a_p, a_s, w_out_bf16, x, norm_g, mod_p, mod_s, col0, router_w, router_b):
    tm = geo.tm_moe
    n_pt = geo.n_prompt // tm
    n_tiles = geo.nt // tm
    rw = jnp.zeros((D_MODEL, LANES), BF16).at[:, :N_EXPERTS].set(router_w.astype(BF16))
    rb = jnp.zeros((1, LANES), F32).at[0, :N_EXPERTS].set(router_b)
    row = lambda i: (i, 0)
    fixed = lambda i: (0, 0)
    tile_row = lambda i: (i, 0, 0)
    return pl.pallas_call(
        functools.partial(_post_kernel, n_pt=n_pt, tm=tm),
        grid=(n_tiles,),
        in_specs=[
            pl.BlockSpec((tm, D_MODEL), lambda i: (jnp.minimum(i, n_pt - 1), 0)),
            pl.BlockSpec((tm, D_MODEL), lambda i: (jnp.maximum(i - n_pt, 0), 0)),
            pl.BlockSpec((D_MODEL, D_MODEL), fixed),
            pl.BlockSpec((tm, D_MODEL), row),
            *_mod_specs(geo, col0 + 2, tm),
            pl.BlockSpec((1, D_MODEL), fixed),
            *_mod_specs(geo, col0 + 3, tm),
            *_mod_specs(geo, col0 + 4, tm),
            pl.BlockSpec((D_MODEL, LANES), fixed),
            pl.BlockSpec((1, LANES), fixed),
        ],
        out_specs=[
            pl.BlockSpec((tm, D_MODEL), row),
            pl.BlockSpec((tm, D_MODEL), row),
            pl.BlockSpec((tm, LANES), row),
            pl.BlockSpec((tm, LANES), row),
            pl.BlockSpec((1, 1, LANES), tile_row),
            pl.BlockSpec((1, 1, LANES), tile_row),
        ],
        out_shape=[
            jax.ShapeDtypeStruct((geo.nt, D_MODEL), F32),
            jax.ShapeDtypeStruct((geo.nt, D_MODEL), BF16),
            jax.ShapeDtypeStruct((geo.nt, LANES), I32),
            jax.ShapeDtypeStruct((geo.nt, LANES), F32),
            jax.ShapeDtypeStruct((n_tiles, 1, LANES), I32),
            jax.ShapeDtypeStruct((n_tiles, 1, LANES), I32),
        ],
        compiler_params=_cparams(("arbitrary",)),
        name="post_mixer_router",
    )(a_p, a_s.astype(BF16), w_out_bf16, x, mod_p, mod_s, norm_g, mod_p, mod_s, mod_p, mod_s, rw, rb)


def _ffn_kernel(be_ref, nreal_ref, x_ref, win_ref, bin_ref, wout_ref, bout_ref, y_ref, win_scr, wout_scr):
    b = pl.program_id(0)
    n_real = nreal_ref[0]

    @pl.when(b < n_real)
    def _():
        changed = (b == 0) | (be_ref[b] != be_ref[jnp.maximum(b - 1, 0)])

        @pl.when(changed)
        def _():
            win_scr[...] = win_ref[0].astype(BF16)
            wout_scr[...] = wout_ref[0].astype(BF16)

        gu = _dot(x_ref[...].astype(BF16), win_scr[...]) + bin_ref[0]
        gate = jnp.minimum(gu[:, :D_MODEL], SWIGLU_LIMIT)
        up = jnp.clip(gu[:, D_MODEL:], -SWIGLU_LIMIT, SWIGLU_LIMIT)
        act = gate * _sigmoid(SWIGLU_ALPHA * gate) * (up + 1.0)
        y_ref[...] = _dot(act.astype(BF16), wout_scr[...]) + bout_ref[0]

    @pl.when(b >= n_real)
    def _():
        y_ref[...] = jnp.zeros_like(y_ref)


def _ffn(xs, block_e, n_real, w_in, b_in, w_out, b_out, bm):
    n_blocks = xs.shape[0] // bm
    n_e, d, d2 = w_in.shape
    return pl.pallas_call(
        _ffn_kernel,
        grid_spec=pltpu.PrefetchScalarGridSpec(
            num_scalar_prefetch=2,
            grid=(n_blocks,),
            in_specs=[
                pl.BlockSpec((bm, d), lambda b, be, nr: (jnp.minimum(b, nr[0] - 1), 0)),
                pl.BlockSpec((1, d, d2), lambda b, be, nr: (be[b], 0, 0)),
                pl.BlockSpec((1, 1, d2), lambda b, be, nr: (be[b], 0, 0)),
                pl.BlockSpec((1, d2 // 2, d), lambda b, be, nr: (be[b], 0, 0)),
                pl.BlockSpec((1, 1, d), lambda b, be, nr: (be[b], 0, 0)),
            ],
            out_specs=pl.BlockSpec((bm, d), lambda b, be, nr: (b, 0)),
            scratch_shapes=[
                pltpu.VMEM((d, d2), BF16),
                pltpu.VMEM((d2 // 2, d), BF16),
            ],
        ),
        out_shape=jax.ShapeDtypeStruct((n_blocks * bm, d), F32),
        compiler_params=_cparams(("arbitrary",)),
        name="moe_ffn",
    )(block_e, n_real, xs, w_in, b_in.reshape(n_e, 1, d2), w_out, b_out.reshape(n_e, 1, d))


def _chunk_rows(chunk):
    return pl.ds(pl.multiple_of(chunk * ROW_CHUNK, ROW_CHUNK), ROW_CHUNK)


def _for_each_chunk(nch_ref, soff_ref, dst_ref, tile, fn):
    def per_expert(e, carry):
        s0 = soff_ref[tile, e]
        d0 = dst_ref[tile, e]

        def per_chunk(c, carry2):
            fn(s0 + c, d0 + c)
            return carry2

        lax.fori_loop(0, nch_ref[tile, e], per_chunk, 0)
        return carry

    lax.fori_loop(0, N_EXPERTS, per_expert, 0)


def _dispatch_kernel(nch_ref, soff_ref, dst_ref, tot_ref, fill0_ref, filln_ref,
                     pos_ref, h_ref, xs_hbm, stage, zeros, sem, fill_sem, *, tm, k_stage):
    t = pl.program_id(0)
    n_t = pl.num_programs(0)
    slot = t % 2

    def copy(s_chunk, d_chunk, sl):
        return pltpu.make_async_copy(stage.at[sl, _chunk_rows(s_chunk)], xs_hbm.at[_chunk_rows(d_chunk)], sem.at[sl])

    def wait_tile(tile, sl):
        def one(c, carry):
            copy(0, 0, sl).wait()
            return carry
        lax.fori_loop(0, tot_ref[tile], one, 0)

    @pl.when(t >= 2)
    def _():
        wait_tile(t - 2, slot)

    pos_t = pos_ref[...].astype(F32).T.astype(I32)
    j_i = lax.broadcasted_iota(I32, (k_stage, tm), 0)
    hit = j_i == pos_t[0:1]
    for k in range(1, TOP_K):
        hit = hit | (j_i == pos_t[k:k + 1])
    stage[slot] = _dot(jnp.where(hit, 1.0, 0.0).astype(BF16), h_ref[...])
    _for_each_chunk(nch_ref, soff_ref, dst_ref, t, lambda s, d: copy(s, d, slot).start())

    @pl.when(t == n_t - 1)
    def _():
        zeros[...] = jnp.zeros_like(zeros)

        def fill(d_chunk):
            return pltpu.make_async_copy(zeros, xs_hbm.at[_chunk_rows(d_chunk)], fill_sem.at[0])

        def per_region(r, carry):
            def start_one(c, carry2):
                fill(fill0_ref[r] + c).start()
                return carry2
            lax.fori_loop(0, filln_ref[r], start_one, 0)
            return carry

        lax.fori_loop(0, N_EXPERTS + 1, per_region, 0)

        @pl.when(t >= 1)
        def _():
            wait_tile(t - 1, 1 - slot)

        wait_tile(t, slot)

        def per_region_wait(r, carry):
            def wait_one(c, carry2):
                fill(0).wait()
                return carry2
            lax.fori_loop(0, filln_ref[r], wait_one, 0)
            return carry

        lax.fori_loop(0, N_EXPERTS + 1, per_region_wait, 0)


def _combine_kernel(nch_ref, soff_ref, dst_ref, tot_ref, pos_ref, gate_ref, x_ref, gp_ref, gs_ref, y_hbm,
                    o_ref, stage, sem, *, n_pt, tm, k_stage):
    t = pl.program_id(0)
    n_t = pl.num_programs(0)
    slot = t % 2

    def copy(s_chunk, d_chunk, sl):
        return pltpu.make_async_copy(y_hbm.at[_chunk_rows(d_chunk)], stage.at[sl, _chunk_rows(s_chunk)], sem.at[sl])

    def fetch(tile, sl):
        _for_each_chunk(nch_ref, soff_ref, dst_ref, tile, lambda s, d: copy(s, d, sl).start())

    @pl.when(t == 0)
    def _():
        stage[...] = jnp.zeros_like(stage)
        fetch(0, 0)

    @pl.when(t + 1 < n_t)
    def _():
        fetch(t + 1, 1 - slot)

    def one(c, carry):
        copy(0, 0, slot).wait()
        return carry

    lax.fori_loop(0, tot_ref[t], one, 0)
    lane = lax.broadcasted_iota(I32, (tm, k_stage), 1)
    pos = pos_ref[...]
    g = gate_ref[...]
    p = jnp.zeros((tm, k_stage), F32)
    for k in range(TOP_K):
        p = jnp.where(lane == pos[:, k:k + 1], g[:, k:k + 1], p)
    ffn = _dot(p.astype(BF16), stage[slot].astype(BF16))
    o_ref[...] = x_ref[...] + _pick_mod(t, n_pt, gp_ref, gs_ref) * ffn


def _moe(geo, h2, pos, gates, nch_pad, soff_pad, x1, mod_p, mod_s, col_gate, layer, w_in, b_in, w_out, b_out):
    bm = MOE_BLOCK
    tm = geo.tm_moe
    n_tiles = geo.nt // tm
    depth, n_e, d, d2 = w_in.shape
    w_in = w_in.reshape(depth * n_e, d, d2)
    b_in = b_in.reshape(depth * n_e, d2)
    w_out = w_out.reshape(depth * n_e, d2 // 2, d)
    b_out = b_out.reshape(depth * n_e, d)
    chunks_per_block = bm // ROW_CHUNK
    max_rows = geo.nt * TOP_K + n_tiles * N_EXPERTS * (ROW_CHUNK - 1) + N_EXPERTS * (bm - 1)
    n_blocks = -(-max_rows // bm)
    nch = nch_pad[:, 0, :N_EXPERTS]
    soff = soff_pad[:, 0, :N_EXPERTS]
    tot = jnp.sum(nch, axis=1)
    per_e = jnp.sum(nch, axis=0)
    padded = (per_e + chunks_per_block - 1) // chunks_per_block * chunks_per_block
    pend = jnp.cumsum(padded)
    pstart = pend - padded
    dst = pstart[None, :] + jnp.cumsum(nch, axis=0) - nch
    block_start = jnp.arange(n_blocks, dtype=I32) * chunks_per_block
    block_e = jnp.minimum(jnp.sum(block_start[:, None] >= pend[None, :], axis=1), N_EXPERTS - 1).astype(I32)
    n_real = (pend[-1:] // chunks_per_block).astype(I32)
    fill0 = jnp.concatenate([pstart + per_e, pend[-1:]]).astype(I32)
    filln = jnp.concatenate([padded - per_e, n_blocks * chunks_per_block - pend[-1:]]).astype(I32)

    k_stage = -(-(tm * TOP_K + N_EXPERTS * (ROW_CHUNK - 1)) // LANES) * LANES
    row = lambda i, *_: (i, 0)
    xs = pl.pallas_call(
        functools.partial(_dispatch_kernel, tm=tm, k_stage=k_stage),
        grid_spec=pltpu.PrefetchScalarGridSpec(
            num_scalar_prefetch=6,
            grid=(n_tiles,),
            in_specs=[pl.BlockSpec((tm, LANES), row), pl.BlockSpec((tm, d), row)],
            out_specs=pl.BlockSpec(memory_space=pl.ANY),
            scratch_shapes=[
                pltpu.VMEM((2, k_stage, d), F32),
                pltpu.VMEM((ROW_CHUNK, d), F32),
                pltpu.SemaphoreType.DMA((2,)),
                pltpu.SemaphoreType.DMA((1,)),
            ],
        ),
        out_shape=jax.ShapeDtypeStruct((n_blocks * bm, d), F32),
        compiler_params=_cparams(("arbitrary",)),
        name="moe_dispatch",
    )(nch, soff, dst, tot, fill0, filln, pos, h2)

    ys = _ffn(xs, block_e + layer * n_e, n_real, w_in, b_in, w_out, b_out, bm)

    return pl.pallas_call(
        functools.partial(_combine_kernel, n_pt=geo.n_prompt // tm, tm=tm, k_stage=k_stage),
        grid_spec=pltpu.PrefetchScalarGridSpec(
            num_scalar_prefetch=4,
            grid=(n_tiles,),
            in_specs=[
                pl.BlockSpec((tm, LANES), row),
                pl.BlockSpec((tm, LANES), row),
                pl.BlockSpec((tm, d), row),
                *_mod_specs(geo, col_gate, tm),
                pl.BlockSpec(memory_space=pl.ANY),
            ],
            out_specs=pl.BlockSpec((tm, d), row),
            scratch_shapes=[
                pltpu.VMEM((2, k_stage, d), F32),
                pltpu.SemaphoreType.DMA((2,)),
            ],
        ),
        out_shape=jax.ShapeDtypeStruct((geo.nt, d), F32),
        compiler_params=_cparams(("arbitrary",)),
        name="moe_combine",
    )(nch, soff, dst, tot, pos, gates, x1, mod_p, mod_s, ys)


SW_QW = SW_Q_HEADS * SW_HEAD_DIM
SW_QKW = SW_QW + SW_KV_W
SW_KEYS = 2 * ATT_BLOCK


def _attn_proj_kernel(x_ref, g_ref, shp_ref, shs_ref, scp_ref, scs_ref, w_ref, e_ref, et_ref, qg_ref, kg_ref,
                      q_ref, k_ref, v_ref, *, n_pt):
    i = pl.program_id(0)
    sh = _pick_mod(i, n_pt, shp_ref, shs_ref)
    sc = _pick_mod(i, n_pt, scp_ref, scs_ref)
    h = _norm_mod(x_ref[...], g_ref[...], sc, sh).astype(BF16)
    z = _dot(h, w_ref[...])
    qk = z[:, :SW_QKW]
    sq = qk * qk
    sq_hi = sq.astype(BF16)
    sq_lo = (sq - sq_hi.astype(F32)).astype(BF16)
    ms = (_dot(sq_hi, e_ref[...]) + _dot(sq_lo, e_ref[...])) * (1.0 / SW_HEAD_DIM)
    inv = lax.rsqrt(ms + NORM_EPS)
    inv_hi = inv.astype(BF16)
    inv_lo = (inv - inv_hi.astype(F32)).astype(BF16)
    qk = qk * (_dot(inv_hi, et_ref[...]) + _dot(inv_lo, et_ref[...]))
    q_ref[...] = (qk[:, :SW_QW] * qg_ref[...] * SW_SCALE).astype(q_ref.dtype)
    k_ref[...] = qk[:, SW_QW:] * kg_ref[...]
    v_ref[...] = z[:, SW_QKW:]


def _attn_proj(geo, x, g, mod_p, mod_s, w_bf16, q_gain, k_gain):
    tm = geo.tm
    n_out = w_bf16.shape[1]
    heads = SW_QKW // SW_HEAD_DIM
    member = (np.arange(SW_QKW)[:, None] // SW_HEAD_DIM == np.arange(LANES)[None, :]).astype(np.float32)
    e = jnp.asarray(member, BF16)
    et = jnp.asarray(member.T, BF16)
    assert heads <= LANES
    qg = jnp.tile(q_gain, SW_Q_HEADS).reshape(1, SW_QW)
    kg = jnp.tile(k_gain, SW_KV_HEADS).reshape(1, SW_KV_W)
    row = lambda i: (i, 0)
    fixed = lambda i: (0, 0)
    return pl.pallas_call(
        functools.partial(_attn_proj_kernel, n_pt=geo.n_pt),
        grid=(geo.n_tiles,),
        in_specs=[
            pl.BlockSpec((tm, D_MODEL), row),
            pl.BlockSpec((1, D_MODEL), fixed),
            *_mod_specs(geo, 0, tm),
            *_mod_specs(geo, 1, tm),
            pl.BlockSpec((D_MODEL, n_out), fixed),
            pl.BlockSpec((SW_QKW, LANES), fixed),
            pl.BlockSpec((LANES, SW_QKW), fixed),
            pl.BlockSpec((1, SW_QW), fixed),
            pl.BlockSpec((1, SW_KV_W), fixed),
        ],
        out_specs=[
            pl.BlockSpec((tm, SW_QW), row),
            pl.BlockSpec((tm, SW_KV_W), row),
            pl.BlockSpec((tm, SW_KV_W), row),
        ],
        out_shape=[
            jax.ShapeDtypeStruct((geo.nt, SW_QW), BF16),
            jax.ShapeDtypeStruct((geo.nt, SW_KV_W), F32),
            jax.ShapeDtypeStruct((geo.nt, SW_KV_W), F32),
        ],
        compiler_params=_cparams(("arbitrary",)),
        name="attn_proj_qknorm",
    )(x, g, mod_p, mod_s, mod_p, mod_s, w_bf16, e, et, qg, kg)


def _rel_bucket_np(dist):
    n = np.maximum(dist, 0)
    max_exact = REL_BUCKETS // 2
    ratio = np.log(np.maximum(n, 1).astype(np.float32) / np.float32(max_exact)) / np.float32(
        math.log(REL_MAX_DIST / max_exact))
    large = max_exact + (ratio * np.float32(REL_BUCKETS - max_exact)).astype(np.int32)
    large = np.minimum(large, REL_BUCKETS - 1)
    return np.where(n < max_exact, n, large).astype(np.int32)


def _bucket_table(qpos, kpos, k_ok):
    dist = qpos[:, None] - kpos[None, :]
    ok = (dist >= 0) & (dist <= WINDOW) & k_ok[None, :]
    return np.where(ok, _rel_bucket_np(dist), -1).astype(np.int32)


def _build_bias(bkt_ref, rb_ref, bias_scr, rows):
    bkt = bkt_ref[...]
    base = jnp.where(bkt < 0, -jnp.inf, 0.0)
    for h in range(SW_Q_HEADS):
        bias_scr[h * rows:(h + 1) * rows, :] = base

    def add_bucket(j, carry):
        hit = bkt == j
        for h in range(SW_Q_HEADS):
            sl = slice(h * rows, (h + 1) * rows)
            bias_scr[sl, :] = bias_scr[sl, :] + jnp.where(hit, rb_ref[j, h], 0.0)
        return carry

    lax.fori_loop(0, REL_BUCKETS, add_bucket, 0)


def _sink_softmax_pv(s, sink, vv_g):
    m = jnp.maximum(jnp.max(s, axis=-1, keepdims=True), sink)
    e = jnp.exp(s - m)
    p = e / (jnp.sum(e, axis=-1, keepdims=True) + jnp.exp(sink - m))
    return _dot(p.astype(BF16), vv_g)


def _build_bias_t(bkt_ref, rb_ref, bias_scr, blk):
    bkt = bkt_ref[...]
    base = jnp.where(bkt < 0, -jnp.inf, 0.0)
    slots = [(g, j) for g in range(SW_KV_HEADS) for j in range(SW_GROUP)]
    for g, j in slots:
        bias_scr[g, :, j * blk:(j + 1) * blk] = base

    def add_bucket(b, carry):
        hit = bkt == b
        for g, j in slots:
            cols = slice(j * blk, (j + 1) * blk)
            bias_scr[g, :, cols] = bias_scr[g, :, cols] + jnp.where(hit, rb_ref[b, g * SW_GROUP + j], 0.0)
        return carry

    lax.fori_loop(0, REL_BUCKETS, add_bucket, 0)


def _swa_prompt_kernel(rb_ref, sink_ref, bkt_ref, q_ref, kp_ref, kc_ref, vp_ref, vc_ref, o_ref, bias_scr):
    n = pl.program_id(1)
    blk = ATT_BLOCK
    hd = SW_HEAD_DIM

    @pl.when((pl.program_id(0) == 0) & (n == 0))
    def _():
        _build_bias_t(bkt_ref, rb_ref, bias_scr, blk)

    kk = jnp.concatenate([kp_ref[...], kc_ref[...]], axis=0).astype(BF16)
    vv_t = jnp.concatenate([vp_ref[...], vc_ref[...]], axis=0).T.astype(BF16)
    q_t = q_ref[...].astype(F32).T.astype(BF16)
    key_i = lax.broadcasted_iota(I32, (SW_KEYS, SW_GROUP * blk), 0)
    hide_prev = (n == 0) & (key_i < blk)
    zeros_q = jnp.zeros((hd, SW_GROUP * blk), BF16)
    outs = []
    for g in range(SW_KV_HEADS):
        heads = [g * SW_GROUP + j for j in range(SW_GROUP)]
        q_cat = jnp.concatenate([q_t[hq * hd:(hq + 1) * hd, :] for hq in heads], axis=1)
        q_full = jnp.concatenate([q_cat if gg == g else zeros_q for gg in range(SW_KV_HEADS)], axis=0)
        s = _dot(kk, q_full) + bias_scr[g]
        s = jnp.where(hide_prev, -jnp.inf, s)
        sink = jnp.concatenate([jnp.full((1, blk), sink_ref[hq], F32) for hq in heads], axis=1)
        m = jnp.maximum(jnp.max(s, axis=0, keepdims=True), sink)
        e = jnp.exp(s - m)
        den = jnp.sum(e, axis=0, keepdims=True) + jnp.exp(sink - m)
        o_t = _dot(vv_t[g * hd:(g + 1) * hd, :], e.astype(BF16)) * (1.0 / den)
        outs.extend(o_t[:, j * blk:(j + 1) * blk] for j in range(SW_GROUP))
    o_ref[...] = jnp.concatenate(outs, axis=0).T.astype(o_ref.dtype)


def _swa_prompt(geo, q, k, v, rel_bias, sinks):
    blk = ATT_BLOCK
    assert geo.seq % blk == 0 and WINDOW == blk
    nb = geo.seq // blk
    bkt = jnp.asarray(_bucket_table(blk + np.arange(blk), np.arange(2 * blk), np.ones(2 * blk, bool)).T.copy())
    cur = lambda b, n: (b * nb + n, 0)
    prev = lambda b, n: (b * nb + jnp.maximum(n - 1, 0), 0)
    smem = pl.BlockSpec(memory_space=pltpu.SMEM)
    return pl.pallas_call(
        _swa_prompt_kernel,
        grid=(geo.batch, nb),
        in_specs=[
            smem, smem,
            pl.BlockSpec((SW_KEYS, blk), lambda b, n: (0, 0)),
            pl.BlockSpec((blk, SW_QW), cur),
            pl.BlockSpec((blk, SW_KV_W), prev),
            pl.BlockSpec((blk, SW_KV_W), cur),
            pl.BlockSpec((blk, SW_KV_W), prev),
            pl.BlockSpec((blk, SW_KV_W), cur),
        ],
        out_specs=pl.BlockSpec((blk, SW_QW), cur),
        out_shape=jax.ShapeDtypeStruct((geo.n_prompt, SW_QW), BF16),
        scratch_shapes=[pltpu.VMEM((SW_KV_HEADS, SW_KEYS, SW_GROUP * blk), F32)],
        compiler_params=_cparams(("arbitrary", "arbitrary")),
        name="swa_prompt",
    )(rel_bias, sinks, bkt, q, k, k, v, v)


def _swa_step_kernel(rb_ref, sink_ref, bkt_ref, q_ref, kn_ref, vn_ref, ck_ref, cv_ref,
                     o_ref, nk_ref, nv_ref, bias_scr, *, steps, bb):
    pad = SUBLANES
    win = WINDOW

    @pl.when(pl.program_id(0) == 0)
    def _():
        _build_bias(bkt_ref, rb_ref, bias_scr, pad)

    fill = jnp.zeros((SW_KEYS - win - pad, SW_KV_W), F32)
    row8 = lax.broadcasted_iota(I32, (pad, SW_KV_W), 0)

    def shifted(cache, new):
        rolled = pltpu.roll(cache, win - steps, 0)
        tail = jnp.where(row8 < pad - steps, rolled[win - pad:], pltpu.roll(new, pad - steps, 0))
        return rolled[:win - pad], tail

    def body(s, carry):
        ck, cv, kn, vn = ck_ref[s], cv_ref[s], kn_ref[s], vn_ref[s]
        kk = jnp.concatenate([ck, kn, fill], axis=0).astype(BF16)
        vv = jnp.concatenate([cv, vn, fill], axis=0).astype(BF16)
        q = q_ref[s]
        for g in range(SW_KV_HEADS):
            kcols = slice(g * SW_HEAD_DIM, (g + 1) * SW_HEAD_DIM)
            heads = [g * SW_GROUP + j for j in range(SW_GROUP)]
            qs = jnp.concatenate([q[:, hq * SW_HEAD_DIM:(hq + 1) * SW_HEAD_DIM] for hq in heads], axis=0)
            sc = _dot_nt(qs.astype(BF16), kk[:, kcols]) + bias_scr[heads[0] * pad:(heads[-1] + 1) * pad, :]
            sink = jnp.concatenate([jnp.full((pad, 1), sink_ref[hq], F32) for hq in heads], axis=0)
            og = _sink_softmax_pv(sc, sink, vv[:, kcols])
            for j, hq in enumerate(heads):
                o_ref[s, :, hq * SW_HEAD_DIM:(hq + 1) * SW_HEAD_DIM] = og[j * pad:(j + 1) * pad]
        head, tail = shifted(ck, kn)
        nk_ref[s, 0:win - pad] = head
        nk_ref[s, win - pad:win] = tail
        head, tail = shifted(cv, vn)
        nv_ref[s, 0:win - pad] = head
        nv_ref[s, win - pad:win] = tail
        return carry

    lax.fori_loop(0, bb, body, 0)


def _swa_step(geo, q, k, v, cache_k, cache_v, rel_bias, sinks):
    steps, db = geo.dec_seq, geo.dec_batch
    win = cache_k.shape[1]
    assert win == WINDOW and steps <= SUBLANES
    bb = min(8, db)
    pad = SUBLANES
    q_s = _seq_major(geo, q[geo.n_prompt:].astype(F32))
    k_s = _seq_major(geo, k[geo.n_prompt:])
    v_s = _seq_major(geo, v[geo.n_prompt:])
    kpos = np.arange(SW_KEYS)
    k_ok = kpos < win + steps
    qpos = win + np.arange(pad)
    bkt = _bucket_table(qpos, kpos, k_ok)
    bkt[steps:] = -1
    smem = pl.BlockSpec(memory_space=pltpu.SMEM)
    blk3 = lambda r, w: pl.BlockSpec((bb, r, w), lambda i: (i, 0, 0))
    o_s, nk, nv = pl.pallas_call(
        functools.partial(_swa_step_kernel, steps=steps, bb=bb),
        grid=(db // bb,),
        in_specs=[
            smem, smem,
            pl.BlockSpec((pad, SW_KEYS), lambda i: (0, 0)),
            blk3(pad, SW_QW), blk3(pad, SW_KV_W), blk3(pad, SW_KV_W),
            blk3(win, SW_KV_W), blk3(win, SW_KV_W),
        ],
        out_specs=[blk3(pad, SW_QW), blk3(win, SW_KV_W), blk3(win, SW_KV_W)],
        out_shape=[
            jax.ShapeDtypeStruct((db, pad, SW_QW), F32),
            jax.ShapeDtypeStruct((db, win, SW_KV_W), F32),
            jax.ShapeDtypeStruct((db, win, SW_KV_W), F32),
        ],
        scratch_shapes=[pltpu.VMEM((SW_Q_HEADS * pad, SW_KEYS), F32)],
        compiler_params=_cparams(("arbitrary",)),
        name="swa_step",
    )(rel_bias, sinks, jnp.asarray(bkt), q_s, k_s, v_s, cache_k, cache_v)
    return _time_major(geo, o_s), nk, nv


def kernel(x_prompt, x_sample, state_hgrn, cache_win_k, cache_win_v, c_prompt, c_sample, norm1_g, norm2_g, ada_w, ada_b, hg_w_in, hg_lb_table, hg_onorm_g, hg_w_out, sw_w_in, sw_qnorm_g, sw_knorm_g, sw_sinks, sw_w_out, rel_bias, router_w, router_b, moe_w_in, moe_b_in, moe_w_out, moe_b_out):
    batch, seq, d = x_prompt.shape
    db, steps, _ = x_sample.shape
    assert d == D_MODEL and ada_w.shape[0] == 2 and hg_w_in.shape[0] == 1 and sw_w_in.shape[0] == 1
    geo = _Geom(batch, seq, db, steps)
    x = jnp.concatenate([x_prompt.reshape(batch * seq, d),
                         x_sample.transpose(1, 0, 2).reshape(steps * db, d)], axis=0)
    n_seq = batch + db
    rows = -(-n_seq // SUBLANES) * SUBLANES
    c_all = jnp.concatenate([c_prompt, c_sample, jnp.zeros((rows - n_seq, d), F32)], axis=0)
    mods = _ada_mods(c_all, ada_w, ada_b)

    def layer_mods(layer):
        mod_p = mods[layer, :batch].reshape(batch, 1, 6 * d)
        mod_s = jnp.tile(mods[layer, batch:n_seq], (steps, 1))
        return mod_p, mod_s

    def moe(layer, a_p, a_s, w_out, x_in, mod_p, mod_s):
        x1, h2, pos, gt, nch, soff = _post(geo, a_p, a_s, w_out.astype(BF16), x_in, norm2_g[layer:layer + 1],
                                           mod_p, mod_s, 0, router_w[layer], router_b[layer])
        return _moe(geo, h2, pos, gt, nch, soff, x1, mod_p, mod_s, 5,
                    layer, moe_w_in, moe_b_in, moe_w_out, moe_b_out)

    mod_p, mod_s = layer_mods(0)
    og = hg_onorm_g[0:1]
    z = _proj(geo, x, norm1_g[0:1], mod_p, mod_s, hg_w_in[0].astype(BF16), 0, 1)
    o_p, st_p = _gla_prompt(geo, z, hg_lb_table, og)
    o_s, st_s = _gla_step(geo, z, state_hgrn[0], hg_lb_table, og)
    x = moe(0, o_p, o_s, hg_w_out[0], x, mod_p, mod_s)

    mod_p, mod_s = layer_mods(1)
    q, k, v = _attn_proj(geo, x, norm1_g[1:2], mod_p, mod_s, sw_w_in[0].astype(BF16), sw_qnorm_g[0], sw_knorm_g[0])
    win = cache_win_k.shape[2]
    a_p = _swa_prompt(geo, q, k, v, rel_bias, sw_sinks[0])
    a_s, nk, nv = _swa_step(geo, q, k, v, cache_win_k[0].reshape(db, win, SW_KV_W),
                            cache_win_v[0].reshape(db, win, SW_KV_W), rel_bias, sw_sinks[0])
    x = moe(1, a_p, a_s, sw_w_out[0], x, mod_p, mod_s)

    y_prompt = x[:geo.n_prompt].reshape(batch, seq, d)
    y_sample = x[geo.n_prompt:].reshape(steps, db, d).transpose(1, 0, 2)
    kv_shape = (1, batch, WINDOW, SW_KV_HEADS, SW_HEAD_DIM)
    k_p = k[:geo.n_prompt].reshape(batch, seq, SW_KV_W)[:, seq - WINDOW:].reshape(kv_shape)
    v_p = v[:geo.n_prompt].reshape(batch, seq, SW_KV_W)[:, seq - WINDOW:].reshape(kv_shape)
    cache_shape = (1, db, win, SW_KV_HEADS, SW_HEAD_DIM)
    return (y_prompt, y_sample, jnp.swapaxes(st_p, -1, -2)[None], st_s[None], k_p, v_p,
            nk.reshape(cache_shape), nv.reshape(cache_shape))
```

```python
import functools
import math

import numpy as np
import jax
import jax.numpy as jnp
from jax import lax
from jax.experimental import pallas as pl
from jax.experimental.pallas import tpu as pltpu

F32 = jnp.float32
BF16 = jnp.bfloat16
I32 = jnp.int32

D_MODEL = 1024
LANES = 128
SUBLANES = 8
D_TILES = D_MODEL // LANES
HG_DK = 128
HG_HEADS = D_MODEL // HG_DK
HG_DV = D_MODEL // HG_HEADS
HG_CHUNK = 64
SW_HEAD_DIM = 64
SW_Q_HEADS = D_MODEL // SW_HEAD_DIM
SW_KV_HEADS = 4
SW_GROUP = SW_Q_HEADS // SW_KV_HEADS
SW_KV_W = SW_KV_HEADS * SW_HEAD_DIM
WINDOW = 128
ATT_BLOCK = 128
SW_SCALE = SW_HEAD_DIM ** -0.5
REL_BUCKETS = 32
REL_MAX_DIST = 128
N_EXPERTS = 32
TOP_K = 4
SWIGLU_LIMIT = 7.0
SWIGLU_ALPHA = 1.702
NORM_EPS = 1e-5
MOE_BLOCK = 512
ROW_CHUNK = SUBLANES
COPY_CHUNKS = 4
WAIT_CHUNKS = 16
VMEM_LIMIT = 56 * 1024 * 1024


def _cparams(sem):
    return pltpu.CompilerParams(dimension_semantics=sem, vmem_limit_bytes=VMEM_LIMIT)


def _sigmoid(x):
    return 1.0 / (1.0 + jnp.exp(-x))


def _silu(x):
    return x * _sigmoid(x)


def _dot(a, b):
    return jnp.dot(a, b, preferred_element_type=F32)


def _dot_nt(a, b):
    return lax.dot_general(a, b, (((1,), (1,)), ((), ())), preferred_element_type=F32)


def _dot_tn(a, b):
    return lax.dot_general(a, b, (((0,), (0,)), ((), ())), preferred_element_type=F32)


def _split3(x):
    hi = x.astype(BF16)
    r = x - hi.astype(F32)
    mid = r.astype(BF16)
    lo = (r - mid.astype(F32)).astype(BF16)
    return hi, mid, lo


def _ada_kernel(c_ref, w_ref, b_ref, o_ref):
    s = _silu(c_ref[...]).astype(BF16)
    o_ref[0] = _dot(s, w_ref[0].astype(BF16)) + b_ref[0]


def _ada_mods(c_all, ada_w, ada_b):
    depth, d, n6 = ada_w.shape
    rows = c_all.shape[0]
    tn = 1536
    return pl.pallas_call(
        _ada_kernel,
        grid=(depth, n6 // tn),
        in_specs=[
            pl.BlockSpec((rows, d), lambda l, j: (0, 0)),
            pl.BlockSpec((1, d, tn), lambda l, j: (l, 0, j)),
            pl.BlockSpec((1, 1, tn), lambda l, j: (l, 0, j)),
        ],
        out_specs=pl.BlockSpec((1, rows, tn), lambda l, j: (l, 0, j)),
        out_shape=jax.ShapeDtypeStruct((depth, rows, n6), F32),
        compiler_params=_cparams(("arbitrary", "arbitrary")),
        name="ada_mods",
    )(c_all, ada_w, ada_b.reshape(depth, 1, n6))


class _Geom:
    def __init__(self, batch, seq, dec_batch, dec_seq):
        self.batch, self.seq, self.dec_batch, self.dec_seq = batch, seq, dec_batch, dec_seq
        self.n_prompt = batch * seq
        self.n_sample = dec_batch * dec_seq
        self.nt = self.n_prompt + self.n_sample
        tm = 512
        while seq % tm or self.n_sample % tm:
            tm //= 2
        assert tm >= 8
        self.tm = tm
        self.n_pt = self.n_prompt // tm
        self.n_tiles = self.nt // tm
        self.tm_moe = min(256, tm)


def _mod_specs(geo, col, tm):
    seq, batch = geo.seq, geo.batch
    n_pt = geo.n_prompt // tm

    def p_map(i, *_):
        return (jnp.minimum(i * tm // seq, batch - 1), 0, col)

    def s_map(i, *_):
        return (jnp.maximum(i - n_pt, 0), col)

    return [pl.BlockSpec((1, 1, D_MODEL), p_map), pl.BlockSpec((tm, D_MODEL), s_map)]


def _pick_mod(i, n_pt, p_ref, s_ref):
    return jnp.where(i >= n_pt, s_ref[...], p_ref[0])


def _norm_mod(x, g, sc, sh):
    ms = jnp.mean(x * x, axis=-1, keepdims=True)
    return x * lax.rsqrt(ms + NORM_EPS) * g * (1.0 + sc) + sh


def _proj_kernel(x_ref, g_ref, shp_ref, shs_ref, scp_ref, scs_ref, w_ref, o_ref, h_scr, *, n_pt):
    i = pl.program_id(0)

    @pl.when(pl.program_id(1) == 0)
    def _():
        sh = _pick_mod(i, n_pt, shp_ref, shs_ref)
        sc = _pick_mod(i, n_pt, scp_ref, scs_ref)
        h_scr[...] = _norm_mod(x_ref[...], g_ref[...], sc, sh).astype(BF16)

    o_ref[...] = _dot(h_scr[...], w_ref[...])


def _proj(geo, x, g, mod_p, mod_s, w_bf16, col_shift, col_scale):
    n_out = w_bf16.shape[1]
    tn = n_out if n_out <= 2048 else 1024
    tm = geo.tm
    return pl.pallas_call(
        functools.partial(_proj_kernel, n_pt=geo.n_pt),
        grid=(geo.n_tiles, n_out // tn),
        in_specs=[
            pl.BlockSpec((tm, D_MODEL), lambda i, j: (i, 0)),
            pl.BlockSpec((1, D_MODEL), lambda i, j: (0, 0)),
            *_mod_specs(geo, col_shift, tm),
            *_mod_specs(geo, col_scale, tm),
            pl.BlockSpec((D_MODEL, tn), lambda i, j: (0, j)),
        ],
        out_specs=pl.BlockSpec((tm, tn), lambda i, j: (i, j)),
        out_shape=jax.ShapeDtypeStruct((geo.nt, n_out), F32),
        scratch_shapes=[pltpu.VMEM((tm, D_MODEL), BF16)],
        compiler_params=_cparams(("arbitrary", "arbitrary")),
        name="norm_mod_proj",
    )(x, g, mod_p, mod_s, mod_p, mod_s, w_bf16)


def _hg_lower_bound(lbt_ref):
    t = lbt_ref[...]
    e = jnp.exp(t - jnp.max(t, axis=0, keepdims=True))
    return e[0:1] / jnp.sum(e, axis=0, keepdims=True)


def _hg_gates(fz, lb):
    e = jnp.exp(-jnp.abs(fz))
    inv = 1.0 / (1.0 + e)
    pos = fz >= 0
    sig = jnp.where(pos, inv, e * inv)
    sig_neg = jnp.where(pos, e * inv, inv)
    logf = jnp.log(lb + (1.0 - lb) * sig)
    return logf, (1.0 - lb) * sig_neg


def _hg_out(o, gz, og):
    ms = jnp.mean(o * o, axis=-1, keepdims=True)
    return o * lax.rsqrt(ms + NORM_EPS) * og * _silu(gz)


def _gla_prompt_kernel(z_ref, lbt_ref, og_ref, o_ref, sfin_ref, st_scr, *, chunk, n_chunks):
    t_step = pl.program_id(1)
    kw = HG_HEADS * HG_DK

    @pl.when(t_step == 0)
    def _():
        st_scr[...] = jnp.zeros_like(st_scr)

    lb = _hg_lower_bound(lbt_ref)
    og = og_ref[...]
    r_i = lax.broadcasted_iota(I32, (chunk, chunk), 0)
    c_i = lax.broadcasted_iota(I32, (chunk, chunk), 1)
    causal = c_i <= r_i
    tri = causal.astype(BF16)
    mid = chunk // 2 - 1

    def body(c, carry):
        rows = pl.ds(pl.multiple_of(c * chunk, chunk), chunk)
        logf, kk = _hg_gates(z_ref[rows, kw:2 * kw], lb)
        hi, md, lo = _split3(logf)
        cs = _dot(tri, jnp.concatenate([hi, md, lo], axis=1))
        b = cs[:, :kw] + cs[:, kw:2 * kw] + cs[:, 2 * kw:]
        b_mid = b[mid:mid + 1]
        b_last = b[chunk - 1:chunk]
        q_hat = _silu(z_ref[rows, 0:kw]) * jnp.exp(b - b_mid)
        k_hat = kk * jnp.exp(b_mid - b)
        q_in = (q_hat * jnp.exp(b_mid)).astype(BF16)
        k_dec = (k_hat * jnp.exp(b_last - b_mid)).astype(BF16)
        q_hat = q_hat.astype(BF16)
        k_hat = k_hat.astype(BF16)
        dec = jnp.exp(b_last)
        for h in range(HG_HEADS):
            cols = slice(h * HG_DK, (h + 1) * HG_DK)
            vcols = slice(2 * kw + h * HG_DV, 2 * kw + (h + 1) * HG_DV)
            gcols = slice(2 * kw + HG_HEADS * HG_DV + h * HG_DV, 2 * kw + HG_HEADS * HG_DV + (h + 1) * HG_DV)
            v = z_ref[rows, vcols].astype(BF16)
            att = jnp.where(causal, _dot_nt(q_hat[:, cols], k_hat[:, cols]), 0.0).astype(BF16)
            st = st_scr[h]
            o = _dot(att, v) + _dot_nt(q_in[:, cols], st.astype(BF16))
            st_scr[h] = st * dec[:, cols] + _dot_tn(v, k_dec[:, cols])
            o_ref[rows, h * HG_DV:(h + 1) * HG_DV] = _hg_out(o, z_ref[rows, gcols], og).astype(o_ref.dtype)
        return carry

    lax.fori_loop(0, n_chunks, body, 0)

    @pl.when(t_step == pl.num_programs(1) - 1)
    def _():
        sfin_ref[0] = st_scr[...]


def _gla_prompt(geo, z, lb_table, o_gain):
    tg = min(256, geo.seq)
    chunk = HG_CHUNK if geo.seq % HG_CHUNK == 0 else geo.seq
    assert tg % chunk == 0 and geo.seq % tg == 0
    nt = geo.seq // tg
    return pl.pallas_call(
        functools.partial(_gla_prompt_kernel, chunk=chunk, n_chunks=tg // chunk),
        grid=(geo.batch, nt),
        in_specs=[
            pl.BlockSpec((tg, 4 * D_MODEL), lambda b, t: (b * nt + t, 0)),
            pl.BlockSpec(lb_table.shape, lambda b, t: (0, 0)),
            pl.BlockSpec((1, HG_DV), lambda b, t: (0, 0)),
        ],
        out_specs=[
            pl.BlockSpec((tg, D_MODEL), lambda b, t: (b * nt + t, 0)),
            pl.BlockSpec((1, HG_HEADS, HG_DV, HG_DK), lambda b, t: (b, 0, 0, 0)),
        ],
        out_shape=[
            jax.ShapeDtypeStruct((geo.n_prompt, D_MODEL), BF16),
            jax.ShapeDtypeStruct((geo.batch, HG_HEADS, HG_DV, HG_DK), F32),
        ],
        scratch_shapes=[pltpu.VMEM((HG_HEADS, HG_DV, HG_DK), F32)],
        compiler_params=_cparams(("arbitrary", "arbitrary")),
        name="gla_prompt",
    )(z, lb_table, o_gain)


def _gla_step_kernel(z_ref, s_ref, lbt_ref, og_ref, o_ref, snew_ref, *, steps, bb):
    kw = HG_HEADS * HG_DK
    pad = SUBLANES
    lb = _hg_lower_bound(lbt_ref)
    og = og_ref[...]
    r_i = lax.broadcasted_iota(I32, (pad, pad), 0)
    c_i = lax.broadcasted_iota(I32, (pad, pad), 1)
    causal = c_i <= r_i
    tri = causal.astype(BF16)
    row_w = lax.broadcasted_iota(I32, (pad, kw), 0)
    live = row_w < steps
    row_k = lax.broadcasted_iota(I32, (pad, HG_DK), 0)
    ones_sel = jnp.where((row_k == steps) | (row_k == steps + 1), 1.0, 0.0).astype(BF16)

    def body(s, carry):
        z = z_ref[s]
        logf, kk = _hg_gates(z[:, kw:2 * kw], lb)
        hi, md, lo = _split3(jnp.where(live, logf, 0.0))
        cs = _dot(tri, jnp.concatenate([hi, md, lo], axis=1))
        b = cs[:, :kw] + cs[:, kw:2 * kw] + cs[:, 2 * kw:]
        b_last = b[steps - 1:steps]
        q_in = (_silu(z[:, 0:kw]) * jnp.exp(b)).astype(BF16)
        k_hat = jnp.where(live, kk * jnp.exp(-b), 0.0).astype(BF16)
        k_dec = jnp.where(live, kk * jnp.exp(b_last - b), 0.0).astype(BF16)
        dec = jnp.exp(b_last)
        d_hi = dec.astype(BF16)
        d_lo = (dec - d_hi.astype(F32)).astype(BF16)
        a_all = jnp.where(row_w == steps, d_hi, jnp.where(row_w == steps + 1, d_lo, k_dec))
        for h in range(HG_HEADS):
            cols = slice(h * HG_DK, (h + 1) * HG_DK)
            vcols = slice(2 * kw + h * HG_DV, 2 * kw + (h + 1) * HG_DV)
            gcols = slice(2 * kw + HG_HEADS * HG_DV + h * HG_DV, 2 * kw + HG_HEADS * HG_DV + (h + 1) * HG_DV)
            v = z[:, vcols].astype(BF16)
            s0 = s_ref[s, h]
            att = jnp.where(causal, _dot_nt(q_in[:, cols], k_hat[:, cols]), 0.0).astype(BF16)
            o = _dot(att, v) + _dot(q_in[:, cols], s0.astype(BF16))
            upd = _dot_tn(a_all[:, cols], jnp.concatenate([v, ones_sel], axis=1))
            snew_ref[s, h] = upd[:, HG_DV:] * s0 + upd[:, :HG_DV]
            o_ref[s, :, h * HG_DV:(h + 1) * HG_DV] = _hg_out(o, z[:, gcols], og)
        return carry

    lax.fori_loop(0, bb, body, 0)


def _seq_major(geo, rows):
    steps, db = geo.dec_seq, geo.dec_batch
    w = rows.shape[-1]
    r = rows.reshape(steps, db, w).transpose(1, 0, 2)
    return jnp.concatenate([r, jnp.zeros((db, SUBLANES - steps, w), rows.dtype)], axis=1)


def _time_major(geo, r):
    steps, db = geo.dec_seq, geo.dec_batch
    return r[:, :steps].transpose(1, 0, 2).reshape(steps * db, r.shape[-1])


def _gla_step(geo, z, state, lb_table, o_gain):
    steps, db = geo.dec_seq, geo.dec_batch
    assert steps + 2 <= SUBLANES
    bb = min(8, db)
    z_s = _seq_major(geo, z[geo.n_prompt:])
    o_s, s_new = pl.pallas_call(
        functools.partial(_gla_step_kernel, steps=steps, bb=bb),
        grid=(db // bb,),
        in_specs=[
            pl.BlockSpec((bb, SUBLANES, 4 * D_MODEL), lambda i: (i, 0, 0)),
            pl.BlockSpec((bb, HG_HEADS, HG_DK, HG_DV), lambda i: (i, 0, 0, 0)),
            pl.BlockSpec(lb_table.shape, lambda i: (0, 0)),
            pl.BlockSpec((1, HG_DV), lambda i: (0, 0)),
        ],
        out_specs=[
            pl.BlockSpec((bb, SUBLANES, D_MODEL), lambda i: (i, 0, 0)),
            pl.BlockSpec((bb, HG_HEADS, HG_DK, HG_DV), lambda i: (i, 0, 0, 0)),
        ],
        out_shape=[
            jax.ShapeDtypeStruct((db, SUBLANES, D_MODEL), F32),
            jax.ShapeDtypeStruct(state.shape, F32),
        ],
        compiler_params=_cparams(("arbitrary",)),
        name="gla_step",
    )(z_s, state, lb_table, o_gain)
    return _time_major(geo, o_s), s_new


def _post_kernel(ap_ref, as_ref, wo_ref, x_ref, gp_ref, gs_ref, n2_ref, shp_ref, shs_ref, scp_ref, scs_ref,
                 rw_ref, rb_ref, x1_ref, h2_ref, pos_ref, gt_ref, nch_ref, soff_ref, *, n_pt, tm):
    i = pl.program_id(0)
    g1 = _pick_mod(i, n_pt, gp_ref, gs_ref)
    a = jnp.where(i >= n_pt, as_ref[...], ap_ref[...])
    x1 = x_ref[...] + g1 * _dot(a, wo_ref[...])
    x1_ref[...] = x1
    sh = _pick_mod(i, n_pt, shp_ref, shs_ref)
    sc = _pick_mod(i, n_pt, scp_ref, scs_ref)
    h2 = _norm_mod(x1, n2_ref[...], sc, sh)
    h2_ref[...] = h2.astype(h2_ref.dtype)

    lane = lax.broadcasted_iota(I32, (tm, LANES), 1)
    logits = _dot(h2.astype(BF16), rw_ref[...]) + rb_ref[...]
    work = jnp.where(lane < N_EXPERTS, logits, -jnp.inf)
    vals, idxs, hits = [], [], []
    for _ in range(TOP_K):
        m = jnp.max(work, axis=-1, keepdims=True)
        idx = jnp.min(jnp.where(work == m, lane, LANES), axis=-1, keepdims=True)
        hit = lane == idx
        vals.append(m)
        idxs.append(idx)
        hits.append(hit)
        work = jnp.where(hit, -jnp.inf, work)
    exps = [jnp.exp(v - vals[0]) for v in vals]
    den = exps[0]
    for e in exps[1:]:
        den = den + e
    any_hit = hits[0]
    for hmask in hits[1:]:
        any_hit = any_hit | hmask
    any_f = jnp.where(any_hit, 1.0, 0.0)
    r_i = lax.broadcasted_iota(I32, (tm, tm), 0)
    c_i = lax.broadcasted_iota(I32, (tm, tm), 1)
    before = (c_i < r_i).astype(BF16)
    rank = _dot(before, any_f.astype(BF16))
    n_chunk = jnp.floor((jnp.sum(any_f, axis=0, keepdims=True) + (ROW_CHUNK - 1)) * (1.0 / ROW_CHUNK))
    e_i = lax.broadcasted_iota(I32, (LANES, LANES), 0)
    e_j = lax.broadcasted_iota(I32, (LANES, LANES), 1)
    earlier = (e_i < e_j).astype(BF16)
    seg = _dot(jnp.broadcast_to(n_chunk, (SUBLANES, LANES)).astype(BF16), earlier)[0:1]
    where_to = seg * float(ROW_CHUNK) + rank
    pos = jnp.zeros((tm, LANES), I32)
    gt = jnp.zeros((tm, LANES), F32)
    for k in range(TOP_K):
        p_k = jnp.sum(jnp.where(hits[k], where_to, 0.0), axis=-1, keepdims=True)
        pos = jnp.where(lane == k, p_k.astype(I32), pos)
        gt = jnp.where(lane == k, exps[k] / den, gt)
    pos_ref[...] = pos
    gt_ref[...] = gt
    nch_ref[0] = n_chunk.astype(I32)
    soff_ref[0] = seg.astype(I32)


def _post(geo, a_p, a_s, w_out_bf16, x, norm_g, mod_p, mod_s, col0, router_w, router_b):
    tm = geo.tm_moe
    n_pt = geo.n_prompt // tm
    n_tiles = geo.nt // tm
    rw = jnp.zeros((D_MODEL, LANES), BF16).at[:, :N_EXPERTS].set(router_w.astype(BF16))
    rb = jnp.zeros((1, LANES), F32).at[0, :N_EXPERTS].set(router_b)
    row = lambda i: (i, 0)
    fixed = lambda i: (0, 0)
    tile_row = lambda i: (i, 0, 0)
    return pl.pallas_call(
        functools.partial(_post_kernel, n_pt=n_pt, tm=tm),
        grid=(n_tiles,),
        in_specs=[
            pl.BlockSpec((tm, D_MODEL), lambda i: (jnp.minimum(i, n_pt - 1), 0)),
            pl.BlockSpec((tm, D_MODEL), lambda i: (jnp.maximum(i - n_pt, 0), 0)),
            pl.BlockSpec((D_MODEL, D_MODEL), fixed),
            pl.BlockSpec((tm, D_MODEL), row),
            *_mod_specs(geo, col0 + 2, tm),
            pl.BlockSpec((1, D_MODEL), fixed),
            *_mod_specs(geo, col0 + 3, tm),
            *_mod_specs(geo, col0 + 4, tm),
            pl.BlockSpec((D_MODEL, LANES), fixed),
            pl.BlockSpec((1, LANES), fixed),
        ],
        out_specs=[
            pl.BlockSpec((tm, D_MODEL), row),
            pl.BlockSpec((tm, D_MODEL), row),
            pl.BlockSpec((tm, LANES), row),
            pl.BlockSpec((tm, LANES), row),
            pl.BlockSpec((1, 1, LANES), tile_row),
            pl.BlockSpec((1, 1, LANES), tile_row),
        ],
        out_shape=[
            jax.ShapeDtypeStruct((geo.nt, D_MODEL), F32),
            jax.ShapeDtypeStruct((geo.nt, D_MODEL), BF16),
            jax.ShapeDtypeStruct((geo.nt, LANES), I32),
            jax.ShapeDtypeStruct((geo.nt, LANES), F32),
            jax.ShapeDtypeStruct((n_tiles, 1, LANES), I32),
            jax.ShapeDtypeStruct((n_tiles, 1, LANES), I32),
        ],
        compiler_params=_cparams(("arbitrary",)),
        name="post_mixer_router",
    )(a_p, a_s.astype(BF16), w_out_bf16, x, mod_p, mod_s, norm_g, mod_p, mod_s, mod_p, mod_s, rw, rb)


def _ffn_kernel(be_ref, live_ref, nreal_ref, x_ref, win_ref, bin_ref, wout_ref, bout_ref, y_ref,
                win_scr, wout_scr, *, bm):
    del nreal_ref
    b = pl.program_id(0)
    live = live_ref[b]
    half = bm // 2

    def ffn(rows):
        gu = _dot(x_ref[rows, :].astype(BF16), win_scr[...]) + bin_ref[0]
        gate = jnp.minimum(gu[:, :D_MODEL], SWIGLU_LIMIT)
        up = jnp.clip(gu[:, D_MODEL:], -SWIGLU_LIMIT, SWIGLU_LIMIT)
        act = gate * _sigmoid(SWIGLU_ALPHA * gate) * (up + 1.0)
        return _dot(act.astype(BF16), wout_scr[...]) + bout_ref[0]

    @pl.when(live > 0)
    def _():
        changed = (b == 0) | (be_ref[b] != be_ref[jnp.maximum(b - 1, 0)])

        @pl.when(changed)
        def _():
            win_scr[...] = win_ref[0].astype(BF16)
            wout_scr[...] = wout_ref[0].astype(BF16)

    @pl.when(live == 2)
    def _():
        y0 = ffn(slice(0, half))
        y1 = ffn(slice(half, bm))
        y_ref[0:half, :] = y0
        y_ref[half:bm, :] = y1

    @pl.when(live == 1)
    def _():
        y_ref[0:half, :] = ffn(slice(0, half))
        y_ref[half:bm, :] = jnp.zeros((half, D_MODEL), F32)

    @pl.when(live == 0)
    def _():
        y_ref[...] = jnp.zeros_like(y_ref)


def _ffn(xs, block_e, live, n_real, w_in, b_in, w_out, b_out, bm):
    n_blocks = xs.shape[0] // bm
    n_e, d, d2 = w_in.shape
    return pl.pallas_call(
        functools.partial(_ffn_kernel, bm=bm),
        grid_spec=pltpu.PrefetchScalarGridSpec(
            num_scalar_prefetch=3,
            grid=(n_blocks,),
            in_specs=[
                pl.BlockSpec((bm, d), lambda b, be, lv, nr: (jnp.minimum(b, nr[0] - 1), 0)),
                pl.BlockSpec((1, d, d2), lambda b, be, lv, nr: (be[b], 0, 0)),
                pl.BlockSpec((1, 1, d2), lambda b, be, lv, nr: (be[b], 0, 0)),
                pl.BlockSpec((1, d2 // 2, d), lambda b, be, lv, nr: (be[b], 0, 0)),
                pl.BlockSpec((1, 1, d), lambda b, be, lv, nr: (be[b], 0, 0)),
            ],
            out_specs=pl.BlockSpec((bm, d), lambda b, be, lv, nr: (b, 0)),
            scratch_shapes=[
                pltpu.VMEM((d, d2), BF16),
                pltpu.VMEM((d2 // 2, d), BF16),
            ],
        ),
        out_shape=jax.ShapeDtypeStruct((n_blocks * bm, d), F32),
        compiler_params=_cparams(("arbitrary",)),
        name="moe_ffn",
    )(block_e, live, n_real, xs, w_in, b_in.reshape(n_e, 1, d2), w_out, b_out.reshape(n_e, 1, d))


def _chunk_rows(chunk, n_chunks=1):
    return pl.ds(pl.multiple_of(chunk * ROW_CHUNK, ROW_CHUNK), n_chunks * ROW_CHUNK)


def _run_pieces(n, fn, big):
    n_big = n // big

    def big_piece(i, carry):
        fn(i * big, big)
        return carry

    def small_piece(i, carry):
        fn(n_big * big + i, 1)
        return carry

    lax.fori_loop(0, n_big, big_piece, 0)
    lax.fori_loop(0, n - n_big * big, small_piece, 0)


def _for_each_piece(nch_ref, soff_ref, dst_ref, tile, fn):
    def per_expert(e, carry):
        s0 = soff_ref[tile, e]
        d0 = dst_ref[tile, e]
        _run_pieces(nch_ref[tile, e], lambda off, size: fn(s0 + off, d0 + off, size), COPY_CHUNKS)
        return carry

    lax.fori_loop(0, N_EXPERTS, per_expert, 0)


def _dispatch_kernel(nch_ref, soff_ref, dst_ref, tot_ref, fill0_ref, filln_ref,
                     pos_ref, h_ref, xs_hbm, stage, zeros, sem, fill_sem, *, tm, k_stage):
    t = pl.program_id(0)
    n_t = pl.num_programs(0)
    slot = t % 2

    def copy(s_chunk, d_chunk, size, sl):
        return pltpu.make_async_copy(stage.at[sl, _chunk_rows(s_chunk, size)],
                                     xs_hbm.at[_chunk_rows(d_chunk, size)], sem.at[sl])

    def wait_tile(tile, sl):
        _run_pieces(tot_ref[tile], lambda off, size: copy(0, 0, size, sl).wait(), WAIT_CHUNKS)

    @pl.when(t >= 2)
    def _():
        wait_tile(t - 2, slot)

    pos_t = pos_ref[...].astype(F32).T.astype(I32)
    j_i = lax.broadcasted_iota(I32, (k_stage, tm), 0)
    hit = j_i == pos_t[0:1]
    for k in range(1, TOP_K):
        hit = hit | (j_i == pos_t[k:k + 1])
    stage[slot] = _dot(jnp.where(hit, 1.0, 0.0).astype(BF16), h_ref[...])
    _for_each_piece(nch_ref, soff_ref, dst_ref, t, lambda s, d, size: copy(s, d, size, slot).start())

    @pl.when(t == n_t - 1)
    def _():
        zeros[...] = jnp.zeros_like(zeros)

        def fill(d_chunk, size):
            return pltpu.make_async_copy(zeros.at[_chunk_rows(0, size)], xs_hbm.at[_chunk_rows(d_chunk, size)],
                                         fill_sem.at[0])

        def per_region(r, carry):
            _run_pieces(filln_ref[r], lambda off, size: fill(fill0_ref[r] + off, size).start(), COPY_CHUNKS)
            return carry

        lax.fori_loop(0, N_EXPERTS + 1, per_region, 0)

        @pl.when(t >= 1)
        def _():
            wait_tile(t - 1, 1 - slot)

        wait_tile(t, slot)

        def per_region_wait(r, carry):
            _run_pieces(filln_ref[r], lambda off, size: fill(0, size).wait(), COPY_CHUNKS)
            return carry

        lax.fori_loop(0, N_EXPERTS + 1, per_region_wait, 0)


def _combine_kernel(nch_ref, soff_ref, dst_ref, tot_ref, pos_ref, gate_ref, x_ref, gp_ref, gs_ref, y_hbm,
                    o_ref, stage, sem, *, n_pt, tm, k_stage):
    t = pl.program_id(0)
    n_t = pl.num_programs(0)
    slot = t % 2

    def copy(s_chunk, d_chunk, size, sl):
        return pltpu.make_async_copy(y_hbm.at[_chunk_rows(d_chunk, size)],
                                     stage.at[sl, _chunk_rows(s_chunk, size)], sem.at[sl])

    def fetch(tile, sl):
        _for_each_piece(nch_ref, soff_ref, dst_ref, tile, lambda s, d, size: copy(s, d, size, sl).start())

    @pl.when(t == 0)
    def _():
        stage[...] = jnp.zeros_like(stage)
        fetch(0, 0)

    @pl.when(t + 1 < n_t)
    def _():
        fetch(t + 1, 1 - slot)

    _run_pieces(tot_ref[t], lambda off, size: copy(0, 0, size, slot).wait(), WAIT_CHUNKS)
    lane = lax.broadcasted_iota(I32, (tm, k_stage), 1)
    pos = pos_ref[...]
    g = gate_ref[...]
    p = jnp.zeros((tm, k_stage), F32)
    for k in range(TOP_K):
        p = jnp.where(lane == pos[:, k:k + 1], g[:, k:k + 1], p)
    ffn = _dot(p.astype(BF16), stage[slot].astype(BF16))
    o_ref[...] = x_ref[...] + _pick_mod(t, n_pt, gp_ref, gs_ref) * ffn


def _moe(geo, h2, pos, gates, nch_pad, soff_pad, x1, mod_p, mod_s, col_gate, layer, w_in, b_in, w_out, b_out):
    bm = MOE_BLOCK
    tm = geo.tm_moe
    n_tiles = geo.nt // tm
    depth, n_e, d, d2 = w_in.shape
    w_in = w_in.reshape(depth * n_e, d, d2)
    b_in = b_in.reshape(depth * n_e, d2)
    w_out = w_out.reshape(depth * n_e, d2 // 2, d)
    b_out = b_out.reshape(depth * n_e, d)
    chunks_per_block = bm // ROW_CHUNK
    max_rows = geo.nt * TOP_K + n_tiles * N_EXPERTS * (ROW_CHUNK - 1) + N_EXPERTS * (bm - 1)
    n_blocks = -(-max_rows // bm)
    nch = nch_pad[:, 0, :N_EXPERTS]
    soff = soff_pad[:, 0, :N_EXPERTS]
    tot = jnp.sum(nch, axis=1)
    per_e = jnp.sum(nch, axis=0)
    padded = (per_e + chunks_per_block - 1) // chunks_per_block * chunks_per_block
    pend = jnp.cumsum(padded)
    pstart = pend - padded
    dst = pstart[None, :] + jnp.cumsum(nch, axis=0) - nch
    block_start = jnp.arange(n_blocks, dtype=I32) * chunks_per_block
    block_e = jnp.minimum(jnp.sum(block_start[:, None] >= pend[None, :], axis=1), N_EXPERTS - 1).astype(I32)
    n_real = (pend[-1:] // chunks_per_block).astype(I32)
    used = jnp.clip((pstart + per_e)[block_e] - block_start, 0, chunks_per_block)
    used = jnp.where(block_start < pend[-1], used, 0)
    live = ((used > 0).astype(I32) + (used > chunks_per_block // 2).astype(I32))
    fill0 = jnp.concatenate([pstart + per_e, pend[-1:]]).astype(I32)
    filln = jnp.concatenate([padded - per_e, n_blocks * chunks_per_block - pend[-1:]]).astype(I32)

    k_stage = -(-(tm * TOP_K + N_EXPERTS * (ROW_CHUNK - 1)) // LANES) * LANES
    row = lambda i, *_: (i, 0)
    xs = pl.pallas_call(
        functools.partial(_dispatch_kernel, tm=tm, k_stage=k_stage),
        grid_spec=pltpu.PrefetchScalarGridSpec(
            num_scalar_prefetch=6,
            grid=(n_tiles,),
            in_specs=[pl.BlockSpec((tm, LANES), row), pl.BlockSpec((tm, d), row)],
            out_specs=pl.BlockSpec(memory_space=pl.ANY),
            scratch_shapes=[
                pltpu.VMEM((2, k_stage, d), F32),
                pltpu.VMEM((COPY_CHUNKS * ROW_CHUNK, d), F32),
                pltpu.SemaphoreType.DMA((2,)),
                pltpu.SemaphoreType.DMA((1,)),
            ],
        ),
        out_shape=jax.ShapeDtypeStruct((n_blocks * bm, d), F32),
        compiler_params=_cparams(("arbitrary",)),
        name="moe_dispatch",
    )(nch, soff, dst, tot, fill0, filln, pos, h2)

    ys = _ffn(xs, block_e + layer * n_e, live, n_real, w_in, b_in, w_out, b_out, bm)

    return pl.pallas_call(
        functools.partial(_combine_kernel, n_pt=geo.n_prompt // tm, tm=tm, k_stage=k_stage),
        grid_spec=pltpu.PrefetchScalarGridSpec(
            num_scalar_prefetch=4,
            grid=(n_tiles,),
            in_specs=[
                pl.BlockSpec((tm, LANES), row),
                pl.BlockSpec((tm, LANES), row),
                pl.BlockSpec((tm, d), row),
                *_mod_specs(geo, col_gate, tm),
                pl.BlockSpec(memory_space=pl.ANY),
            ],
            out_specs=pl.BlockSpec((tm, d), row),
            scratch_shapes=[
                pltpu.VMEM((2, k_stage, d), F32),
                pltpu.SemaphoreType.DMA((2,)),
            ],
        ),
        out_shape=jax.ShapeDtypeStruct((geo.nt, d), F32),
        compiler_params=_cparams(("arbitrary",)),
        name="moe_combine",
    )(nch, soff, dst, tot, pos, gates, x1, mod_p, mod_s, ys)


SW_QW = SW_Q_HEADS * SW_HEAD_DIM
SW_QKW = SW_QW + SW_KV_W
SW_KEYS = 2 * ATT_BLOCK


def _attn_proj_kernel(x_ref, g_ref, shp_ref, shs_ref, scp_ref, scs_ref, w_ref, e_ref, et_ref, qg_ref, kg_ref,
                      q_ref, k_ref, v_ref, *, n_pt):
    i = pl.program_id(0)
    sh = _pick_mod(i, n_pt, shp_ref, shs_ref)
    sc = _pick_mod(i, n_pt, scp_ref, scs_ref)
    h = _norm_mod(x_ref[...], g_ref[...], sc, sh).astype(BF16)
    z = _dot(h, w_ref[...])
    qk = z[:, :SW_QKW]
    sq = qk * qk
    sq_hi = sq.astype(BF16)
    sq_lo = (sq - sq_hi.astype(F32)).astype(BF16)
    ms = (_dot(sq_hi, e_ref[...]) + _dot(sq_lo, e_ref[...])) * (1.0 / SW_HEAD_DIM)
    inv = lax.rsqrt(ms + NORM_EPS)
    inv_hi = inv.astype(BF16)
    inv_lo = (inv - inv_hi.astype(F32)).astype(BF16)
    qk = qk * (_dot(inv_hi, et_ref[...]) + _dot(inv_lo, et_ref[...]))
    q_ref[...] = (qk[:, :SW_QW] * qg_ref[...] * SW_SCALE).astype(q_ref.dtype)
    k_ref[...] = qk[:, SW_QW:] * kg_ref[...]
    v_ref[...] = z[:, SW_QKW:]


def _attn_proj(geo, x, g, mod_p, mod_s, w_bf16, q_gain, k_gain):
    tm = geo.tm
    n_out = w_bf16.shape[1]
    heads = SW_QKW // SW_HEAD_DIM
    member = (np.arange(SW_QKW)[:, None] // SW_HEAD_DIM == np.arange(LANES)[None, :]).astype(np.float32)
    e = jnp.asarray(member, BF16)
    et = jnp.asarray(member.T, BF16)
    assert heads <= LANES
    qg = jnp.tile(q_gain, SW_Q_HEADS).reshape(1, SW_QW)
    kg = jnp.tile(k_gain, SW_KV_HEADS).reshape(1, SW_KV_W)
    row = lambda i: (i, 0)
    fixed = lambda i: (0, 0)
    return pl.pallas_call(
        functools.partial(_attn_proj_kernel, n_pt=geo.n_pt),
        grid=(geo.n_tiles,),
        in_specs=[
            pl.BlockSpec((tm, D_MODEL), row),
            pl.BlockSpec((1, D_MODEL), fixed),
            *_mod_specs(geo, 0, tm),
            *_mod_specs(geo, 1, tm),
            pl.BlockSpec((D_MODEL, n_out), fixed),
            pl.BlockSpec((SW_QKW, LANES), fixed),
            pl.BlockSpec((LANES, SW_QKW), fixed),
            pl.BlockSpec((1, SW_QW), fixed),
            pl.BlockSpec((1, SW_KV_W), fixed),
        ],
        out_specs=[
            pl.BlockSpec((tm, SW_QW), row),
            pl.BlockSpec((tm, SW_KV_W), row),
            pl.BlockSpec((tm, SW_KV_W), row),
        ],
        out_shape=[
            jax.ShapeDtypeStruct((geo.nt, SW_QW), BF16),
            jax.ShapeDtypeStruct((geo.nt, SW_KV_W), F32),
            jax.ShapeDtypeStruct((geo.nt, SW_KV_W), F32),
        ],
        compiler_params=_cparams(("arbitrary",)),
        name="attn_proj_qknorm",
    )(x, g, mod_p, mod_s, mod_p, mod_s, w_bf16, e, et, qg, kg)


def _rel_bucket_np(dist):
    n = np.maximum(dist, 0)
    max_exact = REL_BUCKETS // 2
    ratio = np.log(np.maximum(n, 1).astype(np.float32) / np.float32(max_exact)) / np.float32(
        math.log(REL_MAX_DIST / max_exact))
    large = max_exact + (ratio * np.float32(REL_BUCKETS - max_exact)).astype(np.int32)
    large = np.minimum(large, REL_BUCKETS - 1)
    return np.where(n < max_exact, n, large).astype(np.int32)


def _bucket_table(qpos, kpos, k_ok):
    dist = qpos[:, None] - kpos[None, :]
    ok = (dist >= 0) & (dist <= WINDOW) & k_ok[None, :]
    return np.where(ok, _rel_bucket_np(dist), -1).astype(np.int32)


def _build_bias(bkt_ref, rb_ref, bias_scr, rows):
    bkt = bkt_ref[...]
    base = jnp.where(bkt < 0, -jnp.inf, 0.0)
    for h in range(SW_Q_HEADS):
        bias_scr[h * rows:(h + 1) * rows, :] = base

    def add_bucket(j, carry):
        hit = bkt == j
        for h in range(SW_Q_HEADS):
            sl = slice(h * rows, (h + 1) * rows)
            bias_scr[sl, :] = bias_scr[sl, :] + jnp.where(hit, rb_ref[j, h], 0.0)
        return carry

    lax.fori_loop(0, REL_BUCKETS, add_bucket, 0)


def _sink_softmax_pv(s, sink, vv_g):
    m = jnp.maximum(jnp.max(s, axis=-1, keepdims=True), sink)
    e = jnp.exp(s - m)
    p = e / (jnp.sum(e, axis=-1, keepdims=True) + jnp.exp(sink - m))
    return _dot(p.astype(BF16), vv_g)


def _build_bias_t(bkt_ref, rb_ref, bias_scr, blk):
    bkt = bkt_ref[...]
    base = jnp.where(bkt < 0, -jnp.inf, 0.0)
    slots = [(g, j) for g in range(SW_KV_HEADS) for j in range(SW_GROUP)]
    for g, j in slots:
        bias_scr[g, :, j * blk:(j + 1) * blk] = base

    def add_bucket(b, carry):
        hit = bkt == b
        for g, j in slots:
            cols = slice(j * blk, (j + 1) * blk)
            bias_scr[g, :, cols] = bias_scr[g, :, cols] + jnp.where(hit, rb_ref[b, g * SW_GROUP + j], 0.0)
        return carry

    lax.fori_loop(0, REL_BUCKETS, add_bucket, 0)


def _swa_prompt_kernel(rb_ref, sink_ref, bkt_ref, q_ref, kp_ref, kc_ref, vp_ref, vc_ref, o_ref, bias_scr):
    n = pl.program_id(1)
    blk = ATT_BLOCK
    hd = SW_HEAD_DIM

    @pl.when((pl.program_id(0) == 0) & (n == 0))
    def _():
        _build_bias_t(bkt_ref, rb_ref, bias_scr, blk)

    kk = jnp.concatenate([kp_ref[...], kc_ref[...]], axis=0).astype(BF16)
    vv_t = jnp.concatenate([vp_ref[...], vc_ref[...]], axis=0).T.astype(BF16)
    q_t = q_ref[...].astype(F32).T.astype(BF16)
    key_i = lax.broadcasted_iota(I32, (SW_KEYS, SW_GROUP * blk), 0)
    hide_prev = (n == 0) & (key_i < blk)
    zeros_q = jnp.zeros((hd, SW_GROUP * blk), BF16)
    outs = []
    for g in range(SW_KV_HEADS):
        heads = [g * SW_GROUP + j for j in range(SW_GROUP)]
        q_cat = jnp.concatenate([q_t[hq * hd:(hq + 1) * hd, :] for hq in heads], axis=1)
        q_full = jnp.concatenate([q_cat if gg == g else zeros_q for gg in range(SW_KV_HEADS)], axis=0)
        s = _dot(kk, q_full) + bias_scr[g]
        s = jnp.where(hide_prev, -jnp.inf, s)
        sink = jnp.concatenate([jnp.full((1, blk), sink_ref[hq], F32) for hq in heads], axis=1)
        m = jnp.maximum(jnp.max(s, axis=0, keepdims=True), sink)
        e = jnp.exp(s - m)
        den = jnp.sum(e, axis=0, keepdims=True) + jnp.exp(sink - m)
        o_t = _dot(vv_t[g * hd:(g + 1) * hd, :], e.astype(BF16)) * (1.0 / den)
        outs.extend(o_t[:, j * blk:(j + 1) * blk] for j in range(SW_GROUP))
    o_ref[...] = jnp.concatenate(outs, axis=0).T.astype(o_ref.dtype)


def _swa_prompt(geo, q, k, v, rel_bias, sinks):
    blk = ATT_BLOCK
    assert geo.seq % blk == 0 and WINDOW == blk
    nb = geo.seq // blk
    bkt = jnp.asarray(_bucket_table(blk + np.arange(blk), np.arange(2 * blk), np.ones(2 * blk, bool)).T.copy())
    cur = lambda b, n: (b * nb + n, 0)
    prev = lambda b, n: (b * nb + jnp.maximum(n - 1, 0), 0)
    smem = pl.BlockSpec(memory_space=pltpu.SMEM)
    return pl.pallas_call(
        _swa_prompt_kernel,
        grid=(geo.batch, nb),
        in_specs=[
            smem, smem,
            pl.BlockSpec((SW_KEYS, blk), lambda b, n: (0, 0)),
            pl.BlockSpec((blk, SW_QW), cur),
            pl.BlockSpec((blk, SW_KV_W), prev),
            pl.BlockSpec((blk, SW_KV_W), cur),
            pl.BlockSpec((blk, SW_KV_W), prev),
            pl.BlockSpec((blk, SW_KV_W), cur),
        ],
        out_specs=pl.BlockSpec((blk, SW_QW), cur),
        out_shape=jax.ShapeDtypeStruct((geo.n_prompt, SW_QW), BF16),
        scratch_shapes=[pltpu.VMEM((SW_KV_HEADS, SW_KEYS, SW_GROUP * blk), F32)],
        compiler_params=_cparams(("arbitrary", "arbitrary")),
        name="swa_prompt",
    )(rel_bias, sinks, bkt, q, k, k, v, v)


def _swa_step_kernel(rb_ref, sink_ref, bkt_ref, q_ref, kn_ref, vn_ref, ck_ref, cv_ref,
                     o_ref, nk_ref, nv_ref, bias_scr, *, steps, bb):
    pad = SUBLANES
    win = WINDOW

    @pl.when(pl.program_id(0) == 0)
    def _():
        _build_bias(bkt_ref, rb_ref, bias_scr, pad)

    fill = jnp.zeros((SW_KEYS - win - pad, SW_KV_W), F32)
    row8 = lax.broadcasted_iota(I32, (pad, SW_KV_W), 0)

    def shifted(cache, new):
        rolled = pltpu.roll(cache, win - steps, 0)
        tail = jnp.where(row8 < pad - steps, rolled[win - pad:], pltpu.roll(new, pad - steps, 0))
        return rolled[:win - pad], tail

    def body(s, carry):
        ck, cv, kn, vn = ck_ref[s], cv_ref[s], kn_ref[s], vn_ref[s]
        kk = jnp.concatenate([ck, kn, fill], axis=0).astype(BF16)
        vv = jnp.concatenate([cv, vn, fill], axis=0).astype(BF16)
        q = q_ref[s]
        for g in range(SW_KV_HEADS):
            kcols = slice(g * SW_HEAD_DIM, (g + 1) * SW_HEAD_DIM)
            heads = [g * SW_GROUP + j for j in range(SW_GROUP)]
            qs = jnp.concatenate([q[:, hq * SW_HEAD_DIM:(hq + 1) * SW_HEAD_DIM] for hq in heads], axis=0)
            sc = _dot_nt(qs.astype(BF16), kk[:, kcols]) + bias_scr[heads[0] * pad:(heads[-1] + 1) * pad, :]
            sink = jnp.concatenate([jnp.full((pad, 1), sink_ref[hq], F32) for hq in heads], axis=0)
            og = _sink_softmax_pv(sc, sink, vv[:, kcols])
            for j, hq in enumerate(heads):
                o_ref[s, :, hq * SW_HEAD_DIM:(hq + 1) * SW_HEAD_DIM] = og[j * pad:(j + 1) * pad]
        head, tail = shifted(ck, kn)
        nk_ref[s, 0:win - pad] = head
        nk_ref[s, win - pad:win] = tail
        head, tail = shifted(cv, vn)
        nv_ref[s, 0:win - pad] = head
        nv_ref[s, win - pad:win] = tail
        return carry

    lax.fori_loop(0, bb, body, 0)


def _swa_step(geo, q, k, v, cache_k, cache_v, rel_bias, sinks):
    steps, db = geo.dec_seq, geo.dec_batch
    win = cache_k.shape[1]
    assert win == WINDOW and steps <= SUBLANES
    bb = min(8, db)
    pad = SUBLANES
    q_s = _seq_major(geo, q[geo.n_prompt:].astype(F32))
    k_s = _seq_major(geo, k[geo.n_prompt:])
    v_s = _seq_major(geo, v[geo.n_prompt:])
    kpos = np.arange(SW_KEYS)
    k_ok = kpos < win + steps
    qpos = win + np.arange(pad)
    bkt = _bucket_table(qpos, kpos, k_ok)
    bkt[steps:] = -1
    smem = pl.BlockSpec(memory_space=pltpu.SMEM)
    blk3 = lambda r, w: pl.BlockSpec((bb, r, w), lambda i: (i, 0, 0))
    o_s, nk, nv = pl.pallas_call(
        functools.partial(_swa_step_kernel, steps=steps, bb=bb),
        grid=(db // bb,),
        in_specs=[
            smem, smem,
            pl.BlockSpec((pad, SW_KEYS), lambda i: (0, 0)),
            blk3(pad, SW_QW), blk3(pad, SW_KV_W), blk3(pad, SW_KV_W),
            blk3(win, SW_KV_W), blk3(win, SW_KV_W),
        ],
        out_specs=[blk3(pad, SW_QW), blk3(win, SW_KV_W), blk3(win, SW_KV_W)],
        out_shape=[
            jax.ShapeDtypeStruct((db, pad, SW_QW), F32),
            jax.ShapeDtypeStruct((db, win, SW_KV_W), F32),
            jax.ShapeDtypeStruct((db, win, SW_KV_W), F32),
        ],
        scratch_shapes=[pltpu.VMEM((SW_Q_HEADS * pad, SW_KEYS), F32)],
        compiler_params=_cparams(("arbitrary",)),
        name="swa_step",
    )(rel_bias, sinks, jnp.asarray(bkt), q_s, k_s, v_s, cache_k, cache_v)
    return _time_major(geo, o_s), nk, nv


def kernel(x_prompt, x_sample, state_hgrn, cache_win_k, cache_win_v, c_prompt, c_sample, norm1_g, norm2_g, ada_w, ada_b, hg_w_in, hg_lb_table, hg_onorm_g, hg_w_out, sw_w_in, sw_qnorm_g, sw_knorm_g, sw_sinks, sw_w_out, rel_bias, router_w, router_b, moe_w_in, moe_b_in, moe_w_out, moe_b_out):
    batch, seq, d = x_prompt.shape
    db, steps, _ = x_sample.shape
    assert d == D_MODEL and ada_w.shape[0] == 2 and hg_w_in.shape[0] == 1 and sw_w_in.shape[0] == 1
    geo = _Geom(batch, seq, db, steps)
    x = jnp.concatenate([x_prompt.reshape(batch * seq, d),
                         x_sample.transpose(1, 0, 2).reshape(steps * db, d)], axis=0)
    n_seq = batch + db
    rows = -(-n_seq // SUBLANES) * SUBLANES
    c_all = jnp.concatenate([c_prompt, c_sample, jnp.zeros((rows - n_seq, d), F32)], axis=0)
    mods = _ada_mods(c_all, ada_w, ada_b)

    def layer_mods(layer):
        mod_p = mods[layer, :batch].reshape(batch, 1, 6 * d)
        mod_s = jnp.tile(mods[layer, batch:n_seq], (steps, 1))
        return mod_p, mod_s

    def moe(layer, a_p, a_s, w_out, x_in, mod_p, mod_s):
        x1, h2, pos, gt, nch, soff = _post(geo, a_p, a_s, w_out.astype(BF16), x_in, norm2_g[layer:layer + 1],
                                           mod_p, mod_s, 0, router_w[layer], router_b[layer])
        return _moe(geo, h2, pos, gt, nch, soff, x1, mod_p, mod_s, 5,
                    layer, moe_w_in, moe_b_in, moe_w_out, moe_b_out)

    mod_p, mod_s = layer_mods(0)
    og = hg_onorm_g[0:1]
    z = _proj(geo, x, norm1_g[0:1], mod_p, mod_s, hg_w_in[0].astype(BF16), 0, 1)
    o_p, st_p = _gla_prompt(geo, z, hg_lb_table, og)
    o_s, st_s = _gla_step(geo, z, state_hgrn[0], hg_lb_table, og)
    x = moe(0, o_p, o_s, hg_w_out[0], x, mod_p, mod_s)

    mod_p, mod_s = layer_mods(1)
    q, k, v = _attn_proj(geo, x, norm1_g[1:2], mod_p, mod_s, sw_w_in[0].astype(BF16), sw_qnorm_g[0], sw_knorm_g[0])
    win = cache_win_k.shape[2]
    a_p = _swa_prompt(geo, q, k, v, rel_bias, sw_sinks[0])
    a_s, nk, nv = _swa_step(geo, q, k, v, cache_win_k[0].reshape(db, win, SW_KV_W),
                            cache_win_v[0].reshape(db, win, SW_KV_W), rel_bias, sw_sinks[0])
    x = moe(1, a_p, a_s, sw_w_out[0], x, mod_p, mod_s)

    y_prompt = x[:geo.n_prompt].reshape(batch, seq, d)
    y_sample = x[geo.n_prompt:].reshape(steps, db, d).transpose(1, 0, 2)
    kv_shape = (1, batch, WINDOW, SW_KV_HEADS, SW_HEAD_DIM)
    k_p = k[:geo.n_prompt].reshape(batch, seq, SW_KV_W)[:, seq - WINDOW:].reshape(kv_shape)
    v_p = v[:geo.n_prompt].reshape(batch, seq, SW_KV_W)[:, seq - WINDOW:].reshape(kv_shape)
    cache_shape = (1, db, win, SW_KV_HEADS, SW_HEAD_DIM)
    return (y_prompt, y_sample, jnp.swapaxes(st_p, -1, -2)[None], st_s[None], k_p, v_p,
            nk.reshape(cache_shape), nv.reshape(cache_shape))
```

```python
import functools
import math

import numpy as np
import jax
import jax.numpy as jnp
from jax import lax
from jax.experimental import pallas as pl
from jax.experimental.pallas import tpu as pltpu

F32 = jnp.float32
BF16 = jnp.bfloat16
I32 = jnp.int32

D_MODEL = 1024
LANES = 128
SUBLANES = 8
D_TILES = D_MODEL // LANES
HG_DK = 128
HG_HEADS = D_MODEL // HG_DK
HG_DV = D_MODEL // HG_HEADS
HG_CHUNK = 64
SW_HEAD_DIM = 64
SW_Q_HEADS = D_MODEL // SW_HEAD_DIM
SW_KV_HEADS = 4
SW_GROUP = SW_Q_HEADS // SW_KV_HEADS
SW_KV_W = SW_KV_HEADS * SW_HEAD_DIM
WINDOW = 128
ATT_BLOCK = 128
SW_SCALE = SW_HEAD_DIM ** -0.5
REL_BUCKETS = 32
REL_MAX_DIST = 128
N_EXPERTS = 32
TOP_K = 4
SWIGLU_LIMIT = 7.0
SWIGLU_ALPHA = 1.702
NORM_EPS = 1e-5
MOE_BLOCK = 512
ROW_CHUNK = SUBLANES
COPY_CHUNKS = 4
WAIT_CHUNKS = 16
VMEM_LIMIT = 56 * 1024 * 1024


def _cparams(sem):
    return pltpu.CompilerParams(dimension_semantics=sem, vmem_limit_bytes=VMEM_LIMIT)


def _sigmoid(x):
    return 0.5 * jnp.tanh(0.5 * x) + 0.5


def _silu(x):
    return x * _sigmoid(x)


def _dot(a, b):
    return jnp.dot(a, b, preferred_element_type=F32)


def _dot_nt(a, b):
    return lax.dot_general(a, b, (((1,), (1,)), ((), ())), preferred_element_type=F32)


def _dot_tn(a, b):
    return lax.dot_general(a, b, (((0,), (0,)), ((), ())), preferred_element_type=F32)


def _split3(x):
    hi = x.astype(BF16)
    r = x - hi.astype(F32)
    mid = r.astype(BF16)
    lo = (r - mid.astype(F32)).astype(BF16)
    return hi, mid, lo


def _ada_kernel(c_ref, w_ref, b_ref, o_ref):
    s = _silu(c_ref[...]).astype(BF16)
    o_ref[0] = _dot(s, w_ref[0].astype(BF16)) + b_ref[0]


def _ada_mods(c_all, ada_w, ada_b):
    depth, d, n6 = ada_w.shape
    rows = c_all.shape[0]
    tn = 1536
    return pl.pallas_call(
        _ada_kernel,
        grid=(depth, n6 // tn),
        in_specs=[
            pl.BlockSpec((rows, d), lambda l, j: (0, 0)),
            pl.BlockSpec((1, d, tn), lambda l, j: (l, 0, j)),
            pl.BlockSpec((1, 1, tn), lambda l, j: (l, 0, j)),
        ],
        out_specs=pl.BlockSpec((1, rows, tn), lambda l, j: (l, 0, j)),
        out_shape=jax.ShapeDtypeStruct((depth, rows, n6), F32),
        compiler_params=_cparams(("arbitrary", "arbitrary")),
        name="ada_mods",
    )(c_all, ada_w, ada_b.reshape(depth, 1, n6))


class _Geom:
    def __init__(self, batch, seq, dec_batch, dec_seq):
        self.batch, self.seq, self.dec_batch, self.dec_seq = batch, seq, dec_batch, dec_seq
        self.n_prompt = batch * seq
        self.n_sample = dec_batch * dec_seq
        self.nt = self.n_prompt + self.n_sample
        tm = 512
        while seq % tm or self.n_sample % tm:
            tm //= 2
        assert tm >= 8
        self.tm = tm
        self.n_pt = self.n_prompt // tm
        self.n_tiles = self.nt // tm
        self.tm_moe = min(256, tm)


def _mod_specs(geo, col, tm):
    seq, batch = geo.seq, geo.batch
    n_pt = geo.n_prompt // tm

    def p_map(i, *_):
        return (jnp.minimum(i * tm // seq, batch - 1), 0, col)

    def s_map(i, *_):
        return (jnp.maximum(i - n_pt, 0), col)

    return [pl.BlockSpec((1, 1, D_MODEL), p_map), pl.BlockSpec((tm, D_MODEL), s_map)]


def _pick_mod(i, n_pt, p_ref, s_ref):
    return jnp.where(i >= n_pt, s_ref[...], p_ref[0])


def _norm_mod(x, g, sc, sh):
    ms = jnp.mean(x * x, axis=-1, keepdims=True)
    return x * lax.rsqrt(ms + NORM_EPS) * g * (1.0 + sc) + sh


def _pair_specs(geo, tm, width):
    n_pt = geo.n_prompt // tm
    return [pl.BlockSpec((tm, width), lambda i, *_: (jnp.minimum(i, n_pt - 1), 0)),
            pl.BlockSpec((tm, width), lambda i, *_: (jnp.maximum(i - n_pt, 0), 0))]


def _proj_kernel(xp_ref, xs_ref, g_ref, shp_ref, shs_ref, scp_ref, scs_ref, w_ref, o_ref, *, n_pt, tn):
    i = pl.program_id(0)
    sh = _pick_mod(i, n_pt, shp_ref, shs_ref)
    sc = _pick_mod(i, n_pt, scp_ref, scs_ref)
    x = jnp.where(i >= n_pt, xs_ref[...], xp_ref[...])
    h = _norm_mod(x, g_ref[...], sc, sh).astype(BF16)
    for j in range(o_ref.shape[1] // tn):
        o_ref[:, j * tn:(j + 1) * tn] = _dot(h, w_ref[:, j * tn:(j + 1) * tn])


def _proj(geo, x_pair, g, mod_p, mod_s, w_bf16, col_shift, col_scale):
    n_out = w_bf16.shape[1]
    tm = geo.tm
    fixed = lambda i: (0, 0)
    return pl.pallas_call(
        functools.partial(_proj_kernel, n_pt=geo.n_pt, tn=1024),
        grid=(geo.n_tiles,),
        in_specs=[
            *_pair_specs(geo, tm, D_MODEL),
            pl.BlockSpec((1, D_MODEL), fixed),
            *_mod_specs(geo, col_shift, tm),
            *_mod_specs(geo, col_scale, tm),
            pl.BlockSpec((D_MODEL, n_out), fixed),
        ],
        out_specs=pl.BlockSpec((tm, n_out), lambda i: (i, 0)),
        out_shape=jax.ShapeDtypeStruct((geo.nt, n_out), F32),
        compiler_params=_cparams(("arbitrary",)),
        name="norm_mod_proj",
    )(*x_pair, g, mod_p, mod_s, mod_p, mod_s, w_bf16)


def _hg_lower_bound(lbt_ref):
    t = lbt_ref[...]
    e = jnp.exp(t - jnp.max(t, axis=0, keepdims=True))
    return e[0:1] / jnp.sum(e, axis=0, keepdims=True)


def _hg_gates(fz, lb):
    e = jnp.exp(-jnp.abs(fz))
    inv = 1.0 / (1.0 + e)
    pos = fz >= 0
    sig = jnp.where(pos, inv, e * inv)
    sig_neg = jnp.where(pos, e * inv, inv)
    logf = jnp.log(lb + (1.0 - lb) * sig)
    return logf, (1.0 - lb) * sig_neg


def _hg_out(o, gz, og):
    ms = jnp.mean(o * o, axis=-1, keepdims=True)
    return o * lax.rsqrt(ms + NORM_EPS) * og * _silu(gz)


def _gla_prompt_kernel(z_ref, lbt_ref, og_ref, o_ref, sfin_ref, st_scr, *, chunk, n_chunks):
    t_step = pl.program_id(1)
    kw = HG_HEADS * HG_DK

    @pl.when(t_step == 0)
    def _():
        st_scr[...] = jnp.zeros_like(st_scr)

    lb = _hg_lower_bound(lbt_ref)
    og = og_ref[...]
    r_i = lax.broadcasted_iota(I32, (chunk, chunk), 0)
    c_i = lax.broadcasted_iota(I32, (chunk, chunk), 1)
    causal = c_i <= r_i
    tri = causal.astype(BF16)
    mid = chunk // 2 - 1

    def body(c, carry):
        rows = pl.ds(pl.multiple_of(c * chunk, chunk), chunk)
        logf, kk = _hg_gates(z_ref[rows, kw:2 * kw], lb)
        hi, md, lo = _split3(logf)
        cs = _dot(tri, jnp.concatenate([hi, md, lo], axis=1))
        b = cs[:, :kw] + cs[:, kw:2 * kw] + cs[:, 2 * kw:]
        b_mid = b[mid:mid + 1]
        b_last = b[chunk - 1:chunk]
        q_hat = _silu(z_ref[rows, 0:kw]) * jnp.exp(b - b_mid)
        k_hat = kk * jnp.exp(b_mid - b)
        q_in = (q_hat * jnp.exp(b_mid)).astype(BF16)
        k_dec = (k_hat * jnp.exp(b_last - b_mid)).astype(BF16)
        q_hat = q_hat.astype(BF16)
        k_hat = k_hat.astype(BF16)
        dec = jnp.exp(b_last)
        for h in range(HG_HEADS):
            cols = slice(h * HG_DK, (h + 1) * HG_DK)
            vcols = slice(2 * kw + h * HG_DV, 2 * kw + (h + 1) * HG_DV)
            gcols = slice(2 * kw + HG_HEADS * HG_DV + h * HG_DV, 2 * kw + HG_HEADS * HG_DV + (h + 1) * HG_DV)
            v = z_ref[rows, vcols].astype(BF16)
            att = jnp.where(causal, _dot_nt(q_hat[:, cols], k_hat[:, cols]), 0.0).astype(BF16)
            st = st_scr[h]
            o = _dot(att, v) + _dot_nt(q_in[:, cols], st.astype(BF16))
            st_scr[h] = st * dec[:, cols] + _dot_tn(v, k_dec[:, cols])
            o_ref[rows, h * HG_DV:(h + 1) * HG_DV] = _hg_out(o, z_ref[rows, gcols], og).astype(o_ref.dtype)
        return carry

    lax.fori_loop(0, n_chunks, body, 0)

    @pl.when(t_step == pl.num_programs(1) - 1)
    def _():
        sfin_ref[0] = st_scr[...]


def _gla_prompt(geo, z, lb_table, o_gain):
    tg = min(256, geo.seq)
    chunk = HG_CHUNK if geo.seq % HG_CHUNK == 0 else geo.seq
    assert tg % chunk == 0 and geo.seq % tg == 0
    nt = geo.seq // tg
    return pl.pallas_call(
        functools.partial(_gla_prompt_kernel, chunk=chunk, n_chunks=tg // chunk),
        grid=(geo.batch, nt),
        in_specs=[
            pl.BlockSpec((tg, 4 * D_MODEL), lambda b, t: (b * nt + t, 0)),
            pl.BlockSpec(lb_table.shape, lambda b, t: (0, 0)),
            pl.BlockSpec((1, HG_DV), lambda b, t: (0, 0)),
        ],
        out_specs=[
            pl.BlockSpec((tg, D_MODEL), lambda b, t: (b * nt + t, 0)),
            pl.BlockSpec((1, HG_HEADS, HG_DV, HG_DK), lambda b, t: (b, 0, 0, 0)),
        ],
        out_shape=[
            jax.ShapeDtypeStruct((geo.n_prompt, D_MODEL), BF16),
            jax.ShapeDtypeStruct((geo.batch, HG_HEADS, HG_DV, HG_DK), F32),
        ],
        scratch_shapes=[pltpu.VMEM((HG_HEADS, HG_DV, HG_DK), F32)],
        compiler_params=_cparams(("arbitrary", "arbitrary")),
        name="gla_prompt",
    )(z, lb_table, o_gain)


def _gla_step_kernel(z_ref, s_ref, lbt_ref, og_ref, o_ref, snew_ref, *, steps, bb):
    kw = HG_HEADS * HG_DK
    pad = SUBLANES
    lb = _hg_lower_bound(lbt_ref)
    og = og_ref[...]
    r_i = lax.broadcasted_iota(I32, (pad, pad), 0)
    c_i = lax.broadcasted_iota(I32, (pad, pad), 1)
    causal = c_i <= r_i
    tri = causal.astype(BF16)
    row_w = lax.broadcasted_iota(I32, (pad, kw), 0)
    live = row_w < steps
    row_k = lax.broadcasted_iota(I32, (pad, HG_DK), 0)
    ones_sel = jnp.where((row_k == steps) | (row_k == steps + 1), 1.0, 0.0).astype(BF16)

    def body(s, carry):
        z = z_ref[s]
        logf, kk = _hg_gates(z[:, kw:2 * kw], lb)
        hi, md, lo = _split3(jnp.where(live, logf, 0.0))
        cs = _dot(tri, jnp.concatenate([hi, md, lo], axis=1))
        b = cs[:, :kw] + cs[:, kw:2 * kw] + cs[:, 2 * kw:]
        b_last = b[steps - 1:steps]
        q_in = (_silu(z[:, 0:kw]) * jnp.exp(b)).astype(BF16)
        k_hat = jnp.where(live, kk * jnp.exp(-b), 0.0).astype(BF16)
        k_dec = jnp.where(live, kk * jnp.exp(b_last - b), 0.0).astype(BF16)
        dec = jnp.exp(b_last)
        d_hi = dec.astype(BF16)
        d_lo = (dec - d_hi.astype(F32)).astype(BF16)
        a_all = jnp.where(row_w == steps, d_hi, jnp.where(row_w == steps + 1, d_lo, k_dec))
        for h in range(HG_HEADS):
            cols = slice(h * HG_DK, (h + 1) * HG_DK)
            vcols = slice(2 * kw + h * HG_DV, 2 * kw + (h + 1) * HG_DV)
            gcols = slice(2 * kw + HG_HEADS * HG_DV + h * HG_DV, 2 * kw + HG_HEADS * HG_DV + (h + 1) * HG_DV)
            v = z[:, vcols].astype(BF16)
            s0 = s_ref[s, h]
            att = jnp.where(causal, _dot_nt(q_in[:, cols], k_hat[:, cols]), 0.0).astype(BF16)
            o = _dot(att, v) + _dot(q_in[:, cols], s0.astype(BF16))
            upd = _dot_tn(a_all[:, cols], jnp.concatenate([v, ones_sel], axis=1))
            snew_ref[s, h] = upd[:, HG_DV:] * s0 + upd[:, :HG_DV]
            o_ref[s, :, h * HG_DV:(h + 1) * HG_DV] = _hg_out(o, z[:, gcols], og)
        return carry

    lax.fori_loop(0, bb, body, 0, unroll=min(4, bb))


def _seq_major(geo, rows):
    steps, db = geo.dec_seq, geo.dec_batch
    w = rows.shape[-1]
    r = rows.reshape(steps, db, w).transpose(1, 0, 2)
    return jnp.concatenate([r, jnp.zeros((db, SUBLANES - steps, w), rows.dtype)], axis=1)


def _time_major(geo, r):
    steps, db = geo.dec_seq, geo.dec_batch
    return r[:, :steps].transpose(1, 0, 2).reshape(steps * db, r.shape[-1])


def _gla_step(geo, z, state, lb_table, o_gain):
    steps, db = geo.dec_seq, geo.dec_batch
    assert steps + 2 <= SUBLANES
    bb = min(8, db)
    z_s = _seq_major(geo, z[geo.n_prompt:])
    o_s, s_new = pl.pallas_call(
        functools.partial(_gla_step_kernel, steps=steps, bb=bb),
        grid=(db // bb,),
        in_specs=[
            pl.BlockSpec((bb, SUBLANES, 4 * D_MODEL), lambda i: (i, 0, 0)),
            pl.BlockSpec((bb, HG_HEADS, HG_DK, HG_DV), lambda i: (i, 0, 0, 0)),
            pl.BlockSpec(lb_table.shape, lambda i: (0, 0)),
            pl.BlockSpec((1, HG_DV), lambda i: (0, 0)),
        ],
        out_specs=[
            pl.BlockSpec((bb, SUBLANES, D_MODEL), lambda i: (i, 0, 0)),
            pl.BlockSpec((bb, HG_HEADS, HG_DK, HG_DV), lambda i: (i, 0, 0, 0)),
        ],
        out_shape=[
            jax.ShapeDtypeStruct((db, SUBLANES, D_MODEL), F32),
            jax.ShapeDtypeStruct(state.shape, F32),
        ],
        compiler_params=_cparams(("arbitrary",)),
        name="gla_step",
    )(z_s, state, lb_table, o_gain)
    return _time_major(geo, o_s), s_new


def _post_kernel(ap_ref, as_ref, wo_ref, xp_ref, xs_ref, gp_ref, gs_ref, n2_ref, shp_ref, shs_ref, scp_ref,
                 scs_ref, rw_ref, rb_ref, x1_ref, h2_ref, pos_ref, gt_ref, nch_ref, soff_ref, *, n_pt, tm):
    i = pl.program_id(0)
    g1 = _pick_mod(i, n_pt, gp_ref, gs_ref)
    a = jnp.where(i >= n_pt, as_ref[...], ap_ref[...])
    x = jnp.where(i >= n_pt, xs_ref[...], xp_ref[...])
    x1 = x + g1 * _dot(a, wo_ref[...])
    x1_ref[...] = x1
    sh = _pick_mod(i, n_pt, shp_ref, shs_ref)
    sc = _pick_mod(i, n_pt, scp_ref, scs_ref)
    h2 = _norm_mod(x1, n2_ref[...], sc, sh)
    h2_ref[...] = h2.astype(h2_ref.dtype)

    lane = lax.broadcasted_iota(I32, (tm, LANES), 1)
    logits = _dot(h2.astype(BF16), rw_ref[...]) + rb_ref[...]
    work = jnp.where(lane < N_EXPERTS, logits, -jnp.inf)
    vals, idxs, hits = [], [], []
    for _ in range(TOP_K):
        m = jnp.max(work, axis=-1, keepdims=True)
        idx = jnp.min(jnp.where(work == m, lane, LANES), axis=-1, keepdims=True)
        hit = lane == idx
        vals.append(m)
        idxs.append(idx)
        hits.append(hit)
        work = jnp.where(hit, -jnp.inf, work)
    exps = [jnp.exp(v - vals[0]) for v in vals]
    den = exps[0]
    for e in exps[1:]:
        den = den + e
    any_hit = hits[0]
    for hmask in hits[1:]:
        any_hit = any_hit | hmask
    any_f = jnp.where(any_hit, 1.0, 0.0)
    r_i = lax.broadcasted_iota(I32, (tm, tm), 0)
    c_i = lax.broadcasted_iota(I32, (tm, tm), 1)
    before = (c_i < r_i).astype(BF16)
    rank = _dot(before, any_f.astype(BF16))
    n_chunk = jnp.floor((jnp.sum(any_f, axis=0, keepdims=True) + (ROW_CHUNK - 1)) * (1.0 / ROW_CHUNK))
    e_i = lax.broadcasted_iota(I32, (LANES, LANES), 0)
    e_j = lax.broadcasted_iota(I32, (LANES, LANES), 1)
    earlier = (e_i < e_j).astype(BF16)
    seg = _dot(jnp.broadcast_to(n_chunk, (SUBLANES, LANES)).astype(BF16), earlier)[0:1]
    where_to = seg * float(ROW_CHUNK) + rank
    pos = jnp.zeros((tm, LANES), I32)
    gt = jnp.zeros((tm, LANES), F32)
    for k in range(TOP_K):
        p_k = jnp.sum(jnp.where(hits[k], where_to, 0.0), axis=-1, keepdims=True)
        pos = jnp.where(lane == k, p_k.astype(I32), pos)
        gt = jnp.where(lane == k, exps[k] / den, gt)
    pos_ref[...] = pos
    gt_ref[...] = gt
    nch_ref[0] = n_chunk.astype(I32)
    soff_ref[0] = seg.astype(I32)


def _post(geo, a_p, a_s, w_out_bf16, x_pair, norm_g, mod_p, mod_s, col0, router_w, router_b):
    tm = geo.tm_moe
    n_pt = geo.n_prompt // tm
    n_tiles = geo.nt // tm
    rw = jnp.zeros((D_MODEL, LANES), BF16).at[:, :N_EXPERTS].set(router_w.astype(BF16))
    rb = jnp.zeros((1, LANES), F32).at[0, :N_EXPERTS].set(router_b)
    row = lambda i: (i, 0)
    fixed = lambda i: (0, 0)
    tile_row = lambda i: (i, 0, 0)
    return pl.pallas_call(
        functools.partial(_post_kernel, n_pt=n_pt, tm=tm),
        grid=(n_tiles,),
        in_specs=[
            *_pair_specs(geo, tm, D_MODEL),
            pl.BlockSpec((D_MODEL, D_MODEL), fixed),
            *_pair_specs(geo, tm, D_MODEL),
            *_mod_specs(geo, col0 + 2, tm),
            pl.BlockSpec((1, D_MODEL), fixed),
            *_mod_specs(geo, col0 + 3, tm),
            *_mod_specs(geo, col0 + 4, tm),
            pl.BlockSpec((D_MODEL, LANES), fixed),
            pl.BlockSpec((1, LANES), fixed),
        ],
        out_specs=[
            pl.BlockSpec((tm, D_MODEL), row),
            pl.BlockSpec((tm, D_MODEL), row),
            pl.BlockSpec((tm, LANES), row),
            pl.BlockSpec((tm, LANES), row),
            pl.BlockSpec((1, 1, LANES), tile_row),
            pl.BlockSpec((1, 1, LANES), tile_row),
        ],
        out_shape=[
            jax.ShapeDtypeStruct((geo.nt, D_MODEL), F32),
            jax.ShapeDtypeStruct((geo.nt, D_MODEL), BF16),
            jax.ShapeDtypeStruct((geo.nt, LANES), I32),
            jax.ShapeDtypeStruct((geo.nt, LANES), F32),
            jax.ShapeDtypeStruct((n_tiles, 1, LANES), I32),
            jax.ShapeDtypeStruct((n_tiles, 1, LANES), I32),
        ],
        compiler_params=_cparams(("arbitrary",)),
        name="post_mixer_router",
    )(a_p, a_s.astype(BF16), w_out_bf16, *x_pair, mod_p, mod_s, norm_g, mod_p, mod_s, mod_p, mod_s, rw, rb)


def _ffn_kernel(be_ref, live_ref, nreal_ref, x_ref, win_ref, bin_ref, wout_ref, bout_ref, y_ref,
                win_scr, wout_scr, *, bm):
    del nreal_ref
    b = pl.program_id(0)
    live = live_ref[b]
    half = bm // 2

    def ffn(rows):
        gu = _dot(x_ref[rows, :].astype(BF16), win_scr[...]) + bin_ref[0]
        gate = jnp.minimum(gu[:, :D_MODEL], SWIGLU_LIMIT)
        up = jnp.clip(gu[:, D_MODEL:], -SWIGLU_LIMIT, SWIGLU_LIMIT)
        act = gate * _sigmoid(SWIGLU_ALPHA * gate) * (up + 1.0)
        return _dot(act.astype(BF16), wout_scr[...]) + bout_ref[0]

    @pl.when(live > 0)
    def _():
        changed = (b == 0) | (be_ref[b] != be_ref[jnp.maximum(b - 1, 0)])

        @pl.when(changed)
        def _():
            win_scr[...] = win_ref[0].astype(BF16)
            wout_scr[...] = wout_ref[0].astype(BF16)

    @pl.when(live == 2)
    def _():
        y0 = ffn(slice(0, half))
        y1 = ffn(slice(half, bm))
        y_ref[0:half, :] = y0
        y_ref[half:bm, :] = y1

    @pl.when(live == 1)
    def _():
        y_ref[0:half, :] = ffn(slice(0, half))
        y_ref[half:bm, :] = jnp.zeros((half, D_MODEL), F32)

    @pl.when(live == 0)
    def _():
        y_ref[...] = jnp.zeros_like(y_ref)


def _ffn(xs, block_e, live, n_real, w_in, b_in, w_out, b_out, bm):
    n_blocks = xs.shape[0] // bm
    n_e, d, d2 = w_in.shape
    return pl.pallas_call(
        functools.partial(_ffn_kernel, bm=bm),
        grid_spec=pltpu.PrefetchScalarGridSpec(
            num_scalar_prefetch=3,
            grid=(n_blocks,),
            in_specs=[
                pl.BlockSpec((bm, d), lambda b, be, lv, nr: (jnp.minimum(b, nr[0] - 1), 0)),
                pl.BlockSpec((1, d, d2), lambda b, be, lv, nr: (be[b], 0, 0)),
                pl.BlockSpec((1, 1, d2), lambda b, be, lv, nr: (be[b], 0, 0)),
                pl.BlockSpec((1, d2 // 2, d), lambda b, be, lv, nr: (be[b], 0, 0)),
                pl.BlockSpec((1, 1, d), lambda b, be, lv, nr: (be[b], 0, 0)),
            ],
            out_specs=pl.BlockSpec((bm, d), lambda b, be, lv, nr: (b, 0)),
            scratch_shapes=[
                pltpu.VMEM((d, d2), BF16),
                pltpu.VMEM((d2 // 2, d), BF16),
            ],
        ),
        out_shape=jax.ShapeDtypeStruct((n_blocks * bm, d), F32),
        compiler_params=_cparams(("arbitrary",)),
        name="moe_ffn",
    )(block_e, live, n_real, xs, w_in, b_in.reshape(n_e, 1, d2), w_out, b_out.reshape(n_e, 1, d))


def _chunk_rows(chunk, n_chunks=1):
    return pl.ds(pl.multiple_of(chunk * ROW_CHUNK, ROW_CHUNK), n_chunks * ROW_CHUNK)


def _run_pieces(n, fn, big):
    n_big = n // big

    def big_piece(i, carry):
        fn(i * big, big)
        return carry

    def small_piece(i, carry):
        fn(n_big * big + i, 1)
        return carry

    lax.fori_loop(0, n_big, big_piece, 0)
    lax.fori_loop(0, n - n_big * big, small_piece, 0)


def _for_each_piece(nch_ref, soff_ref, dst_ref, tile, fn):
    def per_expert(e, carry):
        s0 = soff_ref[tile, e]
        d0 = dst_ref[tile, e]
        _run_pieces(nch_ref[tile, e], lambda off, size: fn(s0 + off, d0 + off, size), COPY_CHUNKS)
        return carry

    lax.fori_loop(0, N_EXPERTS, per_expert, 0)


def _dispatch_kernel(nch_ref, soff_ref, dst_ref, tot_ref, fill0_ref, filln_ref,
                     pos_ref, h_ref, xs_hbm, stage, zeros, sem, fill_sem, *, tm, k_stage):
    t = pl.program_id(0)
    n_t = pl.num_programs(0)
    slot = t % 2

    def copy(s_chunk, d_chunk, size, sl):
        return pltpu.make_async_copy(stage.at[sl, _chunk_rows(s_chunk, size)],
                                     xs_hbm.at[_chunk_rows(d_chunk, size)], sem.at[sl])

    def wait_tile(tile, sl):
        _run_pieces(tot_ref[tile], lambda off, size: copy(0, 0, size, sl).wait(), WAIT_CHUNKS)

    @pl.when(t >= 2)
    def _():
        wait_tile(t - 2, slot)

    pos_t = pos_ref[...].astype(F32).T.astype(I32)
    j_i = lax.broadcasted_iota(I32, (k_stage, tm), 0)
    hit = j_i == pos_t[0:1]
    for k in range(1, TOP_K):
        hit = hit | (j_i == pos_t[k:k + 1])
    stage[slot] = _dot(jnp.where(hit, 1.0, 0.0).astype(BF16), h_ref[...])
    _for_each_piece(nch_ref, soff_ref, dst_ref, t, lambda s, d, size: copy(s, d, size, slot).start())

    @pl.when(t == n_t - 1)
    def _():
        zeros[...] = jnp.zeros_like(zeros)

        def fill(d_chunk, size):
            return pltpu.make_async_copy(zeros.at[_chunk_rows(0, size)], xs_hbm.at[_chunk_rows(d_chunk, size)],
                                         fill_sem.at[0])

        def per_region(r, carry):
            _run_pieces(filln_ref[r], lambda off, size: fill(fill0_ref[r] + off, size).start(), COPY_CHUNKS)
            return carry

        lax.fori_loop(0, N_EXPERTS + 1, per_region, 0)

        @pl.when(t >= 1)
        def _():
            wait_tile(t - 1, 1 - slot)

        wait_tile(t, slot)

        def per_region_wait(r, carry):
            _run_pieces(filln_ref[r], lambda off, size: fill(0, size).wait(), COPY_CHUNKS)
            return carry

        lax.fori_loop(0, N_EXPERTS + 1, per_region_wait, 0)


def _combine_kernel(nch_ref, soff_ref, dst_ref, tot_ref, pos_ref, gate_ref, x_ref, gp_ref, gs_ref, y_hbm,
                    op_ref, os_ref, stage, sem, *, n_pt, tm, k_stage):
    t = pl.program_id(0)
    n_t = pl.num_programs(0)
    slot = t % 2

    def copy(s_chunk, d_chunk, size, sl):
        return pltpu.make_async_copy(y_hbm.at[_chunk_rows(d_chunk, size)],
                                     stage.at[sl, _chunk_rows(s_chunk, size)], sem.at[sl])

    def fetch(tile, sl):
        _for_each_piece(nch_ref, soff_ref, dst_ref, tile, lambda s, d, size: copy(s, d, size, sl).start())

    @pl.when(t == 0)
    def _():
        stage[...] = jnp.zeros_like(stage)
        fetch(0, 0)

    @pl.when(t + 1 < n_t)
    def _():
        fetch(t + 1, 1 - slot)

    _run_pieces(tot_ref[t], lambda off, size: copy(0, 0, size, slot).wait(), WAIT_CHUNKS)
    lane = lax.broadcasted_iota(I32, (tm, k_stage), 1)
    pos = pos_ref[...]
    g = gate_ref[...]
    p = jnp.zeros((tm, k_stage), F32)
    for k in range(TOP_K):
        p = jnp.where(lane == pos[:, k:k + 1], g[:, k:k + 1], p)
    ffn = _dot(p.astype(BF16), stage[slot].astype(BF16))
    out = x_ref[...] + _pick_mod(t, n_pt, gp_ref, gs_ref) * ffn

    @pl.when(t < n_pt)
    def _():
        op_ref[...] = out

    @pl.when(t >= n_pt)
    def _():
        os_ref[...] = out


def _moe(geo, h2, pos, gates, nch_pad, soff_pad, x1, mod_p, mod_s, col_gate, layer, w_in, b_in, w_out, b_out):
    bm = MOE_BLOCK
    tm = geo.tm_moe
    n_tiles = geo.nt // tm
    depth, n_e, d, d2 = w_in.shape
    w_in = w_in.reshape(depth * n_e, d, d2)
    b_in = b_in.reshape(depth * n_e, d2)
    w_out = w_out.reshape(depth * n_e, d2 // 2, d)
    b_out = b_out.reshape(depth * n_e, d)
    chunks_per_block = bm // ROW_CHUNK
    max_rows = geo.nt * TOP_K + n_tiles * N_EXPERTS * (ROW_CHUNK - 1) + N_EXPERTS * (bm - 1)
    n_blocks = -(-max_rows // bm)
    nch = nch_pad[:, 0, :N_EXPERTS]
    soff = soff_pad[:, 0, :N_EXPERTS]
    tot = jnp.sum(nch, axis=1)
    per_e = jnp.sum(nch, axis=0)
    padded = (per_e + chunks_per_block - 1) // chunks_per_block * chunks_per_block
    pend = jnp.cumsum(padded)
    pstart = pend - padded
    dst = pstart[None, :] + jnp.cumsum(nch, axis=0) - nch
    block_start = jnp.arange(n_blocks, dtype=I32) * chunks_per_block
    block_e = jnp.minimum(jnp.sum(block_start[:, None] >= pend[None, :], axis=1), N_EXPERTS - 1).astype(I32)
    n_real = (pend[-1:] // chunks_per_block).astype(I32)
    used = jnp.clip((pstart + per_e)[block_e] - block_start, 0, chunks_per_block)
    used = jnp.where(block_start < pend[-1], used, 0)
    live = ((used > 0).astype(I32) + (used > chunks_per_block // 2).astype(I32))
    fill0 = jnp.concatenate([pstart + per_e, pend[-1:]]).astype(I32)
    filln = jnp.concatenate([padded - per_e, n_blocks * chunks_per_block - pend[-1:]]).astype(I32)

    k_stage = -(-(tm * TOP_K + N_EXPERTS * (ROW_CHUNK - 1)) // LANES) * LANES
    row = lambda i, *_: (i, 0)
    xs = pl.pallas_call(
        functools.partial(_dispatch_kernel, tm=tm, k_stage=k_stage),
        grid_spec=pltpu.PrefetchScalarGridSpec(
            num_scalar_prefetch=6,
            grid=(n_tiles,),
            in_specs=[pl.BlockSpec((tm, LANES), row), pl.BlockSpec((tm, d), row)],
            out_specs=pl.BlockSpec(memory_space=pl.ANY),
            scratch_shapes=[
                pltpu.VMEM((2, k_stage, d), F32),
                pltpu.VMEM((COPY_CHUNKS * ROW_CHUNK, d), F32),
                pltpu.SemaphoreType.DMA((2,)),
                pltpu.SemaphoreType.DMA((1,)),
            ],
        ),
        out_shape=jax.ShapeDtypeStruct((n_blocks * bm, d), F32),
        compiler_params=_cparams(("arbitrary",)),
        name="moe_dispatch",
    )(nch, soff, dst, tot, fill0, filln, pos, h2)

    ys = _ffn(xs, block_e + layer * n_e, live, n_real, w_in, b_in, w_out, b_out, bm)

    return pl.pallas_call(
        functools.partial(_combine_kernel, n_pt=geo.n_prompt // tm, tm=tm, k_stage=k_stage),
        grid_spec=pltpu.PrefetchScalarGridSpec(
            num_scalar_prefetch=4,
            grid=(n_tiles,),
            in_specs=[
                pl.BlockSpec((tm, LANES), row),
                pl.BlockSpec((tm, LANES), row),
                pl.BlockSpec((tm, d), row),
                *_mod_specs(geo, col_gate, tm),
                pl.BlockSpec(memory_space=pl.ANY),
            ],
            out_specs=_pair_specs(geo, tm, d),
            scratch_shapes=[
                pltpu.VMEM((2, k_stage, d), F32),
                pltpu.SemaphoreType.DMA((2,)),
            ],
        ),
        out_shape=[jax.ShapeDtypeStruct((geo.n_prompt, d), F32), jax.ShapeDtypeStruct((geo.n_sample, d), F32)],
        compiler_params=_cparams(("arbitrary",)),
        name="moe_combine",
    )(nch, soff, dst, tot, pos, gates, x1, mod_p, mod_s, ys)


SW_QW = SW_Q_HEADS * SW_HEAD_DIM
SW_QKW = SW_QW + SW_KV_W
SW_KEYS = 2 * ATT_BLOCK


def _attn_proj_kernel(xp_ref, xs_ref, g_ref, shp_ref, shs_ref, scp_ref, scs_ref, w_ref, e_ref, et_ref, qg_ref,
                      kg_ref, q_ref, k_ref, v_ref, *, n_pt):
    i = pl.program_id(0)
    sh = _pick_mod(i, n_pt, shp_ref, shs_ref)
    sc = _pick_mod(i, n_pt, scp_ref, scs_ref)
    x = jnp.where(i >= n_pt, xs_ref[...], xp_ref[...])
    h = _norm_mod(x, g_ref[...], sc, sh).astype(BF16)
    z = _dot(h, w_ref[...])
    qk = z[:, :SW_QKW]
    sq = qk * qk
    sq_hi = sq.astype(BF16)
    sq_lo = (sq - sq_hi.astype(F32)).astype(BF16)
    ms = (_dot(sq_hi, e_ref[...]) + _dot(sq_lo, e_ref[...])) * (1.0 / SW_HEAD_DIM)
    inv = lax.rsqrt(ms + NORM_EPS)
    inv_hi = inv.astype(BF16)
    inv_lo = (inv - inv_hi.astype(F32)).astype(BF16)
    qk = qk * (_dot(inv_hi, et_ref[...]) + _dot(inv_lo, et_ref[...]))
    q_ref[...] = (qk[:, :SW_QW] * qg_ref[...] * SW_SCALE).astype(q_ref.dtype)
    k_ref[...] = qk[:, SW_QW:] * kg_ref[...]
    v_ref[...] = z[:, SW_QKW:]


def _attn_proj(geo, x_pair, g, mod_p, mod_s, w_bf16, q_gain, k_gain):
    tm = geo.tm
    n_out = w_bf16.shape[1]
    heads = SW_QKW // SW_HEAD_DIM
    member = (np.arange(SW_QKW)[:, None] // SW_HEAD_DIM == np.arange(LANES)[None, :]).astype(np.float32)
    e = jnp.asarray(member, BF16)
    et = jnp.asarray(member.T, BF16)
    assert heads <= LANES
    qg = jnp.tile(q_gain, SW_Q_HEADS).reshape(1, SW_QW)
    kg = jnp.tile(k_gain, SW_KV_HEADS).reshape(1, SW_KV_W)
    row = lambda i: (i, 0)
    fixed = lambda i: (0, 0)
    return pl.pallas_call(
        functools.partial(_attn_proj_kernel, n_pt=geo.n_pt),
        grid=(geo.n_tiles,),
        in_specs=[
            *_pair_specs(geo, tm, D_MODEL),
            pl.BlockSpec((1, D_MODEL), fixed),
            *_mod_specs(geo, 0, tm),
            *_mod_specs(geo, 1, tm),
            pl.BlockSpec((D_MODEL, n_out), fixed),
            pl.BlockSpec((SW_QKW, LANES), fixed),
            pl.BlockSpec((LANES, SW_QKW), fixed),
            pl.BlockSpec((1, SW_QW), fixed),
            pl.BlockSpec((1, SW_KV_W), fixed),
        ],
        out_specs=[
            pl.BlockSpec((tm, SW_QW), row),
            pl.BlockSpec((tm, SW_KV_W), row),
            pl.BlockSpec((tm, SW_KV_W), row),
        ],
        out_shape=[
            jax.ShapeDtypeStruct((geo.nt, SW_QW), BF16),
            jax.ShapeDtypeStruct((geo.nt, SW_KV_W), F32),
            jax.ShapeDtypeStruct((geo.nt, SW_KV_W), F32),
        ],
        compiler_params=_cparams(("arbitrary",)),
        name="attn_proj_qknorm",
    )(*x_pair, g, mod_p, mod_s, mod_p, mod_s, w_bf16, e, et, qg, kg)


def _rel_bucket_np(dist):
    n = np.maximum(dist, 0)
    max_exact = REL_BUCKETS // 2
    ratio = np.log(np.maximum(n, 1).astype(np.float32) / np.float32(max_exact)) / np.float32(
        math.log(REL_MAX_DIST / max_exact))
    large = max_exact + (ratio * np.float32(REL_BUCKETS - max_exact)).astype(np.int32)
    large = np.minimum(large, REL_BUCKETS - 1)
    return np.where(n < max_exact, n, large).astype(np.int32)


def _bucket_table(qpos, kpos, k_ok):
    dist = qpos[:, None] - kpos[None, :]
    ok = (dist >= 0) & (dist <= WINDOW) & k_ok[None, :]
    return np.where(ok, _rel_bucket_np(dist), -1).astype(np.int32)


def _build_bias(bkt_ref, rb_ref, bias_scr, rows):
    bkt = bkt_ref[...]
    base = jnp.where(bkt < 0, -jnp.inf, 0.0)
    for h in range(SW_Q_HEADS):
        bias_scr[h * rows:(h + 1) * rows, :] = base

    def add_bucket(j, carry):
        hit = bkt == j
        for h in range(SW_Q_HEADS):
            sl = slice(h * rows, (h + 1) * rows)
            bias_scr[sl, :] = bias_scr[sl, :] + jnp.where(hit, rb_ref[j, h], 0.0)
        return carry

    lax.fori_loop(0, REL_BUCKETS, add_bucket, 0)


def _sink_softmax_pv(s, sink, vv_g):
    m = jnp.maximum(jnp.max(s, axis=-1, keepdims=True), sink)
    e = jnp.exp(s - m)
    p = e / (jnp.sum(e, axis=-1, keepdims=True) + jnp.exp(sink - m))
    return _dot(p.astype(BF16), vv_g)


def _build_bias_t(bkt_ref, rb_ref, bias_scr, blk):
    bkt = bkt_ref[...]
    base = jnp.where(bkt < 0, -jnp.inf, 0.0)
    slots = [(g, j) for g in range(SW_KV_HEADS) for j in range(SW_GROUP)]
    for g, j in slots:
        bias_scr[g, :, j * blk:(j + 1) * blk] = base

    def add_bucket(b, carry):
        hit = bkt == b
        for g, j in slots:
            cols = slice(j * blk, (j + 1) * blk)
            bias_scr[g, :, cols] = bias_scr[g, :, cols] + jnp.where(hit, rb_ref[b, g * SW_GROUP + j], 0.0)
        return carry

    lax.fori_loop(0, REL_BUCKETS, add_bucket, 0)


def _swa_prompt_kernel(rb_ref, sink_ref, bkt_ref, q_ref, kp_ref, kc_ref, vp_ref, vc_ref, o_ref, bias_scr):
    n = pl.program_id(1)
    blk = ATT_BLOCK
    hd = SW_HEAD_DIM

    @pl.when((pl.program_id(0) == 0) & (n == 0))
    def _():
        _build_bias_t(bkt_ref, rb_ref, bias_scr, blk)

    kk = jnp.concatenate([kp_ref[...], kc_ref[...]], axis=0).astype(BF16)
    vv_t = jnp.concatenate([vp_ref[...], vc_ref[...]], axis=0).T.astype(BF16)
    q_t = q_ref[...].astype(F32).T.astype(BF16)
    key_i = lax.broadcasted_iota(I32, (SW_KEYS, SW_GROUP * blk), 0)
    hide_prev = (n == 0) & (key_i < blk)
    zeros_q = jnp.zeros((hd, SW_GROUP * blk), BF16)
    outs = []
    for g in range(SW_KV_HEADS):
        heads = [g * SW_GROUP + j for j in range(SW_GROUP)]
        q_cat = jnp.concatenate([q_t[hq * hd:(hq + 1) * hd, :] for hq in heads], axis=1)
        q_full = jnp.concatenate([q_cat if gg == g else zeros_q for gg in range(SW_KV_HEADS)], axis=0)
        s = _dot(kk, q_full) + bias_scr[g]
        s = jnp.where(hide_prev, -jnp.inf, s)
        sink = jnp.concatenate([jnp.full((1, blk), sink_ref[hq], F32) for hq in heads], axis=1)
        m = jnp.maximum(jnp.max(s, axis=0, keepdims=True), sink)
        e = jnp.exp(s - m)
        den = jnp.sum(e, axis=0, keepdims=True) + jnp.exp(sink - m)
        o_t = _dot(vv_t[g * hd:(g + 1) * hd, :], e.astype(BF16)) * (1.0 / den)
        outs.extend(o_t[:, j * blk:(j + 1) * blk] for j in range(SW_GROUP))
    o_ref[...] = jnp.concatenate(outs, axis=0).T.astype(o_ref.dtype)


def _swa_prompt(geo, q, k, v, rel_bias, sinks):
    blk = ATT_BLOCK
    assert geo.seq % blk == 0 and WINDOW == blk
    nb = geo.seq // blk
    bkt = jnp.asarray(_bucket_table(blk + np.arange(blk), np.arange(2 * blk), np.ones(2 * blk, bool)).T.copy())
    cur = lambda b, n: (b * nb + n, 0)
    prev = lambda b, n: (b * nb + jnp.maximum(n - 1, 0), 0)
    smem = pl.BlockSpec(memory_space=pltpu.SMEM)
    return pl.pallas_call(
        _swa_prompt_kernel,
        grid=(geo.batch, nb),
        in_specs=[
            smem, smem,
            pl.BlockSpec((SW_KEYS, blk), lambda b, n: (0, 0)),
            pl.BlockSpec((blk, SW_QW), cur),
            pl.BlockSpec((blk, SW_KV_W), prev),
            pl.BlockSpec((blk, SW_KV_W), cur),
            pl.BlockSpec((blk, SW_KV_W), prev),
            pl.BlockSpec((blk, SW_KV_W), cur),
        ],
        out_specs=pl.BlockSpec((blk, SW_QW), cur),
        out_shape=jax.ShapeDtypeStruct((geo.n_prompt, SW_QW), BF16),
        scratch_shapes=[pltpu.VMEM((SW_KV_HEADS, SW_KEYS, SW_GROUP * blk), F32)],
        compiler_params=_cparams(("arbitrary", "arbitrary")),
        name="swa_prompt",
    )(rel_bias, sinks, bkt, q, k, k, v, v)


def _swa_step_kernel(rb_ref, sink_ref, bkt_ref, q_ref, kn_ref, vn_ref, ck_ref, cv_ref,
                     o_ref, nk_ref, nv_ref, bias_scr, *, steps, bb):
    pad = SUBLANES
    win = WINDOW

    @pl.when(pl.program_id(0) == 0)
    def _():
        _build_bias(bkt_ref, rb_ref, bias_scr, pad)

    fill = jnp.zeros((SW_KEYS - win - pad, SW_KV_W), F32)
    row8 = lax.broadcasted_iota(I32, (pad, SW_KV_W), 0)

    def shifted(cache, new):
        rolled = pltpu.roll(cache, win - steps, 0)
        tail = jnp.where(row8 < pad - steps, rolled[win - pad:], pltpu.roll(new, pad - steps, 0))
        return rolled[:win - pad], tail

    def body(s, carry):
        ck, cv, kn, vn = ck_ref[s], cv_ref[s], kn_ref[s], vn_ref[s]
        kk = jnp.concatenate([ck, kn, fill], axis=0).astype(BF16)
        vv = jnp.concatenate([cv, vn, fill], axis=0).astype(BF16)
        q = q_ref[s]
        for g in range(SW_KV_HEADS):
            kcols = slice(g * SW_HEAD_DIM, (g + 1) * SW_HEAD_DIM)
            heads = [g * SW_GROUP + j for j in range(SW_GROUP)]
            qs = jnp.concatenate([q[:, hq * SW_HEAD_DIM:(hq + 1) * SW_HEAD_DIM] for hq in heads], axis=0)
            sc = _dot_nt(qs.astype(BF16), kk[:, kcols]) + bias_scr[heads[0] * pad:(heads[-1] + 1) * pad, :]
            sink = jnp.concatenate([jnp.full((pad, 1), sink_ref[hq], F32) for hq in heads], axis=0)
            og = _sink_softmax_pv(sc, sink, vv[:, kcols])
            for j, hq in enumerate(heads):
                o_ref[s, :, hq * SW_HEAD_DIM:(hq + 1) * SW_HEAD_DIM] = og[j * pad:(j + 1) * pad]
        head, tail = shifted(ck, kn)
        nk_ref[s, 0:win - pad] = head
        nk_ref[s, win - pad:win] = tail
        head, tail = shifted(cv, vn)
        nv_ref[s, 0:win - pad] = head
        nv_ref[s, win - pad:win] = tail
        return carry

    lax.fori_loop(0, bb, body, 0, unroll=min(4, bb))


def _swa_step(geo, q, k, v, cache_k, cache_v, rel_bias, sinks):
    steps, db = geo.dec_seq, geo.dec_batch
    win = cache_k.shape[1]
    assert win == WINDOW and steps <= SUBLANES
    bb = min(8, db)
    pad = SUBLANES
    q_s = _seq_major(geo, q[geo.n_prompt:].astype(F32))
    k_s = _seq_major(geo, k[geo.n_prompt:])
    v_s = _seq_major(geo, v[geo.n_prompt:])
    kpos = np.arange(SW_KEYS)
    k_ok = kpos < win + steps
    qpos = win + np.arange(pad)
    bkt = _bucket_table(qpos, kpos, k_ok)
    bkt[steps:] = -1
    smem = pl.BlockSpec(memory_space=pltpu.SMEM)
    blk3 = lambda r, w: pl.BlockSpec((bb, r, w), lambda i: (i, 0, 0))
    o_s, nk, nv = pl.pallas_call(
        functools.partial(_swa_step_kernel, steps=steps, bb=bb),
        grid=(db // bb,),
        in_specs=[
            smem, smem,
            pl.BlockSpec((pad, SW_KEYS), lambda i: (0, 0)),
            blk3(pad, SW_QW), blk3(pad, SW_KV_W), blk3(pad, SW_KV_W),
            blk3(win, SW_KV_W), blk3(win, SW_KV_W),
        ],
        out_specs=[blk3(pad, SW_QW), blk3(win, SW_KV_W), blk3(win, SW_KV_W)],
        out_shape=[
            jax.ShapeDtypeStruct((db, pad, SW_QW), F32),
            jax.ShapeDtypeStruct((db, win, SW_KV_W), F32),
            jax.ShapeDtypeStruct((db, win, SW_KV_W), F32),
        ],
        scratch_shapes=[pltpu.VMEM((SW_Q_HEADS * pad, SW_KEYS), F32)],
        compiler_params=_cparams(("arbitrary",)),
        name="swa_step",
    )(rel_bias, sinks, jnp.asarray(bkt), q_s, k_s, v_s, cache_k, cache_v)
    return _time_major(geo, o_s), nk, nv


def kernel(x_prompt, x_sample, state_hgrn, cache_win_k, cache_win_v, c_prompt, c_sample, norm1_g, norm2_g, ada_w, ada_b, hg_w_in, hg_lb_table, hg_onorm_g, hg_w_out, sw_w_in, sw_qnorm_g, sw_knorm_g, sw_sinks, sw_w_out, rel_bias, router_w, router_b, moe_w_in, moe_b_in, moe_w_out, moe_b_out):
    batch, seq, d = x_prompt.shape
    db, steps, _ = x_sample.shape
    assert d == D_MODEL and ada_w.shape[0] == 2 and hg_w_in.shape[0] == 1 and sw_w_in.shape[0] == 1
    geo = _Geom(batch, seq, db, steps)
    x = (x_prompt.reshape(batch * seq, d), x_sample.transpose(1, 0, 2).reshape(steps * db, d))
    n_seq = batch + db
    rows = -(-n_seq // SUBLANES) * SUBLANES
    c_all = jnp.concatenate([c_prompt, c_sample, jnp.zeros((rows - n_seq, d), F32)], axis=0)
    mods = _ada_mods(c_all, ada_w, ada_b)

    def layer_mods(layer):
        mod_p = mods[layer, :batch].reshape(batch, 1, 6 * d)
        mod_s = jnp.tile(mods[layer, batch:n_seq], (steps, 1))
        return mod_p, mod_s

    def moe(layer, a_p, a_s, w_out, x_in, mod_p, mod_s):
        x1, h2, pos, gt, nch, soff = _post(geo, a_p, a_s, w_out.astype(BF16), x_in, norm2_g[layer:layer + 1],
                                           mod_p, mod_s, 0, router_w[layer], router_b[layer])
        return _moe(geo, h2, pos, gt, nch, soff, x1, mod_p, mod_s, 5,
                    layer, moe_w_in, moe_b_in, moe_w_out, moe_b_out)

    mod_p, mod_s = layer_mods(0)
    og = hg_onorm_g[0:1]
    z = _proj(geo, x, norm1_g[0:1], mod_p, mod_s, hg_w_in[0].astype(BF16), 0, 1)
    o_p, st_p = _gla_prompt(geo, z, hg_lb_table, og)
    o_s, st_s = _gla_step(geo, z, state_hgrn[0], hg_lb_table, og)
    x = moe(0, o_p, o_s, hg_w_out[0], x, mod_p, mod_s)

    mod_p, mod_s = layer_mods(1)
    q, k, v = _attn_proj(geo, x, norm1_g[1:2], mod_p, mod_s, sw_w_in[0].astype(BF16), sw_qnorm_g[0], sw_knorm_g[0])
    win = cache_win_k.shape[2]
    a_p = _swa_prompt(geo, q, k, v, rel_bias, sw_sinks[0])
    a_s, nk, nv = _swa_step(geo, q, k, v, cache_win_k[0].reshape(db, win, SW_KV_W),
                            cache_win_v[0].reshape(db, win, SW_KV_W), rel_bias, sw_sinks[0])
    x = moe(1, a_p, a_s, sw_w_out[0], x, mod_p, mod_s)

    y_prompt = x[0].reshape(batch, seq, d)
    y_sample = x[1].reshape(steps, db, d).transpose(1, 0, 2)
    kv_shape = (1, batch, WINDOW, SW_KV_HEADS, SW_HEAD_DIM)
    k_p = k[:geo.n_prompt].reshape(batch, seq, SW_KV_W)[:, seq - WINDOW:].reshape(kv_shape)
    v_p = v[:geo.n_prompt].reshape(batch, seq, SW_KV_W)[:, seq - WINDOW:].reshape(kv_shape)
    cache_shape = (1, db, win, SW_KV_HEADS, SW_HEAD_DIM)
    return (y_prompt, y_sample, jnp.swapaxes(st_p, -1, -2)[None], st_s[None], k_p, v_p,
            nk.reshape(cache_shape), nv.reshape(cache_shape))
```

```python
import functools
import math

import numpy as np
import jax
import jax.numpy as jnp
from jax import lax
from jax.experimental import pallas as pl
from jax.experimental.pallas import tpu as pltpu

F32 = jnp.float32
BF16 = jnp.bfloat16
I32 = jnp.int32

D_MODEL = 1024
LANES = 128
SUBLANES = 8
D_TILES = D_MODEL // LANES
HG_DK = 128
HG_HEADS = D_MODEL // HG_DK
HG_DV = D_MODEL // HG_HEADS
HG_CHUNK = 64
SW_HEAD_DIM = 64
SW_Q_HEADS = D_MODEL // SW_HEAD_DIM
SW_KV_HEADS = 4
SW_GROUP = SW_Q_HEADS // SW_KV_HEADS
SW_KV_W = SW_KV_HEADS * SW_HEAD_DIM
WINDOW = 128
ATT_BLOCK = 128
SW_SCALE = SW_HEAD_DIM ** -0.5
REL_BUCKETS = 32
REL_MAX_DIST = 128
N_EXPERTS = 32
TOP_K = 4
SWIGLU_LIMIT = 7.0
SWIGLU_ALPHA = 1.702
NORM_EPS = 1e-5
MOE_BLOCK = 256
ROW_CHUNK = SUBLANES
COPY_CHUNKS = 4
WAIT_CHUNKS = 16
VMEM_LIMIT = 56 * 1024 * 1024


def _cparams(sem):
    return pltpu.CompilerParams(dimension_semantics=sem, vmem_limit_bytes=VMEM_LIMIT)


def _sigmoid(x):
    return 0.5 * jnp.tanh(0.5 * x) + 0.5


def _silu(x):
    return x * _sigmoid(x)


def _dot(a, b):
    return jnp.dot(a, b, preferred_element_type=F32)


def _dot_nt(a, b):
    return lax.dot_general(a, b, (((1,), (1,)), ((), ())), preferred_element_type=F32)


def _dot_tn(a, b):
    return lax.dot_general(a, b, (((0,), (0,)), ((), ())), preferred_element_type=F32)


def _split3(x):
    hi = x.astype(BF16)
    r = x - hi.astype(F32)
    mid = r.astype(BF16)
    lo = (r - mid.astype(F32)).astype(BF16)
    return hi, mid, lo


def _ada_kernel(c_ref, w_ref, b_ref, o_ref):
    s = _silu(c_ref[...]).astype(BF16)
    o_ref[0] = _dot(s, w_ref[0].astype(BF16)) + b_ref[0]


def _ada_mods(c_all, ada_w, ada_b):
    depth, d, n6 = ada_w.shape
    rows = c_all.shape[0]
    tn = 1536
    return pl.pallas_call(
        _ada_kernel,
        grid=(depth, n6 // tn),
        in_specs=[
            pl.BlockSpec((rows, d), lambda l, j: (0, 0)),
            pl.BlockSpec((1, d, tn), lambda l, j: (l, 0, j)),
            pl.BlockSpec((1, 1, tn), lambda l, j: (l, 0, j)),
        ],
        out_specs=pl.BlockSpec((1, rows, tn), lambda l, j: (l, 0, j)),
        out_shape=jax.ShapeDtypeStruct((depth, rows, n6), F32),
        compiler_params=_cparams(("arbitrary", "arbitrary")),
        name="ada_mods",
    )(c_all, ada_w, ada_b.reshape(depth, 1, n6))


class _Geom:
    def __init__(self, batch, seq, dec_batch, dec_seq):
        self.batch, self.seq, self.dec_batch, self.dec_seq = batch, seq, dec_batch, dec_seq
        self.n_prompt = batch * seq
        self.n_sample = dec_batch * dec_seq
        self.nt = self.n_prompt + self.n_sample
        tm = 512
        while seq % tm or self.n_sample % tm:
            tm //= 2
        assert tm >= 8
        self.tm = tm
        self.n_pt = self.n_prompt // tm
        self.n_tiles = self.nt // tm
        self.tm_moe = min(256, tm)


def _mod_specs(geo, col, tm):
    seq, batch = geo.seq, geo.batch
    n_pt = geo.n_prompt // tm

    def p_map(i, *_):
        return (jnp.minimum(i * tm // seq, batch - 1), 0, col)

    def s_map(i, *_):
        return (jnp.maximum(i - n_pt, 0), col)

    return [pl.BlockSpec((1, 1, D_MODEL), p_map), pl.BlockSpec((tm, D_MODEL), s_map)]


def _pick_mod(i, n_pt, p_ref, s_ref):
    return jnp.where(i >= n_pt, s_ref[...], p_ref[0])


def _norm_mod(x, g, sc, sh):
    ms = jnp.mean(x * x, axis=-1, keepdims=True)
    return x * lax.rsqrt(ms + NORM_EPS) * g * (1.0 + sc) + sh


def _pair_specs(geo, tm, width):
    n_pt = geo.n_prompt // tm
    return [pl.BlockSpec((tm, width), lambda i, *_: (jnp.minimum(i, n_pt - 1), 0)),
            pl.BlockSpec((tm, width), lambda i, *_: (jnp.maximum(i - n_pt, 0), 0))]


def _proj_kernel(xp_ref, xs_ref, g_ref, shp_ref, shs_ref, scp_ref, scs_ref, w_ref, o_ref, *, n_pt, tn):
    i = pl.program_id(0)
    sh = _pick_mod(i, n_pt, shp_ref, shs_ref)
    sc = _pick_mod(i, n_pt, scp_ref, scs_ref)
    x = jnp.where(i >= n_pt, xs_ref[...], xp_ref[...])
    h = _norm_mod(x, g_ref[...], sc, sh).astype(BF16)
    for j in range(o_ref.shape[1] // tn):
        o_ref[:, j * tn:(j + 1) * tn] = _dot(h, w_ref[:, j * tn:(j + 1) * tn])


def _proj(geo, x_pair, g, mod_p, mod_s, w_bf16, col_shift, col_scale):
    n_out = w_bf16.shape[1]
    tm = geo.tm
    fixed = lambda i: (0, 0)
    return pl.pallas_call(
        functools.partial(_proj_kernel, n_pt=geo.n_pt, tn=1024),
        grid=(geo.n_tiles,),
        in_specs=[
            *_pair_specs(geo, tm, D_MODEL),
            pl.BlockSpec((1, D_MODEL), fixed),
            *_mod_specs(geo, col_shift, tm),
            *_mod_specs(geo, col_scale, tm),
            pl.BlockSpec((D_MODEL, n_out), fixed),
        ],
        out_specs=pl.BlockSpec((tm, n_out), lambda i: (i, 0)),
        out_shape=jax.ShapeDtypeStruct((geo.nt, n_out), F32),
        compiler_params=_cparams(("arbitrary",)),
        name="norm_mod_proj",
    )(*x_pair, g, mod_p, mod_s, mod_p, mod_s, w_bf16)


def _hg_lower_bound(lbt_ref):
    t = lbt_ref[...]
    e = jnp.exp(t - jnp.max(t, axis=0, keepdims=True))
    return e[0:1] / jnp.sum(e, axis=0, keepdims=True)


def _hg_gates(fz, lb):
    e = jnp.exp(-jnp.abs(fz))
    inv = 1.0 / (1.0 + e)
    pos = fz >= 0
    sig = jnp.where(pos, inv, e * inv)
    sig_neg = jnp.where(pos, e * inv, inv)
    logf = jnp.log(lb + (1.0 - lb) * sig)
    return logf, (1.0 - lb) * sig_neg


def _hg_out(o, gz, og):
    ms = jnp.mean(o * o, axis=-1, keepdims=True)
    return o * lax.rsqrt(ms + NORM_EPS) * og * _silu(gz)


def _gla_prompt_kernel(z_ref, lbt_ref, og_ref, o_ref, sfin_ref, st_scr, *, chunk, n_chunks):
    t_step = pl.program_id(1)
    kw = HG_HEADS * HG_DK

    @pl.when(t_step == 0)
    def _():
        st_scr[...] = jnp.zeros_like(st_scr)

    lb = _hg_lower_bound(lbt_ref)
    og = og_ref[...]
    r_i = lax.broadcasted_iota(I32, (chunk, chunk), 0)
    c_i = lax.broadcasted_iota(I32, (chunk, chunk), 1)
    causal = c_i <= r_i
    tri = causal.astype(BF16)
    mid = chunk // 2 - 1

    def body(c, carry):
        rows = pl.ds(pl.multiple_of(c * chunk, chunk), chunk)
        logf, kk = _hg_gates(z_ref[rows, kw:2 * kw], lb)
        hi, md, lo = _split3(logf)
        cs = _dot(tri, jnp.concatenate([hi, md, lo], axis=1))
        b = cs[:, :kw] + cs[:, kw:2 * kw] + cs[:, 2 * kw:]
        b_mid = b[mid:mid + 1]
        b_last = b[chunk - 1:chunk]
        q_hat = _silu(z_ref[rows, 0:kw]) * jnp.exp(b - b_mid)
        k_hat = kk * jnp.exp(b_mid - b)
        q_in = (q_hat * jnp.exp(b_mid)).astype(BF16)
        k_dec = (k_hat * jnp.exp(b_last - b_mid)).astype(BF16)
        q_hat = q_hat.astype(BF16)
        k_hat = k_hat.astype(BF16)
        dec = jnp.exp(b_last)
        for h in range(HG_HEADS):
            cols = slice(h * HG_DK, (h + 1) * HG_DK)
            vcols = slice(2 * kw + h * HG_DV, 2 * kw + (h + 1) * HG_DV)
            gcols = slice(2 * kw + HG_HEADS * HG_DV + h * HG_DV, 2 * kw + HG_HEADS * HG_DV + (h + 1) * HG_DV)
            v = z_ref[rows, vcols].astype(BF16)
            att = jnp.where(causal, _dot_nt(q_hat[:, cols], k_hat[:, cols]), 0.0).astype(BF16)
            st = st_scr[h]
            o = _dot(att, v) + _dot_nt(q_in[:, cols], st.astype(BF16))
            st_scr[h] = st * dec[:, cols] + _dot_tn(v, k_dec[:, cols])
            o_ref[rows, h * HG_DV:(h + 1) * HG_DV] = _hg_out(o, z_ref[rows, gcols], og).astype(o_ref.dtype)
        return carry

    lax.fori_loop(0, n_chunks, body, 0)

    @pl.when(t_step == pl.num_programs(1) - 1)
    def _():
        sfin_ref[0] = st_scr[...]


def _gla_prompt(geo, z, lb_table, o_gain):
    tg = min(512, geo.seq)
    chunk = HG_CHUNK if geo.seq % HG_CHUNK == 0 else geo.seq
    assert tg % chunk == 0 and geo.seq % tg == 0
    nt = geo.seq // tg
    return pl.pallas_call(
        functools.partial(_gla_prompt_kernel, chunk=chunk, n_chunks=tg // chunk),
        grid=(geo.batch, nt),
        in_specs=[
            pl.BlockSpec((tg, 4 * D_MODEL), lambda b, t: (b * nt + t, 0)),
            pl.BlockSpec(lb_table.shape, lambda b, t: (0, 0)),
            pl.BlockSpec((1, HG_DV), lambda b, t: (0, 0)),
        ],
        out_specs=[
            pl.BlockSpec((tg, D_MODEL), lambda b, t: (b * nt + t, 0)),
            pl.BlockSpec((1, HG_HEADS, HG_DV, HG_DK), lambda b, t: (b, 0, 0, 0)),
        ],
        out_shape=[
            jax.ShapeDtypeStruct((geo.n_prompt, D_MODEL), BF16),
            jax.ShapeDtypeStruct((geo.batch, HG_HEADS, HG_DV, HG_DK), F32),
        ],
        scratch_shapes=[pltpu.VMEM((HG_HEADS, HG_DV, HG_DK), F32)],
        compiler_params=_cparams(("arbitrary", "arbitrary")),
        name="gla_prompt",
    )(z, lb_table, o_gain)


def _gla_step_kernel(z_ref, s_ref, lbt_ref, og_ref, o_ref, snew_ref, *, steps, bb):
    kw = HG_HEADS * HG_DK
    pad = SUBLANES
    lb = _hg_lower_bound(lbt_ref)
    og = og_ref[...]
    r_i = lax.broadcasted_iota(I32, (pad, pad), 0)
    c_i = lax.broadcasted_iota(I32, (pad, pad), 1)
    causal = c_i <= r_i
    tri = causal.astype(BF16)
    row_w = lax.broadcasted_iota(I32, (pad, kw), 0)
    live = row_w < steps
    row_k = lax.broadcasted_iota(I32, (pad, HG_DK), 0)
    ones_sel = jnp.where((row_k == steps) | (row_k == steps + 1), 1.0, 0.0).astype(BF16)

    def body(s, carry):
        z = z_ref[s]
        logf, kk = _hg_gates(z[:, kw:2 * kw], lb)
        hi, md, lo = _split3(jnp.where(live, logf, 0.0))
        cs = _dot(tri, jnp.concatenate([hi, md, lo], axis=1))
        b = cs[:, :kw] + cs[:, kw:2 * kw] + cs[:, 2 * kw:]
        b_last = b[steps - 1:steps]
        q_in = (_silu(z[:, 0:kw]) * jnp.exp(b)).astype(BF16)
        k_hat = jnp.where(live, kk * jnp.exp(-b), 0.0).astype(BF16)
        k_dec = jnp.where(live, kk * jnp.exp(b_last - b), 0.0).astype(BF16)
        dec = jnp.exp(b_last)
        d_hi = dec.astype(BF16)
        d_lo = (dec - d_hi.astype(F32)).astype(BF16)
        a_all = jnp.where(row_w == steps, d_hi, jnp.where(row_w == steps + 1, d_lo, k_dec))
        for h in range(HG_HEADS):
            cols = slice(h * HG_DK, (h + 1) * HG_DK)
            vcols = slice(2 * kw + h * HG_DV, 2 * kw + (h + 1) * HG_DV)
            gcols = slice(2 * kw + HG_HEADS * HG_DV + h * HG_DV, 2 * kw + HG_HEADS * HG_DV + (h + 1) * HG_DV)
            v = z[:, vcols].astype(BF16)
            s0 = s_ref[s, h]
            att = jnp.where(causal, _dot_nt(q_in[:, cols], k_hat[:, cols]), 0.0).astype(BF16)
            o = _dot(att, v) + _dot(q_in[:, cols], s0.astype(BF16))
            upd = _dot_tn(a_all[:, cols], jnp.concatenate([v, ones_sel], axis=1))
            snew_ref[s, h] = upd[:, HG_DV:] * s0 + upd[:, :HG_DV]
            o_ref[s, :, h * HG_DV:(h + 1) * HG_DV] = _hg_out(o, z[:, gcols], og)
        return carry

    lax.fori_loop(0, bb, body, 0, unroll=min(4, bb))


def _seq_major(geo, rows):
    steps, db = geo.dec_seq, geo.dec_batch
    w = rows.shape[-1]
    r = rows.reshape(steps, db, w).transpose(1, 0, 2)
    return jnp.concatenate([r, jnp.zeros((db, SUBLANES - steps, w), rows.dtype)], axis=1)


def _time_major(geo, r):
    steps, db = geo.dec_seq, geo.dec_batch
    return r[:, :steps].transpose(1, 0, 2).reshape(steps * db, r.shape[-1])


def _gla_step(geo, z, state, lb_table, o_gain):
    steps, db = geo.dec_seq, geo.dec_batch
    assert steps + 2 <= SUBLANES
    bb = min(8, db)
    z_s = _seq_major(geo, z[geo.n_prompt:])
    o_s, s_new = pl.pallas_call(
        functools.partial(_gla_step_kernel, steps=steps, bb=bb),
        grid=(db // bb,),
        in_specs=[
            pl.BlockSpec((bb, SUBLANES, 4 * D_MODEL), lambda i: (i, 0, 0)),
            pl.BlockSpec((bb, HG_HEADS, HG_DK, HG_DV), lambda i: (i, 0, 0, 0)),
            pl.BlockSpec(lb_table.shape, lambda i: (0, 0)),
            pl.BlockSpec((1, HG_DV), lambda i: (0, 0)),
        ],
        out_specs=[
            pl.BlockSpec((bb, SUBLANES, D_MODEL), lambda i: (i, 0, 0)),
            pl.BlockSpec((bb, HG_HEADS, HG_DK, HG_DV), lambda i: (i, 0, 0, 0)),
        ],
        out_shape=[
            jax.ShapeDtypeStruct((db, SUBLANES, D_MODEL), F32),
            jax.ShapeDtypeStruct(state.shape, F32),
        ],
        compiler_params=_cparams(("arbitrary",)),
        name="gla_step",
    )(z_s, state, lb_table, o_gain)
    return _time_major(geo, o_s), s_new


def _post_kernel(ap_ref, as_ref, wo_ref, xp_ref, xs_ref, gp_ref, gs_ref, n2_ref, shp_ref, shs_ref, scp_ref,
                 scs_ref, rw_ref, rb_ref, x1_ref, h2_ref, pos_ref, gt_ref, nch_ref, soff_ref, *, n_pt, tm):
    i = pl.program_id(0)
    g1 = _pick_mod(i, n_pt, gp_ref, gs_ref)
    a = jnp.where(i >= n_pt, as_ref[...], ap_ref[...])
    x = jnp.where(i >= n_pt, xs_ref[...], xp_ref[...])
    x1 = x + g1 * _dot(a, wo_ref[...])
    x1_ref[...] = x1
    sh = _pick_mod(i, n_pt, shp_ref, shs_ref)
    sc = _pick_mod(i, n_pt, scp_ref, scs_ref)
    h2 = _norm_mod(x1, n2_ref[...], sc, sh)
    h2_ref[...] = h2.astype(h2_ref.dtype)

    lane = lax.broadcasted_iota(I32, (tm, LANES), 1)
    logits = _dot(h2.astype(BF16), rw_ref[...]) + rb_ref[...]
    work = jnp.where(lane < N_EXPERTS, logits, -jnp.inf)
    vals, idxs, hits = [], [], []
    for _ in range(TOP_K):
        m = jnp.max(work, axis=-1, keepdims=True)
        idx = jnp.min(jnp.where(work == m, lane, LANES), axis=-1, keepdims=True)
        hit = lane == idx
        vals.append(m)
        idxs.append(idx)
        hits.append(hit)
        work = jnp.where(hit, -jnp.inf, work)
    exps = [jnp.exp(v - vals[0]) for v in vals]
    den = exps[0]
    for e in exps[1:]:
        den = den + e
    any_hit = hits[0]
    for hmask in hits[1:]:
        any_hit = any_hit | hmask
    any_f = jnp.where(any_hit, 1.0, 0.0)
    r_i = lax.broadcasted_iota(I32, (tm, tm), 0)
    c_i = lax.broadcasted_iota(I32, (tm, tm), 1)
    before = (c_i < r_i).astype(BF16)
    rank = _dot(before, any_f.astype(BF16))
    n_chunk = jnp.floor((jnp.sum(any_f, axis=0, keepdims=True) + (ROW_CHUNK - 1)) * (1.0 / ROW_CHUNK))
    e_i = lax.broadcasted_iota(I32, (LANES, LANES), 0)
    e_j = lax.broadcasted_iota(I32, (LANES, LANES), 1)
    earlier = (e_i < e_j).astype(BF16)
    seg = _dot(jnp.broadcast_to(n_chunk, (SUBLANES, LANES)).astype(BF16), earlier)[0:1]
    where_to = seg * float(ROW_CHUNK) + rank
    pos = jnp.zeros((tm, LANES), I32)
    gt = jnp.zeros((tm, LANES), F32)
    for k in range(TOP_K):
        p_k = jnp.sum(jnp.where(hits[k], where_to, 0.0), axis=-1, keepdims=True)
        pos = jnp.where(lane == k, p_k.astype(I32), pos)
        gt = jnp.where(lane == k, exps[k] / den, gt)
    pos_ref[...] = pos
    gt_ref[...] = gt
    nch_ref[0] = n_chunk.astype(I32)
    soff_ref[0] = seg.astype(I32)


def _post(geo, a_p, a_s, w_out_bf16, x_pair, norm_g, mod_p, mod_s, col0, router_w, router_b):
    tm = geo.tm_moe
    n_pt = geo.n_prompt // tm
    n_tiles = geo.nt // tm
    rw = jnp.zeros((D_MODEL, LANES), BF16).at[:, :N_EXPERTS].set(router_w.astype(BF16))
    rb = jnp.zeros((1, LANES), F32).at[0, :N_EXPERTS].set(router_b)
    row = lambda i: (i, 0)
    fixed = lambda i: (0, 0)
    tile_row = lambda i: (i, 0, 0)
    return pl.pallas_call(
        functools.partial(_post_kernel, n_pt=n_pt, tm=tm),
        grid=(n_tiles,),
        in_specs=[
            *_pair_specs(geo, tm, D_MODEL),
            pl.BlockSpec((D_MODEL, D_MODEL), fixed),
            *_pair_specs(geo, tm, D_MODEL),
            *_mod_specs(geo, col0 + 2, tm),
            pl.BlockSpec((1, D_MODEL), fixed),
            *_mod_specs(geo, col0 + 3, tm),
            *_mod_specs(geo, col0 + 4, tm),
            pl.BlockSpec((D_MODEL, LANES), fixed),
            pl.BlockSpec((1, LANES), fixed),
        ],
        out_specs=[
            pl.BlockSpec((tm, D_MODEL), row),
            pl.BlockSpec((tm, D_MODEL), row),
            pl.BlockSpec((tm, LANES), row),
            pl.BlockSpec((tm, LANES), row),
            pl.BlockSpec((1, 1, LANES), tile_row),
            pl.BlockSpec((1, 1, LANES), tile_row),
        ],
        out_shape=[
            jax.ShapeDtypeStruct((geo.nt, D_MODEL), F32),
            jax.ShapeDtypeStruct((geo.nt, D_MODEL), BF16),
            jax.ShapeDtypeStruct((geo.nt, LANES), I32),
            jax.ShapeDtypeStruct((geo.nt, LANES), F32),
            jax.ShapeDtypeStruct((n_tiles, 1, LANES), I32),
            jax.ShapeDtypeStruct((n_tiles, 1, LANES), I32),
        ],
        compiler_params=_cparams(("arbitrary",)),
        name="post_mixer_router",
    )(a_p, a_s.astype(BF16), w_out_bf16, *x_pair, mod_p, mod_s, norm_g, mod_p, mod_s, mod_p, mod_s, rw, rb)


def _ffn_kernel(first_ref, count_ref, xs_hbm, win_ref, bin_ref, wout_ref, bout_ref, ys_hbm,
                xbuf, ybuf, win_scr, wout_scr, xsem, ysem, *, half):
    e = pl.program_id(0)
    first = first_ref[e]
    n_sub = count_ref[e]
    pairs = n_sub // 2
    odd = n_sub - 2 * pairs
    win_scr[...] = win_ref[0].astype(BF16)
    wout_scr[...] = wout_ref[0].astype(BF16)

    def rows_at(sub, n_rows):
        return pl.ds(pl.multiple_of((first + sub) * half, half), n_rows)

    def x_copy(sub, n_rows, slot):
        return pltpu.make_async_copy(xs_hbm.at[rows_at(sub, n_rows)], xbuf.at[slot, pl.ds(0, n_rows)], xsem.at[slot])

    def y_copy(sub, n_rows, slot):
        return pltpu.make_async_copy(ybuf.at[slot, pl.ds(0, n_rows)], ys_hbm.at[rows_at(sub, n_rows)], ysem.at[slot])

    def ffn(slot, rows):
        gu = _dot(xbuf[slot, rows, :].astype(BF16), win_scr[...]) + bin_ref[0]
        gate = jnp.minimum(gu[:, :D_MODEL], SWIGLU_LIMIT)
        up = jnp.clip(gu[:, D_MODEL:], -SWIGLU_LIMIT, SWIGLU_LIMIT)
        act = gate * _sigmoid(SWIGLU_ALPHA * gate) * (up + 1.0)
        return _dot(act.astype(BF16), wout_scr[...]) + bout_ref[0]

    @pl.when(pairs > 0)
    def _():
        x_copy(0, 2 * half, 0).start()

    @pl.when((pairs == 0) & (odd == 1))
    def _():
        x_copy(0, half, 0).start()

    def unit(u, carry):
        slot = u % 2
        x_copy(2 * u, 2 * half, slot).wait()

        @pl.when(u + 1 < pairs)
        def _():
            x_copy(2 * (u + 1), 2 * half, 1 - slot).start()

        @pl.when((u + 1 == pairs) & (odd == 1))
        def _():
            x_copy(2 * pairs, half, 1 - slot).start()

        @pl.when(u >= 2)
        def _():
            y_copy(0, 2 * half, slot).wait()

        y0 = ffn(slot, slice(0, half))
        y1 = ffn(slot, slice(half, 2 * half))
        ybuf[slot, 0:half, :] = y0
        ybuf[slot, half:2 * half, :] = y1
        y_copy(2 * u, 2 * half, slot).start()
        return carry

    lax.fori_loop(0, pairs, unit, 0)
    tail_slot = pairs % 2

    @pl.when(odd == 1)
    def _():
        x_copy(2 * pairs, half, tail_slot).wait()

        @pl.when(pairs >= 2)
        def _():
            y_copy(0, 2 * half, tail_slot).wait()

        ybuf[tail_slot, 0:half, :] = ffn(tail_slot, slice(0, half))
        y_copy(2 * pairs, half, tail_slot).start()

        @pl.when(pairs >= 1)
        def _():
            y_copy(0, 2 * half, 1 - tail_slot).wait()

        y_copy(0, half, tail_slot).wait()

    @pl.when(odd == 0)
    def _():
        @pl.when(pairs >= 2)
        def _():
            y_copy(0, 2 * half, tail_slot).wait()

        @pl.when(pairs >= 1)
        def _():
            y_copy(0, 2 * half, 1 - tail_slot).wait()


def _ffn(xs, first_sub, n_sub, layer, w_in, b_in, w_out, b_out):
    n_e, d, d2 = w_in.shape
    half = MOE_BLOCK
    w_map = lambda e, *_: (layer * N_EXPERTS + e, 0, 0)
    return pl.pallas_call(
        functools.partial(_ffn_kernel, half=half),
        grid_spec=pltpu.PrefetchScalarGridSpec(
            num_scalar_prefetch=2,
            grid=(N_EXPERTS,),
            in_specs=[
                pl.BlockSpec(memory_space=pl.ANY),
                pl.BlockSpec((1, d, d2), w_map),
                pl.BlockSpec((1, 1, d2), w_map),
                pl.BlockSpec((1, d2 // 2, d), w_map),
                pl.BlockSpec((1, 1, d), w_map),
            ],
            out_specs=pl.BlockSpec(memory_space=pl.ANY),
            scratch_shapes=[
                pltpu.VMEM((2, 2 * half, d), F32),
                pltpu.VMEM((2, 2 * half, d), F32),
                pltpu.VMEM((d, d2), BF16),
                pltpu.VMEM((d2 // 2, d), BF16),
                pltpu.SemaphoreType.DMA((2,)),
                pltpu.SemaphoreType.DMA((2,)),
            ],
        ),
        out_shape=jax.ShapeDtypeStruct(xs.shape, xs.dtype),
        input_output_aliases={2: 0},
        compiler_params=_cparams(("arbitrary",)),
        name="moe_ffn",
    )(first_sub, n_sub, xs, w_in, b_in.reshape(n_e, 1, d2), w_out, b_out.reshape(n_e, 1, d))


def _chunk_rows(chunk, n_chunks=1):
    return pl.ds(pl.multiple_of(chunk * ROW_CHUNK, ROW_CHUNK), n_chunks * ROW_CHUNK)


def _run_pieces(n, fn, big):
    n_big = n // big

    def big_piece(i, carry):
        fn(i * big, big)
        return carry

    def small_piece(i, carry):
        fn(n_big * big + i, 1)
        return carry

    lax.fori_loop(0, n_big, big_piece, 0)
    lax.fori_loop(0, n - n_big * big, small_piece, 0)


def _for_each_piece(nch_ref, soff_ref, dst_ref, tile, fn):
    def per_expert(e, carry):
        s0 = soff_ref[tile, e]
        d0 = dst_ref[tile, e]
        _run_pieces(nch_ref[tile, e], lambda off, size: fn(s0 + off, d0 + off, size), COPY_CHUNKS)
        return carry

    lax.fori_loop(0, N_EXPERTS, per_expert, 0)


def _dispatch_kernel(nch_ref, soff_ref, dst_ref, tot_ref, fill0_ref, filln_ref,
                     pos_ref, h_ref, xs_hbm, stage, zeros, sem, fill_sem, *, tm, k_stage):
    t = pl.program_id(0)
    n_t = pl.num_programs(0)
    slot = t % 2

    def copy(s_chunk, d_chunk, size, sl):
        return pltpu.make_async_copy(stage.at[sl, _chunk_rows(s_chunk, size)],
                                     xs_hbm.at[_chunk_rows(d_chunk, size)], sem.at[sl])

    def wait_tile(tile, sl):
        _run_pieces(tot_ref[tile], lambda off, size: copy(0, 0, size, sl).wait(), WAIT_CHUNKS)

    @pl.when(t >= 2)
    def _():
        wait_tile(t - 2, slot)

    pos_t = pos_ref[...].astype(F32).T.astype(I32)
    j_i = lax.broadcasted_iota(I32, (k_stage, tm), 0)
    hit = j_i == pos_t[0:1]
    for k in range(1, TOP_K):
        hit = hit | (j_i == pos_t[k:k + 1])
    stage[slot] = _dot(jnp.where(hit, 1.0, 0.0).astype(BF16), h_ref[...])
    _for_each_piece(nch_ref, soff_ref, dst_ref, t, lambda s, d, size: copy(s, d, size, slot).start())

    @pl.when(t == n_t - 1)
    def _():
        zeros[...] = jnp.zeros_like(zeros)

        def fill(d_chunk, size):
            return pltpu.make_async_copy(zeros.at[_chunk_rows(0, size)], xs_hbm.at[_chunk_rows(d_chunk, size)],
                                         fill_sem.at[0])

        def per_region(r, carry):
            _run_pieces(filln_ref[r], lambda off, size: fill(fill0_ref[r] + off, size).start(), COPY_CHUNKS)
            return carry

        lax.fori_loop(0, N_EXPERTS + 1, per_region, 0)

        @pl.when(t >= 1)
        def _():
            wait_tile(t - 1, 1 - slot)

        wait_tile(t, slot)

        def per_region_wait(r, carry):
            _run_pieces(filln_ref[r], lambda off, size: fill(0, size).wait(), COPY_CHUNKS)
            return carry

        lax.fori_loop(0, N_EXPERTS + 1, per_region_wait, 0)


def _combine_kernel(nch_ref, soff_ref, dst_ref, tot_ref, pos_ref, gate_ref, x_ref, gp_ref, gs_ref, y_hbm,
                    op_ref, os_ref, stage, sem, *, n_pt, tm, k_stage):
    t = pl.program_id(0)
    n_t = pl.num_programs(0)
    slot = t % 2

    def copy(s_chunk, d_chunk, size, sl):
        return pltpu.make_async_copy(y_hbm.at[_chunk_rows(d_chunk, size)],
                                     stage.at[sl, _chunk_rows(s_chunk, size)], sem.at[sl])

    def fetch(tile, sl):
        _for_each_piece(nch_ref, soff_ref, dst_ref, tile, lambda s, d, size: copy(s, d, size, sl).start())

    @pl.when(t == 0)
    def _():
        stage[...] = jnp.zeros_like(stage)
        fetch(0, 0)

    @pl.when(t + 1 < n_t)
    def _():
        fetch(t + 1, 1 - slot)

    _run_pieces(tot_ref[t], lambda off, size: copy(0, 0, size, slot).wait(), WAIT_CHUNKS)
    lane = lax.broadcasted_iota(I32, (tm, k_stage), 1)
    pos = pos_ref[...]
    g = gate_ref[...]
    p = jnp.zeros((tm, k_stage), F32)
    for k in range(TOP_K):
        p = jnp.where(lane == pos[:, k:k + 1], g[:, k:k + 1], p)
    ffn = _dot(p.astype(BF16), stage[slot].astype(BF16))
    out = x_ref[...] + _pick_mod(t, n_pt, gp_ref, gs_ref) * ffn

    @pl.when(t < n_pt)
    def _():
        op_ref[...] = out

    @pl.when(t >= n_pt)
    def _():
        os_ref[...] = out


def _moe(geo, h2, pos, gates, nch_pad, soff_pad, x1, mod_p, mod_s, col_gate, layer, w_in, b_in, w_out, b_out):
    bm = MOE_BLOCK
    tm = geo.tm_moe
    n_tiles = geo.nt // tm
    depth, n_e, d, d2 = w_in.shape
    w_in = w_in.reshape(depth * n_e, d, d2)
    b_in = b_in.reshape(depth * n_e, d2)
    w_out = w_out.reshape(depth * n_e, d2 // 2, d)
    b_out = b_out.reshape(depth * n_e, d)
    chunks_per_block = bm // ROW_CHUNK
    max_rows = geo.nt * TOP_K + n_tiles * N_EXPERTS * (ROW_CHUNK - 1) + N_EXPERTS * (bm - 1)
    n_blocks = -(-max_rows // bm)
    nch = nch_pad[:, 0, :N_EXPERTS]
    soff = soff_pad[:, 0, :N_EXPERTS]
    tot = jnp.sum(nch, axis=1)
    per_e = jnp.sum(nch, axis=0)
    padded = (per_e + chunks_per_block - 1) // chunks_per_block * chunks_per_block
    pend = jnp.cumsum(padded)
    pstart = pend - padded
    dst = pstart[None, :] + jnp.cumsum(nch, axis=0) - nch
    first_sub = (pstart // chunks_per_block).astype(I32)
    n_sub = (padded // chunks_per_block).astype(I32)
    fill0 = jnp.concatenate([pstart + per_e, pend[-1:]]).astype(I32)
    filln = jnp.concatenate([padded - per_e, n_blocks * chunks_per_block - pend[-1:]]).astype(I32)

    k_stage = -(-(tm * TOP_K + N_EXPERTS * (ROW_CHUNK - 1)) // LANES) * LANES
    row = lambda i, *_: (i, 0)
    xs = pl.pallas_call(
        functools.partial(_dispatch_kernel, tm=tm, k_stage=k_stage),
        grid_spec=pltpu.PrefetchScalarGridSpec(
            num_scalar_prefetch=6,
            grid=(n_tiles,),
            in_specs=[pl.BlockSpec((tm, LANES), row), pl.BlockSpec((tm, d), row)],
            out_specs=pl.BlockSpec(memory_space=pl.ANY),
            scratch_shapes=[
                pltpu.VMEM((2, k_stage, d), F32),
                pltpu.VMEM((COPY_CHUNKS * ROW_CHUNK, d), F32),
                pltpu.SemaphoreType.DMA((2,)),
                pltpu.SemaphoreType.DMA((1,)),
            ],
        ),
        out_shape=jax.ShapeDtypeStruct((n_blocks * bm, d), F32),
        compiler_params=_cparams(("arbitrary",)),
        name="moe_dispatch",
    )(nch, soff, dst, tot, fill0, filln, pos, h2)

    ys = _ffn(xs, first_sub, n_sub, layer, w_in, b_in, w_out, b_out)

    return pl.pallas_call(
        functools.partial(_combine_kernel, n_pt=geo.n_prompt // tm, tm=tm, k_stage=k_stage),
        grid_spec=pltpu.PrefetchScalarGridSpec(
            num_scalar_prefetch=4,
            grid=(n_tiles,),
            in_specs=[
                pl.BlockSpec((tm, LANES), row),
                pl.BlockSpec((tm, LANES), row),
                pl.BlockSpec((tm, d), row),
                *_mod_specs(geo, col_gate, tm),
                pl.BlockSpec(memory_space=pl.ANY),
            ],
            out_specs=_pair_specs(geo, tm, d),
            scratch_shapes=[
                pltpu.VMEM((2, k_stage, d), F32),
                pltpu.SemaphoreType.DMA((2,)),
            ],
        ),
        out_shape=[jax.ShapeDtypeStruct((geo.n_prompt, d), F32), jax.ShapeDtypeStruct((geo.n_sample, d), F32)],
        compiler_params=_cparams(("arbitrary",)),
        name="moe_combine",
    )(nch, soff, dst, tot, pos, gates, x1, mod_p, mod_s, ys)


SW_QW = SW_Q_HEADS * SW_HEAD_DIM
SW_QKW = SW_QW + SW_KV_W
SW_KEYS = 2 * ATT_BLOCK


def _attn_proj_kernel(xp_ref, xs_ref, g_ref, shp_ref, shs_ref, scp_ref, scs_ref, w_ref, e_ref, et_ref, qg_ref,
                      kg_ref, q_ref, k_ref, v_ref, *, n_pt):
    i = pl.program_id(0)
    sh = _pick_mod(i, n_pt, shp_ref, shs_ref)
    sc = _pick_mod(i, n_pt, scp_ref, scs_ref)
    x = jnp.where(i >= n_pt, xs_ref[...], xp_ref[...])
    h = _norm_mod(x, g_ref[...], sc, sh).astype(BF16)
    z = _dot(h, w_ref[...])
    qk = z[:, :SW_QKW]
    sq = qk * qk
    sq_hi = sq.astype(BF16)
    sq_lo = (sq - sq_hi.astype(F32)).astype(BF16)
    ms = (_dot(sq_hi, e_ref[...]) + _dot(sq_lo, e_ref[...])) * (1.0 / SW_HEAD_DIM)
    inv = lax.rsqrt(ms + NORM_EPS)
    inv_hi = inv.astype(BF16)
    inv_lo = (inv - inv_hi.astype(F32)).astype(BF16)
    qk = qk * (_dot(inv_hi, et_ref[...]) + _dot(inv_lo, et_ref[...]))
    q_ref[...] = (qk[:, :SW_QW] * qg_ref[...] * SW_SCALE).astype(q_ref.dtype)
    k_ref[...] = qk[:, SW_QW:] * kg_ref[...]
    v_ref[...] = z[:, SW_QKW:]


def _attn_proj(geo, x_pair, g, mod_p, mod_s, w_bf16, q_gain, k_gain):
    tm = geo.tm
    n_out = w_bf16.shape[1]
    heads = SW_QKW // SW_HEAD_DIM
    member = (np.arange(SW_QKW)[:, None] // SW_HEAD_DIM == np.arange(LANES)[None, :]).astype(np.float32)
    e = jnp.asarray(member, BF16)
    et = jnp.asarray(member.T, BF16)
    assert heads <= LANES
    qg = jnp.tile(q_gain, SW_Q_HEADS).reshape(1, SW_QW)
    kg = jnp.tile(k_gain, SW_KV_HEADS).reshape(1, SW_KV_W)
    row = lambda i: (i, 0)
    fixed = lambda i: (0, 0)
    return pl.pallas_call(
        functools.partial(_attn_proj_kernel, n_pt=geo.n_pt),
        grid=(geo.n_tiles,),
        in_specs=[
            *_pair_specs(geo, tm, D_MODEL),
            pl.BlockSpec((1, D_MODEL), fixed),
            *_mod_specs(geo, 0, tm),
            *_mod_specs(geo, 1, tm),
            pl.BlockSpec((D_MODEL, n_out), fixed),
            pl.BlockSpec((SW_QKW, LANES), fixed),
            pl.BlockSpec((LANES, SW_QKW), fixed),
            pl.BlockSpec((1, SW_QW), fixed),
            pl.BlockSpec((1, SW_KV_W), fixed),
        ],
        out_specs=[
            pl.BlockSpec((tm, SW_QW), row),
            pl.BlockSpec((tm, SW_KV_W), row),
            pl.BlockSpec((tm, SW_KV_W), row),
        ],
        out_shape=[
            jax.ShapeDtypeStruct((geo.nt, SW_QW), BF16),
            jax.ShapeDtypeStruct((geo.nt, SW_KV_W), F32),
            jax.ShapeDtypeStruct((geo.nt, SW_KV_W), F32),
        ],
        compiler_params=_cparams(("arbitrary",)),
        name="attn_proj_qknorm",
    )(*x_pair, g, mod_p, mod_s, mod_p, mod_s, w_bf16, e, et, qg, kg)


def _rel_bucket_np(dist):
    n = np.maximum(dist, 0)
    max_exact = REL_BUCKETS // 2
    ratio = np.log(np.maximum(n, 1).astype(np.float32) / np.float32(max_exact)) / np.float32(
        math.log(REL_MAX_DIST / max_exact))
    large = max_exact + (ratio * np.float32(REL_BUCKETS - max_exact)).astype(np.int32)
    large = np.minimum(large, REL_BUCKETS - 1)
    return np.where(n < max_exact, n, large).astype(np.int32)


def _bucket_table(qpos, kpos, k_ok):
    dist = qpos[:, None] - kpos[None, :]
    ok = (dist >= 0) & (dist <= WINDOW) & k_ok[None, :]
    return np.where(ok, _rel_bucket_np(dist), -1).astype(np.int32)


def _build_bias(bkt_ref, rb_ref, bias_scr, rows):
    bkt = bkt_ref[...]
    base = jnp.where(bkt < 0, -jnp.inf, 0.0)
    for h in range(SW_Q_HEADS):
        bias_scr[h * rows:(h + 1) * rows, :] = base

    def add_bucket(j, carry):
        hit = bkt == j
        for h in range(SW_Q_HEADS):
            sl = slice(h * rows, (h + 1) * rows)
            bias_scr[sl, :] = bias_scr[sl, :] + jnp.where(hit, rb_ref[j, h], 0.0)
        return carry

    lax.fori_loop(0, REL_BUCKETS, add_bucket, 0)


def _sink_softmax_pv(s, sink, vv_g):
    m = jnp.maximum(jnp.max(s, axis=-1, keepdims=True), sink)
    e = jnp.exp(s - m)
    p = e / (jnp.sum(e, axis=-1, keepdims=True) + jnp.exp(sink - m))
    return _dot(p.astype(BF16), vv_g)


def _build_bias_t(bkt_ref, rb_ref, bias_scr, blk):
    bkt = bkt_ref[...]
    base = jnp.where(bkt < 0, -jnp.inf, 0.0)
    slots = [(g, j) for g in range(SW_KV_HEADS) for j in range(SW_GROUP)]
    for g, j in slots:
        bias_scr[g, :, j * blk:(j + 1) * blk] = base

    def add_bucket(b, carry):
        hit = bkt == b
        for g, j in slots:
            cols = slice(j * blk, (j + 1) * blk)
            bias_scr[g, :, cols] = bias_scr[g, :, cols] + jnp.where(hit, rb_ref[b, g * SW_GROUP + j], 0.0)
        return carry

    lax.fori_loop(0, REL_BUCKETS, add_bucket, 0)


def _swa_prompt_kernel(rb_ref, sink_ref, bkt_ref, q_ref, kp_ref, kc_ref, vp_ref, vc_ref, o_ref, bias_scr, *, qb):
    n = pl.program_id(1)
    blk = ATT_BLOCK
    hd = SW_HEAD_DIM

    @pl.when((pl.program_id(0) == 0) & (n == 0))
    def _():
        _build_bias_t(bkt_ref, rb_ref, bias_scr, blk)

    key_i = lax.broadcasted_iota(I32, (SW_KEYS, SW_GROUP * blk), 0)
    hide_prev = (n == 0) & (key_i < blk)
    zeros_q = jnp.zeros((hd, SW_GROUP * blk), BF16)
    sinks = [jnp.concatenate([jnp.full((1, blk), sink_ref[g * SW_GROUP + j], F32) for j in range(SW_GROUP)], axis=1)
             for g in range(SW_KV_HEADS)]
    for i in range(qb):
        rows = slice(i * blk, (i + 1) * blk)
        if i == 0:
            k2 = jnp.concatenate([kp_ref[...], kc_ref[rows, :]], axis=0)
            v2 = jnp.concatenate([vp_ref[...], vc_ref[rows, :]], axis=0)
        else:
            k2 = kc_ref[(i - 1) * blk:(i + 1) * blk, :]
            v2 = vc_ref[(i - 1) * blk:(i + 1) * blk, :]
        kk = k2.astype(BF16)
        vv_t = v2.T.astype(BF16)
        q_t = q_ref[rows, :].astype(F32).T.astype(BF16)
        outs = []
        for g in range(SW_KV_HEADS):
            heads = [g * SW_GROUP + j for j in range(SW_GROUP)]
            q_cat = jnp.concatenate([q_t[hq * hd:(hq + 1) * hd, :] for hq in heads], axis=1)
            q_full = jnp.concatenate([q_cat if gg == g else zeros_q for gg in range(SW_KV_HEADS)], axis=0)
            s = _dot(kk, q_full) + bias_scr[g]
            if i == 0:
                s = jnp.where(hide_prev, -jnp.inf, s)
            m = jnp.maximum(jnp.max(s, axis=0, keepdims=True), sinks[g])
            e = jnp.exp(s - m)
            den = jnp.sum(e, axis=0, keepdims=True) + jnp.exp(sinks[g] - m)
            o_t = _dot(vv_t[g * hd:(g + 1) * hd, :], e.astype(BF16)) * (1.0 / den)
            outs.extend(o_t[:, j * blk:(j + 1) * blk] for j in range(SW_GROUP))
        o_ref[rows, :] = jnp.concatenate(outs, axis=0).T.astype(o_ref.dtype)


def _swa_prompt(geo, q, k, v, rel_bias, sinks):
    blk = ATT_BLOCK
    assert geo.seq % blk == 0 and WINDOW == blk
    n_blk = geo.seq // blk
    qb = 4 if n_blk % 4 == 0 else (2 if n_blk % 2 == 0 else 1)
    nb = n_blk // qb
    bkt = jnp.asarray(_bucket_table(blk + np.arange(blk), np.arange(2 * blk), np.ones(2 * blk, bool)).T.copy())
    cur = lambda b, n: (b * nb + n, 0)
    prev = lambda b, n: (b * n_blk + jnp.maximum(n * qb - 1, 0), 0)
    smem = pl.BlockSpec(memory_space=pltpu.SMEM)
    return pl.pallas_call(
        functools.partial(_swa_prompt_kernel, qb=qb),
        grid=(geo.batch, nb),
        in_specs=[
            smem, smem,
            pl.BlockSpec((SW_KEYS, blk), lambda b, n: (0, 0)),
            pl.BlockSpec((qb * blk, SW_QW), cur),
            pl.BlockSpec((blk, SW_KV_W), prev),
            pl.BlockSpec((qb * blk, SW_KV_W), cur),
            pl.BlockSpec((blk, SW_KV_W), prev),
            pl.BlockSpec((qb * blk, SW_KV_W), cur),
        ],
        out_specs=pl.BlockSpec((qb * blk, SW_QW), cur),
        out_shape=jax.ShapeDtypeStruct((geo.n_prompt, SW_QW), BF16),
        scratch_shapes=[pltpu.VMEM((SW_KV_HEADS, SW_KEYS, SW_GROUP * blk), F32)],
        compiler_params=_cparams(("arbitrary", "arbitrary")),
        name="swa_prompt",
    )(rel_bias, sinks, bkt, q, k, k, v, v)


def _swa_step_kernel(rb_ref, sink_ref, bkt_ref, q_ref, kn_ref, vn_ref, ck_ref, cv_ref,
                     o_ref, nk_ref, nv_ref, bias_scr, *, steps, bb):
    pad = SUBLANES
    win = WINDOW

    @pl.when(pl.program_id(0) == 0)
    def _():
        _build_bias(bkt_ref, rb_ref, bias_scr, pad)

    fill = jnp.zeros((SW_KEYS - win - pad, SW_KV_W), F32)
    row8 = lax.broadcasted_iota(I32, (pad, SW_KV_W), 0)

    def shifted(cache, new):
        rolled = pltpu.roll(cache, win - steps, 0)
        tail = jnp.where(row8 < pad - steps, rolled[win - pad:], pltpu.roll(new, pad - steps, 0))
        return rolled[:win - pad], tail

    def body(s, carry):
        ck, cv, kn, vn = ck_ref[s], cv_ref[s], kn_ref[s], vn_ref[s]
        kk = jnp.concatenate([ck, kn, fill], axis=0).astype(BF16)
        vv = jnp.concatenate([cv, vn, fill], axis=0).astype(BF16)
        q = q_ref[s]
        for g in range(SW_KV_HEADS):
            kcols = slice(g * SW_HEAD_DIM, (g + 1) * SW_HEAD_DIM)
            heads = [g * SW_GROUP + j for j in range(SW_GROUP)]
            qs = jnp.concatenate([q[:, hq * SW_HEAD_DIM:(hq + 1) * SW_HEAD_DIM] for hq in heads], axis=0)
            sc = _dot_nt(qs.astype(BF16), kk[:, kcols]) + bias_scr[heads[0] * pad:(heads[-1] + 1) * pad, :]
            sink = jnp.concatenate([jnp.full((pad, 1), sink_ref[hq], F32) for hq in heads], axis=0)
            og = _sink_softmax_pv(sc, sink, vv[:, kcols])
            for j, hq in enumerate(heads):
                o_ref[s, :, hq * SW_HEAD_DIM:(hq + 1) * SW_HEAD_DIM] = og[j * pad:(j + 1) * pad]
        head, tail = shifted(ck, kn)
        nk_ref[s, 0:win - pad] = head
        nk_ref[s, win - pad:win] = tail
        head, tail = shifted(cv, vn)
        nv_ref[s, 0:win - pad] = head
        nv_ref[s, win - pad:win] = tail
        return carry

    lax.fori_loop(0, bb, body, 0, unroll=min(4, bb))


def _swa_step(geo, q, k, v, cache_k, cache_v, rel_bias, sinks):
    steps, db = geo.dec_seq, geo.dec_batch
    win = cache_k.shape[1]
    assert win == WINDOW and steps <= SUBLANES
    bb = min(8, db)
    pad = SUBLANES
    q_s = _seq_major(geo, q[geo.n_prompt:].astype(F32))
    k_s = _seq_major(geo, k[geo.n_prompt:])
    v_s = _seq_major(geo, v[geo.n_prompt:])
    kpos = np.arange(SW_KEYS)
    k_ok = kpos < win + steps
    qpos = win + np.arange(pad)
    bkt = _bucket_table(qpos, kpos, k_ok)
    bkt[steps:] = -1
    smem = pl.BlockSpec(memory_space=pltpu.SMEM)
    blk3 = lambda r, w: pl.BlockSpec((bb, r, w), lambda i: (i, 0, 0))
    o_s, nk, nv = pl.pallas_call(
        functools.partial(_swa_step_kernel, steps=steps, bb=bb),
        grid=(db // bb,),
        in_specs=[
            smem, smem,
            pl.BlockSpec((pad, SW_KEYS), lambda i: (0, 0)),
            blk3(pad, SW_QW), blk3(pad, SW_KV_W), blk3(pad, SW_KV_W),
            blk3(win, SW_KV_W), blk3(win, SW_KV_W),
        ],
        out_specs=[blk3(pad, SW_QW), blk3(win, SW_KV_W), blk3(win, SW_KV_W)],
        out_shape=[
            jax.ShapeDtypeStruct((db, pad, SW_QW), F32),
            jax.ShapeDtypeStruct((db, win, SW_KV_W), F32),
            jax.ShapeDtypeStruct((db, win, SW_KV_W), F32),
        ],
        scratch_shapes=[pltpu.VMEM((SW_Q_HEADS * pad, SW_KEYS), F32)],
        compiler_params=_cparams(("arbitrary",)),
        name="swa_step",
    )(rel_bias, sinks, jnp.asarray(bkt), q_s, k_s, v_s, cache_k, cache_v)
    return _time_major(geo, o_s), nk, nv


def kernel(x_prompt, x_sample, state_hgrn, cache_win_k, cache_win_v, c_prompt, c_sample, norm1_g, norm2_g, ada_w, ada_b, hg_w_in, hg_lb_table, hg_onorm_g, hg_w_out, sw_w_in, sw_qnorm_g, sw_knorm_g, sw_sinks, sw_w_out, rel_bias, router_w, router_b, moe_w_in, moe_b_in, moe_w_out, moe_b_out):
    batch, seq, d = x_prompt.shape
    db, steps, _ = x_sample.shape
    assert d == D_MODEL and ada_w.shape[0] == 2 and hg_w_in.shape[0] == 1 and sw_w_in.shape[0] == 1
    geo = _Geom(batch, seq, db, steps)
    x = (x_prompt.reshape(batch * seq, d), x_sample.transpose(1, 0, 2).reshape(steps * db, d))
    n_seq = batch + db
    rows = -(-n_seq // SUBLANES) * SUBLANES
    c_all = jnp.concatenate([c_prompt, c_sample, jnp.zeros((rows - n_seq, d), F32)], axis=0)
    mods = _ada_mods(c_all, ada_w, ada_b)

    def layer_mods(layer):
        mod_p = mods[layer, :batch].reshape(batch, 1, 6 * d)
        mod_s = jnp.tile(mods[layer, batch:n_seq], (steps, 1))
        return mod_p, mod_s

    def moe(layer, a_p, a_s, w_out, x_in, mod_p, mod_s):
        x1, h2, pos, gt, nch, soff = _post(geo, a_p, a_s, w_out.astype(BF16), x_in, norm2_g[layer:layer + 1],
                                           mod_p, mod_s, 0, router_w[layer], router_b[layer])
        return _moe(geo, h2, pos, gt, nch, soff, x1, mod_p, mod_s, 5,
                    layer, moe_w_in, moe_b_in, moe_w_out, moe_b_out)

    mod_p, mod_s = layer_mods(0)
    og = hg_onorm_g[0:1]
    z = _proj(geo, x, norm1_g[0:1], mod_p, mod_s, hg_w_in[0].astype(BF16), 0, 1)
    o_p, st_p = _gla_prompt(geo, z, hg_lb_table, og)
    o_s, st_s = _gla_step(geo, z, state_hgrn[0], hg_lb_table, og)
    x = moe(0, o_p, o_s, hg_w_out[0], x, mod_p, mod_s)

    mod_p, mod_s = layer_mods(1)
    q, k, v = _attn_proj(geo, x, norm1_g[1:2], mod_p, mod_s, sw_w_in[0].astype(BF16), sw_qnorm_g[0], sw_knorm_g[0])
    win = cache_win_k.shape[2]
    a_p = _swa_prompt(geo, q, k, v, rel_bias, sw_sinks[0])
    a_s, nk, nv = _swa_step(geo, q, k, v, cache_win_k[0].reshape(db, win, SW_KV_W),
                            cache_win_v[0].reshape(db, win, SW_KV_W), rel_bias, sw_sinks[0])
    x = moe(1, a_p, a_s, sw_w_out[0], x, mod_p, mod_s)

    y_prompt = x[0].reshape(batch, seq, d)
    y_sample = x[1].reshape(steps, db, d).transpose(1, 0, 2)
    kv_shape = (1, batch, WINDOW, SW_KV_HEADS, SW_HEAD_DIM)
    k_p = k[:geo.n_prompt].reshape(batch, seq, SW_KV_W)[:, seq - WINDOW:].reshape(kv_shape)
    v_p = v[:geo.n_prompt].reshape(batch, seq, SW_KV_W)[:, seq - WINDOW:].reshape(kv_shape)
    cache_shape = (1, db, win, SW_KV_HEADS, SW_HEAD_DIM)
    return (y_prompt, y_sample, jnp.swapaxes(st_p, -1, -2)[None], st_s[None], k_p, v_p,
            nk.reshape(cache_shape), nv.reshape(cache_shape))
```

```python
import functools
import math

import numpy as np
import jax
import jax.numpy as jnp
from jax import lax
from jax.experimental import pallas as pl
from jax.experimental.pallas import tpu as pltpu

F32 = jnp.float32
BF16 = jnp.bfloat16
I32 = jnp.int32

D_MODEL = 1024
LANES = 128
SUBLANES = 8
D_TILES = D_MODEL // LANES
HG_DK = 128
HG_HEADS = D_MODEL // HG_DK
HG_DV = D_MODEL // HG_HEADS
HG_CHUNK = 64
SW_HEAD_DIM = 64
SW_Q_HEADS = D_MODEL // SW_HEAD_DIM
SW_KV_HEADS = 4
SW_GROUP = SW_Q_HEADS // SW_KV_HEADS
SW_KV_W = SW_KV_HEADS * SW_HEAD_DIM
WINDOW = 128
ATT_BLOCK = 128
SW_SCALE = SW_HEAD_DIM ** -0.5
REL_BUCKETS = 32
REL_MAX_DIST = 128
N_EXPERTS = 32
TOP_K = 4
SWIGLU_LIMIT = 7.0
SWIGLU_ALPHA = 1.702
NORM_EPS = 1e-5
MOE_BLOCK = 256
ROW_CHUNK = SUBLANES
COPY_CHUNKS = 4
CHUNK_UNROLL = 8
WAIT_CHUNKS = 16
VMEM_LIMIT = 56 * 1024 * 1024


def _cparams(sem):
    return pltpu.CompilerParams(dimension_semantics=sem, vmem_limit_bytes=VMEM_LIMIT)


def _sigmoid(x):
    return 0.5 * jnp.tanh(0.5 * x) + 0.5


def _silu(x):
    return x * _sigmoid(x)


def _dot(a, b):
    return jnp.dot(a, b, preferred_element_type=F32)


def _dot_nt(a, b):
    return lax.dot_general(a, b, (((1,), (1,)), ((), ())), preferred_element_type=F32)


def _dot_tn(a, b):
    return lax.dot_general(a, b, (((0,), (0,)), ((), ())), preferred_element_type=F32)


def _split3(x):
    hi = x.astype(BF16)
    r = x - hi.astype(F32)
    mid = r.astype(BF16)
    lo = (r - mid.astype(F32)).astype(BF16)
    return hi, mid, lo


def _ada_kernel(c_ref, w_ref, b_ref, o_ref):
    s = _silu(c_ref[...]).astype(BF16)
    o_ref[0] = _dot(s, w_ref[0].astype(BF16)) + b_ref[0]


def _ada_mods(c_all, ada_w, ada_b):
    depth, d, n6 = ada_w.shape
    rows = c_all.shape[0]
    tn = 1536
    return pl.pallas_call(
        _ada_kernel,
        grid=(depth, n6 // tn),
        in_specs=[
            pl.BlockSpec((rows, d), lambda l, j: (0, 0)),
            pl.BlockSpec((1, d, tn), lambda l, j: (l, 0, j)),
            pl.BlockSpec((1, 1, tn), lambda l, j: (l, 0, j)),
        ],
        out_specs=pl.BlockSpec((1, rows, tn), lambda l, j: (l, 0, j)),
        out_shape=jax.ShapeDtypeStruct((depth, rows, n6), F32),
        compiler_params=_cparams(("arbitrary", "arbitrary")),
        name="ada_mods",
    )(c_all, ada_w, ada_b.reshape(depth, 1, n6))


class _Geom:
    def __init__(self, batch, seq, dec_batch, dec_seq):
        self.batch, self.seq, self.dec_batch, self.dec_seq = batch, seq, dec_batch, dec_seq
        self.n_prompt = batch * seq
        self.n_sample = dec_batch * dec_seq
        self.nt = self.n_prompt + self.n_sample
        tm = 512
        while seq % tm or self.n_sample % tm:
            tm //= 2
        assert tm >= 8
        self.tm = tm
        self.n_pt = self.n_prompt // tm
        self.n_tiles = self.nt // tm
        self.tm_moe = min(256, tm)


def _mod_specs(geo, col, tm):
    seq, batch = geo.seq, geo.batch
    n_pt = geo.n_prompt // tm

    def p_map(i, *_):
        return (jnp.minimum(i * tm // seq, batch - 1), 0, col)

    def s_map(i, *_):
        return (jnp.maximum(i - n_pt, 0), col)

    return [pl.BlockSpec((1, 1, D_MODEL), p_map), pl.BlockSpec((tm, D_MODEL), s_map)]


def _pick_mod(i, n_pt, p_ref, s_ref):
    return jnp.where(i >= n_pt, s_ref[...], p_ref[0])


def _norm_mod(x, g, sc, sh):
    ms = jnp.mean(x * x, axis=-1, keepdims=True)
    return x * lax.rsqrt(ms + NORM_EPS) * g * (1.0 + sc) + sh


def _pair_specs(geo, tm, width):
    n_pt = geo.n_prompt // tm
    return [pl.BlockSpec((tm, width), lambda i, *_: (jnp.minimum(i, n_pt - 1), 0)),
            pl.BlockSpec((tm, width), lambda i, *_: (jnp.maximum(i - n_pt, 0), 0))]


def _proj_kernel(xp_ref, xs_ref, g_ref, shp_ref, shs_ref, scp_ref, scs_ref, w_ref, o_ref, *, n_pt, tn):
    i = pl.program_id(0)
    sh = _pick_mod(i, n_pt, shp_ref, shs_ref)
    sc = _pick_mod(i, n_pt, scp_ref, scs_ref)
    x = jnp.where(i >= n_pt, xs_ref[...], xp_ref[...])
    h = _norm_mod(x, g_ref[...], sc, sh).astype(BF16)
    for j in range(o_ref.shape[1] // tn):
        o_ref[:, j * tn:(j + 1) * tn] = _dot(h, w_ref[:, j * tn:(j + 1) * tn])


def _proj(geo, x_pair, g, mod_p, mod_s, w_bf16, col_shift, col_scale):
    n_out = w_bf16.shape[1]
    tm = geo.tm
    fixed = lambda i: (0, 0)
    return pl.pallas_call(
        functools.partial(_proj_kernel, n_pt=geo.n_pt, tn=1024),
        grid=(geo.n_tiles,),
        in_specs=[
            *_pair_specs(geo, tm, D_MODEL),
            pl.BlockSpec((1, D_MODEL), fixed),
            *_mod_specs(geo, col_shift, tm),
            *_mod_specs(geo, col_scale, tm),
            pl.BlockSpec((D_MODEL, n_out), fixed),
        ],
        out_specs=pl.BlockSpec((tm, n_out), lambda i: (i, 0)),
        out_shape=jax.ShapeDtypeStruct((geo.nt, n_out), F32),
        compiler_params=_cparams(("arbitrary",)),
        name="norm_mod_proj",
    )(*x_pair, g, mod_p, mod_s, mod_p, mod_s, w_bf16)


def _hg_lower_bound(lbt_ref):
    t = lbt_ref[...]
    e = jnp.exp(t - jnp.max(t, axis=0, keepdims=True))
    return e[0:1] / jnp.sum(e, axis=0, keepdims=True)


def _hg_gates(fz, lb):
    e = jnp.exp(-jnp.abs(fz))
    inv = 1.0 / (1.0 + e)
    pos = fz >= 0
    sig = jnp.where(pos, inv, e * inv)
    sig_neg = jnp.where(pos, e * inv, inv)
    logf = jnp.log(lb + (1.0 - lb) * sig)
    return logf, (1.0 - lb) * sig_neg


def _hg_out(o, gz, og):
    ms = jnp.mean(o * o, axis=-1, keepdims=True)
    return o * lax.rsqrt(ms + NORM_EPS) * og * _silu(gz)


def _gla_prompt_kernel(z_ref, lbt_ref, og_ref, o_ref, sfin_ref, st_scr, *, chunk, n_chunks):
    t_step = pl.program_id(1)
    kw = HG_HEADS * HG_DK

    @pl.when(t_step == 0)
    def _():
        st_scr[...] = jnp.zeros_like(st_scr)

    lb = _hg_lower_bound(lbt_ref)
    og = og_ref[...]
    r_i = lax.broadcasted_iota(I32, (chunk, chunk), 0)
    c_i = lax.broadcasted_iota(I32, (chunk, chunk), 1)
    causal = c_i <= r_i
    tri = causal.astype(BF16)
    mid = chunk // 2 - 1

    def body(c, carry):
        rows = pl.ds(pl.multiple_of(c * chunk, chunk), chunk)
        logf, kk = _hg_gates(z_ref[rows, kw:2 * kw], lb)
        hi, md, lo = _split3(logf)
        cs = _dot(tri, jnp.concatenate([hi, md, lo], axis=1))
        b = cs[:, :kw] + cs[:, kw:2 * kw] + cs[:, 2 * kw:]
        b_mid = b[mid:mid + 1]
        b_last = b[chunk - 1:chunk]
        q_hat = _silu(z_ref[rows, 0:kw]) * jnp.exp(b - b_mid)
        k_hat = kk * jnp.exp(b_mid - b)
        q_in = (q_hat * jnp.exp(b_mid)).astype(BF16)
        k_dec = (k_hat * jnp.exp(b_last - b_mid)).astype(BF16)
        q_hat = q_hat.astype(BF16)
        k_hat = k_hat.astype(BF16)
        dec = jnp.exp(b_last)
        for h in range(HG_HEADS):
            cols = slice(h * HG_DK, (h + 1) * HG_DK)
            vcols = slice(2 * kw + h * HG_DV, 2 * kw + (h + 1) * HG_DV)
            gcols = slice(2 * kw + HG_HEADS * HG_DV + h * HG_DV, 2 * kw + HG_HEADS * HG_DV + (h + 1) * HG_DV)
            v = z_ref[rows, vcols].astype(BF16)
            att = jnp.where(causal, _dot_nt(q_hat[:, cols], k_hat[:, cols]), 0.0).astype(BF16)
            st = st_scr[h]
            o = _dot(att, v) + _dot_nt(q_in[:, cols], st.astype(BF16))
            st_scr[h] = st * dec[:, cols] + _dot_tn(v, k_dec[:, cols])
            o_ref[rows, h * HG_DV:(h + 1) * HG_DV] = _hg_out(o, z_ref[rows, gcols], og).astype(o_ref.dtype)
        return carry

    lax.fori_loop(0, n_chunks, body, 0)

    @pl.when(t_step == pl.num_programs(1) - 1)
    def _():
        sfin_ref[0] = st_scr[...]


def _gla_prompt(geo, z, lb_table, o_gain):
    tg = min(512, geo.seq)
    chunk = HG_CHUNK if geo.seq % HG_CHUNK == 0 else geo.seq
    assert tg % chunk == 0 and geo.seq % tg == 0
    nt = geo.seq // tg
    return pl.pallas_call(
        functools.partial(_gla_prompt_kernel, chunk=chunk, n_chunks=tg // chunk),
        grid=(geo.batch, nt),
        in_specs=[
            pl.BlockSpec((tg, 4 * D_MODEL), lambda b, t: (b * nt + t, 0)),
            pl.BlockSpec(lb_table.shape, lambda b, t: (0, 0)),
            pl.BlockSpec((1, HG_DV), lambda b, t: (0, 0)),
        ],
        out_specs=[
            pl.BlockSpec((tg, D_MODEL), lambda b, t: (b * nt + t, 0)),
            pl.BlockSpec((1, HG_HEADS, HG_DV, HG_DK), lambda b, t: (b, 0, 0, 0)),
        ],
        out_shape=[
            jax.ShapeDtypeStruct((geo.n_prompt, D_MODEL), BF16),
            jax.ShapeDtypeStruct((geo.batch, HG_HEADS, HG_DV, HG_DK), F32),
        ],
        scratch_shapes=[pltpu.VMEM((HG_HEADS, HG_DV, HG_DK), F32)],
        compiler_params=_cparams(("arbitrary", "arbitrary")),
        name="gla_prompt",
    )(z, lb_table, o_gain)


def _gla_step_kernel(z_ref, s_ref, lbt_ref, og_ref, o_ref, snew_ref, *, steps, bb):
    kw = HG_HEADS * HG_DK
    pad = SUBLANES
    lb = _hg_lower_bound(lbt_ref)
    og = og_ref[...]
    r_i = lax.broadcasted_iota(I32, (pad, pad), 0)
    c_i = lax.broadcasted_iota(I32, (pad, pad), 1)
    causal = c_i <= r_i
    tri = causal.astype(BF16)
    row_w = lax.broadcasted_iota(I32, (pad, kw), 0)
    live = row_w < steps
    row_k = lax.broadcasted_iota(I32, (pad, HG_DK), 0)
    ones_sel = jnp.where((row_k == steps) | (row_k == steps + 1), 1.0, 0.0).astype(BF16)

    def body(s, carry):
        z = z_ref[s]
        logf, kk = _hg_gates(z[:, kw:2 * kw], lb)
        hi, md, lo = _split3(jnp.where(live, logf, 0.0))
        cs = _dot(tri, jnp.concatenate([hi, md, lo], axis=1))
        b = cs[:, :kw] + cs[:, kw:2 * kw] + cs[:, 2 * kw:]
        b_last = b[steps - 1:steps]
        q_in = (_silu(z[:, 0:kw]) * jnp.exp(b)).astype(BF16)
        k_hat = jnp.where(live, kk * jnp.exp(-b), 0.0).astype(BF16)
        k_dec = jnp.where(live, kk * jnp.exp(b_last - b), 0.0).astype(BF16)
        dec = jnp.exp(b_last)
        d_hi = dec.astype(BF16)
        d_lo = (dec - d_hi.astype(F32)).astype(BF16)
        a_all = jnp.where(row_w == steps, d_hi, jnp.where(row_w == steps + 1, d_lo, k_dec))
        for h in range(HG_HEADS):
            cols = slice(h * HG_DK, (h + 1) * HG_DK)
            vcols = slice(2 * kw + h * HG_DV, 2 * kw + (h + 1) * HG_DV)
            gcols = slice(2 * kw + HG_HEADS * HG_DV + h * HG_DV, 2 * kw + HG_HEADS * HG_DV + (h + 1) * HG_DV)
            v = z[:, vcols].astype(BF16)
            s0 = s_ref[s, h]
            att = jnp.where(causal, _dot_nt(q_in[:, cols], k_hat[:, cols]), 0.0).astype(BF16)
            o = _dot(att, v) + _dot(q_in[:, cols], s0.astype(BF16))
            upd = _dot_tn(a_all[:, cols], jnp.concatenate([v, ones_sel], axis=1))
            snew_ref[s, h] = upd[:, HG_DV:] * s0 + upd[:, :HG_DV]
            o_ref[s, :, h * HG_DV:(h + 1) * HG_DV] = _hg_out(o, z[:, gcols], og)
        return carry

    lax.fori_loop(0, bb, body, 0, unroll=min(4, bb))


def _seq_major(geo, rows):
    steps, db = geo.dec_seq, geo.dec_batch
    w = rows.shape[-1]
    r = rows.reshape(steps, db, w).transpose(1, 0, 2)
    return jnp.concatenate([r, jnp.zeros((db, SUBLANES - steps, w), rows.dtype)], axis=1)


def _time_major(geo, r):
    steps, db = geo.dec_seq, geo.dec_batch
    return r[:, :steps].transpose(1, 0, 2).reshape(steps * db, r.shape[-1])


def _gla_step(geo, z, state, lb_table, o_gain):
    steps, db = geo.dec_seq, geo.dec_batch
    assert steps + 2 <= SUBLANES
    bb = min(8, db)
    z_s = _seq_major(geo, z[geo.n_prompt:])
    o_s, s_new = pl.pallas_call(
        functools.partial(_gla_step_kernel, steps=steps, bb=bb),
        grid=(db // bb,),
        in_specs=[
            pl.BlockSpec((bb, SUBLANES, 4 * D_MODEL), lambda i: (i, 0, 0)),
            pl.BlockSpec((bb, HG_HEADS, HG_DK, HG_DV), lambda i: (i, 0, 0, 0)),
            pl.BlockSpec(lb_table.shape, lambda i: (0, 0)),
            pl.BlockSpec((1, HG_DV), lambda i: (0, 0)),
        ],
        out_specs=[
            pl.BlockSpec((bb, SUBLANES, D_MODEL), lambda i: (i, 0, 0)),
            pl.BlockSpec((bb, HG_HEADS, HG_DK, HG_DV), lambda i: (i, 0, 0, 0)),
        ],
        out_shape=[
            jax.ShapeDtypeStruct((db, SUBLANES, D_MODEL), F32),
            jax.ShapeDtypeStruct(state.shape, F32),
        ],
        compiler_params=_cparams(("arbitrary",)),
        name="gla_step",
    )(z_s, state, lb_table, o_gain)
    return _time_major(geo, o_s), s_new


def _post_kernel(ap_ref, as_ref, wo_ref, xp_ref, xs_ref, gp_ref, gs_ref, n2_ref, shp_ref, shs_ref, scp_ref,
                 scs_ref, rw_ref, rb_ref, x1_ref, h2_ref, pos_ref, pg_ref, nch_ref, soff_ref, *, n_pt, tm):
    i = pl.program_id(0)
    g1 = _pick_mod(i, n_pt, gp_ref, gs_ref)
    a = jnp.where(i >= n_pt, as_ref[...], ap_ref[...])
    x = jnp.where(i >= n_pt, xs_ref[...], xp_ref[...])
    x1 = x + g1 * _dot(a, wo_ref[...])
    x1_ref[...] = x1
    sh = _pick_mod(i, n_pt, shp_ref, shs_ref)
    sc = _pick_mod(i, n_pt, scp_ref, scs_ref)
    h2 = _norm_mod(x1, n2_ref[...], sc, sh)
    h2_ref[...] = h2.astype(h2_ref.dtype)

    row_e = lax.broadcasted_iota(I32, (LANES, tm), 0)
    logits = _dot_nt(rw_ref[...], h2.astype(BF16)) + rb_ref[...]
    work = jnp.where(row_e < N_EXPERTS, logits, -jnp.inf)
    vals, hits = [], []
    for _ in range(TOP_K):
        m = jnp.max(work, axis=0, keepdims=True)
        idx = jnp.min(jnp.where(work == m, row_e, LANES), axis=0, keepdims=True)
        hit = row_e == idx
        vals.append(m)
        hits.append(hit)
        work = jnp.where(hit, -jnp.inf, work)
    exps = [jnp.exp(v - vals[0]) for v in vals]
    den = exps[0]
    for e in exps[1:]:
        den = den + e
    any_hit = hits[0]
    for hmask in hits[1:]:
        any_hit = any_hit | hmask
    any_f = jnp.where(any_hit, 1.0, 0.0)
    t_i = lax.broadcasted_iota(I32, (tm, tm), 0)
    t_j = lax.broadcasted_iota(I32, (tm, tm), 1)
    rank = _dot(any_f.astype(BF16), (t_i < t_j).astype(BF16))
    n_chunk = jnp.floor((jnp.sum(any_f, axis=1, keepdims=True) + (ROW_CHUNK - 1)) * (1.0 / ROW_CHUNK))
    e_i = lax.broadcasted_iota(I32, (LANES, LANES), 0)
    e_j = lax.broadcasted_iota(I32, (LANES, LANES), 1)
    seg = _dot((e_j < e_i).astype(BF16), jnp.broadcast_to(n_chunk, (LANES, LANES)).astype(BF16))[:, 0:1]
    where_to = seg * float(ROW_CHUNK) + rank
    pos_rows = [jnp.sum(jnp.where(hits[k], where_to, 0.0), axis=0, keepdims=True) for k in range(TOP_K)]
    gate_rows = [exps[k] / den for k in range(TOP_K)]
    pos_ref[0] = jnp.concatenate(pos_rows + [jnp.zeros((SUBLANES - TOP_K, tm), F32)], axis=0).astype(I32)
    rows = jnp.concatenate(pos_rows + gate_rows + [jnp.zeros((LANES - 2 * TOP_K, tm), F32)], axis=0)
    pg_ref[...] = rows.T
    nch_ref[0] = n_chunk.astype(I32)
    soff_ref[0] = seg.astype(I32)


def _post(geo, a_p, a_s, w_out_bf16, x_pair, norm_g, mod_p, mod_s, col0, router_w, router_b):
    tm = geo.tm_moe
    n_pt = geo.n_prompt // tm
    n_tiles = geo.nt // tm
    rw = jnp.zeros((LANES, D_MODEL), BF16).at[:N_EXPERTS].set(router_w.T.astype(BF16))
    rb = jnp.zeros((LANES, 1), F32).at[:N_EXPERTS, 0].set(router_b)
    row = lambda i: (i, 0)
    fixed = lambda i: (0, 0)
    tile_row = lambda i: (i, 0, 0)
    return pl.pallas_call(
        functools.partial(_post_kernel, n_pt=n_pt, tm=tm),
        grid=(n_tiles,),
        in_specs=[
            *_pair_specs(geo, tm, D_MODEL),
            pl.BlockSpec((D_MODEL, D_MODEL), fixed),
            *_pair_specs(geo, tm, D_MODEL),
            *_mod_specs(geo, col0 + 2, tm),
            pl.BlockSpec((1, D_MODEL), fixed),
            *_mod_specs(geo, col0 + 3, tm),
            *_mod_specs(geo, col0 + 4, tm),
            pl.BlockSpec((LANES, D_MODEL), fixed),
            pl.BlockSpec((LANES, 1), fixed),
        ],
        out_specs=[
            pl.BlockSpec((tm, D_MODEL), row),
            pl.BlockSpec((tm, D_MODEL), row),
            pl.BlockSpec((1, SUBLANES, tm), tile_row),
            pl.BlockSpec((tm, LANES), row),
            pl.BlockSpec((1, LANES, 1), tile_row),
            pl.BlockSpec((1, LANES, 1), tile_row),
        ],
        out_shape=[
            jax.ShapeDtypeStruct((geo.nt, D_MODEL), F32),
            jax.ShapeDtypeStruct((geo.nt, D_MODEL), BF16),
            jax.ShapeDtypeStruct((n_tiles, SUBLANES, tm), I32),
            jax.ShapeDtypeStruct((geo.nt, LANES), F32),
            jax.ShapeDtypeStruct((n_tiles, LANES, 1), I32),
            jax.ShapeDtypeStruct((n_tiles, LANES, 1), I32),
        ],
        compiler_params=_cparams(("arbitrary",)),
        name="post_mixer_router",
    )(a_p, a_s.astype(BF16), w_out_bf16, *x_pair, mod_p, mod_s, norm_g, mod_p, mod_s, mod_p, mod_s, rw, rb)


def _ffn_kernel(first_ref, count_ref, xs_hbm, win_ref, bin_ref, wout_ref, bout_ref, ys_hbm,
                xbuf, ybuf, win_scr, wout_scr, xsem, ysem, *, half, chains):
    e = pl.program_id(0)
    first = first_ref[e]
    n_sub = count_ref[e]
    pairs = n_sub // 2
    odd = n_sub - 2 * pairs
    win_scr[...] = win_ref[0].astype(BF16)
    wout_scr[...] = wout_ref[0].astype(BF16)

    def rows_at(sub, n_rows):
        return pl.ds(pl.multiple_of((first + sub) * half, half), n_rows)

    def x_copy(sub, n_rows, slot):
        return pltpu.make_async_copy(xs_hbm.at[rows_at(sub, n_rows)], xbuf.at[slot, pl.ds(0, n_rows)], xsem.at[slot])

    def y_copy(sub, n_rows, slot):
        return pltpu.make_async_copy(ybuf.at[slot, pl.ds(0, n_rows)], ys_hbm.at[rows_at(sub, n_rows)], ysem.at[slot])

    def ffn(slot, rows):
        gu = _dot(xbuf[slot, rows, :].astype(BF16), win_scr[...]) + bin_ref[0]
        gate = jnp.minimum(gu[:, :D_MODEL], SWIGLU_LIMIT)
        up = jnp.clip(gu[:, D_MODEL:], -SWIGLU_LIMIT, SWIGLU_LIMIT)
        act = gate * _sigmoid(SWIGLU_ALPHA * gate) * (up + 1.0)
        return _dot(act.astype(BF16), wout_scr[...]) + bout_ref[0]

    @pl.when(pairs > 0)
    def _():
        x_copy(0, 2 * half, 0).start()

    @pl.when((pairs == 0) & (odd == 1))
    def _():
        x_copy(0, half, 0).start()

    def unit(u, carry):
        slot = u % 2
        x_copy(2 * u, 2 * half, slot).wait()

        @pl.when(u + 1 < pairs)
        def _():
            x_copy(2 * (u + 1), 2 * half, 1 - slot).start()

        @pl.when((u + 1 == pairs) & (odd == 1))
        def _():
            x_copy(2 * pairs, half, 1 - slot).start()

        @pl.when(u >= 2)
        def _():
            y_copy(0, 2 * half, slot).wait()

        if chains == 2:
            y0 = ffn(slot, slice(0, half))
            y1 = ffn(slot, slice(half, 2 * half))
            ybuf[slot, 0:half, :] = y0
            ybuf[slot, half:2 * half, :] = y1
        else:
            ybuf[slot] = ffn(slot, slice(0, 2 * half))
        y_copy(2 * u, 2 * half, slot).start()
        return carry

    lax.fori_loop(0, pairs, unit, 0)
    tail_slot = pairs % 2

    @pl.when(odd == 1)
    def _():
        x_copy(2 * pairs, half, tail_slot).wait()

        @pl.when(pairs >= 2)
        def _():
            y_copy(0, 2 * half, tail_slot).wait()

        ybuf[tail_slot, 0:half, :] = ffn(tail_slot, slice(0, half))
        y_copy(2 * pairs, half, tail_slot).start()

        @pl.when(pairs >= 1)
        def _():
            y_copy(0, 2 * half, 1 - tail_slot).wait()

        y_copy(0, half, tail_slot).wait()

    @pl.when(odd == 0)
    def _():
        @pl.when(pairs >= 2)
        def _():
            y_copy(0, 2 * half, tail_slot).wait()

        @pl.when(pairs >= 1)
        def _():
            y_copy(0, 2 * half, 1 - tail_slot).wait()


def _ffn(xs, first_sub, n_sub, layer, w_in, b_in, w_out, b_out, chains):
    n_e, d, d2 = w_in.shape
    half = MOE_BLOCK
    w_map = lambda e, *_: (layer * N_EXPERTS + e, 0, 0)
    return pl.pallas_call(
        functools.partial(_ffn_kernel, half=half, chains=chains),
        grid_spec=pltpu.PrefetchScalarGridSpec(
            num_scalar_prefetch=2,
            grid=(N_EXPERTS,),
            in_specs=[
                pl.BlockSpec(memory_space=pl.ANY),
                pl.BlockSpec((1, d, d2), w_map),
                pl.BlockSpec((1, 1, d2), w_map),
                pl.BlockSpec((1, d2 // 2, d), w_map),
                pl.BlockSpec((1, 1, d), w_map),
            ],
            out_specs=pl.BlockSpec(memory_space=pl.ANY),
            scratch_shapes=[
                pltpu.VMEM((2, 2 * half, d), F32),
                pltpu.VMEM((2, 2 * half, d), F32),
                pltpu.VMEM((d, d2), BF16),
                pltpu.VMEM((d2 // 2, d), BF16),
                pltpu.SemaphoreType.DMA((2,)),
                pltpu.SemaphoreType.DMA((2,)),
            ],
        ),
        out_shape=jax.ShapeDtypeStruct(xs.shape, xs.dtype),
        input_output_aliases={2: 0},
        compiler_params=_cparams(("arbitrary",)),
        name="moe_ffn",
    )(first_sub, n_sub, xs, w_in, b_in.reshape(n_e, 1, d2), w_out, b_out.reshape(n_e, 1, d))


def _chunk_rows(chunk, n_chunks=1):
    return pl.ds(pl.multiple_of(chunk * ROW_CHUNK, ROW_CHUNK), n_chunks * ROW_CHUNK)


def _run_pieces(n, fn, big):
    n_big = n // big

    def big_piece(i, carry):
        fn(i * big, big)
        return carry

    def small_piece(i, carry):
        fn(n_big * big + i, 1)
        return carry

    lax.fori_loop(0, n_big, big_piece, 0)
    lax.fori_loop(0, n - n_big * big, small_piece, 0)


def _for_each_chunk(where_ref, tot_ref, tile, fn):
    n = tot_ref[tile]
    n_groups = n // CHUNK_UNROLL

    def group(i, carry):
        for u in range(CHUNK_UNROLL):
            j = i * CHUNK_UNROLL + u
            fn(j, where_ref[tile, j])
        return carry

    def single(i, carry):
        j = n_groups * CHUNK_UNROLL + i
        fn(j, where_ref[tile, j])
        return carry

    lax.fori_loop(0, n_groups, group, 0)
    lax.fori_loop(0, n - n_groups * CHUNK_UNROLL, single, 0)


def _dispatch_kernel(where_ref, tot_ref, fill0_ref, filln_ref,
                     pos_ref, h_ref, xs_hbm, stage, zeros, sem, fill_sem, *, tm, k_stage):
    t = pl.program_id(0)
    n_t = pl.num_programs(0)
    slot = t % 2

    def copy(s_chunk, d_chunk, size, sl):
        return pltpu.make_async_copy(stage.at[sl, _chunk_rows(s_chunk, size)],
                                     xs_hbm.at[_chunk_rows(d_chunk, size)], sem.at[sl])

    def wait_tile(tile, sl):
        _run_pieces(tot_ref[tile], lambda off, size: copy(0, 0, size, sl).wait(), WAIT_CHUNKS)

    @pl.when(t >= 2)
    def _():
        wait_tile(t - 2, slot)

    pos_t = pos_ref[0]
    j_i = lax.broadcasted_iota(I32, (k_stage, tm), 0)
    hit = j_i == pos_t[0:1]
    for k in range(1, TOP_K):
        hit = hit | (j_i == pos_t[k:k + 1])
    stage[slot] = _dot(jnp.where(hit, 1.0, 0.0).astype(BF16), h_ref[...])
    _for_each_chunk(where_ref, tot_ref, t, lambda s, d: copy(s, d, 1, slot).start())

    @pl.when(t == n_t - 1)
    def _():
        zeros[...] = jnp.zeros_like(zeros)

        def fill(d_chunk, size):
            return pltpu.make_async_copy(zeros.at[_chunk_rows(0, size)], xs_hbm.at[_chunk_rows(d_chunk, size)],
                                         fill_sem.at[0])

        def per_region(r, carry):
            _run_pieces(filln_ref[r], lambda off, size: fill(fill0_ref[r] + off, size).start(), COPY_CHUNKS)
            return carry

        lax.fori_loop(0, N_EXPERTS + 1, per_region, 0)

        @pl.when(t >= 1)
        def _():
            wait_tile(t - 1, 1 - slot)

        wait_tile(t, slot)

        def per_region_wait(r, carry):
            _run_pieces(filln_ref[r], lambda off, size: fill(0, size).wait(), COPY_CHUNKS)
            return carry

        lax.fori_loop(0, N_EXPERTS + 1, per_region_wait, 0)


def _combine_kernel(where_ref, tot_ref, pg_ref, x_ref, gp_ref, gs_ref, y_hbm,
                    op_ref, os_ref, stage, sem, *, n_pt, tm, k_stage):
    t = pl.program_id(0)
    n_t = pl.num_programs(0)
    slot = t % 2

    def copy(s_chunk, d_chunk, size, sl):
        return pltpu.make_async_copy(y_hbm.at[_chunk_rows(d_chunk, size)],
                                     stage.at[sl, _chunk_rows(s_chunk, size)], sem.at[sl])

    def fetch(tile, sl):
        _for_each_chunk(where_ref, tot_ref, tile, lambda s, d: copy(s, d, 1, sl).start())

    @pl.when(t == 0)
    def _():
        stage[...] = jnp.zeros_like(stage)
        fetch(0, 0)

    @pl.when(t + 1 < n_t)
    def _():
        fetch(t + 1, 1 - slot)

    _run_pieces(tot_ref[t], lambda off, size: copy(0, 0, size, slot).wait(), WAIT_CHUNKS)
    lane = lax.broadcasted_iota(I32, (tm, k_stage), 1)
    pg = pg_ref[...]
    p = jnp.zeros((tm, k_stage), F32)
    for k in range(TOP_K):
        p = jnp.where(lane == pg[:, k:k + 1].astype(I32), pg[:, TOP_K + k:TOP_K + k + 1], p)
    ffn = _dot(p.astype(BF16), stage[slot].astype(BF16))
    out = x_ref[...] + _pick_mod(t, n_pt, gp_ref, gs_ref) * ffn

    @pl.when(t < n_pt)
    def _():
        op_ref[...] = out

    @pl.when(t >= n_pt)
    def _():
        os_ref[...] = out


def _moe(geo, h2, pos, pg, nch_pad, soff_pad, x1, mod_p, mod_s, col_gate, layer, w_in, b_in, w_out, b_out,
         chains):
    bm = MOE_BLOCK
    tm = geo.tm_moe
    n_tiles = geo.nt // tm
    depth, n_e, d, d2 = w_in.shape
    w_in = w_in.reshape(depth * n_e, d, d2)
    b_in = b_in.reshape(depth * n_e, d2)
    w_out = w_out.reshape(depth * n_e, d2 // 2, d)
    b_out = b_out.reshape(depth * n_e, d)
    chunks_per_block = bm // ROW_CHUNK
    max_rows = geo.nt * TOP_K + n_tiles * N_EXPERTS * (ROW_CHUNK - 1) + N_EXPERTS * (bm - 1)
    n_blocks = -(-max_rows // bm)
    nch = nch_pad[:, :N_EXPERTS, 0]
    soff = soff_pad[:, :N_EXPERTS, 0]
    tot = jnp.sum(nch, axis=1)
    per_e = jnp.sum(nch, axis=0)
    padded = (per_e + chunks_per_block - 1) // chunks_per_block * chunks_per_block
    pend = jnp.cumsum(padded)
    pstart = pend - padded
    dst = pstart[None, :] + jnp.cumsum(nch, axis=0) - nch
    first_sub = (pstart // chunks_per_block).astype(I32)
    n_sub = (padded // chunks_per_block).astype(I32)
    fill0 = jnp.concatenate([pstart + per_e, pend[-1:]]).astype(I32)
    filln = jnp.concatenate([padded - per_e, n_blocks * chunks_per_block - pend[-1:]]).astype(I32)

    k_stage = -(-(tm * TOP_K + N_EXPERTS * (ROW_CHUNK - 1)) // LANES) * LANES
    j = jnp.arange(k_stage // ROW_CHUNK, dtype=I32)
    owner = jnp.minimum(jnp.sum(j[None, :, None] >= (soff + nch)[:, None, :], axis=-1), N_EXPERTS - 1)
    shift = jnp.sum(jnp.where(owner[:, :, None] == jnp.arange(N_EXPERTS, dtype=I32), (dst - soff)[:, None, :], 0),
                    axis=-1)
    where = (j[None, :] + shift).astype(I32)
    row = lambda i, *_: (i, 0)
    xs = pl.pallas_call(
        functools.partial(_dispatch_kernel, tm=tm, k_stage=k_stage),
        grid_spec=pltpu.PrefetchScalarGridSpec(
            num_scalar_prefetch=4,
            grid=(n_tiles,),
            in_specs=[pl.BlockSpec((1, SUBLANES, tm), lambda i, *_: (i, 0, 0)), pl.BlockSpec((tm, d), row)],
            out_specs=pl.BlockSpec(memory_space=pl.ANY),
            scratch_shapes=[
                pltpu.VMEM((2, k_stage, d), F32),
                pltpu.VMEM((COPY_CHUNKS * ROW_CHUNK, d), F32),
                pltpu.SemaphoreType.DMA((2,)),
                pltpu.SemaphoreType.DMA((1,)),
            ],
        ),
        out_shape=jax.ShapeDtypeStruct((n_blocks * bm, d), F32),
        compiler_params=_cparams(("arbitrary",)),
        name="moe_dispatch",
    )(where, tot, fill0, filln, pos, h2)

    ys = _ffn(xs, first_sub, n_sub, layer, w_in, b_in, w_out, b_out, chains)

    return pl.pallas_call(
        functools.partial(_combine_kernel, n_pt=geo.n_prompt // tm, tm=tm, k_stage=k_stage),
        grid_spec=pltpu.PrefetchScalarGridSpec(
            num_scalar_prefetch=2,
            grid=(n_tiles,),
            in_specs=[
                pl.BlockSpec((tm, LANES), row),
                pl.BlockSpec((tm, d), row),
                *_mod_specs(geo, col_gate, tm),
                pl.BlockSpec(memory_space=pl.ANY),
            ],
            out_specs=_pair_specs(geo, tm, d),
            scratch_shapes=[
                pltpu.VMEM((2, k_stage, d), F32),
                pltpu.SemaphoreType.DMA((2,)),
            ],
        ),
        out_shape=[jax.ShapeDtypeStruct((geo.n_prompt, d), F32), jax.ShapeDtypeStruct((geo.n_sample, d), F32)],
        compiler_params=_cparams(("arbitrary",)),
        name="moe_combine",
    )(where, tot, pg, x1, mod_p, mod_s, ys)


SW_QW = SW_Q_HEADS * SW_HEAD_DIM
SW_QKW = SW_QW + SW_KV_W
SW_KEYS = 2 * ATT_BLOCK


def _attn_proj_kernel(xp_ref, xs_ref, g_ref, shp_ref, shs_ref, scp_ref, scs_ref, w_ref, e_ref, et_ref, qg_ref,
                      kg_ref, q_ref, k_ref, v_ref, *, n_pt):
    i = pl.program_id(0)
    sh = _pick_mod(i, n_pt, shp_ref, shs_ref)
    sc = _pick_mod(i, n_pt, scp_ref, scs_ref)
    x = jnp.where(i >= n_pt, xs_ref[...], xp_ref[...])
    h = _norm_mod(x, g_ref[...], sc, sh).astype(BF16)
    z = _dot(h, w_ref[...])
    qk = z[:, :SW_QKW]
    sq = qk * qk
    sq_hi = sq.astype(BF16)
    sq_lo = (sq - sq_hi.astype(F32)).astype(BF16)
    ms = (_dot(sq_hi, e_ref[...]) + _dot(sq_lo, e_ref[...])) * (1.0 / SW_HEAD_DIM)
    inv = lax.rsqrt(ms + NORM_EPS)
    inv_hi = inv.astype(BF16)
    inv_lo = (inv - inv_hi.astype(F32)).astype(BF16)
    qk = qk * (_dot(inv_hi, et_ref[...]) + _dot(inv_lo, et_ref[...]))
    q_ref[...] = (qk[:, :SW_QW] * qg_ref[...] * SW_SCALE).astype(q_ref.dtype)
    k_ref[...] = qk[:, SW_QW:] * kg_ref[...]
    v_ref[...] = z[:, SW_QKW:]


def _attn_proj(geo, x_pair, g, mod_p, mod_s, w_bf16, q_gain, k_gain):
    tm = geo.tm
    n_out = w_bf16.shape[1]
    heads = SW_QKW // SW_HEAD_DIM
    member = (np.arange(SW_QKW)[:, None] // SW_HEAD_DIM == np.arange(LANES)[None, :]).astype(np.float32)
    e = jnp.asarray(member, BF16)
    et = jnp.asarray(member.T, BF16)
    assert heads <= LANES
    qg = jnp.tile(q_gain, SW_Q_HEADS).reshape(1, SW_QW)
    kg = jnp.tile(k_gain, SW_KV_HEADS).reshape(1, SW_KV_W)
    row = lambda i: (i, 0)
    fixed = lambda i: (0, 0)
    return pl.pallas_call(
        functools.partial(_attn_proj_kernel, n_pt=geo.n_pt),
        grid=(geo.n_tiles,),
        in_specs=[
            *_pair_specs(geo, tm, D_MODEL),
            pl.BlockSpec((1, D_MODEL), fixed),
            *_mod_specs(geo, 0, tm),
            *_mod_specs(geo, 1, tm),
            pl.BlockSpec((D_MODEL, n_out), fixed),
            pl.BlockSpec((SW_QKW, LANES), fixed),
            pl.BlockSpec((LANES, SW_QKW), fixed),
            pl.BlockSpec((1, SW_QW), fixed),
            pl.BlockSpec((1, SW_KV_W), fixed),
        ],
        out_specs=[
            pl.BlockSpec((tm, SW_QW), row),
            pl.BlockSpec((tm, SW_KV_W), row),
            pl.BlockSpec((tm, SW_KV_W), row),
        ],
        out_shape=[
            jax.ShapeDtypeStruct((geo.nt, SW_QW), BF16),
            jax.ShapeDtypeStruct((geo.nt, SW_KV_W), F32),
            jax.ShapeDtypeStruct((geo.nt, SW_KV_W), F32),
        ],
        compiler_params=_cparams(("arbitrary",)),
        name="attn_proj_qknorm",
    )(*x_pair, g, mod_p, mod_s, mod_p, mod_s, w_bf16, e, et, qg, kg)


def _rel_bucket_np(dist):
    n = np.maximum(dist, 0)
    max_exact = REL_BUCKETS // 2
    ratio = np.log(np.maximum(n, 1).astype(np.float32) / np.float32(max_exact)) / np.float32(
        math.log(REL_MAX_DIST / max_exact))
    large = max_exact + (ratio * np.float32(REL_BUCKETS - max_exact)).astype(np.int32)
    large = np.minimum(large, REL_BUCKETS - 1)
    return np.where(n < max_exact, n, large).astype(np.int32)


def _bucket_table(qpos, kpos, k_ok):
    dist = qpos[:, None] - kpos[None, :]
    ok = (dist >= 0) & (dist <= WINDOW) & k_ok[None, :]
    return np.where(ok, _rel_bucket_np(dist), -1).astype(np.int32)


def _build_bias(bkt_ref, rb_ref, bias_scr, rows):
    bkt = bkt_ref[...]
    base = jnp.where(bkt < 0, -jnp.inf, 0.0)
    for h in range(SW_Q_HEADS):
        bias_scr[h * rows:(h + 1) * rows, :] = base

    def add_bucket(j, carry):
        hit = bkt == j
        for h in range(SW_Q_HEADS):
            sl = slice(h * rows, (h + 1) * rows)
            bias_scr[sl, :] = bias_scr[sl, :] + jnp.where(hit, rb_ref[j, h], 0.0)
        return carry

    lax.fori_loop(0, REL_BUCKETS, add_bucket, 0)


def _sink_softmax_pv(s, sink, vv_g):
    m = jnp.maximum(jnp.max(s, axis=-1, keepdims=True), sink)
    e = jnp.exp(s - m)
    p = e / (jnp.sum(e, axis=-1, keepdims=True) + jnp.exp(sink - m))
    return _dot(p.astype(BF16), vv_g)


def _build_bias_t(bkt_ref, rb_ref, bias_scr, blk):
    bkt = bkt_ref[...]
    base = jnp.where(bkt < 0, -jnp.inf, 0.0)
    slots = [(g, j) for g in range(SW_KV_HEADS) for j in range(SW_GROUP)]
    for g, j in slots:
        bias_scr[g, :, j * blk:(j + 1) * blk] = base

    def add_bucket(b, carry):
        hit = bkt == b
        for g, j in slots:
            cols = slice(j * blk, (j + 1) * blk)
            bias_scr[g, :, cols] = bias_scr[g, :, cols] + jnp.where(hit, rb_ref[b, g * SW_GROUP + j], 0.0)
        return carry

    lax.fori_loop(0, REL_BUCKETS, add_bucket, 0)


def _swa_prompt_kernel(rb_ref, sink_ref, bkt_ref, q_ref, kp_ref, kc_ref, vp_ref, vc_ref, o_ref, bias_scr, *, qb):
    n = pl.program_id(1)
    blk = ATT_BLOCK
    hd = SW_HEAD_DIM

    @pl.when((pl.program_id(0) == 0) & (n == 0))
    def _():
        _build_bias_t(bkt_ref, rb_ref, bias_scr, blk)

    key_i = lax.broadcasted_iota(I32, (SW_KEYS, SW_GROUP * blk), 0)
    hide_prev = (n == 0) & (key_i < blk)
    zeros_q = jnp.zeros((hd, SW_GROUP * blk), BF16)
    sinks = [jnp.concatenate([jnp.full((1, blk), sink_ref[g * SW_GROUP + j], F32) for j in range(SW_GROUP)], axis=1)
             for g in range(SW_KV_HEADS)]
    for i in range(qb):
        rows = slice(i * blk, (i + 1) * blk)
        if i == 0:
            k2 = jnp.concatenate([kp_ref[...], kc_ref[rows, :]], axis=0)
            v2 = jnp.concatenate([vp_ref[...], vc_ref[rows, :]], axis=0)
        else:
            k2 = kc_ref[(i - 1) * blk:(i + 1) * blk, :]
            v2 = vc_ref[(i - 1) * blk:(i + 1) * blk, :]
        kk = k2.astype(BF16)
        vv_t = v2.T.astype(BF16)
        q_t = q_ref[rows, :].astype(F32).T.astype(BF16)
        outs = []
        for g in range(SW_KV_HEADS):
            heads = [g * SW_GROUP + j for j in range(SW_GROUP)]
            q_cat = jnp.concatenate([q_t[hq * hd:(hq + 1) * hd, :] for hq in heads], axis=1)
            q_full = jnp.concatenate([q_cat if gg == g else zeros_q for gg in range(SW_KV_HEADS)], axis=0)
            s = _dot(kk, q_full) + bias_scr[g]
            if i == 0:
                s = jnp.where(hide_prev, -jnp.inf, s)
            m = jnp.maximum(jnp.max(s, axis=0, keepdims=True), sinks[g])
            e = jnp.exp(s - m)
            den = jnp.sum(e, axis=0, keepdims=True) + jnp.exp(sinks[g] - m)
            o_t = _dot(vv_t[g * hd:(g + 1) * hd, :], e.astype(BF16)) * (1.0 / den)
            outs.extend(o_t[:, j * blk:(j + 1) * blk] for j in range(SW_GROUP))
        o_ref[rows, :] = jnp.concatenate(outs, axis=0).T.astype(o_ref.dtype)


def _swa_prompt(geo, q, k, v, rel_bias, sinks):
    blk = ATT_BLOCK
    assert geo.seq % blk == 0 and WINDOW == blk
    n_blk = geo.seq // blk
    qb = 4 if n_blk % 4 == 0 else (2 if n_blk % 2 == 0 else 1)
    nb = n_blk // qb
    bkt = jnp.asarray(_bucket_table(blk + np.arange(blk), np.arange(2 * blk), np.ones(2 * blk, bool)).T.copy())
    cur = lambda b, n: (b * nb + n, 0)
    prev = lambda b, n: (b * n_blk + jnp.maximum(n * qb - 1, 0), 0)
    smem = pl.BlockSpec(memory_space=pltpu.SMEM)
    return pl.pallas_call(
        functools.partial(_swa_prompt_kernel, qb=qb),
        grid=(geo.batch, nb),
        in_specs=[
            smem, smem,
            pl.BlockSpec((SW_KEYS, blk), lambda b, n: (0, 0)),
            pl.BlockSpec((qb * blk, SW_QW), cur),
            pl.BlockSpec((blk, SW_KV_W), prev),
            pl.BlockSpec((qb * blk, SW_KV_W), cur),
            pl.BlockSpec((blk, SW_KV_W), prev),
            pl.BlockSpec((qb * blk, SW_KV_W), cur),
        ],
        out_specs=pl.BlockSpec((qb * blk, SW_QW), cur),
        out_shape=jax.ShapeDtypeStruct((geo.n_prompt, SW_QW), BF16),
        scratch_shapes=[pltpu.VMEM((SW_KV_HEADS, SW_KEYS, SW_GROUP * blk), F32)],
        compiler_params=_cparams(("arbitrary", "arbitrary")),
        name="swa_prompt",
    )(rel_bias, sinks, bkt, q, k, k, v, v)


def _swa_step_kernel(rb_ref, sink_ref, bkt_ref, q_ref, kn_ref, vn_ref, ck_ref, cv_ref,
                     o_ref, nk_ref, nv_ref, bias_scr, *, steps, bb):
    pad = SUBLANES
    win = WINDOW

    @pl.when(pl.program_id(0) == 0)
    def _():
        _build_bias(bkt_ref, rb_ref, bias_scr, pad)

    fill = jnp.zeros((SW_KEYS - win - pad, SW_KV_W), F32)
    row8 = lax.broadcasted_iota(I32, (pad, SW_KV_W), 0)

    def shifted(cache, new):
        rolled = pltpu.roll(cache, win - steps, 0)
        tail = jnp.where(row8 < pad - steps, rolled[win - pad:], pltpu.roll(new, pad - steps, 0))
        return rolled[:win - pad], tail

    def body(s, carry):
        ck, cv, kn, vn = ck_ref[s], cv_ref[s], kn_ref[s], vn_ref[s]
        kk = jnp.concatenate([ck, kn, fill], axis=0).astype(BF16)
        vv = jnp.concatenate([cv, vn, fill], axis=0).astype(BF16)
        q = q_ref[s]
        for g in range(SW_KV_HEADS):
            kcols = slice(g * SW_HEAD_DIM, (g + 1) * SW_HEAD_DIM)
            heads = [g * SW_GROUP + j for j in range(SW_GROUP)]
            qs = jnp.concatenate([q[:, hq * SW_HEAD_DIM:(hq + 1) * SW_HEAD_DIM] for hq in heads], axis=0)
            sc = _dot_nt(qs.astype(BF16), kk[:, kcols]) + bias_scr[heads[0] * pad:(heads[-1] + 1) * pad, :]
            sink = jnp.concatenate([jnp.full((pad, 1), sink_ref[hq], F32) for hq in heads], axis=0)
            og = _sink_softmax_pv(sc, sink, vv[:, kcols])
            for j, hq in enumerate(heads):
                o_ref[s, :, hq * SW_HEAD_DIM:(hq + 1) * SW_HEAD_DIM] = og[j * pad:(j + 1) * pad]
        head, tail = shifted(ck, kn)
        nk_ref[s, 0:win - pad] = head
        nk_ref[s, win - pad:win] = tail
        head, tail = shifted(cv, vn)
        nv_ref[s, 0:win - pad] = head
        nv_ref[s, win - pad:win] = tail
        return carry

    lax.fori_loop(0, bb, body, 0, unroll=min(4, bb))


def _swa_step(geo, q, k, v, cache_k, cache_v, rel_bias, sinks):
    steps, db = geo.dec_seq, geo.dec_batch
    win = cache_k.shape[1]
    assert win == WINDOW and steps <= SUBLANES
    bb = min(8, db)
    pad = SUBLANES
    q_s = _seq_major(geo, q[geo.n_prompt:].astype(F32))
    k_s = _seq_major(geo, k[geo.n_prompt:])
    v_s = _seq_major(geo, v[geo.n_prompt:])
    kpos = np.arange(SW_KEYS)
    k_ok = kpos < win + steps
    qpos = win + np.arange(pad)
    bkt = _bucket_table(qpos, kpos, k_ok)
    bkt[steps:] = -1
    smem = pl.BlockSpec(memory_space=pltpu.SMEM)
    blk3 = lambda r, w: pl.BlockSpec((bb, r, w), lambda i: (i, 0, 0))
    o_s, nk, nv = pl.pallas_call(
        functools.partial(_swa_step_kernel, steps=steps, bb=bb),
        grid=(db // bb,),
        in_specs=[
            smem, smem,
            pl.BlockSpec((pad, SW_KEYS), lambda i: (0, 0)),
            blk3(pad, SW_QW), blk3(pad, SW_KV_W), blk3(pad, SW_KV_W),
            blk3(win, SW_KV_W), blk3(win, SW_KV_W),
        ],
        out_specs=[blk3(pad, SW_QW), blk3(win, SW_KV_W), blk3(win, SW_KV_W)],
        out_shape=[
            jax.ShapeDtypeStruct((db, pad, SW_QW), F32),
            jax.ShapeDtypeStruct((db, win, SW_KV_W), F32),
            jax.ShapeDtypeStruct((db, win, SW_KV_W), F32),
        ],
        scratch_shapes=[pltpu.VMEM((SW_Q_HEADS * pad, SW_KEYS), F32)],
        compiler_params=_cparams(("arbitrary",)),
        name="swa_step",
    )(rel_bias, sinks, jnp.asarray(bkt), q_s, k_s, v_s, cache_k, cache_v)
    return _time_major(geo, o_s), nk, nv


def kernel(x_prompt, x_sample, state_hgrn, cache_win_k, cache_win_v, c_prompt, c_sample, norm1_g, norm2_g, ada_w, ada_b, hg_w_in, hg_lb_table, hg_onorm_g, hg_w_out, sw_w_in, sw_qnorm_g, sw_knorm_g, sw_sinks, sw_w_out, rel_bias, router_w, router_b, moe_w_in, moe_b_in, moe_w_out, moe_b_out):
    batch, seq, d = x_prompt.shape
    db, steps, _ = x_sample.shape
    assert d == D_MODEL and ada_w.shape[0] == 2 and hg_w_in.shape[0] == 1 and sw_w_in.shape[0] == 1
    geo = _Geom(batch, seq, db, steps)
    x = (x_prompt.reshape(batch * seq, d), x_sample.transpose(1, 0, 2).reshape(steps * db, d))
    n_seq = batch + db
    rows = -(-n_seq // SUBLANES) * SUBLANES
    c_all = jnp.concatenate([c_prompt, c_sample, jnp.zeros((rows - n_seq, d), F32)], axis=0)
    mods = _ada_mods(c_all, ada_w, ada_b)

    def layer_mods(layer):
        mod_p = mods[layer, :batch].reshape(batch, 1, 6 * d)
        mod_s = jnp.tile(mods[layer, batch:n_seq], (steps, 1))
        return mod_p, mod_s

    def moe(layer, a_p, a_s, w_out, x_in, mod_p, mod_s):
        x1, h2, pos, pg, nch, soff = _post(geo, a_p, a_s, w_out.astype(BF16), x_in, norm2_g[layer:layer + 1],
                                           mod_p, mod_s, 0, router_w[layer], router_b[layer])
        return _moe(geo, h2, pos, pg, nch, soff, x1, mod_p, mod_s, 5,
                    layer, moe_w_in, moe_b_in, moe_w_out, moe_b_out, chains=1 + layer)

    mod_p, mod_s = layer_mods(0)
    og = hg_onorm_g[0:1]
    z = _proj(geo, x, norm1_g[0:1], mod_p, mod_s, hg_w_in[0].astype(BF16), 0, 1)
    o_p, st_p = _gla_prompt(geo, z, hg_lb_table, og)
    o_s, st_s = _gla_step(geo, z, state_hgrn[0], hg_lb_table, og)
    x = moe(0, o_p, o_s, hg_w_out[0], x, mod_p, mod_s)

    mod_p, mod_s = layer_mods(1)
    q, k, v = _attn_proj(geo, x, norm1_g[1:2], mod_p, mod_s, sw_w_in[0].astype(BF16), sw_qnorm_g[0], sw_knorm_g[0])
    win = cache_win_k.shape[2]
    a_p = _swa_prompt(geo, q, k, v, rel_bias, sw_sinks[0])
    a_s, nk, nv = _swa_step(geo, q, k, v, cache_win_k[0].reshape(db, win, SW_KV_W),
                            cache_win_v[0].reshape(db, win, SW_KV_W), rel_bias, sw_sinks[0])
    x = moe(1, a_p, a_s, sw_w_out[0], x, mod_p, mod_s)

    y_prompt = x[0].reshape(batch, seq, d)
    y_sample = x[1].reshape(steps, db, d).transpose(1, 0, 2)
    kv_shape = (1, batch, WINDOW, SW_KV_HEADS, SW_HEAD_DIM)
    k_p = k[:geo.n_prompt].reshape(batch, seq, SW_KV_W)[:, seq - WINDOW:].reshape(kv_shape)
    v_p = v[:geo.n_prompt].reshape(batch, seq, SW_KV_W)[:, seq - WINDOW:].reshape(kv_shape)
    cache_shape = (1, db, win, SW_KV_HEADS, SW_HEAD_DIM)
    return (y_prompt, y_sample, jnp.swapaxes(st_p, -1, -2)[None], st_s[None], k_p, v_p,
            nk.reshape(cache_shape), nv.reshape(cache_shape))
```

```python
import functools
import math

import numpy as np
import jax
import jax.numpy as jnp
from jax import lax
from jax.experimental import pallas as pl
from jax.experimental.pallas import tpu as pltpu

F32 = jnp.float32
BF16 = jnp.bfloat16
I32 = jnp.int32

D_MODEL = 1024
LANES = 128
SUBLANES = 8
D_TILES = D_MODEL // LANES
HG_DK = 128
HG_HEADS = D_MODEL // HG_DK
HG_DV = D_MODEL // HG_HEADS
HG_CHUNK = 64
SW_HEAD_DIM = 64
SW_Q_HEADS = D_MODEL // SW_HEAD_DIM
SW_KV_HEADS = 4
SW_GROUP = SW_Q_HEADS // SW_KV_HEADS
SW_KV_W = SW_KV_HEADS * SW_HEAD_DIM
WINDOW = 128
ATT_BLOCK = 128
SW_SCALE = SW_HEAD_DIM ** -0.5
REL_BUCKETS = 32
REL_MAX_DIST = 128
N_EXPERTS = 32
TOP_K = 4
SWIGLU_LIMIT = 7.0
SWIGLU_ALPHA = 1.702
NORM_EPS = 1e-5
MOE_BLOCK = 256
ROW_CHUNK = SUBLANES
COPY_CHUNKS = 4
CHUNK_UNROLL = 8
WAIT_CHUNKS = 16
VMEM_LIMIT = 56 * 1024 * 1024


def _cparams(sem):
    return pltpu.CompilerParams(dimension_semantics=sem, vmem_limit_bytes=VMEM_LIMIT)


def _sigmoid(x):
    return 0.5 * jnp.tanh(0.5 * x) + 0.5


def _silu(x):
    return x * _sigmoid(x)


def _dot(a, b):
    return jnp.dot(a, b, preferred_element_type=F32)


def _dot_nt(a, b):
    return lax.dot_general(a, b, (((1,), (1,)), ((), ())), preferred_element_type=F32)


def _dot_tn(a, b):
    return lax.dot_general(a, b, (((0,), (0,)), ((), ())), preferred_element_type=F32)


def _split3(x):
    hi = x.astype(BF16)
    r = x - hi.astype(F32)
    mid = r.astype(BF16)
    lo = (r - mid.astype(F32)).astype(BF16)
    return hi, mid, lo


def _ada_kernel(c_ref, w_ref, b_ref, o_ref):
    s = _silu(c_ref[...]).astype(BF16)
    o_ref[0] = _dot(s, w_ref[0].astype(BF16)) + b_ref[0]


def _ada_mods(c_all, ada_w, ada_b):
    depth, d, n6 = ada_w.shape
    rows = c_all.shape[0]
    tn = 1536
    return pl.pallas_call(
        _ada_kernel,
        grid=(depth, n6 // tn),
        in_specs=[
            pl.BlockSpec((rows, d), lambda l, j: (0, 0)),
            pl.BlockSpec((1, d, tn), lambda l, j: (l, 0, j)),
            pl.BlockSpec((1, 1, tn), lambda l, j: (l, 0, j)),
        ],
        out_specs=pl.BlockSpec((1, rows, tn), lambda l, j: (l, 0, j)),
        out_shape=jax.ShapeDtypeStruct((depth, rows, n6), F32),
        compiler_params=_cparams(("arbitrary", "arbitrary")),
        name="ada_mods",
    )(c_all, ada_w, ada_b.reshape(depth, 1, n6))


class _Geom:
    def __init__(self, batch, seq, dec_batch, dec_seq):
        self.batch, self.seq, self.dec_batch, self.dec_seq = batch, seq, dec_batch, dec_seq
        self.n_prompt = batch * seq
        self.n_sample = dec_batch * dec_seq
        self.nt = self.n_prompt + self.n_sample
        tm = 512
        while seq % tm or self.n_sample % tm:
            tm //= 2
        assert tm >= 8
        self.tm = tm
        self.n_pt = self.n_prompt // tm
        self.n_tiles = self.nt // tm
        self.tm_moe = min(256, tm)


def _mod_specs(geo, col, tm):
    seq, batch = geo.seq, geo.batch
    n_pt = geo.n_prompt // tm

    def p_map(i, *_):
        return (jnp.minimum(i * tm // seq, batch - 1), 0, col)

    def s_map(i, *_):
        return (jnp.maximum(i - n_pt, 0), col)

    return [pl.BlockSpec((1, 1, D_MODEL), p_map), pl.BlockSpec((tm, D_MODEL), s_map)]


def _pick_mod(i, n_pt, p_ref, s_ref):
    return jnp.where(i >= n_pt, s_ref[...], p_ref[0])


def _norm_mod(x, g, sc, sh):
    ms = jnp.mean(x * x, axis=-1, keepdims=True)
    return x * lax.rsqrt(ms + NORM_EPS) * g * (1.0 + sc) + sh


def _pair_specs(geo, tm, width):
    n_pt = geo.n_prompt // tm
    return [pl.BlockSpec((tm, width), lambda i, *_: (jnp.minimum(i, n_pt - 1), 0)),
            pl.BlockSpec((tm, width), lambda i, *_: (jnp.maximum(i - n_pt, 0), 0))]


def _proj_kernel(xp_ref, xs_ref, g_ref, shp_ref, shs_ref, scp_ref, scs_ref, w_ref, o_ref, *, n_pt, tn):
    i = pl.program_id(0)
    sh = _pick_mod(i, n_pt, shp_ref, shs_ref)
    sc = _pick_mod(i, n_pt, scp_ref, scs_ref)
    x = jnp.where(i >= n_pt, xs_ref[...], xp_ref[...])
    h = _norm_mod(x, g_ref[...], sc, sh).astype(BF16)
    for j in range(o_ref.shape[1] // tn):
        o_ref[:, j * tn:(j + 1) * tn] = _dot(h, w_ref[:, j * tn:(j + 1) * tn])


def _proj(geo, x_pair, g, mod_p, mod_s, w_bf16, col_shift, col_scale):
    n_out = w_bf16.shape[1]
    tm = geo.tm
    fixed = lambda i: (0, 0)
    return pl.pallas_call(
        functools.partial(_proj_kernel, n_pt=geo.n_pt, tn=1024),
        grid=(geo.n_tiles,),
        in_specs=[
            *_pair_specs(geo, tm, D_MODEL),
            pl.BlockSpec((1, D_MODEL), fixed),
            *_mod_specs(geo, col_shift, tm),
            *_mod_specs(geo, col_scale, tm),
            pl.BlockSpec((D_MODEL, n_out), fixed),
        ],
        out_specs=pl.BlockSpec((tm, n_out), lambda i: (i, 0)),
        out_shape=jax.ShapeDtypeStruct((geo.nt, n_out), F32),
        compiler_params=_cparams(("arbitrary",)),
        name="norm_mod_proj",
    )(*x_pair, g, mod_p, mod_s, mod_p, mod_s, w_bf16)


def _hg_lower_bound(lbt_ref):
    t = lbt_ref[...]
    e = jnp.exp(t - jnp.max(t, axis=0, keepdims=True))
    return e[0:1] / jnp.sum(e, axis=0, keepdims=True)


def _hg_gates(fz, lb):
    t = jnp.tanh(0.5 * fz)
    logf = jnp.log(lb + (1.0 - lb) * (0.5 + 0.5 * t))
    return logf, (1.0 - lb) * (0.5 - 0.5 * t)


def _hg_out(o, gz, og):
    ms = jnp.mean(o * o, axis=-1, keepdims=True)
    return o * lax.rsqrt(ms + NORM_EPS) * og * _silu(gz)


def _gla_prompt_kernel(*refs, batch, chunk, n_chunks):
    z_refs = refs[:batch]
    lbt_ref, og_ref, o_ref, sfin_ref, st_scr = refs[batch:]
    t_step = pl.program_id(0)
    kw = HG_HEADS * HG_DK

    @pl.when(t_step == 0)
    def _():
        st_scr[...] = jnp.zeros_like(st_scr)

    lb = _hg_lower_bound(lbt_ref)
    og = og_ref[...]
    r_i = lax.broadcasted_iota(I32, (chunk, chunk), 0)
    c_i = lax.broadcasted_iota(I32, (chunk, chunk), 1)
    causal = c_i <= r_i
    tri = causal.astype(BF16)
    mid = chunk // 2 - 1

    def body(c, carry):
        rows = pl.ds(pl.multiple_of(c * chunk, chunk), chunk)
        for bi, z_ref in enumerate(z_refs):
            logf, kk = _hg_gates(z_ref[rows, kw:2 * kw], lb)
            hi, md, lo = _split3(logf)
            cs = _dot(tri, jnp.concatenate([hi, md, lo], axis=1))
            b = cs[:, :kw] + cs[:, kw:2 * kw] + cs[:, 2 * kw:]
            b_mid = b[mid:mid + 1]
            b_last = b[chunk - 1:chunk]
            q_hat = _silu(z_ref[rows, 0:kw]) * jnp.exp(b - b_mid)
            k_hat = kk * jnp.exp(b_mid - b)
            q_in = (q_hat * jnp.exp(b_mid)).astype(BF16)
            k_dec = (k_hat * jnp.exp(b_last - b_mid)).astype(BF16)
            q_hat = q_hat.astype(BF16)
            k_hat = k_hat.astype(BF16)
            dec = jnp.exp(b_last)
            for h in range(HG_HEADS):
                cols = slice(h * HG_DK, (h + 1) * HG_DK)
                vcols = slice(2 * kw + h * HG_DV, 2 * kw + (h + 1) * HG_DV)
                gcols = slice(2 * kw + HG_HEADS * HG_DV + h * HG_DV, 2 * kw + HG_HEADS * HG_DV + (h + 1) * HG_DV)
                v = z_ref[rows, vcols].astype(BF16)
                att = jnp.where(causal, _dot_nt(q_hat[:, cols], k_hat[:, cols]), 0.0).astype(BF16)
                st = st_scr[bi, h]
                o = _dot(att, v) + _dot_nt(q_in[:, cols], st.astype(BF16))
                st_scr[bi, h] = st * dec[:, cols] + _dot_tn(v, k_dec[:, cols])
                o_ref[bi, rows, h * HG_DV:(h + 1) * HG_DV] = _hg_out(o, z_ref[rows, gcols], og).astype(o_ref.dtype)
        return carry

    lax.fori_loop(0, n_chunks, body, 0)

    @pl.when(t_step == pl.num_programs(0) - 1)
    def _():
        sfin_ref[...] = st_scr[...]


def _gla_prompt(geo, z, lb_table, o_gain):
    tg = min(256, geo.seq)
    chunk = HG_CHUNK if geo.seq % HG_CHUNK == 0 else geo.seq
    assert tg % chunk == 0 and geo.seq % tg == 0
    nt = geo.seq // tg
    batch = geo.batch
    fixed = lambda t: (0, 0)
    state_shape = (batch, HG_HEADS, HG_DV, HG_DK)
    o, st = pl.pallas_call(
        functools.partial(_gla_prompt_kernel, batch=batch, chunk=chunk, n_chunks=tg // chunk),
        grid=(nt,),
        in_specs=[
            *[pl.BlockSpec((tg, 4 * D_MODEL), functools.partial(lambda t, b: (b * nt + t, 0), b=b))
              for b in range(batch)],
            pl.BlockSpec(lb_table.shape, fixed),
            pl.BlockSpec((1, HG_DV), fixed),
        ],
        out_specs=[
            pl.BlockSpec((batch, tg, D_MODEL), lambda t: (0, t, 0)),
            pl.BlockSpec(state_shape, lambda t: (0, 0, 0, 0)),
        ],
        out_shape=[
            jax.ShapeDtypeStruct((batch, geo.seq, D_MODEL), BF16),
            jax.ShapeDtypeStruct(state_shape, F32),
        ],
        scratch_shapes=[pltpu.VMEM(state_shape, F32)],
        compiler_params=_cparams(("arbitrary",)),
        name="gla_prompt",
    )(*([z] * batch), lb_table, o_gain)
    return o.reshape(geo.n_prompt, D_MODEL), st


def _gla_step_kernel(z_ref, s_ref, lbt_ref, og_ref, o_ref, snew_ref, *, steps, bb):
    kw = HG_HEADS * HG_DK
    pad = SUBLANES
    lb = _hg_lower_bound(lbt_ref)
    og = og_ref[...]
    r_i = lax.broadcasted_iota(I32, (pad, pad), 0)
    c_i = lax.broadcasted_iota(I32, (pad, pad), 1)
    causal = c_i <= r_i
    tri = causal.astype(BF16)
    row_w = lax.broadcasted_iota(I32, (pad, kw), 0)
    live = row_w < steps
    row_k = lax.broadcasted_iota(I32, (pad, HG_DK), 0)
    ones_sel = jnp.where((row_k == steps) | (row_k == steps + 1), 1.0, 0.0).astype(BF16)

    def body(s, carry):
        z = z_ref[s]
        logf, kk = _hg_gates(z[:, kw:2 * kw], lb)
        hi, md, lo = _split3(jnp.where(live, logf, 0.0))
        cs = _dot(tri, jnp.concatenate([hi, md, lo], axis=1))
        b = cs[:, :kw] + cs[:, kw:2 * kw] + cs[:, 2 * kw:]
        b_last = b[steps - 1:steps]
        q_in = (_silu(z[:, 0:kw]) * jnp.exp(b)).astype(BF16)
        k_hat = jnp.where(live, kk * jnp.exp(-b), 0.0).astype(BF16)
        k_dec = jnp.where(live, kk * jnp.exp(b_last - b), 0.0).astype(BF16)
        dec = jnp.exp(b_last)
        d_hi = dec.astype(BF16)
        d_lo = (dec - d_hi.astype(F32)).astype(BF16)
        a_all = jnp.where(row_w == steps, d_hi, jnp.where(row_w == steps + 1, d_lo, k_dec))
        for h in range(HG_HEADS):
            cols = slice(h * HG_DK, (h + 1) * HG_DK)
            vcols = slice(2 * kw + h * HG_DV, 2 * kw + (h + 1) * HG_DV)
            gcols = slice(2 * kw + HG_HEADS * HG_DV + h * HG_DV, 2 * kw + HG_HEADS * HG_DV + (h + 1) * HG_DV)
            v = z[:, vcols].astype(BF16)
            s0 = s_ref[s, h]
            att = jnp.where(causal, _dot_nt(q_in[:, cols], k_hat[:, cols]), 0.0).astype(BF16)
            o = _dot(att, v) + _dot(q_in[:, cols], s0.astype(BF16))
            upd = _dot_tn(a_all[:, cols], jnp.concatenate([v, ones_sel], axis=1))
            snew_ref[s, h] = upd[:, HG_DV:] * s0 + upd[:, :HG_DV]
            o_ref[s, :, h * HG_DV:(h + 1) * HG_DV] = _hg_out(o, z[:, gcols], og)
        return carry

    lax.fori_loop(0, bb, body, 0, unroll=True)


def _seq_major(geo, rows):
    steps, db = geo.dec_seq, geo.dec_batch
    w = rows.shape[-1]
    r = rows.reshape(steps, db, w).transpose(1, 0, 2)
    return jnp.concatenate([r, jnp.zeros((db, SUBLANES - steps, w), rows.dtype)], axis=1)


def _time_major(geo, r):
    steps, db = geo.dec_seq, geo.dec_batch
    return r[:, :steps].transpose(1, 0, 2).reshape(steps * db, r.shape[-1])


def _gla_step(geo, z, state, lb_table, o_gain):
    steps, db = geo.dec_seq, geo.dec_batch
    assert steps + 2 <= SUBLANES
    bb = min(8, db)
    z_s = _seq_major(geo, z[geo.n_prompt:])
    o_s, s_new = pl.pallas_call(
        functools.partial(_gla_step_kernel, steps=steps, bb=bb),
        grid=(db // bb,),
        in_specs=[
            pl.BlockSpec((bb, SUBLANES, 4 * D_MODEL), lambda i: (i, 0, 0)),
            pl.BlockSpec((bb, HG_HEADS, HG_DK, HG_DV), lambda i: (i, 0, 0, 0)),
            pl.BlockSpec(lb_table.shape, lambda i: (0, 0)),
            pl.BlockSpec((1, HG_DV), lambda i: (0, 0)),
        ],
        out_specs=[
            pl.BlockSpec((bb, SUBLANES, D_MODEL), lambda i: (i, 0, 0)),
            pl.BlockSpec((bb, HG_HEADS, HG_DK, HG_DV), lambda i: (i, 0, 0, 0)),
        ],
        out_shape=[
            jax.ShapeDtypeStruct((db, SUBLANES, D_MODEL), F32),
            jax.ShapeDtypeStruct(state.shape, F32),
        ],
        compiler_params=_cparams(("arbitrary",)),
        name="gla_step",
    )(z_s, state, lb_table, o_gain)
    return _time_major(geo, o_s), s_new


def _post_kernel(ap_ref, as_ref, wo_ref, xp_ref, xs_ref, gp_ref, gs_ref, n2_ref, shp_ref, shs_ref, scp_ref,
                 scs_ref, rw_ref, rb_ref, x1_ref, h2_ref, pos_ref, pg_ref, nch_ref, soff_ref, *, n_pt, tm):
    i = pl.program_id(0)
    g1 = _pick_mod(i, n_pt, gp_ref, gs_ref)
    a = jnp.where(i >= n_pt, as_ref[...], ap_ref[...])
    x = jnp.where(i >= n_pt, xs_ref[...], xp_ref[...])
    x1 = x + g1 * _dot(a, wo_ref[...])
    x1_ref[...] = x1
    sh = _pick_mod(i, n_pt, shp_ref, shs_ref)
    sc = _pick_mod(i, n_pt, scp_ref, scs_ref)
    h2 = _norm_mod(x1, n2_ref[...], sc, sh)
    h2_ref[...] = h2.astype(h2_ref.dtype)

    row_e = lax.broadcasted_iota(I32, (LANES, tm), 0)
    logits = _dot_nt(rw_ref[...], h2.astype(BF16)) + rb_ref[...]
    work = jnp.where(row_e < N_EXPERTS, logits, -jnp.inf)
    vals, hits = [], []
    for _ in range(TOP_K):
        m = jnp.max(work, axis=0, keepdims=True)
        idx = jnp.min(jnp.where(work == m, row_e, LANES), axis=0, keepdims=True)
        hit = row_e == idx
        vals.append(m)
        hits.append(hit)
        work = jnp.where(hit, -jnp.inf, work)
    exps = [jnp.exp(v - vals[0]) for v in vals]
    den = exps[0]
    for e in exps[1:]:
        den = den + e
    any_hit = hits[0]
    for hmask in hits[1:]:
        any_hit = any_hit | hmask
    any_f = jnp.where(any_hit, 1.0, 0.0)
    t_i = lax.broadcasted_iota(I32, (tm, tm), 0)
    t_j = lax.broadcasted_iota(I32, (tm, tm), 1)
    rank = _dot(any_f.astype(BF16), (t_i < t_j).astype(BF16))
    n_chunk = jnp.floor((jnp.sum(any_f, axis=1, keepdims=True) + (ROW_CHUNK - 1)) * (1.0 / ROW_CHUNK))
    e_i = lax.broadcasted_iota(I32, (LANES, LANES), 0)
    e_j = lax.broadcasted_iota(I32, (LANES, LANES), 1)
    seg = _dot((e_j < e_i).astype(BF16), jnp.broadcast_to(n_chunk, (LANES, LANES)).astype(BF16))[:, 0:1]
    where_to = seg * float(ROW_CHUNK) + rank
    pos_rows = [jnp.sum(jnp.where(hits[k], where_to, 0.0), axis=0, keepdims=True) for k in range(TOP_K)]
    gate_rows = [exps[k] / den for k in range(TOP_K)]
    pos_ref[0] = jnp.concatenate(pos_rows + [jnp.zeros((SUBLANES - TOP_K, tm), F32)], axis=0).astype(I32)
    rows = jnp.concatenate(pos_rows + gate_rows + [jnp.zeros((LANES - 2 * TOP_K, tm), F32)], axis=0)
    pg_ref[...] = rows.T
    nch_ref[0] = n_chunk.astype(I32)
    soff_ref[0] = seg.astype(I32)


def _post(geo, a_p, a_s, w_out_bf16, x_pair, norm_g, mod_p, mod_s, col0, router_w, router_b):
    tm = geo.tm_moe
    n_pt = geo.n_prompt // tm
    n_tiles = geo.nt // tm
    rw = jnp.zeros((LANES, D_MODEL), BF16).at[:N_EXPERTS].set(router_w.T.astype(BF16))
    rb = jnp.zeros((LANES, 1), F32).at[:N_EXPERTS, 0].set(router_b)
    row = lambda i: (i, 0)
    fixed = lambda i: (0, 0)
    tile_row = lambda i: (i, 0, 0)
    return pl.pallas_call(
        functools.partial(_post_kernel, n_pt=n_pt, tm=tm),
        grid=(n_tiles,),
        in_specs=[
            *_pair_specs(geo, tm, D_MODEL),
            pl.BlockSpec((D_MODEL, D_MODEL), fixed),
            *_pair_specs(geo, tm, D_MODEL),
            *_mod_specs(geo, col0 + 2, tm),
            pl.BlockSpec((1, D_MODEL), fixed),
            *_mod_specs(geo, col0 + 3, tm),
            *_mod_specs(geo, col0 + 4, tm),
            pl.BlockSpec((LANES, D_MODEL), fixed),
            pl.BlockSpec((LANES, 1), fixed),
        ],
        out_specs=[
            pl.BlockSpec((tm, D_MODEL), row),
            pl.BlockSpec((tm, D_MODEL), row),
            pl.BlockSpec((1, SUBLANES, tm), tile_row),
            pl.BlockSpec((tm, LANES), row),
            pl.BlockSpec((1, LANES, 1), tile_row),
            pl.BlockSpec((1, LANES, 1), tile_row),
        ],
        out_shape=[
            jax.ShapeDtypeStruct((geo.nt, D_MODEL), F32),
            jax.ShapeDtypeStruct((geo.nt, D_MODEL), BF16),
            jax.ShapeDtypeStruct((n_tiles, SUBLANES, tm), I32),
            jax.ShapeDtypeStruct((geo.nt, LANES), F32),
            jax.ShapeDtypeStruct((n_tiles, LANES, 1), I32),
            jax.ShapeDtypeStruct((n_tiles, LANES, 1), I32),
        ],
        compiler_params=_cparams(("arbitrary",)),
        name="post_mixer_router",
    )(a_p, a_s.astype(BF16), w_out_bf16, *x_pair, mod_p, mod_s, norm_g, mod_p, mod_s, mod_p, mod_s, rw, rb)


def _ffn_kernel(first_ref, count_ref, xs_hbm, win_ref, bin_ref, wout_ref, bout_ref, ys_hbm,
                xbuf, ybuf, win_scr, wout_scr, xsem, ysem, *, half):
    e = pl.program_id(0)
    first = first_ref[e]
    n_sub = count_ref[e]
    pairs = n_sub // 2
    odd = n_sub - 2 * pairs
    win_scr[...] = win_ref[0].astype(BF16)
    wout_scr[...] = wout_ref[0].astype(BF16)

    def rows_at(sub, n_rows):
        return pl.ds(pl.multiple_of((first + sub) * half, half), n_rows)

    def x_copy(sub, n_rows, slot):
        return pltpu.make_async_copy(xs_hbm.at[rows_at(sub, n_rows)], xbuf.at[slot, pl.ds(0, n_rows)], xsem.at[slot])

    def y_copy(sub, n_rows, slot):
        return pltpu.make_async_copy(ybuf.at[slot, pl.ds(0, n_rows)], ys_hbm.at[rows_at(sub, n_rows)], ysem.at[slot])

    def ffn(slot, rows):
        gu = _dot(xbuf[slot, rows, :].astype(BF16), win_scr[...]) + bin_ref[0]
        gate = jnp.minimum(gu[:, :D_MODEL], SWIGLU_LIMIT)
        up = jnp.clip(gu[:, D_MODEL:], -SWIGLU_LIMIT, SWIGLU_LIMIT)
        act = gate * _sigmoid(SWIGLU_ALPHA * gate) * (up + 1.0)
        return _dot(act.astype(BF16), wout_scr[...]) + bout_ref[0]

    @pl.when(pairs > 0)
    def _():
        x_copy(0, 2 * half, 0).start()

    @pl.when((pairs == 0) & (odd == 1))
    def _():
        x_copy(0, half, 0).start()

    def unit(u, carry):
        slot = u % 2
        x_copy(2 * u, 2 * half, slot).wait()

        @pl.when(u + 1 < pairs)
        def _():
            x_copy(2 * (u + 1), 2 * half, 1 - slot).start()

        @pl.when((u + 1 == pairs) & (odd == 1))
        def _():
            x_copy(2 * pairs, half, 1 - slot).start()

        @pl.when(u >= 2)
        def _():
            y_copy(0, 2 * half, slot).wait()

        ybuf[slot] = ffn(slot, slice(0, 2 * half))
        y_copy(2 * u, 2 * half, slot).start()
        return carry

    lax.fori_loop(0, pairs, unit, 0)
    tail_slot = pairs % 2

    @pl.when(odd == 1)
    def _():
        x_copy(2 * pairs, half, tail_slot).wait()

        @pl.when(pairs >= 2)
        def _():
            y_copy(0, 2 * half, tail_slot).wait()

        ybuf[tail_slot, 0:half, :] = ffn(tail_slot, slice(0, half))
        y_copy(2 * pairs, half, tail_slot).start()

        @pl.when(pairs >= 1)
        def _():
            y_copy(0, 2 * half, 1 - tail_slot).wait()

        y_copy(0, half, tail_slot).wait()

    @pl.when(odd == 0)
    def _():
        @pl.when(pairs >= 2)
        def _():
            y_copy(0, 2 * half, tail_slot).wait()

        @pl.when(pairs >= 1)
        def _():
            y_copy(0, 2 * half, 1 - tail_slot).wait()


def _ffn(xs, first_sub, n_sub, layer, w_in, b_in, w_out, b_out):
    n_e, d, d2 = w_in.shape
    half = MOE_BLOCK
    w_map = lambda e, *_: (layer * N_EXPERTS + e, 0, 0)
    return pl.pallas_call(
        functools.partial(_ffn_kernel, half=half),
        grid_spec=pltpu.PrefetchScalarGridSpec(
            num_scalar_prefetch=2,
            grid=(N_EXPERTS,),
            in_specs=[
                pl.BlockSpec(memory_space=pl.ANY),
                pl.BlockSpec((1, d, d2), w_map),
                pl.BlockSpec((1, 1, d2), w_map),
                pl.BlockSpec((1, d2 // 2, d), w_map),
                pl.BlockSpec((1, 1, d), w_map),
            ],
            out_specs=pl.BlockSpec(memory_space=pl.ANY),
            scratch_shapes=[
                pltpu.VMEM((2, 2 * half, d), F32),
                pltpu.VMEM((2, 2 * half, d), F32),
                pltpu.VMEM((d, d2), BF16),
                pltpu.VMEM((d2 // 2, d), BF16),
                pltpu.SemaphoreType.DMA((2,)),
                pltpu.SemaphoreType.DMA((2,)),
            ],
        ),
        out_shape=jax.ShapeDtypeStruct(xs.shape, xs.dtype),
        input_output_aliases={2: 0},
        compiler_params=_cparams(("arbitrary",)),
        name="moe_ffn",
    )(first_sub, n_sub, xs, w_in, b_in.reshape(n_e, 1, d2), w_out, b_out.reshape(n_e, 1, d))


def _chunk_rows(chunk, n_chunks=1):
    return pl.ds(pl.multiple_of(chunk * ROW_CHUNK, ROW_CHUNK), n_chunks * ROW_CHUNK)


def _run_pieces(n, fn, big):
    n_big = n // big

    def big_piece(i, carry):
        fn(i * big, big)
        return carry

    def small_piece(i, carry):
        fn(n_big * big + i, 1)
        return carry

    lax.fori_loop(0, n_big, big_piece, 0)
    lax.fori_loop(0, n - n_big * big, small_piece, 0)


def _for_each_chunk(where_ref, tot_ref, tile, fn):
    n = tot_ref[tile]
    n_groups = n // CHUNK_UNROLL

    def group(i, carry):
        for u in range(CHUNK_UNROLL):
            j = i * CHUNK_UNROLL + u
            fn(j, where_ref[tile, j])
        return carry

    def single(i, carry):
        j = n_groups * CHUNK_UNROLL + i
        fn(j, where_ref[tile, j])
        return carry

    lax.fori_loop(0, n_groups, group, 0)
    lax.fori_loop(0, n - n_groups * CHUNK_UNROLL, single, 0)


def _dispatch_kernel(where_ref, tot_ref, fill0_ref, filln_ref,
                     pos_ref, h_ref, xs_hbm, stage, zeros, sem, fill_sem, *, tm, k_stage):
    t = pl.program_id(0)
    n_t = pl.num_programs(0)
    slot = t % 2

    def copy(s_chunk, d_chunk, size, sl):
        return pltpu.make_async_copy(stage.at[sl, _chunk_rows(s_chunk, size)],
                                     xs_hbm.at[_chunk_rows(d_chunk, size)], sem.at[sl])

    def wait_tile(tile, sl):
        _run_pieces(tot_ref[tile], lambda off, size: copy(0, 0, size, sl).wait(), WAIT_CHUNKS)

    @pl.when(t >= 2)
    def _():
        wait_tile(t - 2, slot)

    pos_t = pos_ref[0]
    j_i = lax.broadcasted_iota(I32, (k_stage, tm), 0)
    hit = j_i == pos_t[0:1]
    for k in range(1, TOP_K):
        hit = hit | (j_i == pos_t[k:k + 1])
    stage[slot] = _dot(jnp.where(hit, 1.0, 0.0).astype(BF16), h_ref[...])
    _for_each_chunk(where_ref, tot_ref, t, lambda s, d: copy(s, d, 1, slot).start())

    @pl.when(t == n_t - 1)
    def _():
        zeros[...] = jnp.zeros_like(zeros)

        def fill(d_chunk, size):
            return pltpu.make_async_copy(zeros.at[_chunk_rows(0, size)], xs_hbm.at[_chunk_rows(d_chunk, size)],
                                         fill_sem.at[0])

        def per_region(r, carry):
            _run_pieces(filln_ref[r], lambda off, size: fill(fill0_ref[r] + off, size).start(), COPY_CHUNKS)
            return carry

        lax.fori_loop(0, N_EXPERTS + 1, per_region, 0)

        @pl.when(t >= 1)
        def _():
            wait_tile(t - 1, 1 - slot)

        wait_tile(t, slot)

        def per_region_wait(r, carry):
            _run_pieces(filln_ref[r], lambda off, size: fill(0, size).wait(), COPY_CHUNKS)
            return carry

        lax.fori_loop(0, N_EXPERTS + 1, per_region_wait, 0)


def _combine_kernel(where_ref, tot_ref, pg_ref, x_ref, gp_ref, gs_ref, y_hbm,
                    op_ref, os_ref, stage, sem, *, n_pt, tm, k_stage):
    t = pl.program_id(0)
    n_t = pl.num_programs(0)
    slot = t % 2

    def copy(s_chunk, d_chunk, size, sl):
        return pltpu.make_async_copy(y_hbm.at[_chunk_rows(d_chunk, size)],
                                     stage.at[sl, _chunk_rows(s_chunk, size)], sem.at[sl])

    def fetch(tile, sl):
        _for_each_chunk(where_ref, tot_ref, tile, lambda s, d: copy(s, d, 1, sl).start())

    @pl.when(t == 0)
    def _():
        stage[...] = jnp.zeros_like(stage)
        fetch(0, 0)

    @pl.when(t + 1 < n_t)
    def _():
        fetch(t + 1, 1 - slot)

    _run_pieces(tot_ref[t], lambda off, size: copy(0, 0, size, slot).wait(), WAIT_CHUNKS)
    lane = lax.broadcasted_iota(I32, (tm, k_stage), 1)
    pg = pg_ref[...]
    p = jnp.zeros((tm, k_stage), F32)
    for k in range(TOP_K):
        p = jnp.where(lane == pg[:, k:k + 1].astype(I32), pg[:, TOP_K + k:TOP_K + k + 1], p)
    ffn = _dot(p.astype(BF16), stage[slot].astype(BF16))
    out = x_ref[...] + _pick_mod(t, n_pt, gp_ref, gs_ref) * ffn

    @pl.when(t < n_pt)
    def _():
        op_ref[...] = out

    @pl.when(t >= n_pt)
    def _():
        os_ref[...] = out


def _moe(geo, h2, pos, pg, nch_pad, soff_pad, x1, mod_p, mod_s, col_gate, layer, w_in, b_in, w_out, b_out):
    bm = MOE_BLOCK
    tm = geo.tm_moe
    n_tiles = geo.nt // tm
    depth, n_e, d, d2 = w_in.shape
    w_in = w_in.reshape(depth * n_e, d, d2)
    b_in = b_in.reshape(depth * n_e, d2)
    w_out = w_out.reshape(depth * n_e, d2 // 2, d)
    b_out = b_out.reshape(depth * n_e, d)
    chunks_per_block = bm // ROW_CHUNK
    max_rows = geo.nt * TOP_K + n_tiles * N_EXPERTS * (ROW_CHUNK - 1) + N_EXPERTS * (bm - 1)
    n_blocks = -(-max_rows // bm)
    nch = nch_pad[:, :N_EXPERTS, 0]
    soff = soff_pad[:, :N_EXPERTS, 0]
    tot = jnp.sum(nch, axis=1)
    per_e = jnp.sum(nch, axis=0)
    padded = (per_e + chunks_per_block - 1) // chunks_per_block * chunks_per_block
    pend = jnp.cumsum(padded)
    pstart = pend - padded
    dst = pstart[None, :] + jnp.cumsum(nch, axis=0) - nch
    first_sub = (pstart // chunks_per_block).astype(I32)
    n_sub = (padded // chunks_per_block).astype(I32)
    fill0 = jnp.concatenate([pstart + per_e, pend[-1:]]).astype(I32)
    filln = jnp.concatenate([padded - per_e, n_blocks * chunks_per_block - pend[-1:]]).astype(I32)

    k_stage = -(-(tm * TOP_K + N_EXPERTS * (ROW_CHUNK - 1)) // LANES) * LANES
    j = jnp.arange(k_stage // ROW_CHUNK, dtype=I32)
    owner = jnp.minimum(jnp.sum(j[None, :, None] >= (soff + nch)[:, None, :], axis=-1), N_EXPERTS - 1)
    shift = jnp.sum(jnp.where(owner[:, :, None] == jnp.arange(N_EXPERTS, dtype=I32), (dst - soff)[:, None, :], 0),
                    axis=-1)
    where = (j[None, :] + shift).astype(I32)
    row = lambda i, *_: (i, 0)
    xs = pl.pallas_call(
        functools.partial(_dispatch_kernel, tm=tm, k_stage=k_stage),
        grid_spec=pltpu.PrefetchScalarGridSpec(
            num_scalar_prefetch=4,
            grid=(n_tiles,),
            in_specs=[pl.BlockSpec((1, SUBLANES, tm), lambda i, *_: (i, 0, 0)), pl.BlockSpec((tm, d), row)],
            out_specs=pl.BlockSpec(memory_space=pl.ANY),
            scratch_shapes=[
                pltpu.VMEM((2, k_stage, d), F32),
                pltpu.VMEM((COPY_CHUNKS * ROW_CHUNK, d), F32),
                pltpu.SemaphoreType.DMA((2,)),
                pltpu.SemaphoreType.DMA((1,)),
            ],
        ),
        out_shape=jax.ShapeDtypeStruct((n_blocks * bm, d), F32),
        compiler_params=_cparams(("arbitrary",)),
        name="moe_dispatch",
    )(where, tot, fill0, filln, pos, h2)

    ys = _ffn(xs, first_sub, n_sub, layer, w_in, b_in, w_out, b_out)

    return pl.pallas_call(
        functools.partial(_combine_kernel, n_pt=geo.n_prompt // tm, tm=tm, k_stage=k_stage),
        grid_spec=pltpu.PrefetchScalarGridSpec(
            num_scalar_prefetch=2,
            grid=(n_tiles,),
            in_specs=[
                pl.BlockSpec((tm, LANES), row),
                pl.BlockSpec((tm, d), row),
                *_mod_specs(geo, col_gate, tm),
                pl.BlockSpec(memory_space=pl.ANY),
            ],
            out_specs=_pair_specs(geo, tm, d),
            scratch_shapes=[
                pltpu.VMEM((2, k_stage, d), F32),
                pltpu.SemaphoreType.DMA((2,)),
            ],
        ),
        out_shape=[jax.ShapeDtypeStruct((geo.n_prompt, d), F32), jax.ShapeDtypeStruct((geo.n_sample, d), F32)],
        compiler_params=_cparams(("arbitrary",)),
        name="moe_combine",
    )(where, tot, pg, x1, mod_p, mod_s, ys)


SW_QW = SW_Q_HEADS * SW_HEAD_DIM
SW_QKW = SW_QW + SW_KV_W
SW_KEYS = 2 * ATT_BLOCK


def _attn_proj_kernel(xp_ref, xs_ref, g_ref, shp_ref, shs_ref, scp_ref, scs_ref, w_ref, e_ref, et_ref, qg_ref,
                      kg_ref, q_ref, k_ref, v_ref, *, n_pt):
    i = pl.program_id(0)
    sh = _pick_mod(i, n_pt, shp_ref, shs_ref)
    sc = _pick_mod(i, n_pt, scp_ref, scs_ref)
    x = jnp.where(i >= n_pt, xs_ref[...], xp_ref[...])
    h = _norm_mod(x, g_ref[...], sc, sh).astype(BF16)
    tm = h.shape[0]
    n_parts = 2 if tm % (2 * SUBLANES) == 0 else 1
    part = tm // n_parts
    for r in range(n_parts):
        rows = slice(r * part, (r + 1) * part)
        z = _dot(h[rows], w_ref[...])
        qk = z[:, :SW_QKW]
        sq = qk * qk
        sq_hi = sq.astype(BF16)
        sq_lo = (sq - sq_hi.astype(F32)).astype(BF16)
        ms = (_dot(sq_hi, e_ref[...]) + _dot(sq_lo, e_ref[...])) * (1.0 / SW_HEAD_DIM)
        inv = lax.rsqrt(ms + NORM_EPS)
        inv_hi = inv.astype(BF16)
        inv_lo = (inv - inv_hi.astype(F32)).astype(BF16)
        qk = qk * (_dot(inv_hi, et_ref[...]) + _dot(inv_lo, et_ref[...]))
        q_ref[rows, :] = (qk[:, :SW_QW] * qg_ref[...] * SW_SCALE).astype(q_ref.dtype)
        k_ref[rows, :] = qk[:, SW_QW:] * kg_ref[...]
        v_ref[rows, :] = z[:, SW_QKW:]


def _attn_proj(geo, x_pair, g, mod_p, mod_s, w_bf16, q_gain, k_gain):
    tm = geo.tm
    n_out = w_bf16.shape[1]
    heads = SW_QKW // SW_HEAD_DIM
    member = (np.arange(SW_QKW)[:, None] // SW_HEAD_DIM == np.arange(LANES)[None, :]).astype(np.float32)
    e = jnp.asarray(member, BF16)
    et = jnp.asarray(member.T, BF16)
    assert heads <= LANES
    qg = jnp.tile(q_gain, SW_Q_HEADS).reshape(1, SW_QW)
    kg = jnp.tile(k_gain, SW_KV_HEADS).reshape(1, SW_KV_W)
    row = lambda i: (i, 0)
    fixed = lambda i: (0, 0)
    return pl.pallas_call(
        functools.partial(_attn_proj_kernel, n_pt=geo.n_pt),
        grid=(geo.n_tiles,),
        in_specs=[
            *_pair_specs(geo, tm, D_MODEL),
            pl.BlockSpec((1, D_MODEL), fixed),
            *_mod_specs(geo, 0, tm),
            *_mod_specs(geo, 1, tm),
            pl.BlockSpec((D_MODEL, n_out), fixed),
            pl.BlockSpec((SW_QKW, LANES), fixed),
            pl.BlockSpec((LANES, SW_QKW), fixed),
            pl.BlockSpec((1, SW_QW), fixed),
            pl.BlockSpec((1, SW_KV_W), fixed),
        ],
        out_specs=[
            pl.BlockSpec((tm, SW_QW), row),
            pl.BlockSpec((tm, SW_KV_W), row),
            pl.BlockSpec((tm, SW_KV_W), row),
        ],
        out_shape=[
            jax.ShapeDtypeStruct((geo.nt, SW_QW), BF16),
            jax.ShapeDtypeStruct((geo.nt, SW_KV_W), F32),
            jax.ShapeDtypeStruct((geo.nt, SW_KV_W), F32),
        ],
        compiler_params=_cparams(("arbitrary",)),
        name="attn_proj_qknorm",
    )(*x_pair, g, mod_p, mod_s, mod_p, mod_s, w_bf16, e, et, qg, kg)


def _rel_bucket_np(dist):
    n = np.maximum(dist, 0)
    max_exact = REL_BUCKETS // 2
    ratio = np.log(np.maximum(n, 1).astype(np.float32) / np.float32(max_exact)) / np.float32(
        math.log(REL_MAX_DIST / max_exact))
    large = max_exact + (ratio * np.float32(REL_BUCKETS - max_exact)).astype(np.int32)
    large = np.minimum(large, REL_BUCKETS - 1)
    return np.where(n < max_exact, n, large).astype(np.int32)


def _bucket_table(qpos, kpos, k_ok):
    dist = qpos[:, None] - kpos[None, :]
    ok = (dist >= 0) & (dist <= WINDOW) & k_ok[None, :]
    return np.where(ok, _rel_bucket_np(dist), -1).astype(np.int32)


def _build_bias(bkt_ref, rb_ref, bias_scr, rows):
    bkt = bkt_ref[...]
    base = jnp.where(bkt < 0, -jnp.inf, 0.0)
    for h in range(SW_Q_HEADS):
        bias_scr[h * rows:(h + 1) * rows, :] = base

    def add_bucket(j, carry):
        hit = bkt == j
        for h in range(SW_Q_HEADS):
            sl = slice(h * rows, (h + 1) * rows)
            bias_scr[sl, :] = bias_scr[sl, :] + jnp.where(hit, rb_ref[j, h], 0.0)
        return carry

    lax.fori_loop(0, REL_BUCKETS, add_bucket, 0)


def _sink_softmax_pv(s, sink, vv_g):
    m = jnp.maximum(jnp.max(s, axis=-1, keepdims=True), sink)
    e = jnp.exp(s - m)
    p = e / (jnp.sum(e, axis=-1, keepdims=True) + jnp.exp(sink - m))
    return _dot(p.astype(BF16), vv_g)


def _build_bias_t(bkt_ref, rb_ref, bias_scr, blk):
    bkt = bkt_ref[...]
    base = jnp.where(bkt < 0, -jnp.inf, 0.0)
    slots = [(g, j) for g in range(SW_KV_HEADS) for j in range(SW_GROUP)]
    for g, j in slots:
        bias_scr[g, :, j * blk:(j + 1) * blk] = base

    def add_bucket(b, carry):
        hit = bkt == b
        for g, j in slots:
            cols = slice(j * blk, (j + 1) * blk)
            bias_scr[g, :, cols] = bias_scr[g, :, cols] + jnp.where(hit, rb_ref[b, g * SW_GROUP + j], 0.0)
        return carry

    lax.fori_loop(0, REL_BUCKETS, add_bucket, 0)


def _swa_prompt_kernel(rb_ref, sink_ref, bkt_ref, q_ref, kp_ref, kc_ref, vp_ref, vc_ref, o_ref, bias_scr, *, qb):
    n = pl.program_id(1)
    blk = ATT_BLOCK
    hd = SW_HEAD_DIM

    @pl.when((pl.program_id(0) == 0) & (n == 0))
    def _():
        _build_bias_t(bkt_ref, rb_ref, bias_scr, blk)

    key_i = lax.broadcasted_iota(I32, (SW_KEYS, SW_GROUP * blk), 0)
    hide_prev = (n == 0) & (key_i < blk)
    zeros_q = jnp.zeros((hd, SW_GROUP * blk), BF16)
    sinks = [jnp.concatenate([jnp.full((1, blk), sink_ref[g * SW_GROUP + j], F32) for j in range(SW_GROUP)], axis=1)
             for g in range(SW_KV_HEADS)]
    for i in range(qb):
        rows = slice(i * blk, (i + 1) * blk)
        if i == 0:
            k2 = jnp.concatenate([kp_ref[...], kc_ref[rows, :]], axis=0)
            v2 = jnp.concatenate([vp_ref[...], vc_ref[rows, :]], axis=0)
        else:
            k2 = kc_ref[(i - 1) * blk:(i + 1) * blk, :]
            v2 = vc_ref[(i - 1) * blk:(i + 1) * blk, :]
        kk = k2.astype(BF16)
        vv_t = v2.T.astype(BF16)
        q_t = q_ref[rows, :].astype(F32).T.astype(BF16)
        outs = []
        for g in range(SW_KV_HEADS):
            heads = [g * SW_GROUP + j for j in range(SW_GROUP)]
            q_cat = jnp.concatenate([q_t[hq * hd:(hq + 1) * hd, :] for hq in heads], axis=1)
            q_full = jnp.concatenate([q_cat if gg == g else zeros_q for gg in range(SW_KV_HEADS)], axis=0)
            s = _dot(kk, q_full) + bias_scr[g]
            if i == 0:
                s = jnp.where(hide_prev, -jnp.inf, s)
            m = jnp.maximum(jnp.max(s, axis=0, keepdims=True), sinks[g])
            e = jnp.exp(s - m)
            den = jnp.sum(e, axis=0, keepdims=True) + jnp.exp(sinks[g] - m)
            o_t = _dot(vv_t[g * hd:(g + 1) * hd, :], e.astype(BF16)) * (1.0 / den)
            outs.extend(o_t[:, j * blk:(j + 1) * blk] for j in range(SW_GROUP))
        o_ref[rows, :] = jnp.concatenate(outs, axis=0).T.astype(o_ref.dtype)


def _swa_prompt(geo, q, k, v, rel_bias, sinks):
    blk = ATT_BLOCK
    assert geo.seq % blk == 0 and WINDOW == blk
    n_blk = geo.seq // blk
    qb = 4 if n_blk % 4 == 0 else (2 if n_blk % 2 == 0 else 1)
    nb = n_blk // qb
    bkt = jnp.asarray(_bucket_table(blk + np.arange(blk), np.arange(2 * blk), np.ones(2 * blk, bool)).T.copy())
    cur = lambda b, n: (b * nb + n, 0)
    prev = lambda b, n: (b * n_blk + jnp.maximum(n * qb - 1, 0), 0)
    smem = pl.BlockSpec(memory_space=pltpu.SMEM)
    return pl.pallas_call(
        functools.partial(_swa_prompt_kernel, qb=qb),
        grid=(geo.batch, nb),
        in_specs=[
            smem, smem,
            pl.BlockSpec((SW_KEYS, blk), lambda b, n: (0, 0)),
            pl.BlockSpec((qb * blk, SW_QW), cur),
            pl.BlockSpec((blk, SW_KV_W), prev),
            pl.BlockSpec((qb * blk, SW_KV_W), cur),
            pl.BlockSpec((blk, SW_KV_W), prev),
            pl.BlockSpec((qb * blk, SW_KV_W), cur),
        ],
        out_specs=pl.BlockSpec((qb * blk, SW_QW), cur),
        out_shape=jax.ShapeDtypeStruct((geo.n_prompt, SW_QW), BF16),
        scratch_shapes=[pltpu.VMEM((SW_KV_HEADS, SW_KEYS, SW_GROUP * blk), F32)],
        compiler_params=_cparams(("arbitrary", "arbitrary")),
        name="swa_prompt",
    )(rel_bias, sinks, bkt, q, k, k, v, v)


def _swa_step_kernel(rb_ref, sink_ref, bkt_ref, q_ref, kn_ref, vn_ref, ck_ref, cv_ref,
                     o_ref, nk_ref, nv_ref, bias_scr, *, steps, bb):
    pad = SUBLANES
    win = WINDOW

    @pl.when(pl.program_id(0) == 0)
    def _():
        _build_bias(bkt_ref, rb_ref, bias_scr, pad)

    fill = jnp.zeros((SW_KEYS - win - pad, SW_KV_W), F32)
    row8 = lax.broadcasted_iota(I32, (pad, SW_KV_W), 0)

    def shifted(cache, new):
        rolled = pltpu.roll(cache, win - steps, 0)
        tail = jnp.where(row8 < pad - steps, rolled[win - pad:], pltpu.roll(new, pad - steps, 0))
        return rolled[:win - pad], tail

    def body(s, carry):
        ck, cv, kn, vn = ck_ref[s], cv_ref[s], kn_ref[s], vn_ref[s]
        kk = jnp.concatenate([ck, kn, fill], axis=0).astype(BF16)
        vv = jnp.concatenate([cv, vn, fill], axis=0).astype(BF16)
        q = q_ref[s]
        for g in range(SW_KV_HEADS):
            kcols = slice(g * SW_HEAD_DIM, (g + 1) * SW_HEAD_DIM)
            heads = [g * SW_GROUP + j for j in range(SW_GROUP)]
            qs = jnp.concatenate([q[:, hq * SW_HEAD_DIM:(hq + 1) * SW_HEAD_DIM] for hq in heads], axis=0)
            sc = _dot_nt(qs.astype(BF16), kk[:, kcols]) + bias_scr[heads[0] * pad:(heads[-1] + 1) * pad, :]
            sink = jnp.concatenate([jnp.full((pad, 1), sink_ref[hq], F32) for hq in heads], axis=0)
            og = _sink_softmax_pv(sc, sink, vv[:, kcols])
            for j, hq in enumerate(heads):
                o_ref[s, :, hq * SW_HEAD_DIM:(hq + 1) * SW_HEAD_DIM] = og[j * pad:(j + 1) * pad]
        head, tail = shifted(ck, kn)
        nk_ref[s, 0:win - pad] = head
        nk_ref[s, win - pad:win] = tail
        head, tail = shifted(cv, vn)
        nv_ref[s, 0:win - pad] = head
        nv_ref[s, win - pad:win] = tail
        return carry

    lax.fori_loop(0, bb, body, 0, unroll=True)


def _swa_step(geo, q, k, v, cache_k, cache_v, rel_bias, sinks):
    steps, db = geo.dec_seq, geo.dec_batch
    win = cache_k.shape[1]
    assert win == WINDOW and steps <= SUBLANES
    bb = min(8, db)
    pad = SUBLANES
    q_s = _seq_major(geo, q[geo.n_prompt:].astype(F32))
    k_s = _seq_major(geo, k[geo.n_prompt:])
    v_s = _seq_major(geo, v[geo.n_prompt:])
    kpos = np.arange(SW_KEYS)
    k_ok = kpos < win + steps
    qpos = win + np.arange(pad)
    bkt = _bucket_table(qpos, kpos, k_ok)
    bkt[steps:] = -1
    smem = pl.BlockSpec(memory_space=pltpu.SMEM)
    blk3 = lambda r, w: pl.BlockSpec((bb, r, w), lambda i: (i, 0, 0))
    o_s, nk, nv = pl.pallas_call(
        functools.partial(_swa_step_kernel, steps=steps, bb=bb),
        grid=(db // bb,),
        in_specs=[
            smem, smem,
            pl.BlockSpec((pad, SW_KEYS), lambda i: (0, 0)),
            blk3(pad, SW_QW), blk3(pad, SW_KV_W), blk3(pad, SW_KV_W),
            blk3(win, SW_KV_W), blk3(win, SW_KV_W),
        ],
        out_specs=[blk3(pad, SW_QW), blk3(win, SW_KV_W), blk3(win, SW_KV_W)],
        out_shape=[
            jax.ShapeDtypeStruct((db, pad, SW_QW), F32),
            jax.ShapeDtypeStruct((db, win, SW_KV_W), F32),
            jax.ShapeDtypeStruct((db, win, SW_KV_W), F32),
        ],
        scratch_shapes=[pltpu.VMEM((SW_Q_HEADS * pad, SW_KEYS), F32)],
        compiler_params=_cparams(("arbitrary",)),
        name="swa_step",
    )(rel_bias, sinks, jnp.asarray(bkt), q_s, k_s, v_s, cache_k, cache_v)
    return _time_major(geo, o_s), nk, nv


def kernel(x_prompt, x_sample, state_hgrn, cache_win_k, cache_win_v, c_prompt, c_sample, norm1_g, norm2_g, ada_w, ada_b, hg_w_in, hg_lb_table, hg_onorm_g, hg_w_out, sw_w_in, sw_qnorm_g, sw_knorm_g, sw_sinks, sw_w_out, rel_bias, router_w, router_b, moe_w_in, moe_b_in, moe_w_out, moe_b_out):
    batch, seq, d = x_prompt.shape
    db, steps, _ = x_sample.shape
    assert d == D_MODEL and ada_w.shape[0] == 2 and hg_w_in.shape[0] == 1 and sw_w_in.shape[0] == 1
    geo = _Geom(batch, seq, db, steps)
    x = (x_prompt.reshape(batch * seq, d), x_sample.transpose(1, 0, 2).reshape(steps * db, d))
    n_seq = batch + db
    rows = -(-n_seq // SUBLANES) * SUBLANES
    c_all = jnp.concatenate([c_prompt, c_sample, jnp.zeros((rows - n_seq, d), F32)], axis=0)
    mods = _ada_mods(c_all, ada_w, ada_b)

    def layer_mods(layer):
        mod_p = mods[layer, :batch].reshape(batch, 1, 6 * d)
        mod_s = jnp.tile(mods[layer, batch:n_seq], (steps, 1))
        return mod_p, mod_s

    def moe(layer, a_p, a_s, w_out, x_in, mod_p, mod_s):
        x1, h2, pos, pg, nch, soff = _post(geo, a_p, a_s, w_out.astype(BF16), x_in, norm2_g[layer:layer + 1],
                                           mod_p, mod_s, 0, router_w[layer], router_b[layer])
        return _moe(geo, h2, pos, pg, nch, soff, x1, mod_p, mod_s, 5,
                    layer, moe_w_in, moe_b_in, moe_w_out, moe_b_out)

    mod_p, mod_s = layer_mods(0)
    og = hg_onorm_g[0:1]
    z = _proj(geo, x, norm1_g[0:1], mod_p, mod_s, hg_w_in[0].astype(BF16), 0, 1)
    o_p, st_p = _gla_prompt(geo, z, hg_lb_table, og)
    o_s, st_s = _gla_step(geo, z, state_hgrn[0], hg_lb_table, og)
    x = moe(0, o_p, o_s, hg_w_out[0], x, mod_p, mod_s)

    mod_p, mod_s = layer_mods(1)
    q, k, v = _attn_proj(geo, x, norm1_g[1:2], mod_p, mod_s, sw_w_in[0].astype(BF16), sw_qnorm_g[0], sw_knorm_g[0])
    win = cache_win_k.shape[2]
    a_p = _swa_prompt(geo, q, k, v, rel_bias, sw_sinks[0])
    a_s, nk, nv = _swa_step(geo, q, k, v, cache_win_k[0].reshape(db, win, SW_KV_W),
                            cache_win_v[0].reshape(db, win, SW_KV_W), rel_bias, sw_sinks[0])
    x = moe(1, a_p, a_s, sw_w_out[0], x, mod_p, mod_s)

    y_prompt = x[0].reshape(batch, seq, d)
    y_sample = x[1].reshape(steps, db, d).transpose(1, 0, 2)
    kv_shape = (1, batch, WINDOW, SW_KV_HEADS, SW_HEAD_DIM)
    k_p = k[:geo.n_prompt].reshape(batch, seq, SW_KV_W)[:, seq - WINDOW:].reshape(kv_shape)
    v_p = v[:geo.n_prompt].reshape(batch, seq, SW_KV_W)[:, seq - WINDOW:].reshape(kv_shape)
    cache_shape = (1, db, win, SW_KV_HEADS, SW_HEAD_DIM)
    return (y_prompt, y_sample, jnp.swapaxes(st_p, -1, -2)[None], st_s[None], k_p, v_p,
            nk.reshape(cache_shape), nv.reshape(cache_shape))
```

```python
import functools
import math

import numpy as np
import jax
import jax.numpy as jnp
from jax import lax
from jax.experimental import pallas as pl
from jax.experimental.pallas import tpu as pltpu

F32 = jnp.float32
BF16 = jnp.bfloat16
I32 = jnp.int32

D_MODEL = 1024
LANES = 128
SUBLANES = 8
D_TILES = D_MODEL // LANES
HG_DK = 128
HG_HEADS = D_MODEL // HG_DK
HG_DV = D_MODEL // HG_HEADS
HG_CHUNK = 64
SW_HEAD_DIM = 64
SW_Q_HEADS = D_MODEL // SW_HEAD_DIM
SW_KV_HEADS = 4
SW_GROUP = SW_Q_HEADS // SW_KV_HEADS
SW_KV_W = SW_KV_HEADS * SW_HEAD_DIM
WINDOW = 128
ATT_BLOCK = 128
SW_SCALE = SW_HEAD_DIM ** -0.5
REL_BUCKETS = 32
REL_MAX_DIST = 128
N_EXPERTS = 32
TOP_K = 4
SWIGLU_LIMIT = 7.0
SWIGLU_ALPHA = 1.702
NORM_EPS = 1e-5
MOE_BLOCK = 256
ROW_CHUNK = SUBLANES
COPY_CHUNKS = 4
CHUNK_UNROLL = 8
WAIT_CHUNKS = 16
VMEM_LIMIT = 56 * 1024 * 1024


def _cparams(sem):
    return pltpu.CompilerParams(dimension_semantics=sem, vmem_limit_bytes=VMEM_LIMIT)


def _sigmoid(x):
    return 0.5 * jnp.tanh(0.5 * x) + 0.5


def _silu(x):
    return x * _sigmoid(x)


def _dot(a, b):
    return jnp.dot(a, b, preferred_element_type=F32)


def _dot_nt(a, b):
    return lax.dot_general(a, b, (((1,), (1,)), ((), ())), preferred_element_type=F32)


def _dot_tn(a, b):
    return lax.dot_general(a, b, (((0,), (0,)), ((), ())), preferred_element_type=F32)


def _split3(x):
    hi = x.astype(BF16)
    r = x - hi.astype(F32)
    mid = r.astype(BF16)
    lo = (r - mid.astype(F32)).astype(BF16)
    return hi, mid, lo


def _ada_kernel(c_ref, w_ref, b_ref, o_ref):
    s = _silu(c_ref[...]).astype(BF16)
    o_ref[0] = _dot(s, w_ref[0].astype(BF16)) + b_ref[0]


def _ada_mods(c_all, ada_w, ada_b):
    depth, d, n6 = ada_w.shape
    rows = c_all.shape[0]
    tn = 1536
    return pl.pallas_call(
        _ada_kernel,
        grid=(depth, n6 // tn),
        in_specs=[
            pl.BlockSpec((rows, d), lambda l, j: (0, 0)),
            pl.BlockSpec((1, d, tn), lambda l, j: (l, 0, j)),
            pl.BlockSpec((1, 1, tn), lambda l, j: (l, 0, j)),
        ],
        out_specs=pl.BlockSpec((1, rows, tn), lambda l, j: (l, 0, j)),
        out_shape=jax.ShapeDtypeStruct((depth, rows, n6), F32),
        compiler_params=_cparams(("arbitrary", "arbitrary")),
        name="ada_mods",
    )(c_all, ada_w, ada_b.reshape(depth, 1, n6))


class _Geom:
    def __init__(self, batch, seq, dec_batch, dec_seq):
        self.batch, self.seq, self.dec_batch, self.dec_seq = batch, seq, dec_batch, dec_seq
        self.n_prompt = batch * seq
        self.n_sample = dec_batch * dec_seq
        self.nt = self.n_prompt + self.n_sample
        tm = 512
        while seq % tm or self.n_sample % tm:
            tm //= 2
        assert tm >= 8
        self.tm = tm
        self.n_pt = self.n_prompt // tm
        self.n_tiles = self.nt // tm
        self.tm_moe = min(256, tm)


def _mod_specs(geo, col, tm):
    seq, batch = geo.seq, geo.batch
    n_pt = geo.n_prompt // tm

    def p_map(i, *_):
        return (jnp.minimum(i * tm // seq, batch - 1), 0, col)

    def s_map(i, *_):
        return (jnp.maximum(i - n_pt, 0), col)

    return [pl.BlockSpec((1, 1, D_MODEL), p_map), pl.BlockSpec((tm, D_MODEL), s_map)]


def _pick_mod(i, n_pt, p_ref, s_ref):
    return jnp.where(i >= n_pt, s_ref[...], p_ref[0])


def _norm_mod(x, g, sc, sh):
    ms = jnp.mean(x * x, axis=-1, keepdims=True)
    return x * lax.rsqrt(ms + NORM_EPS) * g * (1.0 + sc) + sh


def _pair_specs(geo, tm, width):
    n_pt = geo.n_prompt // tm
    return [pl.BlockSpec((tm, width), lambda i, *_: (jnp.minimum(i, n_pt - 1), 0)),
            pl.BlockSpec((tm, width), lambda i, *_: (jnp.maximum(i - n_pt, 0), 0))]


def _proj_kernel(xp_ref, xs_ref, g_ref, shp_ref, shs_ref, scp_ref, scs_ref, w_ref, o_ref, *, n_pt, tn):
    i = pl.program_id(0)
    sh = _pick_mod(i, n_pt, shp_ref, shs_ref)
    sc = _pick_mod(i, n_pt, scp_ref, scs_ref)
    x = jnp.where(i >= n_pt, xs_ref[...], xp_ref[...])
    h = _norm_mod(x, g_ref[...], sc, sh).astype(BF16)
    for j in range(o_ref.shape[1] // tn):
        o_ref[:, j * tn:(j + 1) * tn] = _dot(h, w_ref[:, j * tn:(j + 1) * tn])


def _proj(geo, x_pair, g, mod_p, mod_s, w_bf16, col_shift, col_scale):
    n_out = w_bf16.shape[1]
    tm = geo.tm
    fixed = lambda i: (0, 0)
    return pl.pallas_call(
        functools.partial(_proj_kernel, n_pt=geo.n_pt, tn=1024),
        grid=(geo.n_tiles,),
        in_specs=[
            *_pair_specs(geo, tm, D_MODEL),
            pl.BlockSpec((1, D_MODEL), fixed),
            *_mod_specs(geo, col_shift, tm),
            *_mod_specs(geo, col_scale, tm),
            pl.BlockSpec((D_MODEL, n_out), fixed),
        ],
        out_specs=pl.BlockSpec((tm, n_out), lambda i: (i, 0)),
        out_shape=jax.ShapeDtypeStruct((geo.nt, n_out), F32),
        compiler_params=_cparams(("arbitrary",)),
        name="norm_mod_proj",
    )(*x_pair, g, mod_p, mod_s, mod_p, mod_s, w_bf16)


def _hg_lower_bound(lbt_ref):
    t = lbt_ref[...]
    e = jnp.exp(t - jnp.max(t, axis=0, keepdims=True))
    return e[0:1] / jnp.sum(e, axis=0, keepdims=True)


def _hg_gates(fz, lb):
    t = jnp.tanh(0.5 * fz)
    logf = jnp.log(lb + (1.0 - lb) * (0.5 + 0.5 * t))
    return logf, (1.0 - lb) * (0.5 - 0.5 * t)


def _hg_out(o, gz, og):
    ms = jnp.mean(o * o, axis=-1, keepdims=True)
    return o * lax.rsqrt(ms + NORM_EPS) * og * _silu(gz)


def _gla_prompt_kernel(*refs, batch, chunk, n_chunks):
    z_refs = refs[:batch]
    lbt_ref, og_ref, o_ref, sfin_ref, st_scr = refs[batch:]
    t_step = pl.program_id(0)
    kw = HG_HEADS * HG_DK

    @pl.when(t_step == 0)
    def _():
        st_scr[...] = jnp.zeros_like(st_scr)

    lb = _hg_lower_bound(lbt_ref)
    og = og_ref[...]
    r_i = lax.broadcasted_iota(I32, (chunk, chunk), 0)
    c_i = lax.broadcasted_iota(I32, (chunk, chunk), 1)
    causal = c_i <= r_i
    tri = causal.astype(BF16)
    mid = chunk // 2 - 1

    def body(c, carry):
        rows = pl.ds(pl.multiple_of(c * chunk, chunk), chunk)
        for bi, z_ref in enumerate(z_refs):
            logf, kk = _hg_gates(z_ref[rows, kw:2 * kw], lb)
            hi, md, lo = _split3(logf)
            cs = _dot(tri, jnp.concatenate([hi, md, lo], axis=1))
            b = cs[:, :kw] + cs[:, kw:2 * kw] + cs[:, 2 * kw:]
            b_mid = b[mid:mid + 1]
            b_last = b[chunk - 1:chunk]
            q_hat = _silu(z_ref[rows, 0:kw]) * jnp.exp(b - b_mid)
            k_hat = kk * jnp.exp(b_mid - b)
            q_in = (q_hat * jnp.exp(b_mid)).astype(BF16)
            k_dec = (k_hat * jnp.exp(b_last - b_mid)).astype(BF16)
            q_hat = q_hat.astype(BF16)
            k_hat = k_hat.astype(BF16)
            dec = jnp.exp(b_last)
            for h in range(HG_HEADS):
                cols = slice(h * HG_DK, (h + 1) * HG_DK)
                vcols = slice(2 * kw + h * HG_DV, 2 * kw + (h + 1) * HG_DV)
                gcols = slice(2 * kw + HG_HEADS * HG_DV + h * HG_DV, 2 * kw + HG_HEADS * HG_DV + (h + 1) * HG_DV)
                v = z_ref[rows, vcols].astype(BF16)
                att = jnp.where(causal, _dot_nt(q_hat[:, cols], k_hat[:, cols]), 0.0).astype(BF16)
                st = st_scr[bi, h]
                o = _dot(att, v) + _dot_nt(q_in[:, cols], st.astype(BF16))
                st_scr[bi, h] = st * dec[:, cols] + _dot_tn(v, k_dec[:, cols])
                o_ref[bi, rows, h * HG_DV:(h + 1) * HG_DV] = _hg_out(o, z_ref[rows, gcols], og).astype(o_ref.dtype)
        return carry

    lax.fori_loop(0, n_chunks, body, 0)

    @pl.when(t_step == pl.num_programs(0) - 1)
    def _():
        sfin_ref[...] = st_scr[...]


def _gla_prompt(geo, z, lb_table, o_gain):
    tg = min(256, geo.seq)
    chunk = HG_CHUNK if geo.seq % HG_CHUNK == 0 else geo.seq
    assert tg % chunk == 0 and geo.seq % tg == 0
    nt = geo.seq // tg
    batch = geo.batch
    fixed = lambda t: (0, 0)
    state_shape = (batch, HG_HEADS, HG_DV, HG_DK)
    o, st = pl.pallas_call(
        functools.partial(_gla_prompt_kernel, batch=batch, chunk=chunk, n_chunks=tg // chunk),
        grid=(nt,),
        in_specs=[
            *[pl.BlockSpec((tg, 4 * D_MODEL), functools.partial(lambda t, b: (b * nt + t, 0), b=b))
              for b in range(batch)],
            pl.BlockSpec(lb_table.shape, fixed),
            pl.BlockSpec((1, HG_DV), fixed),
        ],
        out_specs=[
            pl.BlockSpec((batch, tg, D_MODEL), lambda t: (0, t, 0)),
            pl.BlockSpec(state_shape, lambda t: (0, 0, 0, 0)),
        ],
        out_shape=[
            jax.ShapeDtypeStruct((batch, geo.seq, D_MODEL), BF16),
            jax.ShapeDtypeStruct(state_shape, F32),
        ],
        scratch_shapes=[pltpu.VMEM(state_shape, F32)],
        compiler_params=_cparams(("arbitrary",)),
        name="gla_prompt",
    )(*([z] * batch), lb_table, o_gain)
    return o.reshape(geo.n_prompt, D_MODEL), st


def _gla_step_kernel(z_ref, s_ref, lbt_ref, og_ref, o_ref, snew_ref, *, steps, bb):
    kw = HG_HEADS * HG_DK
    pad = SUBLANES
    lb = _hg_lower_bound(lbt_ref)
    og = og_ref[...]
    r_i = lax.broadcasted_iota(I32, (pad, pad), 0)
    c_i = lax.broadcasted_iota(I32, (pad, pad), 1)
    causal = c_i <= r_i
    tri = causal.astype(BF16)
    row_w = lax.broadcasted_iota(I32, (pad, kw), 0)
    live = row_w < steps
    row_k = lax.broadcasted_iota(I32, (pad, HG_DK), 0)
    ones_sel = jnp.where((row_k == steps) | (row_k == steps + 1), 1.0, 0.0).astype(BF16)

    def body(s, carry):
        z = z_ref[s]
        logf, kk = _hg_gates(z[:, kw:2 * kw], lb)
        hi, md, lo = _split3(jnp.where(live, logf, 0.0))
        cs = _dot(tri, jnp.concatenate([hi, md, lo], axis=1))
        b = cs[:, :kw] + cs[:, kw:2 * kw] + cs[:, 2 * kw:]
        b_last = b[steps - 1:steps]
        q_in = (_silu(z[:, 0:kw]) * jnp.exp(b)).astype(BF16)
        k_hat = jnp.where(live, kk * jnp.exp(-b), 0.0).astype(BF16)
        k_dec = jnp.where(live, kk * jnp.exp(b_last - b), 0.0).astype(BF16)
        dec = jnp.exp(b_last)
        d_hi = dec.astype(BF16)
        d_lo = (dec - d_hi.astype(F32)).astype(BF16)
        a_all = jnp.where(row_w == steps, d_hi, jnp.where(row_w == steps + 1, d_lo, k_dec))
        for h in range(HG_HEADS):
            cols = slice(h * HG_DK, (h + 1) * HG_DK)
            vcols = slice(2 * kw + h * HG_DV, 2 * kw + (h + 1) * HG_DV)
            gcols = slice(2 * kw + HG_HEADS * HG_DV + h * HG_DV, 2 * kw + HG_HEADS * HG_DV + (h + 1) * HG_DV)
            v = z[:, vcols].astype(BF16)
            s0 = s_ref[s, h]
            att = jnp.where(causal, _dot_nt(q_in[:, cols], k_hat[:, cols]), 0.0).astype(BF16)
            o = _dot(att, v) + _dot(q_in[:, cols], s0.astype(BF16))
            upd = _dot_tn(a_all[:, cols], jnp.concatenate([v, ones_sel], axis=1))
            snew_ref[s, h] = upd[:, HG_DV:] * s0 + upd[:, :HG_DV]
            o_ref[s, :, h * HG_DV:(h + 1) * HG_DV] = _hg_out(o, z[:, gcols], og)
        return carry

    lax.fori_loop(0, bb, body, 0, unroll=True)


def _seq_major(geo, rows):
    steps, db = geo.dec_seq, geo.dec_batch
    w = rows.shape[-1]
    r = rows.reshape(steps, db, w).transpose(1, 0, 2)
    return jnp.concatenate([r, jnp.zeros((db, SUBLANES - steps, w), rows.dtype)], axis=1)


def _time_major(geo, r):
    steps, db = geo.dec_seq, geo.dec_batch
    return r[:, :steps].transpose(1, 0, 2).reshape(steps * db, r.shape[-1])


def _gla_step(geo, z, state, lb_table, o_gain):
    steps, db = geo.dec_seq, geo.dec_batch
    assert steps + 2 <= SUBLANES
    bb = min(8, db)
    z_s = _seq_major(geo, z[geo.n_prompt:])
    o_s, s_new = pl.pallas_call(
        functools.partial(_gla_step_kernel, steps=steps, bb=bb),
        grid=(db // bb,),
        in_specs=[
            pl.BlockSpec((bb, SUBLANES, 4 * D_MODEL), lambda i: (i, 0, 0)),
            pl.BlockSpec((bb, HG_HEADS, HG_DK, HG_DV), lambda i: (i, 0, 0, 0)),
            pl.BlockSpec(lb_table.shape, lambda i: (0, 0)),
            pl.BlockSpec((1, HG_DV), lambda i: (0, 0)),
        ],
        out_specs=[
            pl.BlockSpec((bb, SUBLANES, D_MODEL), lambda i: (i, 0, 0)),
            pl.BlockSpec((bb, HG_HEADS, HG_DK, HG_DV), lambda i: (i, 0, 0, 0)),
        ],
        out_shape=[
            jax.ShapeDtypeStruct((db, SUBLANES, D_MODEL), F32),
            jax.ShapeDtypeStruct(state.shape, F32),
        ],
        compiler_params=_cparams(("arbitrary",)),
        name="gla_step",
    )(z_s, state, lb_table, o_gain)
    return _time_major(geo, o_s), s_new


def _post_kernel(ap_ref, as_ref, wo_ref, xp_ref, xs_ref, gp_ref, gs_ref, n2_ref, shp_ref, shs_ref, scp_ref,
                 scs_ref, rw_ref, rb_ref, x1_ref, h2_ref, pos_ref, pg_ref, nch_ref, soff_ref, *, n_pt, tm):
    i = pl.program_id(0)
    g1 = _pick_mod(i, n_pt, gp_ref, gs_ref)
    a = jnp.where(i >= n_pt, as_ref[...], ap_ref[...])
    x = jnp.where(i >= n_pt, xs_ref[...], xp_ref[...])
    x1 = x + g1 * _dot(a, wo_ref[...])
    x1_ref[...] = x1
    sh = _pick_mod(i, n_pt, shp_ref, shs_ref)
    sc = _pick_mod(i, n_pt, scp_ref, scs_ref)
    h2 = _norm_mod(x1, n2_ref[...], sc, sh)
    h2_ref[...] = h2.astype(h2_ref.dtype)

    row_e = lax.broadcasted_iota(I32, (LANES, tm), 0)
    logits = _dot_nt(rw_ref[...], h2.astype(BF16)) + rb_ref[...]
    work = jnp.where(row_e < N_EXPERTS, logits, -jnp.inf)
    vals, hits = [], []
    for _ in range(TOP_K):
        m = jnp.max(work, axis=0, keepdims=True)
        idx = jnp.min(jnp.where(work == m, row_e, LANES), axis=0, keepdims=True)
        hit = row_e == idx
        vals.append(m)
        hits.append(hit)
        work = jnp.where(hit, -jnp.inf, work)
    exps = [jnp.exp(v - vals[0]) for v in vals]
    den = exps[0]
    for e in exps[1:]:
        den = den + e
    any_hit = hits[0]
    for hmask in hits[1:]:
        any_hit = any_hit | hmask
    any_f = jnp.where(any_hit, 1.0, 0.0)
    t_i = lax.broadcasted_iota(I32, (tm, tm), 0)
    t_j = lax.broadcasted_iota(I32, (tm, tm), 1)
    rank = _dot(any_f.astype(BF16), (t_i < t_j).astype(BF16))
    n_chunk = jnp.floor((jnp.sum(any_f, axis=1, keepdims=True) + (ROW_CHUNK - 1)) * (1.0 / ROW_CHUNK))
    e_i = lax.broadcasted_iota(I32, (LANES, LANES), 0)
    e_j = lax.broadcasted_iota(I32, (LANES, LANES), 1)
    seg = _dot((e_j < e_i).astype(BF16), jnp.broadcast_to(n_chunk, (LANES, LANES)).astype(BF16))[:, 0:1]
    where_to = seg * float(ROW_CHUNK) + rank
    pos_rows = [jnp.sum(jnp.where(hits[k], where_to, 0.0), axis=0, keepdims=True) for k in range(TOP_K)]
    gate_rows = [exps[k] / den for k in range(TOP_K)]
    pos_ref[0] = jnp.concatenate(pos_rows + [jnp.zeros((SUBLANES - TOP_K, tm), F32)], axis=0).astype(I32)
    rows = jnp.concatenate(pos_rows + gate_rows + [jnp.zeros((LANES - 2 * TOP_K, tm), F32)], axis=0)
    pg_ref[...] = rows.T
    nch_ref[0] = n_chunk.astype(I32)
    soff_ref[0] = seg.astype(I32)


def _post(geo, a_p, a_s, w_out_bf16, x_pair, norm_g, mod_p, mod_s, col0, router_w, router_b):
    tm = geo.tm_moe
    n_pt = geo.n_prompt // tm
    n_tiles = geo.nt // tm
    rw = jnp.zeros((LANES, D_MODEL), BF16).at[:N_EXPERTS].set(router_w.T.astype(BF16))
    rb = jnp.zeros((LANES, 1), F32).at[:N_EXPERTS, 0].set(router_b)
    row = lambda i: (i, 0)
    fixed = lambda i: (0, 0)
    tile_row = lambda i: (i, 0, 0)
    return pl.pallas_call(
        functools.partial(_post_kernel, n_pt=n_pt, tm=tm),
        grid=(n_tiles,),
        in_specs=[
            *_pair_specs(geo, tm, D_MODEL),
            pl.BlockSpec((D_MODEL, D_MODEL), fixed),
            *_pair_specs(geo, tm, D_MODEL),
            *_mod_specs(geo, col0 + 2, tm),
            pl.BlockSpec((1, D_MODEL), fixed),
            *_mod_specs(geo, col0 + 3, tm),
            *_mod_specs(geo, col0 + 4, tm),
            pl.BlockSpec((LANES, D_MODEL), fixed),
            pl.BlockSpec((LANES, 1), fixed),
        ],
        out_specs=[
            pl.BlockSpec((tm, D_MODEL), row),
            pl.BlockSpec((tm, D_MODEL), row),
            pl.BlockSpec((1, SUBLANES, tm), tile_row),
            pl.BlockSpec((tm, LANES), row),
            pl.BlockSpec((1, LANES, 1), tile_row),
            pl.BlockSpec((1, LANES, 1), tile_row),
        ],
        out_shape=[
            jax.ShapeDtypeStruct((geo.nt, D_MODEL), F32),
            jax.ShapeDtypeStruct((geo.nt, D_MODEL), BF16),
            jax.ShapeDtypeStruct((n_tiles, SUBLANES, tm), I32),
            jax.ShapeDtypeStruct((geo.nt, LANES), F32),
            jax.ShapeDtypeStruct((n_tiles, LANES, 1), I32),
            jax.ShapeDtypeStruct((n_tiles, LANES, 1), I32),
        ],
        compiler_params=_cparams(("arbitrary",)),
        name="post_mixer_router",
    )(a_p, a_s.astype(BF16), w_out_bf16, *x_pair, mod_p, mod_s, norm_g, mod_p, mod_s, mod_p, mod_s, rw, rb)


def _ffn_kernel(first_ref, count_ref, xs_hbm, win_ref, bin_ref, wout_ref, bout_ref, ys_hbm,
                xbuf, ybuf, win_scr, wout_scr, xsem, ysem, *, half):
    e = pl.program_id(0)
    first = first_ref[e]
    n_sub = count_ref[e]
    pairs = n_sub // 2
    odd = n_sub - 2 * pairs
    win_scr[...] = win_ref[0].astype(BF16)
    wout_scr[...] = wout_ref[0].astype(BF16)

    def rows_at(sub, n_rows):
        return pl.ds(pl.multiple_of((first + sub) * half, half), n_rows)

    def x_copy(sub, n_rows, slot):
        return pltpu.make_async_copy(xs_hbm.at[rows_at(sub, n_rows)], xbuf.at[slot, pl.ds(0, n_rows)], xsem.at[slot])

    def y_copy(sub, n_rows, slot):
        return pltpu.make_async_copy(ybuf.at[slot, pl.ds(0, n_rows)], ys_hbm.at[rows_at(sub, n_rows)], ysem.at[slot])

    def ffn(slot, rows):
        gu = _dot(xbuf[slot, rows, :].astype(BF16), win_scr[...]) + bin_ref[0]
        gate = jnp.minimum(gu[:, :D_MODEL], SWIGLU_LIMIT)
        up = jnp.clip(gu[:, D_MODEL:], -SWIGLU_LIMIT, SWIGLU_LIMIT)
        act = gate * _sigmoid(SWIGLU_ALPHA * gate) * (up + 1.0)
        return _dot(act.astype(BF16), wout_scr[...]) + bout_ref[0]

    def start_first(expert):
        begin = pl.multiple_of(first_ref[expert] * half, half)
        n = count_ref[expert]

        @pl.when(n >= 2)
        def _():
            pltpu.make_async_copy(xs_hbm.at[pl.ds(begin, 2 * half)], xbuf.at[0], xsem.at[0]).start()

        @pl.when(n == 1)
        def _():
            pltpu.make_async_copy(xs_hbm.at[pl.ds(begin, half)], xbuf.at[0, pl.ds(0, half)], xsem.at[0]).start()

    @pl.when(e == 0)
    def _():
        start_first(0)

    def unit(u, carry):
        slot = u % 2
        x_copy(2 * u, 2 * half, slot).wait()

        @pl.when(u + 1 < pairs)
        def _():
            x_copy(2 * (u + 1), 2 * half, 1 - slot).start()

        @pl.when((u + 1 == pairs) & (odd == 1))
        def _():
            x_copy(2 * pairs, half, 1 - slot).start()

        @pl.when(u >= 2)
        def _():
            y_copy(0, 2 * half, slot).wait()

        ybuf[slot] = ffn(slot, slice(0, 2 * half))
        y_copy(2 * u, 2 * half, slot).start()
        return carry

    lax.fori_loop(0, pairs, unit, 0)
    tail_slot = pairs % 2

    @pl.when(odd == 1)
    def _():
        x_copy(2 * pairs, half, tail_slot).wait()

        @pl.when(pairs >= 2)
        def _():
            y_copy(0, 2 * half, tail_slot).wait()

        ybuf[tail_slot, 0:half, :] = ffn(tail_slot, slice(0, half))
        y_copy(2 * pairs, half, tail_slot).start()

    @pl.when(e + 1 < pl.num_programs(0))
    def _():
        start_first(e + 1)

    @pl.when(odd == 1)
    def _():
        @pl.when(pairs >= 1)
        def _():
            y_copy(0, 2 * half, 1 - tail_slot).wait()

        y_copy(0, half, tail_slot).wait()

    @pl.when(odd == 0)
    def _():
        @pl.when(pairs >= 2)
        def _():
            y_copy(0, 2 * half, tail_slot).wait()

        @pl.when(pairs >= 1)
        def _():
            y_copy(0, 2 * half, 1 - tail_slot).wait()


def _ffn(xs, first_sub, n_sub, layer, w_in, b_in, w_out, b_out):
    n_e, d, d2 = w_in.shape
    half = MOE_BLOCK
    w_map = lambda e, *_: (layer * N_EXPERTS + e, 0, 0)
    return pl.pallas_call(
        functools.partial(_ffn_kernel, half=half),
        grid_spec=pltpu.PrefetchScalarGridSpec(
            num_scalar_prefetch=2,
            grid=(N_EXPERTS,),
            in_specs=[
                pl.BlockSpec(memory_space=pl.ANY),
                pl.BlockSpec((1, d, d2), w_map),
                pl.BlockSpec((1, 1, d2), w_map),
                pl.BlockSpec((1, d2 // 2, d), w_map),
                pl.BlockSpec((1, 1, d), w_map),
            ],
            out_specs=pl.BlockSpec(memory_space=pl.ANY),
            scratch_shapes=[
                pltpu.VMEM((2, 2 * half, d), F32),
                pltpu.VMEM((2, 2 * half, d), F32),
                pltpu.VMEM((d, d2), BF16),
                pltpu.VMEM((d2 // 2, d), BF16),
                pltpu.SemaphoreType.DMA((2,)),
                pltpu.SemaphoreType.DMA((2,)),
            ],
        ),
        out_shape=jax.ShapeDtypeStruct(xs.shape, xs.dtype),
        input_output_aliases={2: 0},
        compiler_params=_cparams(("arbitrary",)),
        name="moe_ffn",
    )(first_sub, n_sub, xs, w_in, b_in.reshape(n_e, 1, d2), w_out, b_out.reshape(n_e, 1, d))


def _chunk_rows(chunk, n_chunks=1):
    return pl.ds(pl.multiple_of(chunk * ROW_CHUNK, ROW_CHUNK), n_chunks * ROW_CHUNK)


def _run_pieces(n, fn, big):
    n_big = n // big

    def big_piece(i, carry):
        fn(i * big, big)
        return carry

    def small_piece(i, carry):
        fn(n_big * big + i, 1)
        return carry

    lax.fori_loop(0, n_big, big_piece, 0)
    lax.fori_loop(0, n - n_big * big, small_piece, 0)


def _for_each_chunk(where_ref, tot_ref, tile, fn):
    n = tot_ref[tile]
    n_groups = n // CHUNK_UNROLL

    def group(i, carry):
        for u in range(CHUNK_UNROLL):
            j = i * CHUNK_UNROLL + u
            fn(j, where_ref[tile, j])
        return carry

    def single(i, carry):
        j = n_groups * CHUNK_UNROLL + i
        fn(j, where_ref[tile, j])
        return carry

    lax.fori_loop(0, n_groups, group, 0)
    lax.fori_loop(0, n - n_groups * CHUNK_UNROLL, single, 0)


def _dispatch_kernel(where_ref, tot_ref, fill0_ref, filln_ref,
                     pos_ref, h_ref, xs_hbm, stage, zeros, sem, fill_sem, *, tm, k_stage):
    t = pl.program_id(0)
    n_t = pl.num_programs(0)
    slot = t % 2

    def copy(s_chunk, d_chunk, size, sl):
        return pltpu.make_async_copy(stage.at[sl, _chunk_rows(s_chunk, size)],
                                     xs_hbm.at[_chunk_rows(d_chunk, size)], sem.at[sl])

    def wait_tile(tile, sl):
        _run_pieces(tot_ref[tile], lambda off, size: copy(0, 0, size, sl).wait(), WAIT_CHUNKS)

    @pl.when(t >= 2)
    def _():
        wait_tile(t - 2, slot)

    pos_t = pos_ref[0]
    j_i = lax.broadcasted_iota(I32, (k_stage, tm), 0)
    hit = j_i == pos_t[0:1]
    for k in range(1, TOP_K):
        hit = hit | (j_i == pos_t[k:k + 1])
    stage[slot] = _dot(jnp.where(hit, 1.0, 0.0).astype(BF16), h_ref[...])
    _for_each_chunk(where_ref, tot_ref, t, lambda s, d: copy(s, d, 1, slot).start())

    @pl.when(t == n_t - 1)
    def _():
        zeros[...] = jnp.zeros_like(zeros)

        def fill(d_chunk, size):
            return pltpu.make_async_copy(zeros.at[_chunk_rows(0, size)], xs_hbm.at[_chunk_rows(d_chunk, size)],
                                         fill_sem.at[0])

        def per_region(r, carry):
            _run_pieces(filln_ref[r], lambda off, size: fill(fill0_ref[r] + off, size).start(), COPY_CHUNKS)
            return carry

        lax.fori_loop(0, N_EXPERTS + 1, per_region, 0)

        @pl.when(t >= 1)
        def _():
            wait_tile(t - 1, 1 - slot)

        wait_tile(t, slot)

        def per_region_wait(r, carry):
            _run_pieces(filln_ref[r], lambda off, size: fill(0, size).wait(), COPY_CHUNKS)
            return carry

        lax.fori_loop(0, N_EXPERTS + 1, per_region_wait, 0)


def _combine_kernel(where_ref, tot_ref, pg_ref, x_ref, gp_ref, gs_ref, y_hbm,
                    op_ref, os_ref, stage, sem, *, n_pt, tm, k_stage):
    t = pl.program_id(0)
    n_t = pl.num_programs(0)
    slot = t % 2

    def copy(s_chunk, d_chunk, size, sl):
        return pltpu.make_async_copy(y_hbm.at[_chunk_rows(d_chunk, size)],
                                     stage.at[sl, _chunk_rows(s_chunk, size)], sem.at[sl])

    def fetch(tile, sl):
        _for_each_chunk(where_ref, tot_ref, tile, lambda s, d: copy(s, d, 1, sl).start())

    @pl.when(t == 0)
    def _():
        stage[...] = jnp.zeros_like(stage)
        fetch(0, 0)

    @pl.when(t + 1 < n_t)
    def _():
        fetch(t + 1, 1 - slot)

    _run_pieces(tot_ref[t], lambda off, size: copy(0, 0, size, slot).wait(), WAIT_CHUNKS)
    lane = lax.broadcasted_iota(I32, (tm, k_stage), 1)
    pg = pg_ref[...]
    p = jnp.zeros((tm, k_stage), F32)
    for k in range(TOP_K):
        p = jnp.where(lane == pg[:, k:k + 1].astype(I32), pg[:, TOP_K + k:TOP_K + k + 1], p)
    ffn = _dot(p.astype(BF16), stage[slot].astype(BF16))
    out = x_ref[...] + _pick_mod(t, n_pt, gp_ref, gs_ref) * ffn

    @pl.when(t < n_pt)
    def _():
        op_ref[...] = out

    @pl.when(t >= n_pt)
    def _():
        os_ref[...] = out


def _moe(geo, h2, pos, pg, nch_pad, soff_pad, x1, mod_p, mod_s, col_gate, layer, w_in, b_in, w_out, b_out):
    bm = MOE_BLOCK
    tm = geo.tm_moe
    n_tiles = geo.nt // tm
    depth, n_e, d, d2 = w_in.shape
    w_in = w_in.reshape(depth * n_e, d, d2)
    b_in = b_in.reshape(depth * n_e, d2)
    w_out = w_out.reshape(depth * n_e, d2 // 2, d)
    b_out = b_out.reshape(depth * n_e, d)
    chunks_per_block = bm // ROW_CHUNK
    max_rows = geo.nt * TOP_K + n_tiles * N_EXPERTS * (ROW_CHUNK - 1) + N_EXPERTS * (bm - 1)
    n_blocks = -(-max_rows // bm)
    nch = nch_pad[:, :N_EXPERTS, 0]
    soff = soff_pad[:, :N_EXPERTS, 0]
    tot = jnp.sum(nch, axis=1)
    per_e = jnp.sum(nch, axis=0)
    padded = (per_e + chunks_per_block - 1) // chunks_per_block * chunks_per_block
    pend = jnp.cumsum(padded)
    pstart = pend - padded
    dst = pstart[None, :] + jnp.cumsum(nch, axis=0) - nch
    first_sub = (pstart // chunks_per_block).astype(I32)
    n_sub = (padded // chunks_per_block).astype(I32)
    fill0 = jnp.concatenate([pstart + per_e, pend[-1:]]).astype(I32)
    filln = jnp.concatenate([padded - per_e, n_blocks * chunks_per_block - pend[-1:]]).astype(I32)

    k_stage = -(-(tm * TOP_K + N_EXPERTS * (ROW_CHUNK - 1)) // LANES) * LANES
    j = jnp.arange(k_stage // ROW_CHUNK, dtype=I32)
    owner = jnp.minimum(jnp.sum(j[None, :, None] >= (soff + nch)[:, None, :], axis=-1), N_EXPERTS - 1)
    shift = jnp.sum(jnp.where(owner[:, :, None] == jnp.arange(N_EXPERTS, dtype=I32), (dst - soff)[:, None, :], 0),
                    axis=-1)
    where = (j[None, :] + shift).astype(I32)
    row = lambda i, *_: (i, 0)
    xs = pl.pallas_call(
        functools.partial(_dispatch_kernel, tm=tm, k_stage=k_stage),
        grid_spec=pltpu.PrefetchScalarGridSpec(
            num_scalar_prefetch=4,
            grid=(n_tiles,),
            in_specs=[pl.BlockSpec((1, SUBLANES, tm), lambda i, *_: (i, 0, 0)), pl.BlockSpec((tm, d), row)],
            out_specs=pl.BlockSpec(memory_space=pl.ANY),
            scratch_shapes=[
                pltpu.VMEM((2, k_stage, d), F32),
                pltpu.VMEM((COPY_CHUNKS * ROW_CHUNK, d), F32),
                pltpu.SemaphoreType.DMA((2,)),
                pltpu.SemaphoreType.DMA((1,)),
            ],
        ),
        out_shape=jax.ShapeDtypeStruct((n_blocks * bm, d), F32),
        compiler_params=_cparams(("arbitrary",)),
        name="moe_dispatch",
    )(where, tot, fill0, filln, pos, h2)

    ys = _ffn(xs, first_sub, n_sub, layer, w_in, b_in, w_out, b_out)

    return pl.pallas_call(
        functools.partial(_combine_kernel, n_pt=geo.n_prompt // tm, tm=tm, k_stage=k_stage),
        grid_spec=pltpu.PrefetchScalarGridSpec(
            num_scalar_prefetch=2,
            grid=(n_tiles,),
            in_specs=[
                pl.BlockSpec((tm, LANES), row),
                pl.BlockSpec((tm, d), row),
                *_mod_specs(geo, col_gate, tm),
                pl.BlockSpec(memory_space=pl.ANY),
            ],
            out_specs=_pair_specs(geo, tm, d),
            scratch_shapes=[
                pltpu.VMEM((2, k_stage, d), F32),
                pltpu.SemaphoreType.DMA((2,)),
            ],
        ),
        out_shape=[jax.ShapeDtypeStruct((geo.n_prompt, d), F32), jax.ShapeDtypeStruct((geo.n_sample, d), F32)],
        compiler_params=_cparams(("arbitrary",)),
        name="moe_combine",
    )(where, tot, pg, x1, mod_p, mod_s, ys)


SW_QW = SW_Q_HEADS * SW_HEAD_DIM
SW_QKW = SW_QW + SW_KV_W
SW_KEYS = 2 * ATT_BLOCK


def _attn_proj_kernel(xp_ref, xs_ref, g_ref, shp_ref, shs_ref, scp_ref, scs_ref, w_ref, e_ref, et_ref, qg_ref,
                      kg_ref, q_ref, k_ref, v_ref, *, n_pt):
    i = pl.program_id(0)
    sh = _pick_mod(i, n_pt, shp_ref, shs_ref)
    sc = _pick_mod(i, n_pt, scp_ref, scs_ref)
    x = jnp.where(i >= n_pt, xs_ref[...], xp_ref[...])
    h = _norm_mod(x, g_ref[...], sc, sh).astype(BF16)
    tm = h.shape[0]
    n_parts = 2 if tm % (2 * SUBLANES) == 0 else 1
    part = tm // n_parts
    for r in range(n_parts):
        rows = slice(r * part, (r + 1) * part)
        z = _dot(h[rows], w_ref[...])
        qk = z[:, :SW_QKW]
        sq = qk * qk
        sq_hi = sq.astype(BF16)
        sq_lo = (sq - sq_hi.astype(F32)).astype(BF16)
        ms = (_dot(sq_hi, e_ref[...]) + _dot(sq_lo, e_ref[...])) * (1.0 / SW_HEAD_DIM)
        inv = lax.rsqrt(ms + NORM_EPS)
        inv_hi = inv.astype(BF16)
        inv_lo = (inv - inv_hi.astype(F32)).astype(BF16)
        qk = qk * (_dot(inv_hi, et_ref[...]) + _dot(inv_lo, et_ref[...]))
        q_ref[rows, :] = (qk[:, :SW_QW] * qg_ref[...] * SW_SCALE).astype(q_ref.dtype)
        k_ref[rows, :] = qk[:, SW_QW:] * kg_ref[...]
        v_ref[rows, :] = z[:, SW_QKW:]


def _attn_proj(geo, x_pair, g, mod_p, mod_s, w_bf16, q_gain, k_gain):
    tm = geo.tm
    n_out = w_bf16.shape[1]
    heads = SW_QKW // SW_HEAD_DIM
    member = (np.arange(SW_QKW)[:, None] // SW_HEAD_DIM == np.arange(LANES)[None, :]).astype(np.float32)
    e = jnp.asarray(member, BF16)
    et = jnp.asarray(member.T, BF16)
    assert heads <= LANES
    qg = jnp.tile(q_gain, SW_Q_HEADS).reshape(1, SW_QW)
    kg = jnp.tile(k_gain, SW_KV_HEADS).reshape(1, SW_KV_W)
    row = lambda i: (i, 0)
    fixed = lambda i: (0, 0)
    return pl.pallas_call(
        functools.partial(_attn_proj_kernel, n_pt=geo.n_pt),
        grid=(geo.n_tiles,),
        in_specs=[
            *_pair_specs(geo, tm, D_MODEL),
            pl.BlockSpec((1, D_MODEL), fixed),
            *_mod_specs(geo, 0, tm),
            *_mod_specs(geo, 1, tm),
            pl.BlockSpec((D_MODEL, n_out), fixed),
            pl.BlockSpec((SW_QKW, LANES), fixed),
            pl.BlockSpec((LANES, SW_QKW), fixed),
            pl.BlockSpec((1, SW_QW), fixed),
            pl.BlockSpec((1, SW_KV_W), fixed),
        ],
        out_specs=[
            pl.BlockSpec((tm, SW_QW), row),
            pl.BlockSpec((tm, SW_KV_W), row),
            pl.BlockSpec((tm, SW_KV_W), row),
        ],
        out_shape=[
            jax.ShapeDtypeStruct((geo.nt, SW_QW), BF16),
            jax.ShapeDtypeStruct((geo.nt, SW_KV_W), F32),
            jax.ShapeDtypeStruct((geo.nt, SW_KV_W), F32),
        ],
        compiler_params=_cparams(("arbitrary",)),
        name="attn_proj_qknorm",
    )(*x_pair, g, mod_p, mod_s, mod_p, mod_s, w_bf16, e, et, qg, kg)


def _rel_bucket_np(dist):
    n = np.maximum(dist, 0)
    max_exact = REL_BUCKETS // 2
    ratio = np.log(np.maximum(n, 1).astype(np.float32) / np.float32(max_exact)) / np.float32(
        math.log(REL_MAX_DIST / max_exact))
    large = max_exact + (ratio * np.float32(REL_BUCKETS - max_exact)).astype(np.int32)
    large = np.minimum(large, REL_BUCKETS - 1)
    return np.where(n < max_exact, n, large).astype(np.int32)


def _bucket_table(qpos, kpos, k_ok):
    dist = qpos[:, None] - kpos[None, :]
    ok = (dist >= 0) & (dist <= WINDOW) & k_ok[None, :]
    return np.where(ok, _rel_bucket_np(dist), -1).astype(np.int32)


def _build_bias(bkt_ref, rb_ref, bias_scr, rows):
    bkt = bkt_ref[...]
    base = jnp.where(bkt < 0, -jnp.inf, 0.0)
    for h in range(SW_Q_HEADS):
        bias_scr[h * rows:(h + 1) * rows, :] = base

    def add_bucket(j, carry):
        hit = bkt == j
        for h in range(SW_Q_HEADS):
            sl = slice(h * rows, (h + 1) * rows)
            bias_scr[sl, :] = bias_scr[sl, :] + jnp.where(hit, rb_ref[j, h], 0.0)
        return carry

    lax.fori_loop(0, REL_BUCKETS, add_bucket, 0)


def _sink_softmax_pv(s, sink, vv_g):
    m = jnp.maximum(jnp.max(s, axis=-1, keepdims=True), sink)
    e = jnp.exp(s - m)
    p = e / (jnp.sum(e, axis=-1, keepdims=True) + jnp.exp(sink - m))
    return _dot(p.astype(BF16), vv_g)


def _build_bias_t(bkt_ref, rb_ref, bias_scr, blk):
    bkt = bkt_ref[...]
    base = jnp.where(bkt < 0, -jnp.inf, 0.0)
    slots = [(g, j) for g in range(SW_KV_HEADS) for j in range(SW_GROUP)]
    for g, j in slots:
        bias_scr[g, :, j * blk:(j + 1) * blk] = base

    def add_bucket(b, carry):
        hit = bkt == b
        for g, j in slots:
            cols = slice(j * blk, (j + 1) * blk)
            bias_scr[g, :, cols] = bias_scr[g, :, cols] + jnp.where(hit, rb_ref[b, g * SW_GROUP + j], 0.0)
        return carry

    lax.fori_loop(0, REL_BUCKETS, add_bucket, 0)


def _swa_prompt_kernel(rb_ref, sink_ref, bkt_ref, q_ref, kp_ref, kc_ref, vp_ref, vc_ref, o_ref, bias_scr, *, qb):
    n = pl.program_id(1)
    blk = ATT_BLOCK
    hd = SW_HEAD_DIM

    @pl.when((pl.program_id(0) == 0) & (n == 0))
    def _():
        _build_bias_t(bkt_ref, rb_ref, bias_scr, blk)

    key_i = lax.broadcasted_iota(I32, (SW_KEYS, SW_GROUP * blk), 0)
    hide_prev = (n == 0) & (key_i < blk)
    zeros_q = jnp.zeros((hd, SW_GROUP * blk), BF16)
    sinks = [jnp.concatenate([jnp.full((1, blk), sink_ref[g * SW_GROUP + j], F32) for j in range(SW_GROUP)], axis=1)
             for g in range(SW_KV_HEADS)]
    for i in range(qb):
        rows = slice(i * blk, (i + 1) * blk)
        if i == 0:
            k2 = jnp.concatenate([kp_ref[...], kc_ref[rows, :]], axis=0)
            v2 = jnp.concatenate([vp_ref[...], vc_ref[rows, :]], axis=0)
        else:
            k2 = kc_ref[(i - 1) * blk:(i + 1) * blk, :]
            v2 = vc_ref[(i - 1) * blk:(i + 1) * blk, :]
        kk = k2.astype(BF16)
        vv_t = v2.T.astype(BF16)
        q_t = q_ref[rows, :].astype(F32).T.astype(BF16)
        outs = []
        for g in range(SW_KV_HEADS):
            heads = [g * SW_GROUP + j for j in range(SW_GROUP)]
            q_cat = jnp.concatenate([q_t[hq * hd:(hq + 1) * hd, :] for hq in heads], axis=1)
            q_full = jnp.concatenate([q_cat if gg == g else zeros_q for gg in range(SW_KV_HEADS)], axis=0)
            s = _dot(kk, q_full) + bias_scr[g]
            if i == 0:
                s = jnp.where(hide_prev, -jnp.inf, s)
            m = jnp.maximum(jnp.max(s, axis=0, keepdims=True), sinks[g])
            e = jnp.exp(s - m)
            den = jnp.sum(e, axis=0, keepdims=True) + jnp.exp(sinks[g] - m)
            o_t = _dot(vv_t[g * hd:(g + 1) * hd, :], e.astype(BF16)) * (1.0 / den)
            outs.extend(o_t[:, j * blk:(j + 1) * blk] for j in range(SW_GROUP))
        o_ref[rows, :] = jnp.concatenate(outs, axis=0).T.astype(o_ref.dtype)


def _swa_prompt(geo, q, k, v, rel_bias, sinks):
    blk = ATT_BLOCK
    assert geo.seq % blk == 0 and WINDOW == blk
    n_blk = geo.seq // blk
    qb = 4 if n_blk % 4 == 0 else (2 if n_blk % 2 == 0 else 1)
    nb = n_blk // qb
    bkt = jnp.asarray(_bucket_table(blk + np.arange(blk), np.arange(2 * blk), np.ones(2 * blk, bool)).T.copy())
    cur = lambda b, n: (b * nb + n, 0)
    prev = lambda b, n: (b * n_blk + jnp.maximum(n * qb - 1, 0), 0)
    smem = pl.BlockSpec(memory_space=pltpu.SMEM)
    return pl.pallas_call(
        functools.partial(_swa_prompt_kernel, qb=qb),
        grid=(geo.batch, nb),
        in_specs=[
            smem, smem,
            pl.BlockSpec((SW_KEYS, blk), lambda b, n: (0, 0)),
            pl.BlockSpec((qb * blk, SW_QW), cur),
            pl.BlockSpec((blk, SW_KV_W), prev),
            pl.BlockSpec((qb * blk, SW_KV_W), cur),
            pl.BlockSpec((blk, SW_KV_W), prev),
            pl.BlockSpec((qb * blk, SW_KV_W), cur),
        ],
        out_specs=pl.BlockSpec((qb * blk, SW_QW), cur),
        out_shape=jax.ShapeDtypeStruct((geo.n_prompt, SW_QW), BF16),
        scratch_shapes=[pltpu.VMEM((SW_KV_HEADS, SW_KEYS, SW_GROUP * blk), F32)],
        compiler_params=_cparams(("arbitrary", "arbitrary")),
        name="swa_prompt",
    )(rel_bias, sinks, bkt, q, k, k, v, v)


def _swa_step_kernel(rb_ref, sink_ref, bkt_ref, q_ref, kn_ref, vn_ref, ck_ref, cv_ref,
                     o_ref, nk_ref, nv_ref, bias_scr, *, steps, bb):
    pad = SUBLANES
    win = WINDOW

    @pl.when(pl.program_id(0) == 0)
    def _():
        _build_bias(bkt_ref, rb_ref, bias_scr, pad)

    fill = jnp.zeros((SW_KEYS - win - pad, SW_KV_W), F32)
    row8 = lax.broadcasted_iota(I32, (pad, SW_KV_W), 0)

    def shifted(cache, new):
        rolled = pltpu.roll(cache, win - steps, 0)
        tail = jnp.where(row8 < pad - steps, rolled[win - pad:], pltpu.roll(new, pad - steps, 0))
        return rolled[:win - pad], tail

    def body(s, carry):
        ck, cv, kn, vn = ck_ref[s], cv_ref[s], kn_ref[s], vn_ref[s]
        kk = jnp.concatenate([ck, kn, fill], axis=0).astype(BF16)
        vv = jnp.concatenate([cv, vn, fill], axis=0).astype(BF16)
        q = q_ref[s]
        for g in range(SW_KV_HEADS):
            kcols = slice(g * SW_HEAD_DIM, (g + 1) * SW_HEAD_DIM)
            heads = [g * SW_GROUP + j for j in range(SW_GROUP)]
            qs = jnp.concatenate([q[:, hq * SW_HEAD_DIM:(hq + 1) * SW_HEAD_DIM] for hq in heads], axis=0)
            sc = _dot_nt(qs.astype(BF16), kk[:, kcols]) + bias_scr[heads[0] * pad:(heads[-1] + 1) * pad, :]
            sink = jnp.concatenate([jnp.full((pad, 1), sink_ref[hq], F32) for hq in heads], axis=0)
            og = _sink_softmax_pv(sc, sink, vv[:, kcols])
            for j, hq in enumerate(heads):
                o_ref[s, :, hq * SW_HEAD_DIM:(hq + 1) * SW_HEAD_DIM] = og[j * pad:(j + 1) * pad]
        head, tail = shifted(ck, kn)
        nk_ref[s, 0:win - pad] = head
        nk_ref[s, win - pad:win] = tail
        head, tail = shifted(cv, vn)
        nv_ref[s, 0:win - pad] = head
        nv_ref[s, win - pad:win] = tail
        return carry

    lax.fori_loop(0, bb, body, 0, unroll=True)


def _swa_step(geo, q, k, v, cache_k, cache_v, rel_bias, sinks):
    steps, db = geo.dec_seq, geo.dec_batch
    win = cache_k.shape[1]
    assert win == WINDOW and steps <= SUBLANES
    bb = min(8, db)
    pad = SUBLANES
    q_s = _seq_major(geo, q[geo.n_prompt:].astype(F32))
    k_s = _seq_major(geo, k[geo.n_prompt:])
    v_s = _seq_major(geo, v[geo.n_prompt:])
    kpos = np.arange(SW_KEYS)
    k_ok = kpos < win + steps
    qpos = win + np.arange(pad)
    bkt = _bucket_table(qpos, kpos, k_ok)
    bkt[steps:] = -1
    smem = pl.BlockSpec(memory_space=pltpu.SMEM)
    blk3 = lambda r, w: pl.BlockSpec((bb, r, w), lambda i: (i, 0, 0))
    o_s, nk, nv = pl.pallas_call(
        functools.partial(_swa_step_kernel, steps=steps, bb=bb),
        grid=(db // bb,),
        in_specs=[
            smem, smem,
            pl.BlockSpec((pad, SW_KEYS), lambda i: (0, 0)),
            blk3(pad, SW_QW), blk3(pad, SW_KV_W), blk3(pad, SW_KV_W),
            blk3(win, SW_KV_W), blk3(win, SW_KV_W),
        ],
        out_specs=[blk3(pad, SW_QW), blk3(win, SW_KV_W), blk3(win, SW_KV_W)],
        out_shape=[
            jax.ShapeDtypeStruct((db, pad, SW_QW), F32),
            jax.ShapeDtypeStruct((db, win, SW_KV_W), F32),
            jax.ShapeDtypeStruct((db, win, SW_KV_W), F32),
        ],
        scratch_shapes=[pltpu.VMEM((SW_Q_HEADS * pad, SW_KEYS), F32)],
        compiler_params=_cparams(("arbitrary",)),
        name="swa_step",
    )(rel_bias, sinks, jnp.asarray(bkt), q_s, k_s, v_s, cache_k, cache_v)
    return _time_major(geo, o_s), nk, nv


def kernel(x_prompt, x_sample, state_hgrn, cache_win_k, cache_win_v, c_prompt, c_sample, norm1_g, norm2_g, ada_w, ada_b, hg_w_in, hg_lb_table, hg_onorm_g, hg_w_out, sw_w_in, sw_qnorm_g, sw_knorm_g, sw_sinks, sw_w_out, rel_bias, router_w, router_b, moe_w_in, moe_b_in, moe_w_out, moe_b_out):
    batch, seq, d = x_prompt.shape
    db, steps, _ = x_sample.shape
    assert d == D_MODEL and ada_w.shape[0] == 2 and hg_w_in.shape[0] == 1 and sw_w_in.shape[0] == 1
    geo = _Geom(batch, seq, db, steps)
    x = (x_prompt.reshape(batch * seq, d), x_sample.transpose(1, 0, 2).reshape(steps * db, d))
    n_seq = batch + db
    rows = -(-n_seq // SUBLANES) * SUBLANES
    c_all = jnp.concatenate([c_prompt, c_sample, jnp.zeros((rows - n_seq, d), F32)], axis=0)
    mods = _ada_mods(c_all, ada_w, ada_b)

    def layer_mods(layer):
        mod_p = mods[layer, :batch].reshape(batch, 1, 6 * d)
        mod_s = jnp.tile(mods[layer, batch:n_seq], (steps, 1))
        return mod_p, mod_s

    def moe(layer, a_p, a_s, w_out, x_in, mod_p, mod_s):
        x1, h2, pos, pg, nch, soff = _post(geo, a_p, a_s, w_out.astype(BF16), x_in, norm2_g[layer:layer + 1],
                                           mod_p, mod_s, 0, router_w[layer], router_b[layer])
        return _moe(geo, h2, pos, pg, nch, soff, x1, mod_p, mod_s, 5,
                    layer, moe_w_in, moe_b_in, moe_w_out, moe_b_out)

    mod_p, mod_s = layer_mods(0)
    og = hg_onorm_g[0:1]
    z = _proj(geo, x, norm1_g[0:1], mod_p, mod_s, hg_w_in[0].astype(BF16), 0, 1)
    o_p, st_p = _gla_prompt(geo, z, hg_lb_table, og)
    o_s, st_s = _gla_step(geo, z, state_hgrn[0], hg_lb_table, og)
    x = moe(0, o_p, o_s, hg_w_out[0], x, mod_p, mod_s)

    mod_p, mod_s = layer_mods(1)
    q, k, v = _attn_proj(geo, x, norm1_g[1:2], mod_p, mod_s, sw_w_in[0].astype(BF16), sw_qnorm_g[0], sw_knorm_g[0])
    win = cache_win_k.shape[2]
    a_p = _swa_prompt(geo, q, k, v, rel_bias, sw_sinks[0])
    a_s, nk, nv = _swa_step(geo, q, k, v, cache_win_k[0].reshape(db, win, SW_KV_W),
                            cache_win_v[0].reshape(db, win, SW_KV_W), rel_bias, sw_sinks[0])
    x = moe(1, a_p, a_s, sw_w_out[0], x, mod_p, mod_s)

    y_prompt = x[0].reshape(batch, seq, d)
    y_sample = x[1].reshape(steps, db, d).transpose(1, 0, 2)
    kv_shape = (1, batch, WINDOW, SW_KV_HEADS, SW_HEAD_DIM)
    last_window = lambda a: jnp.stack([a[(b + 1) * seq - WINDOW:(b + 1) * seq] for b in range(batch)])
    k_p = last_window(k).reshape(kv_shape)
    v_p = last_window(v).reshape(kv_shape)
    cache_shape = (1, db, win, SW_KV_HEADS, SW_HEAD_DIM)
    return (y_prompt, y_sample, jnp.swapaxes(st_p, -1, -2)[None], st_s[None], k_p, v_p,
            nk.reshape(cache_shape), nv.reshape(cache_shape))
```

```python
import functools
import math

import numpy as np
import jax
import jax.numpy as jnp
from jax import lax
from jax.experimental import pallas as pl
from jax.experimental.pallas import tpu as pltpu

F32 = jnp.float32
BF16 = jnp.bfloat16
I32 = jnp.int32

D_MODEL = 1024
LANES = 128
SUBLANES = 8
D_TILES = D_MODEL // LANES
HG_DK = 128
HG_HEADS = D_MODEL // HG_DK
HG_DV = D_MODEL // HG_HEADS
HG_CHUNK = 64
SW_HEAD_DIM = 64
SW_Q_HEADS = D_MODEL // SW_HEAD_DIM
SW_KV_HEADS = 4
SW_GROUP = SW_Q_HEADS // SW_KV_HEADS
SW_KV_W = SW_KV_HEADS * SW_HEAD_DIM
WINDOW = 128
ATT_BLOCK = 128
SW_SCALE = SW_HEAD_DIM ** -0.5
REL_BUCKETS = 32
REL_MAX_DIST = 128
N_EXPERTS = 32
TOP_K = 4
SWIGLU_LIMIT = 7.0
SWIGLU_ALPHA = 1.702
NORM_EPS = 1e-5
MOE_BLOCK = 256
ROW_CHUNK = SUBLANES
COPY_CHUNKS = 4
CHUNK_UNROLL = 8
WAIT_CHUNKS = 16
VMEM_LIMIT = 56 * 1024 * 1024


def _cparams(sem):
    return pltpu.CompilerParams(dimension_semantics=sem, vmem_limit_bytes=VMEM_LIMIT)


def _sigmoid(x):
    return 0.5 * jnp.tanh(0.5 * x) + 0.5


def _silu(x):
    return x * _sigmoid(x)


def _dot(a, b):
    return jnp.dot(a, b, preferred_element_type=F32)


def _dot_nt(a, b):
    return lax.dot_general(a, b, (((1,), (1,)), ((), ())), preferred_element_type=F32)


def _dot_tn(a, b):
    return lax.dot_general(a, b, (((0,), (0,)), ((), ())), preferred_element_type=F32)


def _split3(x):
    hi = x.astype(BF16)
    r = x - hi.astype(F32)
    mid = r.astype(BF16)
    lo = (r - mid.astype(F32)).astype(BF16)
    return hi, mid, lo


def _ada_kernel(c_ref, w_ref, b_ref, o_ref):
    s = _silu(c_ref[...]).astype(BF16)
    o_ref[0] = _dot(s, w_ref[0].astype(BF16)) + b_ref[0]


def _ada_mods(c_all, ada_w, ada_b):
    depth, d, n6 = ada_w.shape
    rows = c_all.shape[0]
    tn = 1536
    return pl.pallas_call(
        _ada_kernel,
        grid=(depth, n6 // tn),
        in_specs=[
            pl.BlockSpec((rows, d), lambda l, j: (0, 0)),
            pl.BlockSpec((1, d, tn), lambda l, j: (l, 0, j)),
            pl.BlockSpec((1, 1, tn), lambda l, j: (l, 0, j)),
        ],
        out_specs=pl.BlockSpec((1, rows, tn), lambda l, j: (l, 0, j)),
        out_shape=jax.ShapeDtypeStruct((depth, rows, n6), F32),
        compiler_params=_cparams(("arbitrary", "arbitrary")),
        name="ada_mods",
    )(c_all, ada_w, ada_b.reshape(depth, 1, n6))


class _Geom:
    def __init__(self, batch, seq, dec_batch, dec_seq):
        self.batch, self.seq, self.dec_batch, self.dec_seq = batch, seq, dec_batch, dec_seq
        self.n_prompt = batch * seq
        self.n_sample = dec_batch * dec_seq
        self.nt = self.n_prompt + self.n_sample
        tm = 512
        while seq % tm or self.n_sample % tm:
            tm //= 2
        assert tm >= 8
        self.tm = tm
        self.n_pt = self.n_prompt // tm
        self.n_tiles = self.nt // tm
        self.tm_moe = min(256, tm)


def _mod_specs(geo, col, tm):
    seq, batch = geo.seq, geo.batch
    n_pt = geo.n_prompt // tm

    def p_map(i, *_):
        return (jnp.minimum(i * tm // seq, batch - 1), 0, col)

    def s_map(i, *_):
        return (jnp.maximum(i - n_pt, 0), col)

    return [pl.BlockSpec((1, 1, D_MODEL), p_map), pl.BlockSpec((tm, D_MODEL), s_map)]


def _pick_mod(i, n_pt, p_ref, s_ref):
    return jnp.where(i >= n_pt, s_ref[...], p_ref[0])


def _norm_mod(x, g, sc, sh):
    ms = jnp.mean(x * x, axis=-1, keepdims=True)
    return x * lax.rsqrt(ms + NORM_EPS) * g * (1.0 + sc) + sh


def _pair_specs(geo, tm, width):
    n_pt = geo.n_prompt // tm
    return [pl.BlockSpec((tm, width), lambda i, *_: (jnp.minimum(i, n_pt - 1), 0)),
            pl.BlockSpec((tm, width), lambda i, *_: (jnp.maximum(i - n_pt, 0), 0))]


def _proj_kernel(xp_ref, xs_ref, g_ref, shp_ref, shs_ref, scp_ref, scs_ref, w_ref, o_ref, *, n_pt, tn):
    i = pl.program_id(0)
    sh = _pick_mod(i, n_pt, shp_ref, shs_ref)
    sc = _pick_mod(i, n_pt, scp_ref, scs_ref)
    x = jnp.where(i >= n_pt, xs_ref[...], xp_ref[...])
    h = _norm_mod(x, g_ref[...], sc, sh).astype(BF16)
    for j in range(o_ref.shape[1] // tn):
        o_ref[:, j * tn:(j + 1) * tn] = _dot(h, w_ref[:, j * tn:(j + 1) * tn])


def _proj(geo, x_pair, g, mod_p, mod_s, w_bf16, col_shift, col_scale):
    n_out = w_bf16.shape[1]
    tm = geo.tm
    fixed = lambda i: (0, 0)
    return pl.pallas_call(
        functools.partial(_proj_kernel, n_pt=geo.n_pt, tn=1024),
        grid=(geo.n_tiles,),
        in_specs=[
            *_pair_specs(geo, tm, D_MODEL),
            pl.BlockSpec((1, D_MODEL), fixed),
            *_mod_specs(geo, col_shift, tm),
            *_mod_specs(geo, col_scale, tm),
            pl.BlockSpec((D_MODEL, n_out), fixed),
        ],
        out_specs=pl.BlockSpec((tm, n_out), lambda i: (i, 0)),
        out_shape=jax.ShapeDtypeStruct((geo.nt, n_out), F32),
        compiler_params=_cparams(("arbitrary",)),
        name="norm_mod_proj",
    )(*x_pair, g, mod_p, mod_s, mod_p, mod_s, w_bf16)


def _hg_lower_bound(lbt_ref):
    t = lbt_ref[...]
    e = jnp.exp(t - jnp.max(t, axis=0, keepdims=True))
    return e[0:1] / jnp.sum(e, axis=0, keepdims=True)


def _hg_gates(fz, lb):
    t = jnp.tanh(0.5 * fz)
    logf = jnp.log(lb + (1.0 - lb) * (0.5 + 0.5 * t))
    return logf, (1.0 - lb) * (0.5 - 0.5 * t)


def _hg_out(o, gz, og):
    ms = jnp.mean(o * o, axis=-1, keepdims=True)
    return o * lax.rsqrt(ms + NORM_EPS) * og * _silu(gz)


def _gla_prompt_kernel(*refs, batch, chunk, n_chunks):
    z_refs = refs[:batch]
    lbt_ref, og_ref, o_ref, sfin_ref, st_scr = refs[batch:]
    t_step = pl.program_id(0)
    kw = HG_HEADS * HG_DK

    @pl.when(t_step == 0)
    def _():
        st_scr[...] = jnp.zeros_like(st_scr)

    lb = _hg_lower_bound(lbt_ref)
    og = og_ref[...]
    r_i = lax.broadcasted_iota(I32, (chunk, chunk), 0)
    c_i = lax.broadcasted_iota(I32, (chunk, chunk), 1)
    causal = c_i <= r_i
    tri = causal.astype(BF16)
    mid = chunk // 2 - 1

    def body(c, carry):
        rows = pl.ds(pl.multiple_of(c * chunk, chunk), chunk)
        for bi, z_ref in enumerate(z_refs):
            logf, kk = _hg_gates(z_ref[rows, kw:2 * kw], lb)
            hi, md, lo = _split3(logf)
            cs = _dot(tri, jnp.concatenate([hi, md, lo], axis=1))
            b = cs[:, :kw] + cs[:, kw:2 * kw] + cs[:, 2 * kw:]
            b_mid = b[mid:mid + 1]
            b_last = b[chunk - 1:chunk]
            q_hat = _silu(z_ref[rows, 0:kw]) * jnp.exp(b - b_mid)
            k_hat = kk * jnp.exp(b_mid - b)
            q_in = (q_hat * jnp.exp(b_mid)).astype(BF16)
            k_dec = (k_hat * jnp.exp(b_last - b_mid)).astype(BF16)
            q_hat = q_hat.astype(BF16)
            k_hat = k_hat.astype(BF16)
            dec = jnp.exp(b_last)
            for h in range(HG_HEADS):
                cols = slice(h * HG_DK, (h + 1) * HG_DK)
                vcols = slice(2 * kw + h * HG_DV, 2 * kw + (h + 1) * HG_DV)
                gcols = slice(2 * kw + HG_HEADS * HG_DV + h * HG_DV, 2 * kw + HG_HEADS * HG_DV + (h + 1) * HG_DV)
                v = z_ref[rows, vcols].astype(BF16)
                att = jnp.where(causal, _dot_nt(q_hat[:, cols], k_hat[:, cols]), 0.0).astype(BF16)
                st = st_scr[bi, h]
                o = _dot(att, v) + _dot_nt(q_in[:, cols], st.astype(BF16))
                st_scr[bi, h] = st * dec[:, cols] + _dot_tn(v, k_dec[:, cols])
                o_ref[bi, rows, h * HG_DV:(h + 1) * HG_DV] = _hg_out(o, z_ref[rows, gcols], og).astype(o_ref.dtype)
        return carry

    lax.fori_loop(0, n_chunks, body, 0)

    @pl.when(t_step == pl.num_programs(0) - 1)
    def _():
        sfin_ref[...] = st_scr[...]


def _gla_prompt(geo, z, lb_table, o_gain):
    tg = min(256, geo.seq)
    chunk = HG_CHUNK if geo.seq % HG_CHUNK == 0 else geo.seq
    assert tg % chunk == 0 and geo.seq % tg == 0
    nt = geo.seq // tg
    batch = geo.batch
    fixed = lambda t: (0, 0)
    state_shape = (batch, HG_HEADS, HG_DV, HG_DK)
    o, st = pl.pallas_call(
        functools.partial(_gla_prompt_kernel, batch=batch, chunk=chunk, n_chunks=tg // chunk),
        grid=(nt,),
        in_specs=[
            *[pl.BlockSpec((tg, 4 * D_MODEL), functools.partial(lambda t, b: (b * nt + t, 0), b=b))
              for b in range(batch)],
            pl.BlockSpec(lb_table.shape, fixed),
            pl.BlockSpec((1, HG_DV), fixed),
        ],
        out_specs=[
            pl.BlockSpec((batch, tg, D_MODEL), lambda t: (0, t, 0)),
            pl.BlockSpec(state_shape, lambda t: (0, 0, 0, 0)),
        ],
        out_shape=[
            jax.ShapeDtypeStruct((batch, geo.seq, D_MODEL), BF16),
            jax.ShapeDtypeStruct(state_shape, F32),
        ],
        scratch_shapes=[pltpu.VMEM(state_shape, F32)],
        compiler_params=_cparams(("arbitrary",)),
        name="gla_prompt",
    )(*([z] * batch), lb_table, o_gain)
    return o.reshape(geo.n_prompt, D_MODEL), st


def _gla_step_kernel(z_ref, s_ref, lbt_ref, og_ref, o_ref, snew_ref, *, steps, bb):
    kw = HG_HEADS * HG_DK
    pad = SUBLANES
    lb = _hg_lower_bound(lbt_ref)
    og = og_ref[...]
    r_i = lax.broadcasted_iota(I32, (pad, pad), 0)
    c_i = lax.broadcasted_iota(I32, (pad, pad), 1)
    causal = c_i <= r_i
    tri = causal.astype(BF16)
    row_w = lax.broadcasted_iota(I32, (pad, kw), 0)
    live = row_w < steps
    row_k = lax.broadcasted_iota(I32, (pad, HG_DK), 0)
    ones_sel = jnp.where((row_k == steps) | (row_k == steps + 1), 1.0, 0.0).astype(BF16)

    def body(s, carry):
        z = z_ref[s]
        logf, kk = _hg_gates(z[:, kw:2 * kw], lb)
        hi, md, lo = _split3(jnp.where(live, logf, 0.0))
        cs = _dot(tri, jnp.concatenate([hi, md, lo], axis=1))
        b = cs[:, :kw] + cs[:, kw:2 * kw] + cs[:, 2 * kw:]
        b_last = b[steps - 1:steps]
        q_in = (_silu(z[:, 0:kw]) * jnp.exp(b)).astype(BF16)
        k_hat = jnp.where(live, kk * jnp.exp(-b), 0.0).astype(BF16)
        k_dec = jnp.where(live, kk * jnp.exp(b_last - b), 0.0).astype(BF16)
        dec = jnp.exp(b_last)
        d_hi = dec.astype(BF16)
        d_lo = (dec - d_hi.astype(F32)).astype(BF16)
        a_all = jnp.where(row_w == steps, d_hi, jnp.where(row_w == steps + 1, d_lo, k_dec))
        for h in range(HG_HEADS):
            cols = slice(h * HG_DK, (h + 1) * HG_DK)
            vcols = slice(2 * kw + h * HG_DV, 2 * kw + (h + 1) * HG_DV)
            gcols = slice(2 * kw + HG_HEADS * HG_DV + h * HG_DV, 2 * kw + HG_HEADS * HG_DV + (h + 1) * HG_DV)
            v = z[:, vcols].astype(BF16)
            s0 = s_ref[s, h]
            att = jnp.where(causal, _dot_nt(q_in[:, cols], k_hat[:, cols]), 0.0).astype(BF16)
            o = _dot(att, v) + _dot(q_in[:, cols], s0.astype(BF16))
            upd = _dot_tn(a_all[:, cols], jnp.concatenate([v, ones_sel], axis=1))
            snew_ref[s, h] = upd[:, HG_DV:] * s0 + upd[:, :HG_DV]
            o_ref[s, :, h * HG_DV:(h + 1) * HG_DV] = _hg_out(o, z[:, gcols], og)
        return carry

    lax.fori_loop(0, bb, body, 0, unroll=True)


def _seq_major(geo, rows):
    steps, db = geo.dec_seq, geo.dec_batch
    w = rows.shape[-1]
    r = rows.reshape(steps, db, w).transpose(1, 0, 2)
    return jnp.concatenate([r, jnp.zeros((db, SUBLANES - steps, w), rows.dtype)], axis=1)


def _time_major(geo, r):
    steps, db = geo.dec_seq, geo.dec_batch
    return r[:, :steps].transpose(1, 0, 2).reshape(steps * db, r.shape[-1])


def _gla_step(geo, z, state, lb_table, o_gain):
    steps, db = geo.dec_seq, geo.dec_batch
    assert steps + 2 <= SUBLANES
    bb = min(8, db)
    z_s = _seq_major(geo, z[geo.n_prompt:])
    o_s, s_new = pl.pallas_call(
        functools.partial(_gla_step_kernel, steps=steps, bb=bb),
        grid=(db // bb,),
        in_specs=[
            pl.BlockSpec((bb, SUBLANES, 4 * D_MODEL), lambda i: (i, 0, 0)),
            pl.BlockSpec((bb, HG_HEADS, HG_DK, HG_DV), lambda i: (i, 0, 0, 0)),
            pl.BlockSpec(lb_table.shape, lambda i: (0, 0)),
            pl.BlockSpec((1, HG_DV), lambda i: (0, 0)),
        ],
        out_specs=[
            pl.BlockSpec((bb, SUBLANES, D_MODEL), lambda i: (i, 0, 0)),
            pl.BlockSpec((bb, HG_HEADS, HG_DK, HG_DV), lambda i: (i, 0, 0, 0)),
        ],
        out_shape=[
            jax.ShapeDtypeStruct((db, SUBLANES, D_MODEL), F32),
            jax.ShapeDtypeStruct(state.shape, F32),
        ],
        compiler_params=_cparams(("arbitrary",)),
        name="gla_step",
    )(z_s, state, lb_table, o_gain)
    return _time_major(geo, o_s), s_new


def _post_kernel(ap_ref, as_ref, wo_ref, xp_ref, xs_ref, gp_ref, gs_ref, n2_ref, shp_ref, shs_ref, scp_ref,
                 scs_ref, rw_ref, rb_ref, x1_ref, h2_ref, pos_ref, pg_ref, nch_ref, soff_ref, *, n_pt, tm):
    i = pl.program_id(0)
    g1 = _pick_mod(i, n_pt, gp_ref, gs_ref)
    a = jnp.where(i >= n_pt, as_ref[...], ap_ref[...])
    x = jnp.where(i >= n_pt, xs_ref[...], xp_ref[...])
    x1 = x + g1 * _dot(a, wo_ref[...])
    x1_ref[...] = x1
    sh = _pick_mod(i, n_pt, shp_ref, shs_ref)
    sc = _pick_mod(i, n_pt, scp_ref, scs_ref)
    h2 = _norm_mod(x1, n2_ref[...], sc, sh)
    h2_ref[...] = h2.astype(h2_ref.dtype)

    row_e = lax.broadcasted_iota(I32, (LANES, tm), 0)
    logits = _dot_nt(rw_ref[...], h2.astype(BF16)) + rb_ref[...]
    work = jnp.where(row_e < N_EXPERTS, logits, -jnp.inf)
    vals, hits = [], []
    for _ in range(TOP_K):
        m = jnp.max(work, axis=0, keepdims=True)
        idx = jnp.min(jnp.where(work == m, row_e, LANES), axis=0, keepdims=True)
        hit = row_e == idx
        vals.append(m)
        hits.append(hit)
        work = jnp.where(hit, -jnp.inf, work)
    exps = [jnp.exp(v - vals[0]) for v in vals]
    den = exps[0]
    for e in exps[1:]:
        den = den + e
    any_hit = hits[0]
    for hmask in hits[1:]:
        any_hit = any_hit | hmask
    any_f = jnp.where(any_hit, 1.0, 0.0)
    t_i = lax.broadcasted_iota(I32, (tm, tm), 0)
    t_j = lax.broadcasted_iota(I32, (tm, tm), 1)
    rank = _dot(any_f.astype(BF16), (t_i < t_j).astype(BF16))
    n_chunk = jnp.floor((jnp.sum(any_f, axis=1, keepdims=True) + (ROW_CHUNK - 1)) * (1.0 / ROW_CHUNK))
    e_i = lax.broadcasted_iota(I32, (LANES, LANES), 0)
    e_j = lax.broadcasted_iota(I32, (LANES, LANES), 1)
    seg = _dot((e_j < e_i).astype(BF16), jnp.broadcast_to(n_chunk, (LANES, LANES)).astype(BF16))[:, 0:1]
    where_to = seg * float(ROW_CHUNK) + rank
    pos_rows = [jnp.sum(jnp.where(hits[k], where_to, 0.0), axis=0, keepdims=True) for k in range(TOP_K)]
    gate_rows = [exps[k] / den for k in range(TOP_K)]
    pos_ref[0] = jnp.concatenate(pos_rows + [jnp.zeros((SUBLANES - TOP_K, tm), F32)], axis=0).astype(I32)
    rows = jnp.concatenate(pos_rows + gate_rows + [jnp.zeros((LANES - 2 * TOP_K, tm), F32)], axis=0)
    pg_ref[...] = rows.T
    nch_ref[0] = n_chunk.astype(I32)
    soff_ref[0] = seg.astype(I32)


def _post(geo, a_p, a_s, w_out_bf16, x_pair, norm_g, mod_p, mod_s, col0, router_w, router_b):
    tm = geo.tm_moe
    n_pt = geo.n_prompt // tm
    n_tiles = geo.nt // tm
    rw = jnp.zeros((LANES, D_MODEL), BF16).at[:N_EXPERTS].set(router_w.T.astype(BF16))
    rb = jnp.zeros((LANES, 1), F32).at[:N_EXPERTS, 0].set(router_b)
    row = lambda i: (i, 0)
    fixed = lambda i: (0, 0)
    tile_row = lambda i: (i, 0, 0)
    return pl.pallas_call(
        functools.partial(_post_kernel, n_pt=n_pt, tm=tm),
        grid=(n_tiles,),
        in_specs=[
            *_pair_specs(geo, tm, D_MODEL),
            pl.BlockSpec((D_MODEL, D_MODEL), fixed),
            *_pair_specs(geo, tm, D_MODEL),
            *_mod_specs(geo, col0 + 2, tm),
            pl.BlockSpec((1, D_MODEL), fixed),
            *_mod_specs(geo, col0 + 3, tm),
            *_mod_specs(geo, col0 + 4, tm),
            pl.BlockSpec((LANES, D_MODEL), fixed),
            pl.BlockSpec((LANES, 1), fixed),
        ],
        out_specs=[
            pl.BlockSpec((tm, D_MODEL), row),
            pl.BlockSpec((tm, D_MODEL), row),
            pl.BlockSpec((1, SUBLANES, tm), tile_row),
            pl.BlockSpec((tm, LANES), row),
            pl.BlockSpec((1, LANES, 1), tile_row),
            pl.BlockSpec((1, LANES, 1), tile_row),
        ],
        out_shape=[
            jax.ShapeDtypeStruct((geo.nt, D_MODEL), F32),
            jax.ShapeDtypeStruct((geo.nt, D_MODEL), BF16),
            jax.ShapeDtypeStruct((n_tiles, SUBLANES, tm), I32),
            jax.ShapeDtypeStruct((geo.nt, LANES), F32),
            jax.ShapeDtypeStruct((n_tiles, LANES, 1), I32),
            jax.ShapeDtypeStruct((n_tiles, LANES, 1), I32),
        ],
        compiler_params=_cparams(("arbitrary",)),
        name="post_mixer_router",
    )(a_p, a_s.astype(BF16), w_out_bf16, *x_pair, mod_p, mod_s, norm_g, mod_p, mod_s, mod_p, mod_s, rw, rb)


def _ffn_kernel(first_ref, count_ref, xs_hbm, win_ref, bin_ref, wout_ref, bout_ref, ys_hbm,
                xbuf, ybuf, win_scr, wout_scr, xsem, ysem, pend_ref, *, half):
    e = pl.program_id(0)
    first = first_ref[e]
    n_sub = count_ref[e]
    pairs = n_sub // 2
    odd = n_sub - 2 * pairs
    win_scr[...] = win_ref[0].astype(BF16)
    wout_scr[...] = wout_ref[0].astype(BF16)

    def rows_at(sub, n_rows):
        return pl.ds(pl.multiple_of((first + sub) * half, half), n_rows)

    def x_copy(sub, n_rows, slot):
        return pltpu.make_async_copy(xs_hbm.at[rows_at(sub, n_rows)], xbuf.at[slot, pl.ds(0, n_rows)], xsem.at[slot])

    def y_copy(sub, n_rows, slot):
        return pltpu.make_async_copy(ybuf.at[slot, pl.ds(0, n_rows)], ys_hbm.at[rows_at(sub, n_rows)], ysem.at[slot])

    def ffn(slot, rows):
        gu = _dot(xbuf[slot, rows, :].astype(BF16), win_scr[...]) + bin_ref[0]
        gate = jnp.minimum(gu[:, :D_MODEL], SWIGLU_LIMIT)
        up = jnp.clip(gu[:, D_MODEL:], -SWIGLU_LIMIT, SWIGLU_LIMIT)
        act = gate * _sigmoid(SWIGLU_ALPHA * gate) * (up + 1.0)
        return _dot(act.astype(BF16), wout_scr[...]) + bout_ref[0]

    def start_first(expert):
        begin = pl.multiple_of(first_ref[expert] * half, half)
        n = count_ref[expert]

        @pl.when(n >= 2)
        def _():
            pltpu.make_async_copy(xs_hbm.at[pl.ds(begin, 2 * half)], xbuf.at[0], xsem.at[0]).start()

        @pl.when(n == 1)
        def _():
            pltpu.make_async_copy(xs_hbm.at[pl.ds(begin, half)], xbuf.at[0, pl.ds(0, half)], xsem.at[0]).start()

    def settle(slot):
        outstanding = pend_ref[slot]

        @pl.when(outstanding == 2)
        def _():
            y_copy(0, 2 * half, slot).wait()

        @pl.when(outstanding == 1)
        def _():
            y_copy(0, half, slot).wait()

        pend_ref[slot] = 0

    @pl.when(e == 0)
    def _():
        pend_ref[0] = 0
        pend_ref[1] = 0
        start_first(0)

    def unit(u, carry):
        slot = u % 2
        x_copy(2 * u, 2 * half, slot).wait()

        @pl.when(u + 1 < pairs)
        def _():
            x_copy(2 * (u + 1), 2 * half, 1 - slot).start()

        @pl.when((u + 1 == pairs) & (odd == 1))
        def _():
            x_copy(2 * pairs, half, 1 - slot).start()

        settle(slot)
        ybuf[slot] = ffn(slot, slice(0, 2 * half))
        y_copy(2 * u, 2 * half, slot).start()
        pend_ref[slot] = 2
        return carry

    lax.fori_loop(0, pairs, unit, 0)
    tail_slot = pairs % 2

    @pl.when(odd == 1)
    def _():
        x_copy(2 * pairs, half, tail_slot).wait()
        settle(tail_slot)
        ybuf[tail_slot, 0:half, :] = ffn(tail_slot, slice(0, half))
        y_copy(2 * pairs, half, tail_slot).start()
        pend_ref[tail_slot] = 1

    @pl.when(e + 1 < pl.num_programs(0))
    def _():
        start_first(e + 1)

    @pl.when(e + 1 == pl.num_programs(0))
    def _():
        settle(0)
        settle(1)


def _ffn(xs, first_sub, n_sub, layer, w_in, b_in, w_out, b_out):
    n_e, d, d2 = w_in.shape
    half = MOE_BLOCK
    w_map = lambda e, *_: (layer * N_EXPERTS + e, 0, 0)
    return pl.pallas_call(
        functools.partial(_ffn_kernel, half=half),
        grid_spec=pltpu.PrefetchScalarGridSpec(
            num_scalar_prefetch=2,
            grid=(N_EXPERTS,),
            in_specs=[
                pl.BlockSpec(memory_space=pl.ANY),
                pl.BlockSpec((1, d, d2), w_map),
                pl.BlockSpec((1, 1, d2), w_map),
                pl.BlockSpec((1, d2 // 2, d), w_map),
                pl.BlockSpec((1, 1, d), w_map),
            ],
            out_specs=pl.BlockSpec(memory_space=pl.ANY),
            scratch_shapes=[
                pltpu.VMEM((2, 2 * half, d), F32),
                pltpu.VMEM((2, 2 * half, d), F32),
                pltpu.VMEM((d, d2), BF16),
                pltpu.VMEM((d2 // 2, d), BF16),
                pltpu.SemaphoreType.DMA((2,)),
                pltpu.SemaphoreType.DMA((2,)),
                pltpu.SMEM((2,), I32),
            ],
        ),
        out_shape=jax.ShapeDtypeStruct(xs.shape, xs.dtype),
        input_output_aliases={2: 0},
        compiler_params=_cparams(("arbitrary",)),
        name="moe_ffn",
    )(first_sub, n_sub, xs, w_in, b_in.reshape(n_e, 1, d2), w_out, b_out.reshape(n_e, 1, d))


def _chunk_rows(chunk, n_chunks=1):
    return pl.ds(pl.multiple_of(chunk * ROW_CHUNK, ROW_CHUNK), n_chunks * ROW_CHUNK)


def _run_pieces(n, fn, big):
    n_big = n // big

    def big_piece(i, carry):
        fn(i * big, big)
        return carry

    def small_piece(i, carry):
        fn(n_big * big + i, 1)
        return carry

    lax.fori_loop(0, n_big, big_piece, 0)
    lax.fori_loop(0, n - n_big * big, small_piece, 0)


def _for_each_chunk(where_ref, tot_ref, tile, fn):
    n = tot_ref[tile]
    n_groups = n // CHUNK_UNROLL

    def group(i, carry):
        for u in range(CHUNK_UNROLL):
            j = i * CHUNK_UNROLL + u
            fn(j, where_ref[tile, j])
        return carry

    def single(i, carry):
        j = n_groups * CHUNK_UNROLL + i
        fn(j, where_ref[tile, j])
        return carry

    lax.fori_loop(0, n_groups, group, 0)
    lax.fori_loop(0, n - n_groups * CHUNK_UNROLL, single, 0)


def _dispatch_kernel(where_ref, tot_ref, fill0_ref, filln_ref,
                     pos_ref, h_ref, xs_hbm, stage, zeros, sem, fill_sem, *, tm, k_stage):
    t = pl.program_id(0)
    n_t = pl.num_programs(0)
    slot = t % 2

    def copy(s_chunk, d_chunk, size, sl):
        return pltpu.make_async_copy(stage.at[sl, _chunk_rows(s_chunk, size)],
                                     xs_hbm.at[_chunk_rows(d_chunk, size)], sem.at[sl])

    def wait_tile(tile, sl):
        _run_pieces(tot_ref[tile], lambda off, size: copy(0, 0, size, sl).wait(), WAIT_CHUNKS)

    @pl.when(t >= 2)
    def _():
        wait_tile(t - 2, slot)

    pos_t = pos_ref[0]
    j_i = lax.broadcasted_iota(I32, (k_stage, tm), 0)
    hit = j_i == pos_t[0:1]
    for k in range(1, TOP_K):
        hit = hit | (j_i == pos_t[k:k + 1])
    stage[slot] = _dot(jnp.where(hit, 1.0, 0.0).astype(BF16), h_ref[...])
    _for_each_chunk(where_ref, tot_ref, t, lambda s, d: copy(s, d, 1, slot).start())

    @pl.when(t == n_t - 1)
    def _():
        zeros[...] = jnp.zeros_like(zeros)

        def fill(d_chunk, size):
            return pltpu.make_async_copy(zeros.at[_chunk_rows(0, size)], xs_hbm.at[_chunk_rows(d_chunk, size)],
                                         fill_sem.at[0])

        def per_region(r, carry):
            _run_pieces(filln_ref[r], lambda off, size: fill(fill0_ref[r] + off, size).start(), COPY_CHUNKS)
            return carry

        lax.fori_loop(0, N_EXPERTS + 1, per_region, 0)

        @pl.when(t >= 1)
        def _():
            wait_tile(t - 1, 1 - slot)

        wait_tile(t, slot)

        def per_region_wait(r, carry):
            _run_pieces(filln_ref[r], lambda off, size: fill(0, size).wait(), COPY_CHUNKS)
            return carry

        lax.fori_loop(0, N_EXPERTS + 1, per_region_wait, 0)


def _combine_kernel(where_ref, tot_ref, pg_ref, x_ref, gp_ref, gs_ref, y_hbm,
                    op_ref, os_ref, stage, sem, *, n_pt, tm, k_stage):
    t = pl.program_id(0)
    n_t = pl.num_programs(0)
    slot = t % 2

    def copy(s_chunk, d_chunk, size, sl):
        return pltpu.make_async_copy(y_hbm.at[_chunk_rows(d_chunk, size)],
                                     stage.at[sl, _chunk_rows(s_chunk, size)], sem.at[sl])

    def fetch(tile, sl):
        _for_each_chunk(where_ref, tot_ref, tile, lambda s, d: copy(s, d, 1, sl).start())

    @pl.when(t == 0)
    def _():
        stage[...] = jnp.zeros_like(stage)
        fetch(0, 0)

    @pl.when(t + 1 < n_t)
    def _():
        fetch(t + 1, 1 - slot)

    _run_pieces(tot_ref[t], lambda off, size: copy(0, 0, size, slot).wait(), WAIT_CHUNKS)
    lane = lax.broadcasted_iota(I32, (tm, k_stage), 1)
    pg = pg_ref[...]
    p = jnp.zeros((tm, k_stage), F32)
    for k in range(TOP_K):
        p = jnp.where(lane == pg[:, k:k + 1].astype(I32), pg[:, TOP_K + k:TOP_K + k + 1], p)
    ffn = _dot(p.astype(BF16), stage[slot].astype(BF16))
    out = x_ref[...] + _pick_mod(t, n_pt, gp_ref, gs_ref) * ffn

    @pl.when(t < n_pt)
    def _():
        op_ref[...] = out

    @pl.when(t >= n_pt)
    def _():
        os_ref[...] = out


def _moe(geo, h2, pos, pg, nch_pad, soff_pad, x1, mod_p, mod_s, col_gate, layer, w_in, b_in, w_out, b_out):
    bm = MOE_BLOCK
    tm = geo.tm_moe
    n_tiles = geo.nt // tm
    depth, n_e, d, d2 = w_in.shape
    w_in = w_in.reshape(depth * n_e, d, d2)
    b_in = b_in.reshape(depth * n_e, d2)
    w_out = w_out.reshape(depth * n_e, d2 // 2, d)
    b_out = b_out.reshape(depth * n_e, d)
    chunks_per_block = bm // ROW_CHUNK
    max_rows = geo.nt * TOP_K + n_tiles * N_EXPERTS * (ROW_CHUNK - 1) + N_EXPERTS * (bm - 1)
    n_blocks = -(-max_rows // bm)
    nch = nch_pad[:, :N_EXPERTS, 0]
    soff = soff_pad[:, :N_EXPERTS, 0]
    tot = jnp.sum(nch, axis=1)
    per_e = jnp.sum(nch, axis=0)
    padded = (per_e + chunks_per_block - 1) // chunks_per_block * chunks_per_block
    pend = jnp.cumsum(padded)
    pstart = pend - padded
    dst = pstart[None, :] + jnp.cumsum(nch, axis=0) - nch
    first_sub = (pstart // chunks_per_block).astype(I32)
    n_sub = (padded // chunks_per_block).astype(I32)
    fill0 = jnp.concatenate([pstart + per_e, pend[-1:]]).astype(I32)
    filln = jnp.concatenate([padded - per_e, n_blocks * chunks_per_block - pend[-1:]]).astype(I32)

    k_stage = -(-(tm * TOP_K + N_EXPERTS * (ROW_CHUNK - 1)) // LANES) * LANES
    j = jnp.arange(k_stage // ROW_CHUNK, dtype=I32)
    owner = jnp.minimum(jnp.sum(j[None, :, None] >= (soff + nch)[:, None, :], axis=-1), N_EXPERTS - 1)
    shift = jnp.sum(jnp.where(owner[:, :, None] == jnp.arange(N_EXPERTS, dtype=I32), (dst - soff)[:, None, :], 0),
                    axis=-1)
    where = (j[None, :] + shift).astype(I32)
    row = lambda i, *_: (i, 0)
    xs = pl.pallas_call(
        functools.partial(_dispatch_kernel, tm=tm, k_stage=k_stage),
        grid_spec=pltpu.PrefetchScalarGridSpec(
            num_scalar_prefetch=4,
            grid=(n_tiles,),
            in_specs=[pl.BlockSpec((1, SUBLANES, tm), lambda i, *_: (i, 0, 0)), pl.BlockSpec((tm, d), row)],
            out_specs=pl.BlockSpec(memory_space=pl.ANY),
            scratch_shapes=[
                pltpu.VMEM((2, k_stage, d), F32),
                pltpu.VMEM((COPY_CHUNKS * ROW_CHUNK, d), F32),
                pltpu.SemaphoreType.DMA((2,)),
                pltpu.SemaphoreType.DMA((1,)),
            ],
        ),
        out_shape=jax.ShapeDtypeStruct((n_blocks * bm, d), F32),
        compiler_params=_cparams(("arbitrary",)),
        name="moe_dispatch",
    )(where, tot, fill0, filln, pos, h2)

    ys = _ffn(xs, first_sub, n_sub, layer, w_in, b_in, w_out, b_out)

    return pl.pallas_call(
        functools.partial(_combine_kernel, n_pt=geo.n_prompt // tm, tm=tm, k_stage=k_stage),
        grid_spec=pltpu.PrefetchScalarGridSpec(
            num_scalar_prefetch=2,
            grid=(n_tiles,),
            in_specs=[
                pl.BlockSpec((tm, LANES), row),
                pl.BlockSpec((tm, d), row),
                *_mod_specs(geo, col_gate, tm),
                pl.BlockSpec(memory_space=pl.ANY),
            ],
            out_specs=_pair_specs(geo, tm, d),
            scratch_shapes=[
                pltpu.VMEM((2, k_stage, d), F32),
                pltpu.SemaphoreType.DMA((2,)),
            ],
        ),
        out_shape=[jax.ShapeDtypeStruct((geo.n_prompt, d), F32), jax.ShapeDtypeStruct((geo.n_sample, d), F32)],
        compiler_params=_cparams(("arbitrary",)),
        name="moe_combine",
    )(where, tot, pg, x1, mod_p, mod_s, ys)


SW_QW = SW_Q_HEADS * SW_HEAD_DIM
SW_QKW = SW_QW + SW_KV_W
SW_KEYS = 2 * ATT_BLOCK


def _attn_proj_kernel(xp_ref, xs_ref, g_ref, shp_ref, shs_ref, scp_ref, scs_ref, w_ref, e_ref, et_ref, qg_ref,
                      kg_ref, q_ref, k_ref, v_ref, *, n_pt):
    i = pl.program_id(0)
    sh = _pick_mod(i, n_pt, shp_ref, shs_ref)
    sc = _pick_mod(i, n_pt, scp_ref, scs_ref)
    x = jnp.where(i >= n_pt, xs_ref[...], xp_ref[...])
    h = _norm_mod(x, g_ref[...], sc, sh).astype(BF16)
    tm = h.shape[0]
    n_parts = 2 if tm % (2 * SUBLANES) == 0 else 1
    part = tm // n_parts
    for r in range(n_parts):
        rows = slice(r * part, (r + 1) * part)
        z = _dot(h[rows], w_ref[...])
        qk = z[:, :SW_QKW]
        sq = qk * qk
        sq_hi = sq.astype(BF16)
        sq_lo = (sq - sq_hi.astype(F32)).astype(BF16)
        ms = (_dot(sq_hi, e_ref[...]) + _dot(sq_lo, e_ref[...])) * (1.0 / SW_HEAD_DIM)
        inv = lax.rsqrt(ms + NORM_EPS)
        inv_hi = inv.astype(BF16)
        inv_lo = (inv - inv_hi.astype(F32)).astype(BF16)
        qk = qk * (_dot(inv_hi, et_ref[...]) + _dot(inv_lo, et_ref[...]))
        q_ref[rows, :] = (qk[:, :SW_QW] * qg_ref[...] * SW_SCALE).astype(q_ref.dtype)
        k_ref[rows, :] = qk[:, SW_QW:] * kg_ref[...]
        v_ref[rows, :] = z[:, SW_QKW:]


def _attn_proj(geo, x_pair, g, mod_p, mod_s, w_bf16, q_gain, k_gain):
    tm = geo.tm
    n_out = w_bf16.shape[1]
    heads = SW_QKW // SW_HEAD_DIM
    member = (np.arange(SW_QKW)[:, None] // SW_HEAD_DIM == np.arange(LANES)[None, :]).astype(np.float32)
    e = jnp.asarray(member, BF16)
    et = jnp.asarray(member.T, BF16)
    assert heads <= LANES
    qg = jnp.tile(q_gain, SW_Q_HEADS).reshape(1, SW_QW)
    kg = jnp.tile(k_gain, SW_KV_HEADS).reshape(1, SW_KV_W)
    row = lambda i: (i, 0)
    fixed = lambda i: (0, 0)
    return pl.pallas_call(
        functools.partial(_attn_proj_kernel, n_pt=geo.n_pt),
        grid=(geo.n_tiles,),
        in_specs=[
            *_pair_specs(geo, tm, D_MODEL),
            pl.BlockSpec((1, D_MODEL), fixed),
            *_mod_specs(geo, 0, tm),
            *_mod_specs(geo, 1, tm),
            pl.BlockSpec((D_MODEL, n_out), fixed),
            pl.BlockSpec((SW_QKW, LANES), fixed),
            pl.BlockSpec((LANES, SW_QKW), fixed),
            pl.BlockSpec((1, SW_QW), fixed),
            pl.BlockSpec((1, SW_KV_W), fixed),
        ],
        out_specs=[
            pl.BlockSpec((tm, SW_QW), row),
            pl.BlockSpec((tm, SW_KV_W), row),
            pl.BlockSpec((tm, SW_KV_W), row),
        ],
        out_shape=[
            jax.ShapeDtypeStruct((geo.nt, SW_QW), BF16),
            jax.ShapeDtypeStruct((geo.nt, SW_KV_W), F32),
            jax.ShapeDtypeStruct((geo.nt, SW_KV_W), F32),
        ],
        compiler_params=_cparams(("arbitrary",)),
        name="attn_proj_qknorm",
    )(*x_pair, g, mod_p, mod_s, mod_p, mod_s, w_bf16, e, et, qg, kg)


def _rel_bucket_np(dist):
    n = np.maximum(dist, 0)
    max_exact = REL_BUCKETS // 2
    ratio = np.log(np.maximum(n, 1).astype(np.float32) / np.float32(max_exact)) / np.float32(
        math.log(REL_MAX_DIST / max_exact))
    large = max_exact + (ratio * np.float32(REL_BUCKETS - max_exact)).astype(np.int32)
    large = np.minimum(large, REL_BUCKETS - 1)
    return np.where(n < max_exact, n, large).astype(np.int32)


def _bucket_table(qpos, kpos, k_ok):
    dist = qpos[:, None] - kpos[None, :]
    ok = (dist >= 0) & (dist <= WINDOW) & k_ok[None, :]
    return np.where(ok, _rel_bucket_np(dist), -1).astype(np.int32)


def _build_bias(bkt_ref, rb_ref, bias_scr, rows):
    bkt = bkt_ref[...]
    base = jnp.where(bkt < 0, -jnp.inf, 0.0)
    for h in range(SW_Q_HEADS):
        bias_scr[h * rows:(h + 1) * rows, :] = base

    def add_bucket(j, carry):
        hit = bkt == j
        for h in range(SW_Q_HEADS):
            sl = slice(h * rows, (h + 1) * rows)
            bias_scr[sl, :] = bias_scr[sl, :] + jnp.where(hit, rb_ref[j, h], 0.0)
        return carry

    lax.fori_loop(0, REL_BUCKETS, add_bucket, 0)


def _sink_softmax_pv(s, sink, vv_g):
    m = jnp.maximum(jnp.max(s, axis=-1, keepdims=True), sink)
    e = jnp.exp(s - m)
    p = e / (jnp.sum(e, axis=-1, keepdims=True) + jnp.exp(sink - m))
    return _dot(p.astype(BF16), vv_g)


def _build_bias_t(bkt_ref, rb_ref, bias_scr, blk):
    bkt = bkt_ref[...]
    base = jnp.where(bkt < 0, -jnp.inf, 0.0)
    slots = [(g, j) for g in range(SW_KV_HEADS) for j in range(SW_GROUP)]
    for g, j in slots:
        bias_scr[g, :, j * blk:(j + 1) * blk] = base

    def add_bucket(b, carry):
        hit = bkt == b
        for g, j in slots:
            cols = slice(j * blk, (j + 1) * blk)
            bias_scr[g, :, cols] = bias_scr[g, :, cols] + jnp.where(hit, rb_ref[b, g * SW_GROUP + j], 0.0)
        return carry

    lax.fori_loop(0, REL_BUCKETS, add_bucket, 0)


def _swa_prompt_kernel(rb_ref, sink_ref, bkt_ref, q_ref, kp_ref, kc_ref, vp_ref, vc_ref, o_ref, bias_scr, *, qb):
    n = pl.program_id(1)
    blk = ATT_BLOCK
    hd = SW_HEAD_DIM

    @pl.when((pl.program_id(0) == 0) & (n == 0))
    def _():
        _build_bias_t(bkt_ref, rb_ref, bias_scr, blk)

    key_i = lax.broadcasted_iota(I32, (SW_KEYS, SW_GROUP * blk), 0)
    hide_prev = (n == 0) & (key_i < blk)
    zeros_q = jnp.zeros((hd, SW_GROUP * blk), BF16)
    sinks = [jnp.concatenate([jnp.full((1, blk), sink_ref[g * SW_GROUP + j], F32) for j in range(SW_GROUP)], axis=1)
             for g in range(SW_KV_HEADS)]
    for i in range(qb):
        rows = slice(i * blk, (i + 1) * blk)
        if i == 0:
            k2 = jnp.concatenate([kp_ref[...], kc_ref[rows, :]], axis=0)
            v2 = jnp.concatenate([vp_ref[...], vc_ref[rows, :]], axis=0)
        else:
            k2 = kc_ref[(i - 1) * blk:(i + 1) * blk, :]
            v2 = vc_ref[(i - 1) * blk:(i + 1) * blk, :]
        kk = k2.astype(BF16)
        vv_t = v2.T.astype(BF16)
        q_t = q_ref[rows, :].astype(F32).T.astype(BF16)
        outs = []
        for g in range(SW_KV_HEADS):
            heads = [g * SW_GROUP + j for j in range(SW_GROUP)]
            q_cat = jnp.concatenate([q_t[hq * hd:(hq + 1) * hd, :] for hq in heads], axis=1)
            q_full = jnp.concatenate([q_cat if gg == g else zeros_q for gg in range(SW_KV_HEADS)], axis=0)
            s = _dot(kk, q_full) + bias_scr[g]
            if i == 0:
                s = jnp.where(hide_prev, -jnp.inf, s)
            m = jnp.maximum(jnp.max(s, axis=0, keepdims=True), sinks[g])
            e = jnp.exp(s - m)
            den = jnp.sum(e, axis=0, keepdims=True) + jnp.exp(sinks[g] - m)
            o_t = _dot(vv_t[g * hd:(g + 1) * hd, :], e.astype(BF16)) * (1.0 / den)
            outs.extend(o_t[:, j * blk:(j + 1) * blk] for j in range(SW_GROUP))
        o_ref[rows, :] = jnp.concatenate(outs, axis=0).T.astype(o_ref.dtype)


def _swa_prompt(geo, q, k, v, rel_bias, sinks):
    blk = ATT_BLOCK
    assert geo.seq % blk == 0 and WINDOW == blk
    n_blk = geo.seq // blk
    qb = 4 if n_blk % 4 == 0 else (2 if n_blk % 2 == 0 else 1)
    nb = n_blk // qb
    bkt = jnp.asarray(_bucket_table(blk + np.arange(blk), np.arange(2 * blk), np.ones(2 * blk, bool)).T.copy())
    cur = lambda b, n: (b * nb + n, 0)
    prev = lambda b, n: (b * n_blk + jnp.maximum(n * qb - 1, 0), 0)
    smem = pl.BlockSpec(memory_space=pltpu.SMEM)
    return pl.pallas_call(
        functools.partial(_swa_prompt_kernel, qb=qb),
        grid=(geo.batch, nb),
        in_specs=[
            smem, smem,
            pl.BlockSpec((SW_KEYS, blk), lambda b, n: (0, 0)),
            pl.BlockSpec((qb * blk, SW_QW), cur),
            pl.BlockSpec((blk, SW_KV_W), prev),
            pl.BlockSpec((qb * blk, SW_KV_W), cur),
            pl.BlockSpec((blk, SW_KV_W), prev),
            pl.BlockSpec((qb * blk, SW_KV_W), cur),
        ],
        out_specs=pl.BlockSpec((qb * blk, SW_QW), cur),
        out_shape=jax.ShapeDtypeStruct((geo.n_prompt, SW_QW), BF16),
        scratch_shapes=[pltpu.VMEM((SW_KV_HEADS, SW_KEYS, SW_GROUP * blk), F32)],
        compiler_params=_cparams(("arbitrary", "arbitrary")),
        name="swa_prompt",
    )(rel_bias, sinks, bkt, q, k, k, v, v)


def _swa_step_kernel(rb_ref, sink_ref, bkt_ref, q_ref, kn_ref, vn_ref, ck_ref, cv_ref,
                     o_ref, nk_ref, nv_ref, bias_scr, *, steps, bb):
    pad = SUBLANES
    win = WINDOW

    @pl.when(pl.program_id(0) == 0)
    def _():
        _build_bias(bkt_ref, rb_ref, bias_scr, pad)

    fill = jnp.zeros((SW_KEYS - win - pad, SW_KV_W), F32)
    row8 = lax.broadcasted_iota(I32, (pad, SW_KV_W), 0)

    def shifted(cache, new):
        rolled = pltpu.roll(cache, win - steps, 0)
        tail = jnp.where(row8 < pad - steps, rolled[win - pad:], pltpu.roll(new, pad - steps, 0))
        return rolled[:win - pad], tail

    def body(s, carry):
        ck, cv, kn, vn = ck_ref[s], cv_ref[s], kn_ref[s], vn_ref[s]
        kk = jnp.concatenate([ck, kn, fill], axis=0).astype(BF16)
        vv = jnp.concatenate([cv, vn, fill], axis=0).astype(BF16)
        q = q_ref[s]
        for g in range(SW_KV_HEADS):
            kcols = slice(g * SW_HEAD_DIM, (g + 1) * SW_HEAD_DIM)
            heads = [g * SW_GROUP + j for j in range(SW_GROUP)]
            qs = jnp.concatenate([q[:, hq * SW_HEAD_DIM:(hq + 1) * SW_HEAD_DIM] for hq in heads], axis=0)
            sc = _dot_nt(qs.astype(BF16), kk[:, kcols]) + bias_scr[heads[0] * pad:(heads[-1] + 1) * pad, :]
            sink = jnp.concatenate([jnp.full((pad, 1), sink_ref[hq], F32) for hq in heads], axis=0)
            og = _sink_softmax_pv(sc, sink, vv[:, kcols])
            for j, hq in enumerate(heads):
                o_ref[s, :, hq * SW_HEAD_DIM:(hq + 1) * SW_HEAD_DIM] = og[j * pad:(j + 1) * pad]
        head, tail = shifted(ck, kn)
        nk_ref[s, 0:win - pad] = head
        nk_ref[s, win - pad:win] = tail
        head, tail = shifted(cv, vn)
        nv_ref[s, 0:win - pad] = head
        nv_ref[s, win - pad:win] = tail
        return carry

    lax.fori_loop(0, bb, body, 0, unroll=True)


def _swa_step(geo, q, k, v, cache_k, cache_v, rel_bias, sinks):
    steps, db = geo.dec_seq, geo.dec_batch
    win = cache_k.shape[1]
    assert win == WINDOW and steps <= SUBLANES
    bb = min(8, db)
    pad = SUBLANES
    q_s = _seq_major(geo, q[geo.n_prompt:].astype(F32))
    k_s = _seq_major(geo, k[geo.n_prompt:])
    v_s = _seq_major(geo, v[geo.n_prompt:])
    kpos = np.arange(SW_KEYS)
    k_ok = kpos < win + steps
    qpos = win + np.arange(pad)
    bkt = _bucket_table(qpos, kpos, k_ok)
    bkt[steps:] = -1
    smem = pl.BlockSpec(memory_space=pltpu.SMEM)
    blk3 = lambda r, w: pl.BlockSpec((bb, r, w), lambda i: (i, 0, 0))
    o_s, nk, nv = pl.pallas_call(
        functools.partial(_swa_step_kernel, steps=steps, bb=bb),
        grid=(db // bb,),
        in_specs=[
            smem, smem,
            pl.BlockSpec((pad, SW_KEYS), lambda i: (0, 0)),
            blk3(pad, SW_QW), blk3(pad, SW_KV_W), blk3(pad, SW_KV_W),
            blk3(win, SW_KV_W), blk3(win, SW_KV_W),
        ],
        out_specs=[blk3(pad, SW_QW), blk3(win, SW_KV_W), blk3(win, SW_KV_W)],
        out_shape=[
            jax.ShapeDtypeStruct((db, pad, SW_QW), F32),
            jax.ShapeDtypeStruct((db, win, SW_KV_W), F32),
            jax.ShapeDtypeStruct((db, win, SW_KV_W), F32),
        ],
        scratch_shapes=[pltpu.VMEM((SW_Q_HEADS * pad, SW_KEYS), F32)],
        compiler_params=_cparams(("arbitrary",)),
        name="swa_step",
    )(rel_bias, sinks, jnp.asarray(bkt), q_s, k_s, v_s, cache_k, cache_v)
    return _time_major(geo, o_s), nk, nv


def kernel(x_prompt, x_sample, state_hgrn, cache_win_k, cache_win_v, c_prompt, c_sample, norm1_g, norm2_g, ada_w, ada_b, hg_w_in, hg_lb_table, hg_onorm_g, hg_w_out, sw_w_in, sw_qnorm_g, sw_knorm_g, sw_sinks, sw_w_out, rel_bias, router_w, router_b, moe_w_in, moe_b_in, moe_w_out, moe_b_out):
    batch, seq, d = x_prompt.shape
    db, steps, _ = x_sample.shape
    assert d == D_MODEL and ada_w.shape[0] == 2 and hg_w_in.shape[0] == 1 and sw_w_in.shape[0] == 1
    geo = _Geom(batch, seq, db, steps)
    x = (x_prompt.reshape(batch * seq, d), x_sample.transpose(1, 0, 2).reshape(steps * db, d))
    n_seq = batch + db
    rows = -(-n_seq // SUBLANES) * SUBLANES
    c_all = jnp.concatenate([c_prompt, c_sample, jnp.zeros((rows - n_seq, d), F32)], axis=0)
    mods = _ada_mods(c_all, ada_w, ada_b)

    def layer_mods(layer):
        mod_p = mods[layer, :batch].reshape(batch, 1, 6 * d)
        mod_s = jnp.tile(mods[layer, batch:n_seq], (steps, 1))
        return mod_p, mod_s

    def moe(layer, a_p, a_s, w_out, x_in, mod_p, mod_s):
        x1, h2, pos, pg, nch, soff = _post(geo, a_p, a_s, w_out.astype(BF16), x_in, norm2_g[layer:layer + 1],
                                           mod_p, mod_s, 0, router_w[layer], router_b[layer])
        return _moe(geo, h2, pos, pg, nch, soff, x1, mod_p, mod_s, 5,
                    layer, moe_w_in, moe_b_in, moe_w_out, moe_b_out)

    mod_p, mod_s = layer_mods(0)
    og = hg_onorm_g[0:1]
    z = _proj(geo, x, norm1_g[0:1], mod_p, mod_s, hg_w_in[0].astype(BF16), 0, 1)
    o_p, st_p = _gla_prompt(geo, z, hg_lb_table, og)
    o_s, st_s = _gla_step(geo, z, state_hgrn[0], hg_lb_table, og)
    x = moe(0, o_p, o_s, hg_w_out[0], x, mod_p, mod_s)

    mod_p, mod_s = layer_mods(1)
    q, k, v = _attn_proj(geo, x, norm1_g[1:2], mod_p, mod_s, sw_w_in[0].astype(BF16), sw_qnorm_g[0], sw_knorm_g[0])
    win = cache_win_k.shape[2]
    a_p = _swa_prompt(geo, q, k, v, rel_bias, sw_sinks[0])
    a_s, nk, nv = _swa_step(geo, q, k, v, cache_win_k[0].reshape(db, win, SW_KV_W),
                            cache_win_v[0].reshape(db, win, SW_KV_W), rel_bias, sw_sinks[0])
    x = moe(1, a_p, a_s, sw_w_out[0], x, mod_p, mod_s)

    y_prompt = x[0].reshape(batch, seq, d)
    y_sample = x[1].reshape(steps, db, d).transpose(1, 0, 2)
    kv_shape = (1, batch, WINDOW, SW_KV_HEADS, SW_HEAD_DIM)
    last_window = lambda a: jnp.stack([a[(b + 1) * seq - WINDOW:(b + 1) * seq] for b in range(batch)])
    k_p = last_window(k).reshape(kv_shape)
    v_p = last_window(v).reshape(kv_shape)
    cache_shape = (1, db, win, SW_KV_HEADS, SW_HEAD_DIM)
    return (y_prompt, y_sample, jnp.swapaxes(st_p, -1, -2)[None], st_s[None], k_p, v_p,
            nk.reshape(cache_shape), nv.reshape(cache_shape))
```

```python
import functools
import math

import numpy as np
import jax
import jax.numpy as jnp
from jax import lax
from jax.experimental import pallas as pl
from jax.experimental.pallas import tpu as pltpu

F32 = jnp.float32
BF16 = jnp.bfloat16
I32 = jnp.int32

D_MODEL = 1024
LANES = 128
SUBLANES = 8
TOKEN_TILE = 512
MOE_TILE = 256
GLA_STEP_TOKENS = 256
STEP_SEQS = 8
HG_DK = 128
HG_HEADS = D_MODEL // HG_DK
HG_DV = D_MODEL // HG_HEADS
HG_CHUNK = 64
SW_HEAD_DIM = 64
SW_Q_HEADS = D_MODEL // SW_HEAD_DIM
SW_KV_HEADS = 4
SW_GROUP = SW_Q_HEADS // SW_KV_HEADS
SW_KV_W = SW_KV_HEADS * SW_HEAD_DIM
WINDOW = 128
ATT_BLOCK = 128
SW_SCALE = SW_HEAD_DIM ** -0.5
REL_BUCKETS = 32
REL_MAX_DIST = 128
N_EXPERTS = 32
TOP_K = 4
SWIGLU_LIMIT = 7.0
SWIGLU_ALPHA = 1.702
NORM_EPS = 1e-5
MOE_BLOCK = 256
ROW_CHUNK = SUBLANES
COPY_CHUNKS = 4
CHUNK_UNROLL = 8
WAIT_CHUNKS = 16
VMEM_LIMIT = 56 * 1024 * 1024


def _cparams(sem):
    return pltpu.CompilerParams(dimension_semantics=sem, vmem_limit_bytes=VMEM_LIMIT)


def _sigmoid(x):
    return 0.5 * jnp.tanh(0.5 * x) + 0.5


def _silu(x):
    return x * _sigmoid(x)


def _dot(a, b):
    return jnp.dot(a, b, preferred_element_type=F32)


def _dot_nt(a, b):
    return lax.dot_general(a, b, (((1,), (1,)), ((), ())), preferred_element_type=F32)


def _dot_tn(a, b):
    return lax.dot_general(a, b, (((0,), (0,)), ((), ())), preferred_element_type=F32)


def _split3(x):
    hi = x.astype(BF16)
    r = x - hi.astype(F32)
    mid = r.astype(BF16)
    lo = (r - mid.astype(F32)).astype(BF16)
    return hi, mid, lo


def _ada_kernel(c_ref, w_ref, b_ref, o_ref):
    s = _silu(c_ref[...]).astype(BF16)
    o_ref[0] = _dot(s, w_ref[0].astype(BF16)) + b_ref[0]


def _ada_mods(c_all, ada_w, ada_b):
    depth, d, n6 = ada_w.shape
    rows = c_all.shape[0]
    tn = 1536
    return pl.pallas_call(
        _ada_kernel,
        grid=(depth, n6 // tn),
        in_specs=[
            pl.BlockSpec((rows, d), lambda l, j: (0, 0)),
            pl.BlockSpec((1, d, tn), lambda l, j: (l, 0, j)),
            pl.BlockSpec((1, 1, tn), lambda l, j: (l, 0, j)),
        ],
        out_specs=pl.BlockSpec((1, rows, tn), lambda l, j: (l, 0, j)),
        out_shape=jax.ShapeDtypeStruct((depth, rows, n6), F32),
        compiler_params=_cparams(("arbitrary", "arbitrary")),
        name="ada_mods",
    )(c_all, ada_w, ada_b.reshape(depth, 1, n6))


class _Geom:
    def __init__(self, batch, seq, dec_batch, dec_seq):
        self.batch, self.seq, self.dec_batch, self.dec_seq = batch, seq, dec_batch, dec_seq
        self.n_prompt = batch * seq
        self.n_sample = dec_batch * dec_seq
        self.nt = self.n_prompt + self.n_sample
        tm = TOKEN_TILE
        while seq % tm or self.n_sample % tm:
            tm //= 2
        assert tm >= 8
        self.tm = tm
        self.n_pt = self.n_prompt // tm
        self.n_tiles = self.nt // tm
        self.tm_moe = min(MOE_TILE, tm)


def _mod_specs(geo, col, tm):
    seq, batch = geo.seq, geo.batch
    n_pt = geo.n_prompt // tm

    def p_map(i, *_):
        return (jnp.minimum(i * tm // seq, batch - 1), 0, col)

    def s_map(i, *_):
        return (jnp.maximum(i - n_pt, 0), col)

    return [pl.BlockSpec((1, 1, D_MODEL), p_map), pl.BlockSpec((tm, D_MODEL), s_map)]


def _pick_mod(i, n_pt, p_ref, s_ref):
    return jnp.where(i >= n_pt, s_ref[...], p_ref[0])


def _norm_mod(x, g, sc, sh):
    ms = jnp.mean(x * x, axis=-1, keepdims=True)
    return x * lax.rsqrt(ms + NORM_EPS) * g * (1.0 + sc) + sh


def _pair_specs(geo, tm, width):
    n_pt = geo.n_prompt // tm
    return [pl.BlockSpec((tm, width), lambda i, *_: (jnp.minimum(i, n_pt - 1), 0)),
            pl.BlockSpec((tm, width), lambda i, *_: (jnp.maximum(i - n_pt, 0), 0))]


def _proj_kernel(xp_ref, xs_ref, g_ref, shp_ref, shs_ref, scp_ref, scs_ref, w_ref, o_ref, *, n_pt, tn):
    i = pl.program_id(0)
    sh = _pick_mod(i, n_pt, shp_ref, shs_ref)
    sc = _pick_mod(i, n_pt, scp_ref, scs_ref)
    x = jnp.where(i >= n_pt, xs_ref[...], xp_ref[...])
    h = _norm_mod(x, g_ref[...], sc, sh).astype(BF16)
    for j in range(o_ref.shape[1] // tn):
        o_ref[:, j * tn:(j + 1) * tn] = _dot(h, w_ref[:, j * tn:(j + 1) * tn])


def _proj(geo, x_pair, g, mod_p, mod_s, w_bf16, col_shift, col_scale):
    n_out = w_bf16.shape[1]
    tm = geo.tm
    fixed = lambda i: (0, 0)
    return pl.pallas_call(
        functools.partial(_proj_kernel, n_pt=geo.n_pt, tn=1024),
        grid=(geo.n_tiles,),
        in_specs=[
            *_pair_specs(geo, tm, D_MODEL),
            pl.BlockSpec((1, D_MODEL), fixed),
            *_mod_specs(geo, col_shift, tm),
            *_mod_specs(geo, col_scale, tm),
            pl.BlockSpec((D_MODEL, n_out), fixed),
        ],
        out_specs=pl.BlockSpec((tm, n_out), lambda i: (i, 0)),
        out_shape=jax.ShapeDtypeStruct((geo.nt, n_out), F32),
        compiler_params=_cparams(("arbitrary",)),
        name="norm_mod_proj",
    )(*x_pair, g, mod_p, mod_s, mod_p, mod_s, w_bf16)


def _hg_lower_bound(lbt_ref):
    t = lbt_ref[...]
    e = jnp.exp(t - jnp.max(t, axis=0, keepdims=True))
    return e[0:1] / jnp.sum(e, axis=0, keepdims=True)


def _hg_gates(fz, lb):
    t = jnp.tanh(0.5 * fz)
    logf = jnp.log(lb + (1.0 - lb) * (0.5 + 0.5 * t))
    return logf, (1.0 - lb) * (0.5 - 0.5 * t)


def _hg_out(o, gz, og):
    ms = jnp.mean(o * o, axis=-1, keepdims=True)
    return o * lax.rsqrt(ms + NORM_EPS) * og * _silu(gz)


def _gla_prompt_kernel(*refs, batch, chunk, n_chunks):
    z_refs = refs[:batch]
    lbt_ref, og_ref, o_ref, sfin_ref, st_scr = refs[batch:]
    t_step = pl.program_id(0)
    kw = HG_HEADS * HG_DK

    @pl.when(t_step == 0)
    def _():
        st_scr[...] = jnp.zeros_like(st_scr)

    lb = _hg_lower_bound(lbt_ref)
    og = og_ref[...]
    r_i = lax.broadcasted_iota(I32, (chunk, chunk), 0)
    c_i = lax.broadcasted_iota(I32, (chunk, chunk), 1)
    causal = c_i <= r_i
    tri = causal.astype(BF16)
    mid = chunk // 2 - 1

    def body(c, carry):
        rows = pl.ds(pl.multiple_of(c * chunk, chunk), chunk)
        for bi, z_ref in enumerate(z_refs):
            logf, kk = _hg_gates(z_ref[rows, kw:2 * kw], lb)
            hi, md, lo = _split3(logf)
            cs = _dot(tri, jnp.concatenate([hi, md, lo], axis=1))
            b = cs[:, :kw] + cs[:, kw:2 * kw] + cs[:, 2 * kw:]
            b_mid = b[mid:mid + 1]
            b_last = b[chunk - 1:chunk]
            q_hat = _silu(z_ref[rows, 0:kw]) * jnp.exp(b - b_mid)
            k_hat = kk * jnp.exp(b_mid - b)
            q_in = (q_hat * jnp.exp(b_mid)).astype(BF16)
            k_dec = (k_hat * jnp.exp(b_last - b_mid)).astype(BF16)
            q_hat = q_hat.astype(BF16)
            k_hat = k_hat.astype(BF16)
            dec = jnp.exp(b_last)
            for h in range(HG_HEADS):
                cols = slice(h * HG_DK, (h + 1) * HG_DK)
                vcols = slice(2 * kw + h * HG_DV, 2 * kw + (h + 1) * HG_DV)
                gcols = slice(2 * kw + HG_HEADS * HG_DV + h * HG_DV, 2 * kw + HG_HEADS * HG_DV + (h + 1) * HG_DV)
                v = z_ref[rows, vcols].astype(BF16)
                att = jnp.where(causal, _dot_nt(q_hat[:, cols], k_hat[:, cols]), 0.0).astype(BF16)
                st = st_scr[bi, h]
                o = _dot(att, v) + _dot_nt(q_in[:, cols], st.astype(BF16))
                st_scr[bi, h] = st * dec[:, cols] + _dot_tn(v, k_dec[:, cols])
                o_ref[bi, rows, h * HG_DV:(h + 1) * HG_DV] = _hg_out(o, z_ref[rows, gcols], og).astype(o_ref.dtype)
        return carry

    lax.fori_loop(0, n_chunks, body, 0, unroll=True)

    @pl.when(t_step == pl.num_programs(0) - 1)
    def _():
        sfin_ref[...] = st_scr[...]


def _gla_prompt(geo, z, lb_table, o_gain):
    tg = min(GLA_STEP_TOKENS, geo.seq)
    chunk = HG_CHUNK if geo.seq % HG_CHUNK == 0 else geo.seq
    assert tg % chunk == 0 and geo.seq % tg == 0
    nt = geo.seq // tg
    batch = geo.batch
    fixed = lambda t: (0, 0)
    state_shape = (batch, HG_HEADS, HG_DV, HG_DK)
    o, st = pl.pallas_call(
        functools.partial(_gla_prompt_kernel, batch=batch, chunk=chunk, n_chunks=tg // chunk),
        grid=(nt,),
        in_specs=[
            *[pl.BlockSpec((tg, 4 * D_MODEL), functools.partial(lambda t, b: (b * nt + t, 0), b=b))
              for b in range(batch)],
            pl.BlockSpec(lb_table.shape, fixed),
            pl.BlockSpec((1, HG_DV), fixed),
        ],
        out_specs=[
            pl.BlockSpec((batch, tg, D_MODEL), lambda t: (0, t, 0)),
            pl.BlockSpec(state_shape, lambda t: (0, 0, 0, 0)),
        ],
        out_shape=[
            jax.ShapeDtypeStruct((batch, geo.seq, D_MODEL), BF16),
            jax.ShapeDtypeStruct(state_shape, F32),
        ],
        scratch_shapes=[pltpu.VMEM(state_shape, F32)],
        compiler_params=_cparams(("arbitrary",)),
        name="gla_prompt",
    )(*([z] * batch), lb_table, o_gain)
    return o.reshape(geo.n_prompt, D_MODEL), st


def _gla_step_kernel(z_ref, s_ref, lbt_ref, og_ref, o_ref, snew_ref, *, steps, bb):
    kw = HG_HEADS * HG_DK
    pad = SUBLANES
    lb = _hg_lower_bound(lbt_ref)
    og = og_ref[...]
    r_i = lax.broadcasted_iota(I32, (pad, pad), 0)
    c_i = lax.broadcasted_iota(I32, (pad, pad), 1)
    causal = c_i <= r_i
    tri = causal.astype(BF16)
    row_w = lax.broadcasted_iota(I32, (pad, kw), 0)
    live = row_w < steps
    row_k = lax.broadcasted_iota(I32, (pad, HG_DK), 0)
    ones_sel = jnp.where((row_k == steps) | (row_k == steps + 1), 1.0, 0.0).astype(BF16)

    def body(s, carry):
        z = z_ref[s]
        logf, kk = _hg_gates(z[:, kw:2 * kw], lb)
        hi, md, lo = _split3(jnp.where(live, logf, 0.0))
        cs = _dot(tri, jnp.concatenate([hi, md, lo], axis=1))
        b = cs[:, :kw] + cs[:, kw:2 * kw] + cs[:, 2 * kw:]
        b_last = b[steps - 1:steps]
        q_in = (_silu(z[:, 0:kw]) * jnp.exp(b)).astype(BF16)
        k_hat = jnp.where(live, kk * jnp.exp(-b), 0.0).astype(BF16)
        k_dec = jnp.where(live, kk * jnp.exp(b_last - b), 0.0).astype(BF16)
        dec = jnp.exp(b_last)
        d_hi = dec.astype(BF16)
        d_lo = (dec - d_hi.astype(F32)).astype(BF16)
        a_all = jnp.where(row_w == steps, d_hi, jnp.where(row_w == steps + 1, d_lo, k_dec))
        for h in range(HG_HEADS):
            cols = slice(h * HG_DK, (h + 1) * HG_DK)
            vcols = slice(2 * kw + h * HG_DV, 2 * kw + (h + 1) * HG_DV)
            gcols = slice(2 * kw + HG_HEADS * HG_DV + h * HG_DV, 2 * kw + HG_HEADS * HG_DV + (h + 1) * HG_DV)
            v = z[:, vcols].astype(BF16)
            s0 = s_ref[s, h]
            att = jnp.where(causal, _dot_nt(q_in[:, cols], k_hat[:, cols]), 0.0).astype(BF16)
            o = _dot(att, v) + _dot(q_in[:, cols], s0.astype(BF16))
            upd = _dot_tn(a_all[:, cols], jnp.concatenate([v, ones_sel], axis=1))
            snew_ref[s, h] = upd[:, HG_DV:] * s0 + upd[:, :HG_DV]
            o_ref[s, :, h * HG_DV:(h + 1) * HG_DV] = _hg_out(o, z[:, gcols], og)
        return carry

    lax.fori_loop(0, bb, body, 0, unroll=True)


def _seq_major(geo, rows):
    steps, db = geo.dec_seq, geo.dec_batch
    w = rows.shape[-1]
    r = rows.reshape(steps, db, w).transpose(1, 0, 2)
    return jnp.concatenate([r, jnp.zeros((db, SUBLANES - steps, w), rows.dtype)], axis=1)


def _time_major(geo, r):
    steps, db = geo.dec_seq, geo.dec_batch
    return r[:, :steps].transpose(1, 0, 2).reshape(steps * db, r.shape[-1])


def _gla_step(geo, z, state, lb_table, o_gain):
    steps, db = geo.dec_seq, geo.dec_batch
    assert steps + 2 <= SUBLANES
    bb = min(STEP_SEQS, db)
    z_s = _seq_major(geo, z[geo.n_prompt:])
    o_s, s_new = pl.pallas_call(
        functools.partial(_gla_step_kernel, steps=steps, bb=bb),
        grid=(db // bb,),
        in_specs=[
            pl.BlockSpec((bb, SUBLANES, 4 * D_MODEL), lambda i: (i, 0, 0)),
            pl.BlockSpec((bb, HG_HEADS, HG_DK, HG_DV), lambda i: (i, 0, 0, 0)),
            pl.BlockSpec(lb_table.shape, lambda i: (0, 0)),
            pl.BlockSpec((1, HG_DV), lambda i: (0, 0)),
        ],
        out_specs=[
            pl.BlockSpec((bb, SUBLANES, D_MODEL), lambda i: (i, 0, 0)),
            pl.BlockSpec((bb, HG_HEADS, HG_DK, HG_DV), lambda i: (i, 0, 0, 0)),
        ],
        out_shape=[
            jax.ShapeDtypeStruct((db, SUBLANES, D_MODEL), F32),
            jax.ShapeDtypeStruct(state.shape, F32),
        ],
        compiler_params=_cparams(("arbitrary",)),
        name="gla_step",
    )(z_s, state, lb_table, o_gain)
    return _time_major(geo, o_s), s_new


def _post_kernel(ap_ref, as_ref, wo_ref, xp_ref, xs_ref, gp_ref, gs_ref, n2_ref, shp_ref, shs_ref, scp_ref,
                 scs_ref, rw_ref, rb_ref, x1_ref, h2_ref, pos_ref, pg_ref, nch_ref, soff_ref, *, n_pt, tm):
    i = pl.program_id(0)
    g1 = _pick_mod(i, n_pt, gp_ref, gs_ref)
    a = jnp.where(i >= n_pt, as_ref[...], ap_ref[...])
    x = jnp.where(i >= n_pt, xs_ref[...], xp_ref[...])
    x1 = x + g1 * _dot(a, wo_ref[...])
    x1_ref[...] = x1
    sh = _pick_mod(i, n_pt, shp_ref, shs_ref)
    sc = _pick_mod(i, n_pt, scp_ref, scs_ref)
    h2 = _norm_mod(x1, n2_ref[...], sc, sh)
    h2_ref[...] = h2.astype(h2_ref.dtype)

    row_e = lax.broadcasted_iota(I32, (LANES, tm), 0)
    logits = _dot_nt(rw_ref[...], h2.astype(BF16)) + rb_ref[...]
    work = jnp.where(row_e < N_EXPERTS, logits, -jnp.inf)
    vals, hits = [], []
    for _ in range(TOP_K):
        m = jnp.max(work, axis=0, keepdims=True)
        idx = jnp.min(jnp.where(work == m, row_e, LANES), axis=0, keepdims=True)
        hit = row_e == idx
        vals.append(m)
        hits.append(hit)
        work = jnp.where(hit, -jnp.inf, work)
    exps = [jnp.exp(v - vals[0]) for v in vals]
    den = exps[0]
    for e in exps[1:]:
        den = den + e
    any_hit = hits[0]
    for hmask in hits[1:]:
        any_hit = any_hit | hmask
    any_f = jnp.where(any_hit, 1.0, 0.0)
    t_i = lax.broadcasted_iota(I32, (tm, tm), 0)
    t_j = lax.broadcasted_iota(I32, (tm, tm), 1)
    rank = _dot(any_f.astype(BF16), (t_i < t_j).astype(BF16))
    n_chunk = jnp.floor((jnp.sum(any_f, axis=1, keepdims=True) + (ROW_CHUNK - 1)) * (1.0 / ROW_CHUNK))
    e_i = lax.broadcasted_iota(I32, (LANES, LANES), 0)
    e_j = lax.broadcasted_iota(I32, (LANES, LANES), 1)
    seg = _dot((e_j < e_i).astype(BF16), jnp.broadcast_to(n_chunk, (LANES, LANES)).astype(BF16))[:, 0:1]
    where_to = seg * float(ROW_CHUNK) + rank
    pos_rows = [jnp.sum(jnp.where(hits[k], where_to, 0.0), axis=0, keepdims=True) for k in range(TOP_K)]
    gate_rows = [exps[k] / den for k in range(TOP_K)]
    pos_ref[0] = jnp.concatenate(pos_rows + [jnp.zeros((SUBLANES - TOP_K, tm), F32)], axis=0).astype(I32)
    rows = jnp.concatenate(pos_rows + gate_rows + [jnp.zeros((LANES - 2 * TOP_K, tm), F32)], axis=0)
    pg_ref[...] = rows.T
    nch_ref[0] = n_chunk.astype(I32)
    soff_ref[0] = seg.astype(I32)


def _post(geo, a_p, a_s, w_out_bf16, x_pair, norm_g, mod_p, mod_s, col0, router_w, router_b):
    tm = geo.tm_moe
    n_pt = geo.n_prompt // tm
    n_tiles = geo.nt // tm
    rw = jnp.zeros((LANES, D_MODEL), BF16).at[:N_EXPERTS].set(router_w.T.astype(BF16))
    rb = jnp.zeros((LANES, 1), F32).at[:N_EXPERTS, 0].set(router_b)
    row = lambda i: (i, 0)
    fixed = lambda i: (0, 0)
    tile_row = lambda i: (i, 0, 0)
    return pl.pallas_call(
        functools.partial(_post_kernel, n_pt=n_pt, tm=tm),
        grid=(n_tiles,),
        in_specs=[
            *_pair_specs(geo, tm, D_MODEL),
            pl.BlockSpec((D_MODEL, D_MODEL), fixed),
            *_pair_specs(geo, tm, D_MODEL),
            *_mod_specs(geo, col0 + 2, tm),
            pl.BlockSpec((1, D_MODEL), fixed),
            *_mod_specs(geo, col0 + 3, tm),
            *_mod_specs(geo, col0 + 4, tm),
            pl.BlockSpec((LANES, D_MODEL), fixed),
            pl.BlockSpec((LANES, 1), fixed),
        ],
        out_specs=[
            pl.BlockSpec((tm, D_MODEL), row),
            pl.BlockSpec((tm, D_MODEL), row),
            pl.BlockSpec((1, SUBLANES, tm), tile_row),
            pl.BlockSpec((tm, LANES), row),
            pl.BlockSpec((1, LANES, 1), tile_row),
            pl.BlockSpec((1, LANES, 1), tile_row),
        ],
        out_shape=[
            jax.ShapeDtypeStruct((geo.nt, D_MODEL), F32),
            jax.ShapeDtypeStruct((geo.nt, D_MODEL), BF16),
            jax.ShapeDtypeStruct((n_tiles, SUBLANES, tm), I32),
            jax.ShapeDtypeStruct((geo.nt, LANES), F32),
            jax.ShapeDtypeStruct((n_tiles, LANES, 1), I32),
            jax.ShapeDtypeStruct((n_tiles, LANES, 1), I32),
        ],
        compiler_params=_cparams(("arbitrary",)),
        name="post_mixer_router",
    )(a_p, a_s.astype(BF16), w_out_bf16, *x_pair, mod_p, mod_s, norm_g, mod_p, mod_s, mod_p, mod_s, rw, rb)


def _ffn_kernel(first_ref, count_ref, xs_hbm, win_ref, bin_ref, wout_ref, bout_ref, ys_hbm,
                xbuf, ybuf, win_scr, wout_scr, xsem, ysem, *, half):
    e = pl.program_id(0)
    first = first_ref[e]
    n_sub = count_ref[e]
    pairs = n_sub // 2
    odd = n_sub - 2 * pairs
    win_scr[...] = win_ref[0].astype(BF16)
    wout_scr[...] = wout_ref[0].astype(BF16)

    def rows_at(sub, n_rows):
        return pl.ds(pl.multiple_of((first + sub) * half, half), n_rows)

    def x_copy(sub, n_rows, slot):
        return pltpu.make_async_copy(xs_hbm.at[rows_at(sub, n_rows)], xbuf.at[slot, pl.ds(0, n_rows)], xsem.at[slot])

    def y_copy(sub, n_rows, slot):
        return pltpu.make_async_copy(ybuf.at[slot, pl.ds(0, n_rows)], ys_hbm.at[rows_at(sub, n_rows)], ysem.at[slot])

    def ffn(slot, rows):
        gu = _dot(xbuf[slot, rows, :].astype(BF16), win_scr[...]) + bin_ref[0]
        gate = jnp.minimum(gu[:, :D_MODEL], SWIGLU_LIMIT)
        up = jnp.clip(gu[:, D_MODEL:], -SWIGLU_LIMIT, SWIGLU_LIMIT)
        act = gate * _sigmoid(SWIGLU_ALPHA * gate) * (up + 1.0)
        return _dot(act.astype(BF16), wout_scr[...]) + bout_ref[0]

    def start_first(expert):
        begin = pl.multiple_of(first_ref[expert] * half, half)
        n = count_ref[expert]

        @pl.when(n >= 2)
        def _():
            pltpu.make_async_copy(xs_hbm.at[pl.ds(begin, 2 * half)], xbuf.at[0], xsem.at[0]).start()

        @pl.when(n == 1)
        def _():
            pltpu.make_async_copy(xs_hbm.at[pl.ds(begin, half)], xbuf.at[0, pl.ds(0, half)], xsem.at[0]).start()

    @pl.when(e == 0)
    def _():
        start_first(0)

    def unit(u, carry):
        slot = u % 2
        x_copy(2 * u, 2 * half, slot).wait()

        @pl.when(u + 1 < pairs)
        def _():
            x_copy(2 * (u + 1), 2 * half, 1 - slot).start()

        @pl.when((u + 1 == pairs) & (odd == 1))
        def _():
            x_copy(2 * pairs, half, 1 - slot).start()

        @pl.when(u >= 2)
        def _():
            y_copy(0, 2 * half, slot).wait()

        ybuf[slot] = ffn(slot, slice(0, 2 * half))
        y_copy(2 * u, 2 * half, slot).start()
        return carry

    lax.fori_loop(0, pairs, unit, 0)
    tail_slot = pairs % 2

    @pl.when(odd == 1)
    def _():
        x_copy(2 * pairs, half, tail_slot).wait()

        @pl.when(pairs >= 2)
        def _():
            y_copy(0, 2 * half, tail_slot).wait()

        ybuf[tail_slot, 0:half, :] = ffn(tail_slot, slice(0, half))
        y_copy(2 * pairs, half, tail_slot).start()

    @pl.when(e + 1 < pl.num_programs(0))
    def _():
        start_first(e + 1)

    @pl.when(odd == 1)
    def _():
        @pl.when(pairs >= 1)
        def _():
            y_copy(0, 2 * half, 1 - tail_slot).wait()

        y_copy(0, half, tail_slot).wait()

    @pl.when(odd == 0)
    def _():
        @pl.when(pairs >= 2)
        def _():
            y_copy(0, 2 * half, tail_slot).wait()

        @pl.when(pairs >= 1)
        def _():
            y_copy(0, 2 * half, 1 - tail_slot).wait()


def _ffn(xs, first_sub, n_sub, layer, w_in, b_in, w_out, b_out):
    n_e, d, d2 = w_in.shape
    half = MOE_BLOCK
    w_map = lambda e, *_: (layer * N_EXPERTS + e, 0, 0)
    return pl.pallas_call(
        functools.partial(_ffn_kernel, half=half),
        grid_spec=pltpu.PrefetchScalarGridSpec(
            num_scalar_prefetch=2,
            grid=(N_EXPERTS,),
            in_specs=[
                pl.BlockSpec(memory_space=pl.ANY),
                pl.BlockSpec((1, d, d2), w_map),
                pl.BlockSpec((1, 1, d2), w_map),
                pl.BlockSpec((1, d2 // 2, d), w_map),
                pl.BlockSpec((1, 1, d), w_map),
            ],
            out_specs=pl.BlockSpec(memory_space=pl.ANY),
            scratch_shapes=[
                pltpu.VMEM((2, 2 * half, d), F32),
                pltpu.VMEM((2, 2 * half, d), F32),
                pltpu.VMEM((d, d2), BF16),
                pltpu.VMEM((d2 // 2, d), BF16),
                pltpu.SemaphoreType.DMA((2,)),
                pltpu.SemaphoreType.DMA((2,)),
            ],
        ),
        out_shape=jax.ShapeDtypeStruct(xs.shape, xs.dtype),
        input_output_aliases={2: 0},
        compiler_params=_cparams(("arbitrary",)),
        name="moe_ffn",
    )(first_sub, n_sub, xs, w_in, b_in.reshape(n_e, 1, d2), w_out, b_out.reshape(n_e, 1, d))


def _chunk_rows(chunk, n_chunks=1):
    return pl.ds(pl.multiple_of(chunk * ROW_CHUNK, ROW_CHUNK), n_chunks * ROW_CHUNK)


def _run_pieces(n, fn, big):
    n_big = n // big

    def big_piece(i, carry):
        fn(i * big, big)
        return carry

    def small_piece(i, carry):
        fn(n_big * big + i, 1)
        return carry

    lax.fori_loop(0, n_big, big_piece, 0)
    lax.fori_loop(0, n - n_big * big, small_piece, 0)


def _for_each_chunk(where_ref, tot_ref, tile, fn):
    n = tot_ref[tile]
    n_groups = n // CHUNK_UNROLL

    def group(i, carry):
        for u in range(CHUNK_UNROLL):
            j = i * CHUNK_UNROLL + u
            fn(j, where_ref[tile, j])
        return carry

    def single(i, carry):
        j = n_groups * CHUNK_UNROLL + i
        fn(j, where_ref[tile, j])
        return carry

    lax.fori_loop(0, n_groups, group, 0)
    lax.fori_loop(0, n - n_groups * CHUNK_UNROLL, single, 0)


def _dispatch_kernel(where_ref, tot_ref, fill0_ref, filln_ref,
                     pos_ref, h_ref, xs_hbm, stage, zeros, sem, fill_sem, *, tm, k_stage):
    t = pl.program_id(0)
    n_t = pl.num_programs(0)
    slot = t % 2

    def copy(s_chunk, d_chunk, size, sl):
        return pltpu.make_async_copy(stage.at[sl, _chunk_rows(s_chunk, size)],
                                     xs_hbm.at[_chunk_rows(d_chunk, size)], sem.at[sl])

    def wait_tile(tile, sl):
        _run_pieces(tot_ref[tile], lambda off, size: copy(0, 0, size, sl).wait(), WAIT_CHUNKS)

    @pl.when(t >= 2)
    def _():
        wait_tile(t - 2, slot)

    pos_t = pos_ref[0]
    j_i = lax.broadcasted_iota(I32, (k_stage, tm), 0)
    hit = j_i == pos_t[0:1]
    for k in range(1, TOP_K):
        hit = hit | (j_i == pos_t[k:k + 1])
    stage[slot] = _dot(jnp.where(hit, 1.0, 0.0).astype(BF16), h_ref[...])
    _for_each_chunk(where_ref, tot_ref, t, lambda s, d: copy(s, d, 1, slot).start())

    @pl.when(t == n_t - 1)
    def _():
        zeros[...] = jnp.zeros_like(zeros)

        def fill(d_chunk, size):
            return pltpu.make_async_copy(zeros.at[_chunk_rows(0, size)], xs_hbm.at[_chunk_rows(d_chunk, size)],
                                         fill_sem.at[0])

        def per_region(r, carry):
            _run_pieces(filln_ref[r], lambda off, size: fill(fill0_ref[r] + off, size).start(), COPY_CHUNKS)
            return carry

        lax.fori_loop(0, N_EXPERTS + 1, per_region, 0)

        @pl.when(t >= 1)
        def _():
            wait_tile(t - 1, 1 - slot)

        wait_tile(t, slot)

        def per_region_wait(r, carry):
            _run_pieces(filln_ref[r], lambda off, size: fill(0, size).wait(), COPY_CHUNKS)
            return carry

        lax.fori_loop(0, N_EXPERTS + 1, per_region_wait, 0)


def _combine_kernel(where_ref, tot_ref, pg_ref, x_ref, gp_ref, gs_ref, y_hbm,
                    op_ref, os_ref, stage, sem, *, n_pt, tm, k_stage):
    t = pl.program_id(0)
    n_t = pl.num_programs(0)
    slot = t % 2

    def copy(s_chunk, d_chunk, size, sl):
        return pltpu.make_async_copy(y_hbm.at[_chunk_rows(d_chunk, size)],
                                     stage.at[sl, _chunk_rows(s_chunk, size)], sem.at[sl])

    def fetch(tile, sl):
        _for_each_chunk(where_ref, tot_ref, tile, lambda s, d: copy(s, d, 1, sl).start())

    @pl.when(t == 0)
    def _():
        stage[...] = jnp.zeros_like(stage)
        fetch(0, 0)

    @pl.when(t + 1 < n_t)
    def _():
        fetch(t + 1, 1 - slot)

    _run_pieces(tot_ref[t], lambda off, size: copy(0, 0, size, slot).wait(), WAIT_CHUNKS)
    lane = lax.broadcasted_iota(I32, (tm, k_stage), 1)
    pg = pg_ref[...]
    p = jnp.zeros((tm, k_stage), F32)
    for k in range(TOP_K):
        p = jnp.where(lane == pg[:, k:k + 1].astype(I32), pg[:, TOP_K + k:TOP_K + k + 1], p)
    ffn = _dot(p.astype(BF16), stage[slot].astype(BF16))
    out = x_ref[...] + _pick_mod(t, n_pt, gp_ref, gs_ref) * ffn

    @pl.when(t < n_pt)
    def _():
        op_ref[...] = out

    @pl.when(t >= n_pt)
    def _():
        os_ref[...] = out


def _moe(geo, h2, pos, pg, nch_pad, soff_pad, x1, mod_p, mod_s, col_gate, layer, w_in, b_in, w_out, b_out):
    bm = MOE_BLOCK
    tm = geo.tm_moe
    n_tiles = geo.nt // tm
    depth, n_e, d, d2 = w_in.shape
    w_in = w_in.reshape(depth * n_e, d, d2)
    b_in = b_in.reshape(depth * n_e, d2)
    w_out = w_out.reshape(depth * n_e, d2 // 2, d)
    b_out = b_out.reshape(depth * n_e, d)
    chunks_per_block = bm // ROW_CHUNK
    max_rows = geo.nt * TOP_K + n_tiles * N_EXPERTS * (ROW_CHUNK - 1) + N_EXPERTS * (bm - 1)
    n_blocks = -(-max_rows // bm)
    nch = nch_pad[:, :N_EXPERTS, 0]
    soff = soff_pad[:, :N_EXPERTS, 0]
    tot = jnp.sum(nch, axis=1)
    per_e = jnp.sum(nch, axis=0)
    padded = (per_e + chunks_per_block - 1) // chunks_per_block * chunks_per_block
    pend = jnp.cumsum(padded)
    pstart = pend - padded
    dst = pstart[None, :] + jnp.cumsum(nch, axis=0) - nch
    first_sub = (pstart // chunks_per_block).astype(I32)
    n_sub = (padded // chunks_per_block).astype(I32)
    fill0 = jnp.concatenate([pstart + per_e, pend[-1:]]).astype(I32)
    filln = jnp.concatenate([padded - per_e, n_blocks * chunks_per_block - pend[-1:]]).astype(I32)

    k_stage = -(-(tm * TOP_K + N_EXPERTS * (ROW_CHUNK - 1)) // LANES) * LANES
    j = jnp.arange(k_stage // ROW_CHUNK, dtype=I32)
    owner = jnp.minimum(jnp.sum(j[None, :, None] >= (soff + nch)[:, None, :], axis=-1), N_EXPERTS - 1)
    shift = jnp.sum(jnp.where(owner[:, :, None] == jnp.arange(N_EXPERTS, dtype=I32), (dst - soff)[:, None, :], 0),
                    axis=-1)
    where = (j[None, :] + shift).astype(I32)
    row = lambda i, *_: (i, 0)
    xs = pl.pallas_call(
        functools.partial(_dispatch_kernel, tm=tm, k_stage=k_stage),
        grid_spec=pltpu.PrefetchScalarGridSpec(
            num_scalar_prefetch=4,
            grid=(n_tiles,),
            in_specs=[pl.BlockSpec((1, SUBLANES, tm), lambda i, *_: (i, 0, 0)), pl.BlockSpec((tm, d), row)],
            out_specs=pl.BlockSpec(memory_space=pl.ANY),
            scratch_shapes=[
                pltpu.VMEM((2, k_stage, d), F32),
                pltpu.VMEM((COPY_CHUNKS * ROW_CHUNK, d), F32),
                pltpu.SemaphoreType.DMA((2,)),
                pltpu.SemaphoreType.DMA((1,)),
            ],
        ),
        out_shape=jax.ShapeDtypeStruct((n_blocks * bm, d), F32),
        compiler_params=_cparams(("arbitrary",)),
        name="moe_dispatch",
    )(where, tot, fill0, filln, pos, h2)

    ys = _ffn(xs, first_sub, n_sub, layer, w_in, b_in, w_out, b_out)

    return pl.pallas_call(
        functools.partial(_combine_kernel, n_pt=geo.n_prompt // tm, tm=tm, k_stage=k_stage),
        grid_spec=pltpu.PrefetchScalarGridSpec(
            num_scalar_prefetch=2,
            grid=(n_tiles,),
            in_specs=[
                pl.BlockSpec((tm, LANES), row),
                pl.BlockSpec((tm, d), row),
                *_mod_specs(geo, col_gate, tm),
                pl.BlockSpec(memory_space=pl.ANY),
            ],
            out_specs=_pair_specs(geo, tm, d),
            scratch_shapes=[
                pltpu.VMEM((2, k_stage, d), F32),
                pltpu.SemaphoreType.DMA((2,)),
            ],
        ),
        out_shape=[jax.ShapeDtypeStruct((geo.n_prompt, d), F32), jax.ShapeDtypeStruct((geo.n_sample, d), F32)],
        compiler_params=_cparams(("arbitrary",)),
        name="moe_combine",
    )(where, tot, pg, x1, mod_p, mod_s, ys)


SW_QW = SW_Q_HEADS * SW_HEAD_DIM
SW_QKW = SW_QW + SW_KV_W
SW_KEYS = 2 * ATT_BLOCK


def _attn_proj_kernel(xp_ref, xs_ref, g_ref, shp_ref, shs_ref, scp_ref, scs_ref, w_ref, e_ref, et_ref, qg_ref,
                      kg_ref, q_ref, k_ref, v_ref, *, n_pt):
    i = pl.program_id(0)
    sh = _pick_mod(i, n_pt, shp_ref, shs_ref)
    sc = _pick_mod(i, n_pt, scp_ref, scs_ref)
    x = jnp.where(i >= n_pt, xs_ref[...], xp_ref[...])
    h = _norm_mod(x, g_ref[...], sc, sh).astype(BF16)
    tm = h.shape[0]
    n_parts = 2 if tm % (2 * SUBLANES) == 0 else 1
    part = tm // n_parts
    for r in range(n_parts):
        rows = slice(r * part, (r + 1) * part)
        z = _dot(h[rows], w_ref[...])
        qk = z[:, :SW_QKW]
        sq = qk * qk
        sq_hi = sq.astype(BF16)
        sq_lo = (sq - sq_hi.astype(F32)).astype(BF16)
        ms = (_dot(sq_hi, e_ref[...]) + _dot(sq_lo, e_ref[...])) * (1.0 / SW_HEAD_DIM)
        inv = lax.rsqrt(ms + NORM_EPS)
        inv_hi = inv.astype(BF16)
        inv_lo = (inv - inv_hi.astype(F32)).astype(BF16)
        qk = qk * (_dot(inv_hi, et_ref[...]) + _dot(inv_lo, et_ref[...]))
        q_ref[rows, :] = (qk[:, :SW_QW] * qg_ref[...] * SW_SCALE).astype(q_ref.dtype)
        k_ref[rows, :] = qk[:, SW_QW:] * kg_ref[...]
        v_ref[rows, :] = z[:, SW_QKW:]


def _attn_proj(geo, x_pair, g, mod_p, mod_s, w_bf16, q_gain, k_gain):
    tm = geo.tm
    n_out = w_bf16.shape[1]
    heads = SW_QKW // SW_HEAD_DIM
    member = (np.arange(SW_QKW)[:, None] // SW_HEAD_DIM == np.arange(LANES)[None, :]).astype(np.float32)
    e = jnp.asarray(member, BF16)
    et = jnp.asarray(member.T, BF16)
    assert heads <= LANES
    qg = jnp.tile(q_gain, SW_Q_HEADS).reshape(1, SW_QW)
    kg = jnp.tile(k_gain, SW_KV_HEADS).reshape(1, SW_KV_W)
    row = lambda i: (i, 0)
    fixed = lambda i: (0, 0)
    return pl.pallas_call(
        functools.partial(_attn_proj_kernel, n_pt=geo.n_pt),
        grid=(geo.n_tiles,),
        in_specs=[
            *_pair_specs(geo, tm, D_MODEL),
            pl.BlockSpec((1, D_MODEL), fixed),
            *_mod_specs(geo, 0, tm),
            *_mod_specs(geo, 1, tm),
            pl.BlockSpec((D_MODEL, n_out), fixed),
            pl.BlockSpec((SW_QKW, LANES), fixed),
            pl.BlockSpec((LANES, SW_QKW), fixed),
            pl.BlockSpec((1, SW_QW), fixed),
            pl.BlockSpec((1, SW_KV_W), fixed),
        ],
        out_specs=[
            pl.BlockSpec((tm, SW_QW), row),
            pl.BlockSpec((tm, SW_KV_W), row),
            pl.BlockSpec((tm, SW_KV_W), row),
        ],
        out_shape=[
            jax.ShapeDtypeStruct((geo.nt, SW_QW), BF16),
            jax.ShapeDtypeStruct((geo.nt, SW_KV_W), F32),
            jax.ShapeDtypeStruct((geo.nt, SW_KV_W), F32),
        ],
        compiler_params=_cparams(("arbitrary",)),
        name="attn_proj_qknorm",
    )(*x_pair, g, mod_p, mod_s, mod_p, mod_s, w_bf16, e, et, qg, kg)


def _rel_bucket_np(dist):
    n = np.maximum(dist, 0)
    max_exact = REL_BUCKETS // 2
    ratio = np.log(np.maximum(n, 1).astype(np.float32) / np.float32(max_exact)) / np.float32(
        math.log(REL_MAX_DIST / max_exact))
    large = max_exact + (ratio * np.float32(REL_BUCKETS - max_exact)).astype(np.int32)
    large = np.minimum(large, REL_BUCKETS - 1)
    return np.where(n < max_exact, n, large).astype(np.int32)


def _bucket_table(qpos, kpos, k_ok):
    dist = qpos[:, None] - kpos[None, :]
    ok = (dist >= 0) & (dist <= WINDOW) & k_ok[None, :]
    return np.where(ok, _rel_bucket_np(dist), -1).astype(np.int32)


def _build_bias(bkt_ref, rb_ref, bias_scr, rows):
    bkt = bkt_ref[...]
    base = jnp.where(bkt < 0, -jnp.inf, 0.0)
    for h in range(SW_Q_HEADS):
        bias_scr[h * rows:(h + 1) * rows, :] = base

    def add_bucket(j, carry):
        hit = bkt == j
        for h in range(SW_Q_HEADS):
            sl = slice(h * rows, (h + 1) * rows)
            bias_scr[sl, :] = bias_scr[sl, :] + jnp.where(hit, rb_ref[j, h], 0.0)
        return carry

    lax.fori_loop(0, REL_BUCKETS, add_bucket, 0)


def _sink_softmax_pv(s, sink, vv_g):
    m = jnp.maximum(jnp.max(s, axis=-1, keepdims=True), sink)
    e = jnp.exp(s - m)
    p = e / (jnp.sum(e, axis=-1, keepdims=True) + jnp.exp(sink - m))
    return _dot(p.astype(BF16), vv_g)


def _build_bias_t(bkt_ref, rb_ref, bias_scr, blk):
    bkt = bkt_ref[...]
    base = jnp.where(bkt < 0, -jnp.inf, 0.0)
    slots = [(g, j) for g in range(SW_KV_HEADS) for j in range(SW_GROUP)]
    for g, j in slots:
        bias_scr[g, :, j * blk:(j + 1) * blk] = base

    def add_bucket(b, carry):
        hit = bkt == b
        for g, j in slots:
            cols = slice(j * blk, (j + 1) * blk)
            bias_scr[g, :, cols] = bias_scr[g, :, cols] + jnp.where(hit, rb_ref[b, g * SW_GROUP + j], 0.0)
        return carry

    lax.fori_loop(0, REL_BUCKETS, add_bucket, 0)


def _swa_prompt_kernel(rb_ref, sink_ref, bkt_ref, q_ref, kp_ref, kc_ref, vp_ref, vc_ref, o_ref, bias_scr, *, qb):
    n = pl.program_id(1)
    blk = ATT_BLOCK
    hd = SW_HEAD_DIM

    @pl.when((pl.program_id(0) == 0) & (n == 0))
    def _():
        _build_bias_t(bkt_ref, rb_ref, bias_scr, blk)

    key_i = lax.broadcasted_iota(I32, (SW_KEYS, SW_GROUP * blk), 0)
    hide_prev = (n == 0) & (key_i < blk)
    zeros_q = jnp.zeros((hd, SW_GROUP * blk), BF16)
    sinks = [jnp.concatenate([jnp.full((1, blk), sink_ref[g * SW_GROUP + j], F32) for j in range(SW_GROUP)], axis=1)
             for g in range(SW_KV_HEADS)]
    for i in range(qb):
        rows = slice(i * blk, (i + 1) * blk)
        if i == 0:
            k2 = jnp.concatenate([kp_ref[...], kc_ref[rows, :]], axis=0)
            v2 = jnp.concatenate([vp_ref[...], vc_ref[rows, :]], axis=0)
        else:
            k2 = kc_ref[(i - 1) * blk:(i + 1) * blk, :]
            v2 = vc_ref[(i - 1) * blk:(i + 1) * blk, :]
        kk = k2.astype(BF16)
        vv_t = v2.T.astype(BF16)
        q_t = q_ref[rows, :].astype(F32).T.astype(BF16)
        outs = []
        for g in range(SW_KV_HEADS):
            heads = [g * SW_GROUP + j for j in range(SW_GROUP)]
            q_cat = jnp.concatenate([q_t[hq * hd:(hq + 1) * hd, :] for hq in heads], axis=1)
            q_full = jnp.concatenate([q_cat if gg == g else zeros_q for gg in range(SW_KV_HEADS)], axis=0)
            s = _dot(kk, q_full) + bias_scr[g]
            if i == 0:
                s = jnp.where(hide_prev, -jnp.inf, s)
            m = jnp.maximum(jnp.max(s, axis=0, keepdims=True), sinks[g])
            e = jnp.exp(s - m)
            den = jnp.sum(e, axis=0, keepdims=True) + jnp.exp(sinks[g] - m)
            o_t = _dot(vv_t[g * hd:(g + 1) * hd, :], e.astype(BF16)) * (1.0 / den)
            outs.extend(o_t[:, j * blk:(j + 1) * blk] for j in range(SW_GROUP))
        o_ref[rows, :] = jnp.concatenate(outs, axis=0).T.astype(o_ref.dtype)


def _swa_prompt(geo, q, k, v, rel_bias, sinks):
    blk = ATT_BLOCK
    assert geo.seq % blk == 0 and WINDOW == blk
    n_blk = geo.seq // blk
    qb = 4 if n_blk % 4 == 0 else (2 if n_blk % 2 == 0 else 1)
    nb = n_blk // qb
    bkt = jnp.asarray(_bucket_table(blk + np.arange(blk), np.arange(2 * blk), np.ones(2 * blk, bool)).T.copy())
    cur = lambda b, n: (b * nb + n, 0)
    prev = lambda b, n: (b * n_blk + jnp.maximum(n * qb - 1, 0), 0)
    smem = pl.BlockSpec(memory_space=pltpu.SMEM)
    return pl.pallas_call(
        functools.partial(_swa_prompt_kernel, qb=qb),
        grid=(geo.batch, nb),
        in_specs=[
            smem, smem,
            pl.BlockSpec((SW_KEYS, blk), lambda b, n: (0, 0)),
            pl.BlockSpec((qb * blk, SW_QW), cur),
            pl.BlockSpec((blk, SW_KV_W), prev),
            pl.BlockSpec((qb * blk, SW_KV_W), cur),
            pl.BlockSpec((blk, SW_KV_W), prev),
            pl.BlockSpec((qb * blk, SW_KV_W), cur),
        ],
        out_specs=pl.BlockSpec((qb * blk, SW_QW), cur),
        out_shape=jax.ShapeDtypeStruct((geo.n_prompt, SW_QW), BF16),
        scratch_shapes=[pltpu.VMEM((SW_KV_HEADS, SW_KEYS, SW_GROUP * blk), F32)],
        compiler_params=_cparams(("arbitrary", "arbitrary")),
        name="swa_prompt",
    )(rel_bias, sinks, bkt, q, k, k, v, v)


def _swa_step_kernel(rb_ref, sink_ref, bkt_ref, q_ref, kn_ref, vn_ref, ck_ref, cv_ref,
                     o_ref, nk_ref, nv_ref, bias_scr, *, steps, bb):
    pad = SUBLANES
    win = WINDOW

    @pl.when(pl.program_id(0) == 0)
    def _():
        _build_bias(bkt_ref, rb_ref, bias_scr, pad)

    fill = jnp.zeros((SW_KEYS - win - pad, SW_KV_W), F32)
    row8 = lax.broadcasted_iota(I32, (pad, SW_KV_W), 0)

    def shifted(cache, new):
        rolled = pltpu.roll(cache, win - steps, 0)
        tail = jnp.where(row8 < pad - steps, rolled[win - pad:], pltpu.roll(new, pad - steps, 0))
        return rolled[:win - pad], tail

    def body(s, carry):
        ck, cv, kn, vn = ck_ref[s], cv_ref[s], kn_ref[s], vn_ref[s]
        kk = jnp.concatenate([ck, kn, fill], axis=0).astype(BF16)
        vv = jnp.concatenate([cv, vn, fill], axis=0).astype(BF16)
        q = q_ref[s]
        for g in range(SW_KV_HEADS):
            kcols = slice(g * SW_HEAD_DIM, (g + 1) * SW_HEAD_DIM)
            heads = [g * SW_GROUP + j for j in range(SW_GROUP)]
            qs = jnp.concatenate([q[:, hq * SW_HEAD_DIM:(hq + 1) * SW_HEAD_DIM] for hq in heads], axis=0)
            sc = _dot_nt(qs.astype(BF16), kk[:, kcols]) + bias_scr[heads[0] * pad:(heads[-1] + 1) * pad, :]
            sink = jnp.concatenate([jnp.full((pad, 1), sink_ref[hq], F32) for hq in heads], axis=0)
            og = _sink_softmax_pv(sc, sink, vv[:, kcols])
            for j, hq in enumerate(heads):
                o_ref[s, :, hq * SW_HEAD_DIM:(hq + 1) * SW_HEAD_DIM] = og[j * pad:(j + 1) * pad]
        head, tail = shifted(ck, kn)
        nk_ref[s, 0:win - pad] = head
        nk_ref[s, win - pad:win] = tail
        head, tail = shifted(cv, vn)
        nv_ref[s, 0:win - pad] = head
        nv_ref[s, win - pad:win] = tail
        return carry

    lax.fori_loop(0, bb, body, 0, unroll=True)


def _swa_step(geo, q, k, v, cache_k, cache_v, rel_bias, sinks):
    steps, db = geo.dec_seq, geo.dec_batch
    win = cache_k.shape[1]
    assert win == WINDOW and steps <= SUBLANES
    bb = min(STEP_SEQS, db)
    pad = SUBLANES
    q_s = _seq_major(geo, q[geo.n_prompt:].astype(F32))
    k_s = _seq_major(geo, k[geo.n_prompt:])
    v_s = _seq_major(geo, v[geo.n_prompt:])
    kpos = np.arange(SW_KEYS)
    k_ok = kpos < win + steps
    qpos = win + np.arange(pad)
    bkt = _bucket_table(qpos, kpos, k_ok)
    bkt[steps:] = -1
    smem = pl.BlockSpec(memory_space=pltpu.SMEM)
    blk3 = lambda r, w: pl.BlockSpec((bb, r, w), lambda i: (i, 0, 0))
    o_s, nk, nv = pl.pallas_call(
        functools.partial(_swa_step_kernel, steps=steps, bb=bb),
        grid=(db // bb,),
        in_specs=[
            smem, smem,
            pl.BlockSpec((pad, SW_KEYS), lambda i: (0, 0)),
            blk3(pad, SW_QW), blk3(pad, SW_KV_W), blk3(pad, SW_KV_W),
            blk3(win, SW_KV_W), blk3(win, SW_KV_W),
        ],
        out_specs=[blk3(pad, SW_QW), blk3(win, SW_KV_W), blk3(win, SW_KV_W)],
        out_shape=[
            jax.ShapeDtypeStruct((db, pad, SW_QW), F32),
            jax.ShapeDtypeStruct((db, win, SW_KV_W), F32),
            jax.ShapeDtypeStruct((db, win, SW_KV_W), F32),
        ],
        scratch_shapes=[pltpu.VMEM((SW_Q_HEADS * pad, SW_KEYS), F32)],
        compiler_params=_cparams(("arbitrary",)),
        name="swa_step",
    )(rel_bias, sinks, jnp.asarray(bkt), q_s, k_s, v_s, cache_k, cache_v)
    return _time_major(geo, o_s), nk, nv


def kernel(x_prompt, x_sample, state_hgrn, cache_win_k, cache_win_v, c_prompt, c_sample, norm1_g, norm2_g, ada_w, ada_b, hg_w_in, hg_lb_table, hg_onorm_g, hg_w_out, sw_w_in, sw_qnorm_g, sw_knorm_g, sw_sinks, sw_w_out, rel_bias, router_w, router_b, moe_w_in, moe_b_in, moe_w_out, moe_b_out):
    batch, seq, d = x_prompt.shape
    db, steps, _ = x_sample.shape
    assert d == D_MODEL and ada_w.shape[0] == 2 and hg_w_in.shape[0] == 1 and sw_w_in.shape[0] == 1
    geo = _Geom(batch, seq, db, steps)
    x = (x_prompt.reshape(batch * seq, d), x_sample.transpose(1, 0, 2).reshape(steps * db, d))
    n_seq = batch + db
    rows = -(-n_seq // SUBLANES) * SUBLANES
    c_all = jnp.concatenate([c_prompt, c_sample, jnp.zeros((rows - n_seq, d), F32)], axis=0)
    mods = _ada_mods(c_all, ada_w, ada_b)

    def layer_mods(layer):
        mod_p = mods[layer, :batch].reshape(batch, 1, 6 * d)
        mod_s = jnp.tile(mods[layer, batch:n_seq], (steps, 1))
        return mod_p, mod_s

    def moe(layer, a_p, a_s, w_out, x_in, mod_p, mod_s):
        x1, h2, pos, pg, nch, soff = _post(geo, a_p, a_s, w_out.astype(BF16), x_in, norm2_g[layer:layer + 1],
                                           mod_p, mod_s, 0, router_w[layer], router_b[layer])
        return _moe(geo, h2, pos, pg, nch, soff, x1, mod_p, mod_s, 5,
                    layer, moe_w_in, moe_b_in, moe_w_out, moe_b_out)

    mod_p, mod_s = layer_mods(0)
    og = hg_onorm_g[0:1]
    z = _proj(geo, x, norm1_g[0:1], mod_p, mod_s, hg_w_in[0].astype(BF16), 0, 1)
    o_p, st_p = _gla_prompt(geo, z, hg_lb_table, og)
    o_s, st_s = _gla_step(geo, z, state_hgrn[0], hg_lb_table, og)
    x = moe(0, o_p, o_s, hg_w_out[0], x, mod_p, mod_s)

    mod_p, mod_s = layer_mods(1)
    q, k, v = _attn_proj(geo, x, norm1_g[1:2], mod_p, mod_s, sw_w_in[0].astype(BF16), sw_qnorm_g[0], sw_knorm_g[0])
    win = cache_win_k.shape[2]
    a_p = _swa_prompt(geo, q, k, v, rel_bias, sw_sinks[0])
    a_s, nk, nv = _swa_step(geo, q, k, v, cache_win_k[0].reshape(db, win, SW_KV_W),
                            cache_win_v[0].reshape(db, win, SW_KV_W), rel_bias, sw_sinks[0])
    x = moe(1, a_p, a_s, sw_w_out[0], x, mod_p, mod_s)

    y_prompt = x[0].reshape(batch, seq, d)
    y_sample = x[1].reshape(steps, db, d).transpose(1, 0, 2)
    kv_shape = (1, batch, WINDOW, SW_KV_HEADS, SW_HEAD_DIM)
    last_window = lambda a: jnp.stack([a[(b + 1) * seq - WINDOW:(b + 1) * seq] for b in range(batch)])
    k_p = last_window(k).reshape(kv_shape)
    v_p = last_window(v).reshape(kv_shape)
    cache_shape = (1, db, win, SW_KV_HEADS, SW_HEAD_DIM)
    return (y_prompt, y_sample, jnp.swapaxes(st_p, -1, -2)[None], st_s[None], k_p, v_p,
            nk.reshape(cache_shape), nv.reshape(cache_shape))
```

```python
import functools
import math

import numpy as np
import jax
import jax.numpy as jnp
from jax import lax
from jax.experimental import pallas as pl
from jax.experimental.pallas import tpu as pltpu

F32 = jnp.float32
BF16 = jnp.bfloat16
I32 = jnp.int32

D_MODEL = 1024
LANES = 128
SUBLANES = 8
TOKEN_TILE = 512
MOE_TILE = 256
GLA_STEP_TOKENS = 256
STEP_SEQS = 8
HG_DK = 128
HG_HEADS = D_MODEL // HG_DK
HG_DV = D_MODEL // HG_HEADS
HG_CHUNK = 64
SW_HEAD_DIM = 64
SW_Q_HEADS = D_MODEL // SW_HEAD_DIM
SW_KV_HEADS = 4
SW_GROUP = SW_Q_HEADS // SW_KV_HEADS
SW_KV_W = SW_KV_HEADS * SW_HEAD_DIM
WINDOW = 128
ATT_BLOCK = 128
SW_SCALE = SW_HEAD_DIM ** -0.5
REL_BUCKETS = 32
REL_MAX_DIST = 128
N_EXPERTS = 32
TOP_K = 4
SWIGLU_LIMIT = 7.0
SWIGLU_ALPHA = 1.702
NORM_EPS = 1e-5
MOE_BLOCK = 128
ROW_CHUNK = SUBLANES
COPY_CHUNKS = 4
CHUNK_UNROLL = 8
WAIT_CHUNKS = 16
VMEM_LIMIT = 56 * 1024 * 1024


def _cparams(sem):
    return pltpu.CompilerParams(dimension_semantics=sem, vmem_limit_bytes=VMEM_LIMIT)


def _sigmoid(x):
    return 0.5 * jnp.tanh(0.5 * x) + 0.5


def _silu(x):
    return x * _sigmoid(x)


def _dot(a, b):
    return jnp.dot(a, b, preferred_element_type=F32)


def _dot_nt(a, b):
    return lax.dot_general(a, b, (((1,), (1,)), ((), ())), preferred_element_type=F32)


def _dot_tn(a, b):
    return lax.dot_general(a, b, (((0,), (0,)), ((), ())), preferred_element_type=F32)


def _split3(x):
    hi = x.astype(BF16)
    r = x - hi.astype(F32)
    mid = r.astype(BF16)
    lo = (r - mid.astype(F32)).astype(BF16)
    return hi, mid, lo


def _ada_kernel(c_ref, w_ref, b_ref, o_ref):
    s = _silu(c_ref[...]).astype(BF16)
    o_ref[0] = _dot(s, w_ref[0].astype(BF16)) + b_ref[0]


def _ada_mods(c_all, ada_w, ada_b):
    depth, d, n6 = ada_w.shape
    rows = c_all.shape[0]
    tn = 1536
    return pl.pallas_call(
        _ada_kernel,
        grid=(depth, n6 // tn),
        in_specs=[
            pl.BlockSpec((rows, d), lambda l, j: (0, 0)),
            pl.BlockSpec((1, d, tn), lambda l, j: (l, 0, j)),
            pl.BlockSpec((1, 1, tn), lambda l, j: (l, 0, j)),
        ],
        out_specs=pl.BlockSpec((1, rows, tn), lambda l, j: (l, 0, j)),
        out_shape=jax.ShapeDtypeStruct((depth, rows, n6), F32),
        compiler_params=_cparams(("arbitrary", "arbitrary")),
        name="ada_mods",
    )(c_all, ada_w, ada_b.reshape(depth, 1, n6))


class _Geom:
    def __init__(self, batch, seq, dec_batch, dec_seq):
        self.batch, self.seq, self.dec_batch, self.dec_seq = batch, seq, dec_batch, dec_seq
        self.n_prompt = batch * seq
        self.n_sample = dec_batch * dec_seq
        self.nt = self.n_prompt + self.n_sample
        tm = TOKEN_TILE
        while seq % tm or self.n_sample % tm:
            tm //= 2
        assert tm >= 8
        self.tm = tm
        self.n_pt = self.n_prompt // tm
        self.n_tiles = self.nt // tm
        self.tm_moe = min(MOE_TILE, tm)


def _mod_specs(geo, col, tm):
    seq, batch = geo.seq, geo.batch
    n_pt = geo.n_prompt // tm

    def p_map(i, *_):
        return (jnp.minimum(i * tm // seq, batch - 1), 0, col)

    def s_map(i, *_):
        return (jnp.maximum(i - n_pt, 0), col)

    return [pl.BlockSpec((1, 1, D_MODEL), p_map), pl.BlockSpec((tm, D_MODEL), s_map)]


def _pick_mod(i, n_pt, p_ref, s_ref):
    return jnp.where(i >= n_pt, s_ref[...], p_ref[0])


def _norm_mod(x, g, sc, sh):
    ms = jnp.mean(x * x, axis=-1, keepdims=True)
    return x * lax.rsqrt(ms + NORM_EPS) * g * (1.0 + sc) + sh


def _pair_specs(geo, tm, width):
    n_pt = geo.n_prompt // tm
    return [pl.BlockSpec((tm, width), lambda i, *_: (jnp.minimum(i, n_pt - 1), 0)),
            pl.BlockSpec((tm, width), lambda i, *_: (jnp.maximum(i - n_pt, 0), 0))]


def _proj_kernel(xp_ref, xs_ref, g_ref, shp_ref, shs_ref, scp_ref, scs_ref, w_ref, o_ref, *, n_pt, tn):
    i = pl.program_id(0)
    sh = _pick_mod(i, n_pt, shp_ref, shs_ref)
    sc = _pick_mod(i, n_pt, scp_ref, scs_ref)
    x = jnp.where(i >= n_pt, xs_ref[...], xp_ref[...])
    h = _norm_mod(x, g_ref[...], sc, sh).astype(BF16)
    for j in range(o_ref.shape[1] // tn):
        o_ref[:, j * tn:(j + 1) * tn] = _dot(h, w_ref[:, j * tn:(j + 1) * tn])


def _proj(geo, x_pair, g, mod_p, mod_s, w_bf16, col_shift, col_scale):
    n_out = w_bf16.shape[1]
    tm = geo.tm
    fixed = lambda i: (0, 0)
    return pl.pallas_call(
        functools.partial(_proj_kernel, n_pt=geo.n_pt, tn=1024),
        grid=(geo.n_tiles,),
        in_specs=[
            *_pair_specs(geo, tm, D_MODEL),
            pl.BlockSpec((1, D_MODEL), fixed),
            *_mod_specs(geo, col_shift, tm),
            *_mod_specs(geo, col_scale, tm),
            pl.BlockSpec((D_MODEL, n_out), fixed),
        ],
        out_specs=pl.BlockSpec((tm, n_out), lambda i: (i, 0)),
        out_shape=jax.ShapeDtypeStruct((geo.nt, n_out), F32),
        compiler_params=_cparams(("arbitrary",)),
        name="norm_mod_proj",
    )(*x_pair, g, mod_p, mod_s, mod_p, mod_s, w_bf16)


def _hg_lower_bound(lbt_ref):
    t = lbt_ref[...]
    e = jnp.exp(t - jnp.max(t, axis=0, keepdims=True))
    return e[0:1] / jnp.sum(e, axis=0, keepdims=True)


def _hg_gates(fz, lb):
    t = jnp.tanh(0.5 * fz)
    logf = jnp.log(lb + (1.0 - lb) * (0.5 + 0.5 * t))
    return logf, (1.0 - lb) * (0.5 - 0.5 * t)


def _hg_out(o, gz, og):
    ms = jnp.mean(o * o, axis=-1, keepdims=True)
    return o * lax.rsqrt(ms + NORM_EPS) * og * _silu(gz)


def _gla_prompt_kernel(*refs, batch, chunk, n_chunks):
    z_refs = refs[:batch]
    lbt_ref, og_ref, o_ref, sfin_ref, st_scr = refs[batch:]
    t_step = pl.program_id(0)
    kw = HG_HEADS * HG_DK

    @pl.when(t_step == 0)
    def _():
        st_scr[...] = jnp.zeros_like(st_scr)

    lb = _hg_lower_bound(lbt_ref)
    og = og_ref[...]
    r_i = lax.broadcasted_iota(I32, (chunk, chunk), 0)
    c_i = lax.broadcasted_iota(I32, (chunk, chunk), 1)
    causal = c_i <= r_i
    tri = causal.astype(BF16)
    mid = chunk // 2 - 1

    def body(c, carry):
        rows = pl.ds(pl.multiple_of(c * chunk, chunk), chunk)
        for bi, z_ref in enumerate(z_refs):
            logf, kk = _hg_gates(z_ref[rows, kw:2 * kw], lb)
            hi, md, lo = _split3(logf)
            cs = _dot(tri, jnp.concatenate([hi, md, lo], axis=1))
            b = cs[:, :kw] + cs[:, kw:2 * kw] + cs[:, 2 * kw:]
            b_mid = b[mid:mid + 1]
            b_last = b[chunk - 1:chunk]
            q_hat = _silu(z_ref[rows, 0:kw]) * jnp.exp(b - b_mid)
            k_hat = kk * jnp.exp(b_mid - b)
            q_in = (q_hat * jnp.exp(b_mid)).astype(BF16)
            k_dec = (k_hat * jnp.exp(b_last - b_mid)).astype(BF16)
            q_hat = q_hat.astype(BF16)
            k_hat = k_hat.astype(BF16)
            dec = jnp.exp(b_last)
            for h in range(HG_HEADS):
                cols = slice(h * HG_DK, (h + 1) * HG_DK)
                vcols = slice(2 * kw + h * HG_DV, 2 * kw + (h + 1) * HG_DV)
                gcols = slice(2 * kw + HG_HEADS * HG_DV + h * HG_DV, 2 * kw + HG_HEADS * HG_DV + (h + 1) * HG_DV)
                v = z_ref[rows, vcols].astype(BF16)
                att = jnp.where(causal, _dot_nt(q_hat[:, cols], k_hat[:, cols]), 0.0).astype(BF16)
                st = st_scr[bi, h]
                o = _dot(att, v) + _dot_nt(q_in[:, cols], st.astype(BF16))
                st_scr[bi, h] = st * dec[:, cols] + _dot_tn(v, k_dec[:, cols])
                o_ref[bi, rows, h * HG_DV:(h + 1) * HG_DV] = _hg_out(o, z_ref[rows, gcols], og).astype(o_ref.dtype)
        return carry

    lax.fori_loop(0, n_chunks, body, 0, unroll=True)

    @pl.when(t_step == pl.num_programs(0) - 1)
    def _():
        sfin_ref[...] = st_scr[...]


def _gla_prompt(geo, z, lb_table, o_gain):
    tg = min(GLA_STEP_TOKENS, geo.seq)
    chunk = HG_CHUNK if geo.seq % HG_CHUNK == 0 else geo.seq
    assert tg % chunk == 0 and geo.seq % tg == 0
    nt = geo.seq // tg
    batch = geo.batch
    fixed = lambda t: (0, 0)
    state_shape = (batch, HG_HEADS, HG_DV, HG_DK)
    o, st = pl.pallas_call(
        functools.partial(_gla_prompt_kernel, batch=batch, chunk=chunk, n_chunks=tg // chunk),
        grid=(nt,),
        in_specs=[
            *[pl.BlockSpec((tg, 4 * D_MODEL), functools.partial(lambda t, b: (b * nt + t, 0), b=b))
              for b in range(batch)],
            pl.BlockSpec(lb_table.shape, fixed),
            pl.BlockSpec((1, HG_DV), fixed),
        ],
        out_specs=[
            pl.BlockSpec((batch, tg, D_MODEL), lambda t: (0, t, 0)),
            pl.BlockSpec(state_shape, lambda t: (0, 0, 0, 0)),
        ],
        out_shape=[
            jax.ShapeDtypeStruct((batch, geo.seq, D_MODEL), BF16),
            jax.ShapeDtypeStruct(state_shape, F32),
        ],
        scratch_shapes=[pltpu.VMEM(state_shape, F32)],
        compiler_params=_cparams(("arbitrary",)),
        name="gla_prompt",
    )(*([z] * batch), lb_table, o_gain)
    return o.reshape(geo.n_prompt, D_MODEL), st


def _gla_step_kernel(z_ref, s_ref, lbt_ref, og_ref, o_ref, snew_ref, *, steps, bb):
    kw = HG_HEADS * HG_DK
    pad = SUBLANES
    lb = _hg_lower_bound(lbt_ref)
    og = og_ref[...]
    r_i = lax.broadcasted_iota(I32, (pad, pad), 0)
    c_i = lax.broadcasted_iota(I32, (pad, pad), 1)
    causal = c_i <= r_i
    tri = causal.astype(BF16)
    row_w = lax.broadcasted_iota(I32, (pad, kw), 0)
    live = row_w < steps
    row_k = lax.broadcasted_iota(I32, (pad, HG_DK), 0)
    ones_sel = jnp.where((row_k == steps) | (row_k == steps + 1), 1.0, 0.0).astype(BF16)

    def body(s, carry):
        z = z_ref[s]
        logf, kk = _hg_gates(z[:, kw:2 * kw], lb)
        hi, md, lo = _split3(jnp.where(live, logf, 0.0))
        cs = _dot(tri, jnp.concatenate([hi, md, lo], axis=1))
        b = cs[:, :kw] + cs[:, kw:2 * kw] + cs[:, 2 * kw:]
        b_last = b[steps - 1:steps]
        q_in = (_silu(z[:, 0:kw]) * jnp.exp(b)).astype(BF16)
        k_hat = jnp.where(live, kk * jnp.exp(-b), 0.0).astype(BF16)
        k_dec = jnp.where(live, kk * jnp.exp(b_last - b), 0.0).astype(BF16)
        dec = jnp.exp(b_last)
        d_hi = dec.astype(BF16)
        d_lo = (dec - d_hi.astype(F32)).astype(BF16)
        a_all = jnp.where(row_w == steps, d_hi, jnp.where(row_w == steps + 1, d_lo, k_dec))
        for h in range(HG_HEADS):
            cols = slice(h * HG_DK, (h + 1) * HG_DK)
            vcols = slice(2 * kw + h * HG_DV, 2 * kw + (h + 1) * HG_DV)
            gcols = slice(2 * kw + HG_HEADS * HG_DV + h * HG_DV, 2 * kw + HG_HEADS * HG_DV + (h + 1) * HG_DV)
            v = z[:, vcols].astype(BF16)
            s0 = s_ref[s, h]
            att = jnp.where(causal, _dot_nt(q_in[:, cols], k_hat[:, cols]), 0.0).astype(BF16)
            o = _dot(att, v) + _dot(q_in[:, cols], s0.astype(BF16))
            upd = _dot_tn(a_all[:, cols], jnp.concatenate([v, ones_sel], axis=1))
            snew_ref[s, h] = upd[:, HG_DV:] * s0 + upd[:, :HG_DV]
            o_ref[s, :, h * HG_DV:(h + 1) * HG_DV] = _hg_out(o, z[:, gcols], og)
        return carry

    lax.fori_loop(0, bb, body, 0, unroll=True)


def _seq_major(geo, rows):
    steps, db = geo.dec_seq, geo.dec_batch
    w = rows.shape[-1]
    r = rows.reshape(steps, db, w).transpose(1, 0, 2)
    return jnp.concatenate([r, jnp.zeros((db, SUBLANES - steps, w), rows.dtype)], axis=1)


def _time_major(geo, r):
    steps, db = geo.dec_seq, geo.dec_batch
    return r[:, :steps].transpose(1, 0, 2).reshape(steps * db, r.shape[-1])


def _gla_step(geo, z, state, lb_table, o_gain):
    steps, db = geo.dec_seq, geo.dec_batch
    assert steps + 2 <= SUBLANES
    bb = min(STEP_SEQS, db)
    z_s = _seq_major(geo, z[geo.n_prompt:])
    o_s, s_new = pl.pallas_call(
        functools.partial(_gla_step_kernel, steps=steps, bb=bb),
        grid=(db // bb,),
        in_specs=[
            pl.BlockSpec((bb, SUBLANES, 4 * D_MODEL), lambda i: (i, 0, 0)),
            pl.BlockSpec((bb, HG_HEADS, HG_DK, HG_DV), lambda i: (i, 0, 0, 0)),
            pl.BlockSpec(lb_table.shape, lambda i: (0, 0)),
            pl.BlockSpec((1, HG_DV), lambda i: (0, 0)),
        ],
        out_specs=[
            pl.BlockSpec((bb, SUBLANES, D_MODEL), lambda i: (i, 0, 0)),
            pl.BlockSpec((bb, HG_HEADS, HG_DK, HG_DV), lambda i: (i, 0, 0, 0)),
        ],
        out_shape=[
            jax.ShapeDtypeStruct((db, SUBLANES, D_MODEL), F32),
            jax.ShapeDtypeStruct(state.shape, F32),
        ],
        compiler_params=_cparams(("arbitrary",)),
        name="gla_step",
    )(z_s, state, lb_table, o_gain)
    return _time_major(geo, o_s), s_new


def _post_kernel(ap_ref, as_ref, wo_ref, xp_ref, xs_ref, gp_ref, gs_ref, n2_ref, shp_ref, shs_ref, scp_ref,
                 scs_ref, rw_ref, rb_ref, x1_ref, h2_ref, pos_ref, pg_ref, nch_ref, soff_ref, *, n_pt, tm):
    i = pl.program_id(0)
    g1 = _pick_mod(i, n_pt, gp_ref, gs_ref)
    a = jnp.where(i >= n_pt, as_ref[...], ap_ref[...])
    x = jnp.where(i >= n_pt, xs_ref[...], xp_ref[...])
    x1 = x + g1 * _dot(a, wo_ref[...])
    x1_ref[...] = x1
    sh = _pick_mod(i, n_pt, shp_ref, shs_ref)
    sc = _pick_mod(i, n_pt, scp_ref, scs_ref)
    h2 = _norm_mod(x1, n2_ref[...], sc, sh)
    h2_ref[...] = h2.astype(h2_ref.dtype)

    row_e = lax.broadcasted_iota(I32, (LANES, tm), 0)
    logits = _dot_nt(rw_ref[...], h2.astype(BF16)) + rb_ref[...]
    work = jnp.where(row_e < N_EXPERTS, logits, -jnp.inf)
    vals, hits = [], []
    for _ in range(TOP_K):
        m = jnp.max(work, axis=0, keepdims=True)
        idx = jnp.min(jnp.where(work == m, row_e, LANES), axis=0, keepdims=True)
        hit = row_e == idx
        vals.append(m)
        hits.append(hit)
        work = jnp.where(hit, -jnp.inf, work)
    exps = [jnp.exp(v - vals[0]) for v in vals]
    den = exps[0]
    for e in exps[1:]:
        den = den + e
    any_hit = hits[0]
    for hmask in hits[1:]:
        any_hit = any_hit | hmask
    any_f = jnp.where(any_hit, 1.0, 0.0)
    t_i = lax.broadcasted_iota(I32, (tm, tm), 0)
    t_j = lax.broadcasted_iota(I32, (tm, tm), 1)
    rank = _dot(any_f.astype(BF16), (t_i < t_j).astype(BF16))
    n_chunk = jnp.floor((jnp.sum(any_f, axis=1, keepdims=True) + (ROW_CHUNK - 1)) * (1.0 / ROW_CHUNK))
    e_i = lax.broadcasted_iota(I32, (LANES, LANES), 0)
    e_j = lax.broadcasted_iota(I32, (LANES, LANES), 1)
    seg = _dot((e_j < e_i).astype(BF16), jnp.broadcast_to(n_chunk, (LANES, LANES)).astype(BF16))[:, 0:1]
    where_to = seg * float(ROW_CHUNK) + rank
    pos_rows = [jnp.sum(jnp.where(hits[k], where_to, 0.0), axis=0, keepdims=True) for k in range(TOP_K)]
    gate_rows = [exps[k] / den for k in range(TOP_K)]
    pos_ref[0] = jnp.concatenate(pos_rows + [jnp.zeros((SUBLANES - TOP_K, tm), F32)], axis=0).astype(I32)
    rows = jnp.concatenate(pos_rows + gate_rows + [jnp.zeros((LANES - 2 * TOP_K, tm), F32)], axis=0)
    pg_ref[...] = rows.T
    nch_ref[0] = n_chunk.astype(I32)
    soff_ref[0] = seg.astype(I32)


def _post(geo, a_p, a_s, w_out_bf16, x_pair, norm_g, mod_p, mod_s, col0, router_w, router_b):
    tm = geo.tm_moe
    n_pt = geo.n_prompt // tm
    n_tiles = geo.nt // tm
    rw = jnp.zeros((LANES, D_MODEL), BF16).at[:N_EXPERTS].set(router_w.T.astype(BF16))
    rb = jnp.zeros((LANES, 1), F32).at[:N_EXPERTS, 0].set(router_b)
    row = lambda i: (i, 0)
    fixed = lambda i: (0, 0)
    tile_row = lambda i: (i, 0, 0)
    return pl.pallas_call(
        functools.partial(_post_kernel, n_pt=n_pt, tm=tm),
        grid=(n_tiles,),
        in_specs=[
            *_pair_specs(geo, tm, D_MODEL),
            pl.BlockSpec((D_MODEL, D_MODEL), fixed),
            *_pair_specs(geo, tm, D_MODEL),
            *_mod_specs(geo, col0 + 2, tm),
            pl.BlockSpec((1, D_MODEL), fixed),
            *_mod_specs(geo, col0 + 3, tm),
            *_mod_specs(geo, col0 + 4, tm),
            pl.BlockSpec((LANES, D_MODEL), fixed),
            pl.BlockSpec((LANES, 1), fixed),
        ],
        out_specs=[
            pl.BlockSpec((tm, D_MODEL), row),
            pl.BlockSpec((tm, D_MODEL), row),
            pl.BlockSpec((1, SUBLANES, tm), tile_row),
            pl.BlockSpec((tm, LANES), row),
            pl.BlockSpec((1, LANES, 1), tile_row),
            pl.BlockSpec((1, LANES, 1), tile_row),
        ],
        out_shape=[
            jax.ShapeDtypeStruct((geo.nt, D_MODEL), F32),
            jax.ShapeDtypeStruct((geo.nt, D_MODEL), BF16),
            jax.ShapeDtypeStruct((n_tiles, SUBLANES, tm), I32),
            jax.ShapeDtypeStruct((geo.nt, LANES), F32),
            jax.ShapeDtypeStruct((n_tiles, LANES, 1), I32),
            jax.ShapeDtypeStruct((n_tiles, LANES, 1), I32),
        ],
        compiler_params=_cparams(("arbitrary",)),
        name="post_mixer_router",
    )(a_p, a_s.astype(BF16), w_out_bf16, *x_pair, mod_p, mod_s, norm_g, mod_p, mod_s, mod_p, mod_s, rw, rb)


def _ffn_kernel(first_ref, count_ref, xs_hbm, win_ref, bin_ref, wout_ref, bout_ref, ys_hbm,
                xbuf, ybuf, win_scr, wout_scr, xsem, ysem, *, half):
    e = pl.program_id(0)
    first = first_ref[e]
    n_sub = count_ref[e]
    pairs = n_sub // 2
    odd = n_sub - 2 * pairs
    win_scr[...] = win_ref[0].astype(BF16)
    wout_scr[...] = wout_ref[0].astype(BF16)

    def rows_at(sub, n_rows):
        return pl.ds(pl.multiple_of((first + sub) * half, half), n_rows)

    def x_copy(sub, n_rows, slot):
        return pltpu.make_async_copy(xs_hbm.at[rows_at(sub, n_rows)], xbuf.at[slot, pl.ds(0, n_rows)], xsem.at[slot])

    def y_copy(sub, n_rows, slot):
        return pltpu.make_async_copy(ybuf.at[slot, pl.ds(0, n_rows)], ys_hbm.at[rows_at(sub, n_rows)], ysem.at[slot])

    def ffn(slot, rows):
        gu = _dot(xbuf[slot, rows, :].astype(BF16), win_scr[...]) + bin_ref[0]
        gate = jnp.minimum(gu[:, :D_MODEL], SWIGLU_LIMIT)
        up = jnp.clip(gu[:, D_MODEL:], -SWIGLU_LIMIT, SWIGLU_LIMIT)
        act = gate * _sigmoid(SWIGLU_ALPHA * gate) * (up + 1.0)
        return _dot(act.astype(BF16), wout_scr[...]) + bout_ref[0]

    def start_first(expert):
        begin = pl.multiple_of(first_ref[expert] * half, half)
        n = count_ref[expert]

        @pl.when(n >= 2)
        def _():
            pltpu.make_async_copy(xs_hbm.at[pl.ds(begin, 2 * half)], xbuf.at[0], xsem.at[0]).start()

        @pl.when(n == 1)
        def _():
            pltpu.make_async_copy(xs_hbm.at[pl.ds(begin, half)], xbuf.at[0, pl.ds(0, half)], xsem.at[0]).start()

    @pl.when(e == 0)
    def _():
        start_first(0)

    def unit(u, carry):
        slot = u % 2
        x_copy(2 * u, 2 * half, slot).wait()

        @pl.when(u + 1 < pairs)
        def _():
            x_copy(2 * (u + 1), 2 * half, 1 - slot).start()

        @pl.when((u + 1 == pairs) & (odd == 1))
        def _():
            x_copy(2 * pairs, half, 1 - slot).start()

        @pl.when(u >= 2)
        def _():
            y_copy(0, 2 * half, slot).wait()

        ybuf[slot] = ffn(slot, slice(0, 2 * half))
        y_copy(2 * u, 2 * half, slot).start()
        return carry

    lax.fori_loop(0, pairs, unit, 0)
    tail_slot = pairs % 2

    @pl.when(odd == 1)
    def _():
        x_copy(2 * pairs, half, tail_slot).wait()

        @pl.when(pairs >= 2)
        def _():
            y_copy(0, 2 * half, tail_slot).wait()

        ybuf[tail_slot, 0:half, :] = ffn(tail_slot, slice(0, half))
        y_copy(2 * pairs, half, tail_slot).start()

    @pl.when(e + 1 < pl.num_programs(0))
    def _():
        start_first(e + 1)

    @pl.when(odd == 1)
    def _():
        @pl.when(pairs >= 1)
        def _():
            y_copy(0, 2 * half, 1 - tail_slot).wait()

        y_copy(0, half, tail_slot).wait()

    @pl.when(odd == 0)
    def _():
        @pl.when(pairs >= 2)
        def _():
            y_copy(0, 2 * half, tail_slot).wait()

        @pl.when(pairs >= 1)
        def _():
            y_copy(0, 2 * half, 1 - tail_slot).wait()


def _ffn(xs, first_sub, n_sub, layer, w_in, b_in, w_out, b_out):
    n_e, d, d2 = w_in.shape
    half = MOE_BLOCK
    w_map = lambda e, *_: (layer * N_EXPERTS + e, 0, 0)
    return pl.pallas_call(
        functools.partial(_ffn_kernel, half=half),
        grid_spec=pltpu.PrefetchScalarGridSpec(
            num_scalar_prefetch=2,
            grid=(N_EXPERTS,),
            in_specs=[
                pl.BlockSpec(memory_space=pl.ANY),
                pl.BlockSpec((1, d, d2), w_map),
                pl.BlockSpec((1, 1, d2), w_map),
                pl.BlockSpec((1, d2 // 2, d), w_map),
                pl.BlockSpec((1, 1, d), w_map),
            ],
            out_specs=pl.BlockSpec(memory_space=pl.ANY),
            scratch_shapes=[
                pltpu.VMEM((2, 2 * half, d), F32),
                pltpu.VMEM((2, 2 * half, d), F32),
                pltpu.VMEM((d, d2), BF16),
                pltpu.VMEM((d2 // 2, d), BF16),
                pltpu.SemaphoreType.DMA((2,)),
                pltpu.SemaphoreType.DMA((2,)),
            ],
        ),
        out_shape=jax.ShapeDtypeStruct(xs.shape, xs.dtype),
        input_output_aliases={2: 0},
        compiler_params=_cparams(("arbitrary",)),
        name="moe_ffn",
    )(first_sub, n_sub, xs, w_in, b_in.reshape(n_e, 1, d2), w_out, b_out.reshape(n_e, 1, d))


def _chunk_rows(chunk, n_chunks=1):
    return pl.ds(pl.multiple_of(chunk * ROW_CHUNK, ROW_CHUNK), n_chunks * ROW_CHUNK)


def _run_pieces(n, fn, big):
    n_big = n // big

    def big_piece(i, carry):
        fn(i * big, big)
        return carry

    def small_piece(i, carry):
        fn(n_big * big + i, 1)
        return carry

    lax.fori_loop(0, n_big, big_piece, 0)
    lax.fori_loop(0, n - n_big * big, small_piece, 0)


def _for_each_chunk(where_ref, tot_ref, tile, fn):
    n = tot_ref[tile]
    n_groups = n // CHUNK_UNROLL

    def group(i, carry):
        for u in range(CHUNK_UNROLL):
            j = i * CHUNK_UNROLL + u
            fn(j, where_ref[tile, j])
        return carry

    def single(i, carry):
        j = n_groups * CHUNK_UNROLL + i
        fn(j, where_ref[tile, j])
        return carry

    lax.fori_loop(0, n_groups, group, 0)
    lax.fori_loop(0, n - n_groups * CHUNK_UNROLL, single, 0)


def _dispatch_kernel(where_ref, tot_ref, fill0_ref, filln_ref,
                     pos_ref, h_ref, xs_hbm, stage, zeros, sem, fill_sem, *, tm, k_stage):
    t = pl.program_id(0)
    n_t = pl.num_programs(0)
    slot = t % 2

    def copy(s_chunk, d_chunk, size, sl):
        return pltpu.make_async_copy(stage.at[sl, _chunk_rows(s_chunk, size)],
                                     xs_hbm.at[_chunk_rows(d_chunk, size)], sem.at[sl])

    def wait_tile(tile, sl):
        _run_pieces(tot_ref[tile], lambda off, size: copy(0, 0, size, sl).wait(), WAIT_CHUNKS)

    @pl.when(t >= 2)
    def _():
        wait_tile(t - 2, slot)

    pos_t = pos_ref[0]
    j_i = lax.broadcasted_iota(I32, (k_stage, tm), 0)
    hit = j_i == pos_t[0:1]
    for k in range(1, TOP_K):
        hit = hit | (j_i == pos_t[k:k + 1])
    stage[slot] = _dot(jnp.where(hit, 1.0, 0.0).astype(BF16), h_ref[...])
    _for_each_chunk(where_ref, tot_ref, t, lambda s, d: copy(s, d, 1, slot).start())

    @pl.when(t == n_t - 1)
    def _():
        zeros[...] = jnp.zeros_like(zeros)

        def fill(d_chunk, size):
            return pltpu.make_async_copy(zeros.at[_chunk_rows(0, size)], xs_hbm.at[_chunk_rows(d_chunk, size)],
                                         fill_sem.at[0])

        def per_region(r, carry):
            _run_pieces(filln_ref[r], lambda off, size: fill(fill0_ref[r] + off, size).start(), COPY_CHUNKS)
            return carry

        lax.fori_loop(0, N_EXPERTS + 1, per_region, 0)

        @pl.when(t >= 1)
        def _():
            wait_tile(t - 1, 1 - slot)

        wait_tile(t, slot)

        def per_region_wait(r, carry):
            _run_pieces(filln_ref[r], lambda off, size: fill(0, size).wait(), COPY_CHUNKS)
            return carry

        lax.fori_loop(0, N_EXPERTS + 1, per_region_wait, 0)


def _combine_kernel(where_ref, tot_ref, pg_ref, x_ref, gp_ref, gs_ref, y_hbm,
                    op_ref, os_ref, stage, sem, *, n_pt, tm, k_stage):
    t = pl.program_id(0)
    n_t = pl.num_programs(0)
    slot = t % 2

    def copy(s_chunk, d_chunk, size, sl):
        return pltpu.make_async_copy(y_hbm.at[_chunk_rows(d_chunk, size)],
                                     stage.at[sl, _chunk_rows(s_chunk, size)], sem.at[sl])

    def fetch(tile, sl):
        _for_each_chunk(where_ref, tot_ref, tile, lambda s, d: copy(s, d, 1, sl).start())

    @pl.when(t == 0)
    def _():
        stage[...] = jnp.zeros_like(stage)
        fetch(0, 0)

    @pl.when(t + 1 < n_t)
    def _():
        fetch(t + 1, 1 - slot)

    _run_pieces(tot_ref[t], lambda off, size: copy(0, 0, size, slot).wait(), WAIT_CHUNKS)
    lane = lax.broadcasted_iota(I32, (tm, k_stage), 1)
    pg = pg_ref[...]
    p = jnp.zeros((tm, k_stage), F32)
    for k in range(TOP_K):
        p = jnp.where(lane == pg[:, k:k + 1].astype(I32), pg[:, TOP_K + k:TOP_K + k + 1], p)
    ffn = _dot(p.astype(BF16), stage[slot].astype(BF16))
    out = x_ref[...] + _pick_mod(t, n_pt, gp_ref, gs_ref) * ffn

    @pl.when(t < n_pt)
    def _():
        op_ref[...] = out

    @pl.when(t >= n_pt)
    def _():
        os_ref[...] = out


def _moe(geo, h2, pos, pg, nch_pad, soff_pad, x1, mod_p, mod_s, col_gate, layer, w_in, b_in, w_out, b_out):
    bm = MOE_BLOCK
    tm = geo.tm_moe
    n_tiles = geo.nt // tm
    depth, n_e, d, d2 = w_in.shape
    w_in = w_in.reshape(depth * n_e, d, d2)
    b_in = b_in.reshape(depth * n_e, d2)
    w_out = w_out.reshape(depth * n_e, d2 // 2, d)
    b_out = b_out.reshape(depth * n_e, d)
    chunks_per_block = bm // ROW_CHUNK
    max_rows = geo.nt * TOP_K + n_tiles * N_EXPERTS * (ROW_CHUNK - 1) + N_EXPERTS * (bm - 1)
    n_blocks = -(-max_rows // bm)
    nch = nch_pad[:, :N_EXPERTS, 0]
    soff = soff_pad[:, :N_EXPERTS, 0]
    tot = jnp.sum(nch, axis=1)
    per_e = jnp.sum(nch, axis=0)
    padded = (per_e + chunks_per_block - 1) // chunks_per_block * chunks_per_block
    pend = jnp.cumsum(padded)
    pstart = pend - padded
    dst = pstart[None, :] + jnp.cumsum(nch, axis=0) - nch
    first_sub = (pstart // chunks_per_block).astype(I32)
    n_sub = (padded // chunks_per_block).astype(I32)
    fill0 = jnp.concatenate([pstart + per_e, pend[-1:]]).astype(I32)
    filln = jnp.concatenate([padded - per_e, n_blocks * chunks_per_block - pend[-1:]]).astype(I32)

    k_stage = -(-(tm * TOP_K + N_EXPERTS * (ROW_CHUNK - 1)) // LANES) * LANES
    j = jnp.arange(k_stage // ROW_CHUNK, dtype=I32)
    owner = jnp.minimum(jnp.sum(j[None, :, None] >= (soff + nch)[:, None, :], axis=-1), N_EXPERTS - 1)
    shift = jnp.sum(jnp.where(owner[:, :, None] == jnp.arange(N_EXPERTS, dtype=I32), (dst - soff)[:, None, :], 0),
                    axis=-1)
    where = (j[None, :] + shift).astype(I32)
    row = lambda i, *_: (i, 0)
    xs = pl.pallas_call(
        functools.partial(_dispatch_kernel, tm=tm, k_stage=k_stage),
        grid_spec=pltpu.PrefetchScalarGridSpec(
            num_scalar_prefetch=4,
            grid=(n_tiles,),
            in_specs=[pl.BlockSpec((1, SUBLANES, tm), lambda i, *_: (i, 0, 0)), pl.BlockSpec((tm, d), row)],
            out_specs=pl.BlockSpec(memory_space=pl.ANY),
            scratch_shapes=[
                pltpu.VMEM((2, k_stage, d), F32),
                pltpu.VMEM((COPY_CHUNKS * ROW_CHUNK, d), F32),
                pltpu.SemaphoreType.DMA((2,)),
                pltpu.SemaphoreType.DMA((1,)),
            ],
        ),
        out_shape=jax.ShapeDtypeStruct((n_blocks * bm, d), F32),
        compiler_params=_cparams(("arbitrary",)),
        name="moe_dispatch",
    )(where, tot, fill0, filln, pos, h2)

    ys = _ffn(xs, first_sub, n_sub, layer, w_in, b_in, w_out, b_out)

    return pl.pallas_call(
        functools.partial(_combine_kernel, n_pt=geo.n_prompt // tm, tm=tm, k_stage=k_stage),
        grid_spec=pltpu.PrefetchScalarGridSpec(
            num_scalar_prefetch=2,
            grid=(n_tiles,),
            in_specs=[
                pl.BlockSpec((tm, LANES), row),
                pl.BlockSpec((tm, d), row),
                *_mod_specs(geo, col_gate, tm),
                pl.BlockSpec(memory_space=pl.ANY),
            ],
            out_specs=_pair_specs(geo, tm, d),
            scratch_shapes=[
                pltpu.VMEM((2, k_stage, d), F32),
                pltpu.SemaphoreType.DMA((2,)),
            ],
        ),
        out_shape=[jax.ShapeDtypeStruct((geo.n_prompt, d), F32), jax.ShapeDtypeStruct((geo.n_sample, d), F32)],
        compiler_params=_cparams(("arbitrary",)),
        name="moe_combine",
    )(where, tot, pg, x1, mod_p, mod_s, ys)


SW_QW = SW_Q_HEADS * SW_HEAD_DIM
SW_QKW = SW_QW + SW_KV_W
SW_KEYS = 2 * ATT_BLOCK


def _attn_proj_kernel(xp_ref, xs_ref, g_ref, shp_ref, shs_ref, scp_ref, scs_ref, w_ref, e_ref, et_ref, qg_ref,
                      kg_ref, q_ref, k_ref, v_ref, *, n_pt):
    i = pl.program_id(0)
    sh = _pick_mod(i, n_pt, shp_ref, shs_ref)
    sc = _pick_mod(i, n_pt, scp_ref, scs_ref)
    x = jnp.where(i >= n_pt, xs_ref[...], xp_ref[...])
    h = _norm_mod(x, g_ref[...], sc, sh).astype(BF16)
    tm = h.shape[0]
    n_parts = 2 if tm % (2 * SUBLANES) == 0 else 1
    part = tm // n_parts
    for r in range(n_parts):
        rows = slice(r * part, (r + 1) * part)
        z = _dot(h[rows], w_ref[...])
        qk = z[:, :SW_QKW]
        sq = qk * qk
        sq_hi = sq.astype(BF16)
        sq_lo = (sq - sq_hi.astype(F32)).astype(BF16)
        ms = (_dot(sq_hi, e_ref[...]) + _dot(sq_lo, e_ref[...])) * (1.0 / SW_HEAD_DIM)
        inv = lax.rsqrt(ms + NORM_EPS)
        inv_hi = inv.astype(BF16)
        inv_lo = (inv - inv_hi.astype(F32)).astype(BF16)
        qk = qk * (_dot(inv_hi, et_ref[...]) + _dot(inv_lo, et_ref[...]))
        q_ref[rows, :] = (qk[:, :SW_QW] * qg_ref[...] * SW_SCALE).astype(q_ref.dtype)
        k_ref[rows, :] = qk[:, SW_QW:] * kg_ref[...]
        v_ref[rows, :] = z[:, SW_QKW:]


def _attn_proj(geo, x_pair, g, mod_p, mod_s, w_bf16, q_gain, k_gain):
    tm = geo.tm
    n_out = w_bf16.shape[1]
    heads = SW_QKW // SW_HEAD_DIM
    member = (np.arange(SW_QKW)[:, None] // SW_HEAD_DIM == np.arange(LANES)[None, :]).astype(np.float32)
    e = jnp.asarray(member, BF16)
    et = jnp.asarray(member.T, BF16)
    assert heads <= LANES
    qg = jnp.tile(q_gain, SW_Q_HEADS).reshape(1, SW_QW)
    kg = jnp.tile(k_gain, SW_KV_HEADS).reshape(1, SW_KV_W)
    row = lambda i: (i, 0)
    fixed = lambda i: (0, 0)
    return pl.pallas_call(
        functools.partial(_attn_proj_kernel, n_pt=geo.n_pt),
        grid=(geo.n_tiles,),
        in_specs=[
            *_pair_specs(geo, tm, D_MODEL),
            pl.BlockSpec((1, D_MODEL), fixed),
            *_mod_specs(geo, 0, tm),
            *_mod_specs(geo, 1, tm),
            pl.BlockSpec((D_MODEL, n_out), fixed),
            pl.BlockSpec((SW_QKW, LANES), fixed),
            pl.BlockSpec((LANES, SW_QKW), fixed),
            pl.BlockSpec((1, SW_QW), fixed),
            pl.BlockSpec((1, SW_KV_W), fixed),
        ],
        out_specs=[
            pl.BlockSpec((tm, SW_QW), row),
            pl.BlockSpec((tm, SW_KV_W), row),
            pl.BlockSpec((tm, SW_KV_W), row),
        ],
        out_shape=[
            jax.ShapeDtypeStruct((geo.nt, SW_QW), BF16),
            jax.ShapeDtypeStruct((geo.nt, SW_KV_W), F32),
            jax.ShapeDtypeStruct((geo.nt, SW_KV_W), F32),
        ],
        compiler_params=_cparams(("arbitrary",)),
        name="attn_proj_qknorm",
    )(*x_pair, g, mod_p, mod_s, mod_p, mod_s, w_bf16, e, et, qg, kg)


def _rel_bucket_np(dist):
    n = np.maximum(dist, 0)
    max_exact = REL_BUCKETS // 2
    ratio = np.log(np.maximum(n, 1).astype(np.float32) / np.float32(max_exact)) / np.float32(
        math.log(REL_MAX_DIST / max_exact))
    large = max_exact + (ratio * np.float32(REL_BUCKETS - max_exact)).astype(np.int32)
    large = np.minimum(large, REL_BUCKETS - 1)
    return np.where(n < max_exact, n, large).astype(np.int32)


def _bucket_table(qpos, kpos, k_ok):
    dist = qpos[:, None] - kpos[None, :]
    ok = (dist >= 0) & (dist <= WINDOW) & k_ok[None, :]
    return np.where(ok, _rel_bucket_np(dist), -1).astype(np.int32)


def _build_bias(bkt_ref, rb_ref, bias_scr, rows):
    bkt = bkt_ref[...]
    base = jnp.where(bkt < 0, -jnp.inf, 0.0)
    for h in range(SW_Q_HEADS):
        bias_scr[h * rows:(h + 1) * rows, :] = base

    def add_bucket(j, carry):
        hit = bkt == j
        for h in range(SW_Q_HEADS):
            sl = slice(h * rows, (h + 1) * rows)
            bias_scr[sl, :] = bias_scr[sl, :] + jnp.where(hit, rb_ref[j, h], 0.0)
        return carry

    lax.fori_loop(0, REL_BUCKETS, add_bucket, 0)


def _sink_softmax_pv(s, sink, vv_g):
    m = jnp.maximum(jnp.max(s, axis=-1, keepdims=True), sink)
    e = jnp.exp(s - m)
    p = e / (jnp.sum(e, axis=-1, keepdims=True) + jnp.exp(sink - m))
    return _dot(p.astype(BF16), vv_g)


def _build_bias_t(bkt_ref, rb_ref, bias_scr, blk):
    bkt = bkt_ref[...]
    base = jnp.where(bkt < 0, -jnp.inf, 0.0)
    slots = [(g, j) for g in range(SW_KV_HEADS) for j in range(SW_GROUP)]
    for g, j in slots:
        bias_scr[g, :, j * blk:(j + 1) * blk] = base

    def add_bucket(b, carry):
        hit = bkt == b
        for g, j in slots:
            cols = slice(j * blk, (j + 1) * blk)
            bias_scr[g, :, cols] = bias_scr[g, :, cols] + jnp.where(hit, rb_ref[b, g * SW_GROUP + j], 0.0)
        return carry

    lax.fori_loop(0, REL_BUCKETS, add_bucket, 0)


def _swa_prompt_kernel(rb_ref, sink_ref, bkt_ref, q_ref, kp_ref, kc_ref, vp_ref, vc_ref, o_ref, bias_scr, *, qb):
    n = pl.program_id(1)
    blk = ATT_BLOCK
    hd = SW_HEAD_DIM

    @pl.when((pl.program_id(0) == 0) & (n == 0))
    def _():
        _build_bias_t(bkt_ref, rb_ref, bias_scr, blk)

    key_i = lax.broadcasted_iota(I32, (SW_KEYS, SW_GROUP * blk), 0)
    hide_prev = (n == 0) & (key_i < blk)
    zeros_q = jnp.zeros((hd, SW_GROUP * blk), BF16)
    sinks = [jnp.concatenate([jnp.full((1, blk), sink_ref[g * SW_GROUP + j], F32) for j in range(SW_GROUP)], axis=1)
             for g in range(SW_KV_HEADS)]
    for i in range(qb):
        rows = slice(i * blk, (i + 1) * blk)
        if i == 0:
            k2 = jnp.concatenate([kp_ref[...], kc_ref[rows, :]], axis=0)
            v2 = jnp.concatenate([vp_ref[...], vc_ref[rows, :]], axis=0)
        else:
            k2 = kc_ref[(i - 1) * blk:(i + 1) * blk, :]
            v2 = vc_ref[(i - 1) * blk:(i + 1) * blk, :]
        kk = k2.astype(BF16)
        vv_t = v2.T.astype(BF16)
        q_t = q_ref[rows, :].astype(F32).T.astype(BF16)
        outs = []
        for g in range(SW_KV_HEADS):
            heads = [g * SW_GROUP + j for j in range(SW_GROUP)]
            q_cat = jnp.concatenate([q_t[hq * hd:(hq + 1) * hd, :] for hq in heads], axis=1)
            q_full = jnp.concatenate([q_cat if gg == g else zeros_q for gg in range(SW_KV_HEADS)], axis=0)
            s = _dot(kk, q_full) + bias_scr[g]
            if i == 0:
                s = jnp.where(hide_prev, -jnp.inf, s)
            m = jnp.maximum(jnp.max(s, axis=0, keepdims=True), sinks[g])
            e = jnp.exp(s - m)
            den = jnp.sum(e, axis=0, keepdims=True) + jnp.exp(sinks[g] - m)
            o_t = _dot(vv_t[g * hd:(g + 1) * hd, :], e.astype(BF16)) * (1.0 / den)
            outs.extend(o_t[:, j * blk:(j + 1) * blk] for j in range(SW_GROUP))
        o_ref[rows, :] = jnp.concatenate(outs, axis=0).T.astype(o_ref.dtype)


def _swa_prompt(geo, q, k, v, rel_bias, sinks):
    blk = ATT_BLOCK
    assert geo.seq % blk == 0 and WINDOW == blk
    n_blk = geo.seq // blk
    qb = 4 if n_blk % 4 == 0 else (2 if n_blk % 2 == 0 else 1)
    nb = n_blk // qb
    bkt = jnp.asarray(_bucket_table(blk + np.arange(blk), np.arange(2 * blk), np.ones(2 * blk, bool)).T.copy())
    cur = lambda b, n: (b * nb + n, 0)
    prev = lambda b, n: (b * n_blk + jnp.maximum(n * qb - 1, 0), 0)
    smem = pl.BlockSpec(memory_space=pltpu.SMEM)
    return pl.pallas_call(
        functools.partial(_swa_prompt_kernel, qb=qb),
        grid=(geo.batch, nb),
        in_specs=[
            smem, smem,
            pl.BlockSpec((SW_KEYS, blk), lambda b, n: (0, 0)),
            pl.BlockSpec((qb * blk, SW_QW), cur),
            pl.BlockSpec((blk, SW_KV_W), prev),
            pl.BlockSpec((qb * blk, SW_KV_W), cur),
            pl.BlockSpec((blk, SW_KV_W), prev),
            pl.BlockSpec((qb * blk, SW_KV_W), cur),
        ],
        out_specs=pl.BlockSpec((qb * blk, SW_QW), cur),
        out_shape=jax.ShapeDtypeStruct((geo.n_prompt, SW_QW), BF16),
        scratch_shapes=[pltpu.VMEM((SW_KV_HEADS, SW_KEYS, SW_GROUP * blk), F32)],
        compiler_params=_cparams(("arbitrary", "arbitrary")),
        name="swa_prompt",
    )(rel_bias, sinks, bkt, q, k, k, v, v)


def _swa_step_kernel(rb_ref, sink_ref, bkt_ref, q_ref, kn_ref, vn_ref, ck_ref, cv_ref,
                     o_ref, nk_ref, nv_ref, bias_scr, *, steps, bb):
    pad = SUBLANES
    win = WINDOW

    @pl.when(pl.program_id(0) == 0)
    def _():
        _build_bias(bkt_ref, rb_ref, bias_scr, pad)

    fill = jnp.zeros((SW_KEYS - win - pad, SW_KV_W), F32)
    row8 = lax.broadcasted_iota(I32, (pad, SW_KV_W), 0)

    def shifted(cache, new):
        rolled = pltpu.roll(cache, win - steps, 0)
        tail = jnp.where(row8 < pad - steps, rolled[win - pad:], pltpu.roll(new, pad - steps, 0))
        return rolled[:win - pad], tail

    def body(s, carry):
        ck, cv, kn, vn = ck_ref[s], cv_ref[s], kn_ref[s], vn_ref[s]
        kk = jnp.concatenate([ck, kn, fill], axis=0).astype(BF16)
        vv = jnp.concatenate([cv, vn, fill], axis=0).astype(BF16)
        q = q_ref[s]
        for g in range(SW_KV_HEADS):
            kcols = slice(g * SW_HEAD_DIM, (g + 1) * SW_HEAD_DIM)
            heads = [g * SW_GROUP + j for j in range(SW_GROUP)]
            qs = jnp.concatenate([q[:, hq * SW_HEAD_DIM:(hq + 1) * SW_HEAD_DIM] for hq in heads], axis=0)
            sc = _dot_nt(qs.astype(BF16), kk[:, kcols]) + bias_scr[heads[0] * pad:(heads[-1] + 1) * pad, :]
            sink = jnp.concatenate([jnp.full((pad, 1), sink_ref[hq], F32) for hq in heads], axis=0)
            og = _sink_softmax_pv(sc, sink, vv[:, kcols])
            for j, hq in enumerate(heads):
                o_ref[s, :, hq * SW_HEAD_DIM:(hq + 1) * SW_HEAD_DIM] = og[j * pad:(j + 1) * pad]
        head, tail = shifted(ck, kn)
        nk_ref[s, 0:win - pad] = head
        nk_ref[s, win - pad:win] = tail
        head, tail = shifted(cv, vn)
        nv_ref[s, 0:win - pad] = head
        nv_ref[s, win - pad:win] = tail
        return carry

    lax.fori_loop(0, bb, body, 0, unroll=True)


def _swa_step(geo, q, k, v, cache_k, cache_v, rel_bias, sinks):
    steps, db = geo.dec_seq, geo.dec_batch
    win = cache_k.shape[1]
    assert win == WINDOW and steps <= SUBLANES
    bb = min(STEP_SEQS, db)
    pad = SUBLANES
    q_s = _seq_major(geo, q[geo.n_prompt:].astype(F32))
    k_s = _seq_major(geo, k[geo.n_prompt:])
    v_s = _seq_major(geo, v[geo.n_prompt:])
    kpos = np.arange(SW_KEYS)
    k_ok = kpos < win + steps
    qpos = win + np.arange(pad)
    bkt = _bucket_table(qpos, kpos, k_ok)
    bkt[steps:] = -1
    smem = pl.BlockSpec(memory_space=pltpu.SMEM)
    blk3 = lambda r, w: pl.BlockSpec((bb, r, w), lambda i: (i, 0, 0))
    o_s, nk, nv = pl.pallas_call(
        functools.partial(_swa_step_kernel, steps=steps, bb=bb),
        grid=(db // bb,),
        in_specs=[
            smem, smem,
            pl.BlockSpec((pad, SW_KEYS), lambda i: (0, 0)),
            blk3(pad, SW_QW), blk3(pad, SW_KV_W), blk3(pad, SW_KV_W),
            blk3(win, SW_KV_W), blk3(win, SW_KV_W),
        ],
        out_specs=[blk3(pad, SW_QW), blk3(win, SW_KV_W), blk3(win, SW_KV_W)],
        out_shape=[
            jax.ShapeDtypeStruct((db, pad, SW_QW), F32),
            jax.ShapeDtypeStruct((db, win, SW_KV_W), F32),
            jax.ShapeDtypeStruct((db, win, SW_KV_W), F32),
        ],
        scratch_shapes=[pltpu.VMEM((SW_Q_HEADS * pad, SW_KEYS), F32)],
        compiler_params=_cparams(("arbitrary",)),
        name="swa_step",
    )(rel_bias, sinks, jnp.asarray(bkt), q_s, k_s, v_s, cache_k, cache_v)
    return _time_major(geo, o_s), nk, nv


def kernel(x_prompt, x_sample, state_hgrn, cache_win_k, cache_win_v, c_prompt, c_sample, norm1_g, norm2_g, ada_w, ada_b, hg_w_in, hg_lb_table, hg_onorm_g, hg_w_out, sw_w_in, sw_qnorm_g, sw_knorm_g, sw_sinks, sw_w_out, rel_bias, router_w, router_b, moe_w_in, moe_b_in, moe_w_out, moe_b_out):
    batch, seq, d = x_prompt.shape
    db, steps, _ = x_sample.shape
    assert d == D_MODEL and ada_w.shape[0] == 2 and hg_w_in.shape[0] == 1 and sw_w_in.shape[0] == 1
    geo = _Geom(batch, seq, db, steps)
    x = (x_prompt.reshape(batch * seq, d), x_sample.transpose(1, 0, 2).reshape(steps * db, d))
    n_seq = batch + db
    rows = -(-n_seq // SUBLANES) * SUBLANES
    c_all = jnp.concatenate([c_prompt, c_sample, jnp.zeros((rows - n_seq, d), F32)], axis=0)
    mods = _ada_mods(c_all, ada_w, ada_b)

    def layer_mods(layer):
        mod_p = mods[layer, :batch].reshape(batch, 1, 6 * d)
        mod_s = jnp.tile(mods[layer, batch:n_seq], (steps, 1))
        return mod_p, mod_s

    def moe(layer, a_p, a_s, w_out, x_in, mod_p, mod_s):
        x1, h2, pos, pg, nch, soff = _post(geo, a_p, a_s, w_out.astype(BF16), x_in, norm2_g[layer:layer + 1],
                                           mod_p, mod_s, 0, router_w[layer], router_b[layer])
        return _moe(geo, h2, pos, pg, nch, soff, x1, mod_p, mod_s, 5,
                    layer, moe_w_in, moe_b_in, moe_w_out, moe_b_out)

    mod_p, mod_s = layer_mods(0)
    og = hg_onorm_g[0:1]
    z = _proj(geo, x, norm1_g[0:1], mod_p, mod_s, hg_w_in[0].astype(BF16), 0, 1)
    o_p, st_p = _gla_prompt(geo, z, hg_lb_table, og)
    o_s, st_s = _gla_step(geo, z, state_hgrn[0], hg_lb_table, og)
    x = moe(0, o_p, o_s, hg_w_out[0], x, mod_p, mod_s)

    mod_p, mod_s = layer_mods(1)
    q, k, v = _attn_proj(geo, x, norm1_g[1:2], mod_p, mod_s, sw_w_in[0].astype(BF16), sw_qnorm_g[0], sw_knorm_g[0])
    win = cache_win_k.shape[2]
    a_p = _swa_prompt(geo, q, k, v, rel_bias, sw_sinks[0])
    a_s, nk, nv = _swa_step(geo, q, k, v, cache_win_k[0].reshape(db, win, SW_KV_W),
                            cache_win_v[0].reshape(db, win, SW_KV_W), rel_bias, sw_sinks[0])
    x = moe(1, a_p, a_s, sw_w_out[0], x, mod_p, mod_s)

    y_prompt = x[0].reshape(batch, seq, d)
    y_sample = x[1].reshape(steps, db, d).transpose(1, 0, 2)
    kv_shape = (1, batch, WINDOW, SW_KV_HEADS, SW_HEAD_DIM)
    last_window = lambda a: jnp.stack([a[(b + 1) * seq - WINDOW:(b + 1) * seq] for b in range(batch)])
    k_p = last_window(k).reshape(kv_shape)
    v_p = last_window(v).reshape(kv_shape)
    cache_shape = (1, db, win, SW_KV_HEADS, SW_HEAD_DIM)
    return (y_prompt, y_sample, jnp.swapaxes(st_p, -1, -2)[None], st_s[None], k_p, v_p,
            nk.reshape(cache_shape), nv.reshape(cache_shape))
```

```python
import functools
import math

import numpy as np
import jax
import jax.numpy as jnp
from jax import lax
from jax.experimental import pallas as pl
from jax.experimental.pallas import tpu as pltpu

F32 = jnp.float32
BF16 = jnp.bfloat16
I32 = jnp.int32

D_MODEL = 1024
LANES = 128
SUBLANES = 8
TOKEN_TILE = 512
MOE_TILE = 256
GLA_STEP_TOKENS = 512
STEP_SEQS = 8
HG_DK = 128
HG_HEADS = D_MODEL // HG_DK
HG_DV = D_MODEL // HG_HEADS
HG_CHUNK = 64
SW_HEAD_DIM = 64
SW_Q_HEADS = D_MODEL // SW_HEAD_DIM
SW_KV_HEADS = 4
SW_GROUP = SW_Q_HEADS // SW_KV_HEADS
SW_KV_W = SW_KV_HEADS * SW_HEAD_DIM
WINDOW = 128
ATT_BLOCK = 128
SW_SCALE = SW_HEAD_DIM ** -0.5
REL_BUCKETS = 32
REL_MAX_DIST = 128
N_EXPERTS = 32
TOP_K = 4
SWIGLU_LIMIT = 7.0
SWIGLU_ALPHA = 1.702
NORM_EPS = 1e-5
MOE_BLOCK = 256
ROW_CHUNK = SUBLANES
COPY_CHUNKS = 4
CHUNK_UNROLL = 8
WAIT_CHUNKS = 16
VMEM_LIMIT = 56 * 1024 * 1024


def _cparams(sem):
    return pltpu.CompilerParams(dimension_semantics=sem, vmem_limit_bytes=VMEM_LIMIT)


def _sigmoid(x):
    return 0.5 * jnp.tanh(0.5 * x) + 0.5


def _silu(x):
    return x * _sigmoid(x)


def _dot(a, b):
    return jnp.dot(a, b, preferred_element_type=F32)


def _dot_nt(a, b):
    return lax.dot_general(a, b, (((1,), (1,)), ((), ())), preferred_element_type=F32)


def _dot_tn(a, b):
    return lax.dot_general(a, b, (((0,), (0,)), ((), ())), preferred_element_type=F32)


def _split3(x):
    hi = x.astype(BF16)
    r = x - hi.astype(F32)
    mid = r.astype(BF16)
    lo = (r - mid.astype(F32)).astype(BF16)
    return hi, mid, lo


def _ada_kernel(c_ref, w_ref, b_ref, o_ref):
    s = _silu(c_ref[...]).astype(BF16)
    o_ref[0] = _dot(s, w_ref[0].astype(BF16)) + b_ref[0]


def _ada_mods(c_all, ada_w, ada_b):
    depth, d, n6 = ada_w.shape
    rows = c_all.shape[0]
    tn = 1536
    return pl.pallas_call(
        _ada_kernel,
        grid=(depth, n6 // tn),
        in_specs=[
            pl.BlockSpec((rows, d), lambda l, j: (0, 0)),
            pl.BlockSpec((1, d, tn), lambda l, j: (l, 0, j)),
            pl.BlockSpec((1, 1, tn), lambda l, j: (l, 0, j)),
        ],
        out_specs=pl.BlockSpec((1, rows, tn), lambda l, j: (l, 0, j)),
        out_shape=jax.ShapeDtypeStruct((depth, rows, n6), F32),
        compiler_params=_cparams(("arbitrary", "arbitrary")),
        name="ada_mods",
    )(c_all, ada_w, ada_b.reshape(depth, 1, n6))


class _Geom:
    def __init__(self, batch, seq, dec_batch, dec_seq):
        self.batch, self.seq, self.dec_batch, self.dec_seq = batch, seq, dec_batch, dec_seq
        self.n_prompt = batch * seq
        self.n_sample = dec_batch * dec_seq
        self.nt = self.n_prompt + self.n_sample
        tm = TOKEN_TILE
        while seq % tm or self.n_sample % tm:
            tm //= 2
        assert tm >= 8
        self.tm = tm
        self.n_pt = self.n_prompt // tm
        self.n_tiles = self.nt // tm
        self.tm_moe = min(MOE_TILE, tm)


def _mod_specs(geo, col, tm):
    seq, batch = geo.seq, geo.batch
    n_pt = geo.n_prompt // tm

    def p_map(i, *_):
        return (jnp.minimum(i * tm // seq, batch - 1), 0, col)

    def s_map(i, *_):
        return (jnp.maximum(i - n_pt, 0), col)

    return [pl.BlockSpec((1, 1, D_MODEL), p_map), pl.BlockSpec((tm, D_MODEL), s_map)]


def _pick_mod(i, n_pt, p_ref, s_ref):
    return jnp.where(i >= n_pt, s_ref[...], p_ref[0])


def _norm_mod(x, g, sc, sh):
    ms = jnp.mean(x * x, axis=-1, keepdims=True)
    return x * lax.rsqrt(ms + NORM_EPS) * g * (1.0 + sc) + sh


def _pair_specs(geo, tm, width):
    n_pt = geo.n_prompt // tm
    return [pl.BlockSpec((tm, width), lambda i, *_: (jnp.minimum(i, n_pt - 1), 0)),
            pl.BlockSpec((tm, width), lambda i, *_: (jnp.maximum(i - n_pt, 0), 0))]


def _proj_kernel(xp_ref, xs_ref, g_ref, shp_ref, shs_ref, scp_ref, scs_ref, w_ref, o_ref, *, n_pt, tn):
    i = pl.program_id(0)
    sh = _pick_mod(i, n_pt, shp_ref, shs_ref)
    sc = _pick_mod(i, n_pt, scp_ref, scs_ref)
    x = jnp.where(i >= n_pt, xs_ref[...], xp_ref[...])
    h = _norm_mod(x, g_ref[...], sc, sh).astype(BF16)
    for j in range(o_ref.shape[1] // tn):
        o_ref[:, j * tn:(j + 1) * tn] = _dot(h, w_ref[:, j * tn:(j + 1) * tn])


def _proj(geo, x_pair, g, mod_p, mod_s, w_bf16, col_shift, col_scale):
    n_out = w_bf16.shape[1]
    tm = geo.tm
    fixed = lambda i: (0, 0)
    return pl.pallas_call(
        functools.partial(_proj_kernel, n_pt=geo.n_pt, tn=1024),
        grid=(geo.n_tiles,),
        in_specs=[
            *_pair_specs(geo, tm, D_MODEL),
            pl.BlockSpec((1, D_MODEL), fixed),
            *_mod_specs(geo, col_shift, tm),
            *_mod_specs(geo, col_scale, tm),
            pl.BlockSpec((D_MODEL, n_out), fixed),
        ],
        out_specs=pl.BlockSpec((tm, n_out), lambda i: (i, 0)),
        out_shape=jax.ShapeDtypeStruct((geo.nt, n_out), F32),
        compiler_params=_cparams(("arbitrary",)),
        name="norm_mod_proj",
    )(*x_pair, g, mod_p, mod_s, mod_p, mod_s, w_bf16)


def _hg_lower_bound(lbt_ref):
    t = lbt_ref[...]
    e = jnp.exp(t - jnp.max(t, axis=0, keepdims=True))
    return e[0:1] / jnp.sum(e, axis=0, keepdims=True)


def _hg_gates(fz, lb):
    t = jnp.tanh(0.5 * fz)
    logf = jnp.log(lb + (1.0 - lb) * (0.5 + 0.5 * t))
    return logf, (1.0 - lb) * (0.5 - 0.5 * t)


def _hg_out(o, gz, og):
    ms = jnp.mean(o * o, axis=-1, keepdims=True)
    return o * lax.rsqrt(ms + NORM_EPS) * og * _silu(gz)


def _gla_prompt_kernel(*refs, batch, chunk, n_chunks):
    z_refs = refs[:batch]
    lbt_ref, og_ref, o_ref, sfin_ref, st_scr = refs[batch:]
    t_step = pl.program_id(0)
    kw = HG_HEADS * HG_DK

    @pl.when(t_step == 0)
    def _():
        st_scr[...] = jnp.zeros_like(st_scr)

    lb = _hg_lower_bound(lbt_ref)
    og = og_ref[...]
    r_i = lax.broadcasted_iota(I32, (chunk, chunk), 0)
    c_i = lax.broadcasted_iota(I32, (chunk, chunk), 1)
    causal = c_i <= r_i
    tri = causal.astype(BF16)
    mid = chunk // 2 - 1

    def body(c, carry):
        rows = pl.ds(pl.multiple_of(c * chunk, chunk), chunk)
        for bi, z_ref in enumerate(z_refs):
            logf, kk = _hg_gates(z_ref[rows, kw:2 * kw], lb)
            hi, md, lo = _split3(logf)
            cs = _dot(tri, jnp.concatenate([hi, md, lo], axis=1))
            b = cs[:, :kw] + cs[:, kw:2 * kw] + cs[:, 2 * kw:]
            b_mid = b[mid:mid + 1]
            b_last = b[chunk - 1:chunk]
            q_hat = _silu(z_ref[rows, 0:kw]) * jnp.exp(b - b_mid)
            k_hat = kk * jnp.exp(b_mid - b)
            q_in = (q_hat * jnp.exp(b_mid)).astype(BF16)
            k_dec = (k_hat * jnp.exp(b_last - b_mid)).astype(BF16)
            q_hat = q_hat.astype(BF16)
            k_hat = k_hat.astype(BF16)
            dec = jnp.exp(b_last)
            for h in range(HG_HEADS):
                cols = slice(h * HG_DK, (h + 1) * HG_DK)
                vcols = slice(2 * kw + h * HG_DV, 2 * kw + (h + 1) * HG_DV)
                gcols = slice(2 * kw + HG_HEADS * HG_DV + h * HG_DV, 2 * kw + HG_HEADS * HG_DV + (h + 1) * HG_DV)
                v = z_ref[rows, vcols].astype(BF16)
                att = jnp.where(causal, _dot_nt(q_hat[:, cols], k_hat[:, cols]), 0.0).astype(BF16)
                st = st_scr[bi, h]
                o = _dot(att, v) + _dot_nt(q_in[:, cols], st.astype(BF16))
                st_scr[bi, h] = st * dec[:, cols] + _dot_tn(v, k_dec[:, cols])
                o_ref[bi, rows, h * HG_DV:(h + 1) * HG_DV] = _hg_out(o, z_ref[rows, gcols], og).astype(o_ref.dtype)
        return carry

    lax.fori_loop(0, n_chunks, body, 0, unroll=True)

    @pl.when(t_step == pl.num_programs(0) - 1)
    def _():
        sfin_ref[...] = st_scr[...]


def _gla_prompt(geo, z, lb_table, o_gain):
    tg = min(GLA_STEP_TOKENS, geo.seq)
    chunk = HG_CHUNK if geo.seq % HG_CHUNK == 0 else geo.seq
    assert tg % chunk == 0 and geo.seq % tg == 0
    nt = geo.seq // tg
    batch = geo.batch
    fixed = lambda t: (0, 0)
    state_shape = (batch, HG_HEADS, HG_DV, HG_DK)
    o, st = pl.pallas_call(
        functools.partial(_gla_prompt_kernel, batch=batch, chunk=chunk, n_chunks=tg // chunk),
        grid=(nt,),
        in_specs=[
            *[pl.BlockSpec((tg, 4 * D_MODEL), functools.partial(lambda t, b: (b * nt + t, 0), b=b))
              for b in range(batch)],
            pl.BlockSpec(lb_table.shape, fixed),
            pl.BlockSpec((1, HG_DV), fixed),
        ],
        out_specs=[
            pl.BlockSpec((batch, tg, D_MODEL), lambda t: (0, t, 0)),
            pl.BlockSpec(state_shape, lambda t: (0, 0, 0, 0)),
        ],
        out_shape=[
            jax.ShapeDtypeStruct((batch, geo.seq, D_MODEL), BF16),
            jax.ShapeDtypeStruct(state_shape, F32),
        ],
        scratch_shapes=[pltpu.VMEM(state_shape, F32)],
        compiler_params=_cparams(("arbitrary",)),
        name="gla_prompt",
    )(*([z] * batch), lb_table, o_gain)
    return o.reshape(geo.n_prompt, D_MODEL), st


def _gla_step_kernel(z_ref, s_ref, lbt_ref, og_ref, o_ref, snew_ref, *, steps, bb):
    kw = HG_HEADS * HG_DK
    pad = SUBLANES
    lb = _hg_lower_bound(lbt_ref)
    og = og_ref[...]
    r_i = lax.broadcasted_iota(I32, (pad, pad), 0)
    c_i = lax.broadcasted_iota(I32, (pad, pad), 1)
    causal = c_i <= r_i
    tri = causal.astype(BF16)
    row_w = lax.broadcasted_iota(I32, (pad, kw), 0)
    live = row_w < steps
    row_k = lax.broadcasted_iota(I32, (pad, HG_DK), 0)
    ones_sel = jnp.where((row_k == steps) | (row_k == steps + 1), 1.0, 0.0).astype(BF16)

    def body(s, carry):
        z = z_ref[s]
        logf, kk = _hg_gates(z[:, kw:2 * kw], lb)
        hi, md, lo = _split3(jnp.where(live, logf, 0.0))
        cs = _dot(tri, jnp.concatenate([hi, md, lo], axis=1))
        b = cs[:, :kw] + cs[:, kw:2 * kw] + cs[:, 2 * kw:]
        b_last = b[steps - 1:steps]
        q_in = (_silu(z[:, 0:kw]) * jnp.exp(b)).astype(BF16)
        k_hat = jnp.where(live, kk * jnp.exp(-b), 0.0).astype(BF16)
        k_dec = jnp.where(live, kk * jnp.exp(b_last - b), 0.0).astype(BF16)
        dec = jnp.exp(b_last)
        d_hi = dec.astype(BF16)
        d_lo = (dec - d_hi.astype(F32)).astype(BF16)
        a_all = jnp.where(row_w == steps, d_hi, jnp.where(row_w == steps + 1, d_lo, k_dec))
        for h in range(HG_HEADS):
            cols = slice(h * HG_DK, (h + 1) * HG_DK)
            vcols = slice(2 * kw + h * HG_DV, 2 * kw + (h + 1) * HG_DV)
            gcols = slice(2 * kw + HG_HEADS * HG_DV + h * HG_DV, 2 * kw + HG_HEADS * HG_DV + (h + 1) * HG_DV)
            v = z[:, vcols].astype(BF16)
            s0 = s_ref[s, h]
            att = jnp.where(causal, _dot_nt(q_in[:, cols], k_hat[:, cols]), 0.0).astype(BF16)
            o = _dot(att, v) + _dot(q_in[:, cols], s0.astype(BF16))
            upd = _dot_tn(a_all[:, cols], jnp.concatenate([v, ones_sel], axis=1))
            snew_ref[s, h] = upd[:, HG_DV:] * s0 + upd[:, :HG_DV]
            o_ref[s, :, h * HG_DV:(h + 1) * HG_DV] = _hg_out(o, z[:, gcols], og)
        return carry

    lax.fori_loop(0, bb, body, 0, unroll=True)


def _seq_major(geo, rows):
    steps, db = geo.dec_seq, geo.dec_batch
    w = rows.shape[-1]
    r = rows.reshape(steps, db, w).transpose(1, 0, 2)
    return jnp.concatenate([r, jnp.zeros((db, SUBLANES - steps, w), rows.dtype)], axis=1)


def _time_major(geo, r):
    steps, db = geo.dec_seq, geo.dec_batch
    return r[:, :steps].transpose(1, 0, 2).reshape(steps * db, r.shape[-1])


def _gla_step(geo, z, state, lb_table, o_gain):
    steps, db = geo.dec_seq, geo.dec_batch
    assert steps + 2 <= SUBLANES
    bb = min(STEP_SEQS, db)
    z_s = _seq_major(geo, z[geo.n_prompt:])
    o_s, s_new = pl.pallas_call(
        functools.partial(_gla_step_kernel, steps=steps, bb=bb),
        grid=(db // bb,),
        in_specs=[
            pl.BlockSpec((bb, SUBLANES, 4 * D_MODEL), lambda i: (i, 0, 0)),
            pl.BlockSpec((bb, HG_HEADS, HG_DK, HG_DV), lambda i: (i, 0, 0, 0)),
            pl.BlockSpec(lb_table.shape, lambda i: (0, 0)),
            pl.BlockSpec((1, HG_DV), lambda i: (0, 0)),
        ],
        out_specs=[
            pl.BlockSpec((bb, SUBLANES, D_MODEL), lambda i: (i, 0, 0)),
            pl.BlockSpec((bb, HG_HEADS, HG_DK, HG_DV), lambda i: (i, 0, 0, 0)),
        ],
        out_shape=[
            jax.ShapeDtypeStruct((db, SUBLANES, D_MODEL), F32),
            jax.ShapeDtypeStruct(state.shape, F32),
        ],
        compiler_params=_cparams(("arbitrary",)),
        name="gla_step",
    )(z_s, state, lb_table, o_gain)
    return _time_major(geo, o_s), s_new


def _post_kernel(ap_ref, as_ref, wo_ref, xp_ref, xs_ref, gp_ref, gs_ref, n2_ref, shp_ref, shs_ref, scp_ref,
                 scs_ref, rw_ref, rb_ref, x1_ref, h2_ref, pos_ref, pg_ref, nch_ref, soff_ref, *, n_pt, tm):
    i = pl.program_id(0)
    g1 = _pick_mod(i, n_pt, gp_ref, gs_ref)
    a = jnp.where(i >= n_pt, as_ref[...], ap_ref[...])
    x = jnp.where(i >= n_pt, xs_ref[...], xp_ref[...])
    x1 = x + g1 * _dot(a, wo_ref[...])
    x1_ref[...] = x1
    sh = _pick_mod(i, n_pt, shp_ref, shs_ref)
    sc = _pick_mod(i, n_pt, scp_ref, scs_ref)
    h2 = _norm_mod(x1, n2_ref[...], sc, sh)
    h2_ref[...] = h2.astype(h2_ref.dtype)

    row_e = lax.broadcasted_iota(I32, (LANES, tm), 0)
    logits = _dot_nt(rw_ref[...], h2.astype(BF16)) + rb_ref[...]
    work = jnp.where(row_e < N_EXPERTS, logits, -jnp.inf)
    vals, hits = [], []
    for _ in range(TOP_K):
        m = jnp.max(work, axis=0, keepdims=True)
        idx = jnp.min(jnp.where(work == m, row_e, LANES), axis=0, keepdims=True)
        hit = row_e == idx
        vals.append(m)
        hits.append(hit)
        work = jnp.where(hit, -jnp.inf, work)
    exps = [jnp.exp(v - vals[0]) for v in vals]
    den = exps[0]
    for e in exps[1:]:
        den = den + e
    any_hit = hits[0]
    for hmask in hits[1:]:
        any_hit = any_hit | hmask
    any_f = jnp.where(any_hit, 1.0, 0.0)
    t_i = lax.broadcasted_iota(I32, (tm, tm), 0)
    t_j = lax.broadcasted_iota(I32, (tm, tm), 1)
    rank = _dot(any_f.astype(BF16), (t_i < t_j).astype(BF16))
    n_chunk = jnp.floor((jnp.sum(any_f, axis=1, keepdims=True) + (ROW_CHUNK - 1)) * (1.0 / ROW_CHUNK))
    e_i = lax.broadcasted_iota(I32, (LANES, LANES), 0)
    e_j = lax.broadcasted_iota(I32, (LANES, LANES), 1)
    seg = _dot((e_j < e_i).astype(BF16), jnp.broadcast_to(n_chunk, (LANES, LANES)).astype(BF16))[:, 0:1]
    where_to = seg * float(ROW_CHUNK) + rank
    pos_rows = [jnp.sum(jnp.where(hits[k], where_to, 0.0), axis=0, keepdims=True) for k in range(TOP_K)]
    gate_rows = [exps[k] / den for k in range(TOP_K)]
    pos_ref[0] = jnp.concatenate(pos_rows + [jnp.zeros((SUBLANES - TOP_K, tm), F32)], axis=0).astype(I32)
    rows = jnp.concatenate(pos_rows + gate_rows + [jnp.zeros((LANES - 2 * TOP_K, tm), F32)], axis=0)
    pg_ref[...] = rows.T
    nch_ref[0] = n_chunk.astype(I32)
    soff_ref[0] = seg.astype(I32)


def _post(geo, a_p, a_s, w_out_bf16, x_pair, norm_g, mod_p, mod_s, col0, router_w, router_b):
    tm = geo.tm_moe
    n_pt = geo.n_prompt // tm
    n_tiles = geo.nt // tm
    rw = jnp.zeros((LANES, D_MODEL), BF16).at[:N_EXPERTS].set(router_w.T.astype(BF16))
    rb = jnp.zeros((LANES, 1), F32).at[:N_EXPERTS, 0].set(router_b)
    row = lambda i: (i, 0)
    fixed = lambda i: (0, 0)
    tile_row = lambda i: (i, 0, 0)
    return pl.pallas_call(
        functools.partial(_post_kernel, n_pt=n_pt, tm=tm),
        grid=(n_tiles,),
        in_specs=[
            *_pair_specs(geo, tm, D_MODEL),
            pl.BlockSpec((D_MODEL, D_MODEL), fixed),
            *_pair_specs(geo, tm, D_MODEL),
            *_mod_specs(geo, col0 + 2, tm),
            pl.BlockSpec((1, D_MODEL), fixed),
            *_mod_specs(geo, col0 + 3, tm),
            *_mod_specs(geo, col0 + 4, tm),
            pl.BlockSpec((LANES, D_MODEL), fixed),
            pl.BlockSpec((LANES, 1), fixed),
        ],
        out_specs=[
            pl.BlockSpec((tm, D_MODEL), row),
            pl.BlockSpec((tm, D_MODEL), row),
            pl.BlockSpec((1, SUBLANES, tm), tile_row),
            pl.BlockSpec((tm, LANES), row),
            pl.BlockSpec((1, LANES, 1), tile_row),
            pl.BlockSpec((1, LANES, 1), tile_row),
        ],
        out_shape=[
            jax.ShapeDtypeStruct((geo.nt, D_MODEL), F32),
            jax.ShapeDtypeStruct((geo.nt, D_MODEL), BF16),
            jax.ShapeDtypeStruct((n_tiles, SUBLANES, tm), I32),
            jax.ShapeDtypeStruct((geo.nt, LANES), F32),
            jax.ShapeDtypeStruct((n_tiles, LANES, 1), I32),
            jax.ShapeDtypeStruct((n_tiles, LANES, 1), I32),
        ],
        compiler_params=_cparams(("arbitrary",)),
        name="post_mixer_router",
    )(a_p, a_s.astype(BF16), w_out_bf16, *x_pair, mod_p, mod_s, norm_g, mod_p, mod_s, mod_p, mod_s, rw, rb)


def _ffn_kernel(first_ref, count_ref, xs_hbm, win_ref, bin_ref, wout_ref, bout_ref, ys_hbm,
                xbuf, ybuf, win_scr, wout_scr, xsem, ysem, *, half):
    e = pl.program_id(0)
    first = first_ref[e]
    n_sub = count_ref[e]
    pairs = n_sub // 2
    odd = n_sub - 2 * pairs
    win_scr[...] = win_ref[0].astype(BF16)
    wout_scr[...] = wout_ref[0].astype(BF16)

    def rows_at(sub, n_rows):
        return pl.ds(pl.multiple_of((first + sub) * half, half), n_rows)

    def x_copy(sub, n_rows, slot):
        return pltpu.make_async_copy(xs_hbm.at[rows_at(sub, n_rows)], xbuf.at[slot, pl.ds(0, n_rows)], xsem.at[slot])

    def y_copy(sub, n_rows, slot):
        return pltpu.make_async_copy(ybuf.at[slot, pl.ds(0, n_rows)], ys_hbm.at[rows_at(sub, n_rows)], ysem.at[slot])

    def ffn(slot, rows):
        gu = _dot(xbuf[slot, rows, :].astype(BF16), win_scr[...]) + bin_ref[0]
        gate = jnp.minimum(gu[:, :D_MODEL], SWIGLU_LIMIT)
        up = jnp.clip(gu[:, D_MODEL:], -SWIGLU_LIMIT, SWIGLU_LIMIT)
        act = gate * _sigmoid(SWIGLU_ALPHA * gate) * (up + 1.0)
        return _dot(act.astype(BF16), wout_scr[...]) + bout_ref[0]

    def start_first(expert):
        begin = pl.multiple_of(first_ref[expert] * half, half)
        n = count_ref[expert]

        @pl.when(n >= 2)
        def _():
            pltpu.make_async_copy(xs_hbm.at[pl.ds(begin, 2 * half)], xbuf.at[0], xsem.at[0]).start()

        @pl.when(n == 1)
        def _():
            pltpu.make_async_copy(xs_hbm.at[pl.ds(begin, half)], xbuf.at[0, pl.ds(0, half)], xsem.at[0]).start()

    @pl.when(e == 0)
    def _():
        start_first(0)

    def unit(u, carry):
        slot = u % 2
        x_copy(2 * u, 2 * half, slot).wait()

        @pl.when(u + 1 < pairs)
        def _():
            x_copy(2 * (u + 1), 2 * half, 1 - slot).start()

        @pl.when((u + 1 == pairs) & (odd == 1))
        def _():
            x_copy(2 * pairs, half, 1 - slot).start()

        @pl.when(u >= 2)
        def _():
            y_copy(0, 2 * half, slot).wait()

        ybuf[slot] = ffn(slot, slice(0, 2 * half))
        y_copy(2 * u, 2 * half, slot).start()
        return carry

    lax.fori_loop(0, pairs, unit, 0)
    tail_slot = pairs % 2

    @pl.when(odd == 1)
    def _():
        x_copy(2 * pairs, half, tail_slot).wait()

        @pl.when(pairs >= 2)
        def _():
            y_copy(0, 2 * half, tail_slot).wait()

        ybuf[tail_slot, 0:half, :] = ffn(tail_slot, slice(0, half))
        y_copy(2 * pairs, half, tail_slot).start()

    @pl.when(e + 1 < pl.num_programs(0))
    def _():
        start_first(e + 1)

    @pl.when(odd == 1)
    def _():
        @pl.when(pairs >= 1)
        def _():
            y_copy(0, 2 * half, 1 - tail_slot).wait()

        y_copy(0, half, tail_slot).wait()

    @pl.when(odd == 0)
    def _():
        @pl.when(pairs >= 2)
        def _():
            y_copy(0, 2 * half, tail_slot).wait()

        @pl.when(pairs >= 1)
        def _():
            y_copy(0, 2 * half, 1 - tail_slot).wait()


def _ffn(xs, first_sub, n_sub, layer, w_in, b_in, w_out, b_out):
    n_e, d, d2 = w_in.shape
    half = MOE_BLOCK
    w_map = lambda e, *_: (layer * N_EXPERTS + e, 0, 0)
    return pl.pallas_call(
        functools.partial(_ffn_kernel, half=half),
        grid_spec=pltpu.PrefetchScalarGridSpec(
            num_scalar_prefetch=2,
            grid=(N_EXPERTS,),
            in_specs=[
                pl.BlockSpec(memory_space=pl.ANY),
                pl.BlockSpec((1, d, d2), w_map),
                pl.BlockSpec((1, 1, d2), w_map),
                pl.BlockSpec((1, d2 // 2, d), w_map),
                pl.BlockSpec((1, 1, d), w_map),
            ],
            out_specs=pl.BlockSpec(memory_space=pl.ANY),
            scratch_shapes=[
                pltpu.VMEM((2, 2 * half, d), F32),
                pltpu.VMEM((2, 2 * half, d), F32),
                pltpu.VMEM((d, d2), BF16),
                pltpu.VMEM((d2 // 2, d), BF16),
                pltpu.SemaphoreType.DMA((2,)),
                pltpu.SemaphoreType.DMA((2,)),
            ],
        ),
        out_shape=jax.ShapeDtypeStruct(xs.shape, xs.dtype),
        input_output_aliases={2: 0},
        compiler_params=_cparams(("arbitrary",)),
        name="moe_ffn",
    )(first_sub, n_sub, xs, w_in, b_in.reshape(n_e, 1, d2), w_out, b_out.reshape(n_e, 1, d))


def _chunk_rows(chunk, n_chunks=1):
    return pl.ds(pl.multiple_of(chunk * ROW_CHUNK, ROW_CHUNK), n_chunks * ROW_CHUNK)


def _run_pieces(n, fn, big):
    n_big = n // big

    def big_piece(i, carry):
        fn(i * big, big)
        return carry

    def small_piece(i, carry):
        fn(n_big * big + i, 1)
        return carry

    lax.fori_loop(0, n_big, big_piece, 0)
    lax.fori_loop(0, n - n_big * big, small_piece, 0)


def _for_each_chunk(where_ref, tot_ref, tile, fn):
    n = tot_ref[tile]
    n_groups = n // CHUNK_UNROLL

    def group(i, carry):
        for u in range(CHUNK_UNROLL):
            j = i * CHUNK_UNROLL + u
            fn(j, where_ref[tile, j])
        return carry

    def single(i, carry):
        j = n_groups * CHUNK_UNROLL + i
        fn(j, where_ref[tile, j])
        return carry

    lax.fori_loop(0, n_groups, group, 0)
    lax.fori_loop(0, n - n_groups * CHUNK_UNROLL, single, 0)


def _dispatch_kernel(where_ref, tot_ref, fill0_ref, filln_ref,
                     pos_ref, h_ref, xs_hbm, stage, zeros, sem, fill_sem, *, tm, k_stage):
    t = pl.program_id(0)
    n_t = pl.num_programs(0)
    slot = t % 2

    def copy(s_chunk, d_chunk, size, sl):
        return pltpu.make_async_copy(stage.at[sl, _chunk_rows(s_chunk, size)],
                                     xs_hbm.at[_chunk_rows(d_chunk, size)], sem.at[sl])

    def wait_tile(tile, sl):
        _run_pieces(tot_ref[tile], lambda off, size: copy(0, 0, size, sl).wait(), WAIT_CHUNKS)

    @pl.when(t >= 2)
    def _():
        wait_tile(t - 2, slot)

    pos_t = pos_ref[0]
    j_i = lax.broadcasted_iota(I32, (k_stage, tm), 0)
    hit = j_i == pos_t[0:1]
    for k in range(1, TOP_K):
        hit = hit | (j_i == pos_t[k:k + 1])
    stage[slot] = _dot(jnp.where(hit, 1.0, 0.0).astype(BF16), h_ref[...])
    _for_each_chunk(where_ref, tot_ref, t, lambda s, d: copy(s, d, 1, slot).start())

    @pl.when(t == n_t - 1)
    def _():
        zeros[...] = jnp.zeros_like(zeros)

        def fill(d_chunk, size):
            return pltpu.make_async_copy(zeros.at[_chunk_rows(0, size)], xs_hbm.at[_chunk_rows(d_chunk, size)],
                                         fill_sem.at[0])

        def per_region(r, carry):
            _run_pieces(filln_ref[r], lambda off, size: fill(fill0_ref[r] + off, size).start(), COPY_CHUNKS)
            return carry

        lax.fori_loop(0, N_EXPERTS + 1, per_region, 0)

        @pl.when(t >= 1)
        def _():
            wait_tile(t - 1, 1 - slot)

        wait_tile(t, slot)

        def per_region_wait(r, carry):
            _run_pieces(filln_ref[r], lambda off, size: fill(0, size).wait(), COPY_CHUNKS)
            return carry

        lax.fori_loop(0, N_EXPERTS + 1, per_region_wait, 0)


def _combine_kernel(where_ref, tot_ref, pg_ref, x_ref, gp_ref, gs_ref, y_hbm,
                    op_ref, os_ref, stage, sem, *, n_pt, tm, k_stage):
    t = pl.program_id(0)
    n_t = pl.num_programs(0)
    slot = t % 2

    def copy(s_chunk, d_chunk, size, sl):
        return pltpu.make_async_copy(y_hbm.at[_chunk_rows(d_chunk, size)],
                                     stage.at[sl, _chunk_rows(s_chunk, size)], sem.at[sl])

    def fetch(tile, sl):
        _for_each_chunk(where_ref, tot_ref, tile, lambda s, d: copy(s, d, 1, sl).start())

    @pl.when(t == 0)
    def _():
        stage[...] = jnp.zeros_like(stage)
        fetch(0, 0)

    @pl.when(t + 1 < n_t)
    def _():
        fetch(t + 1, 1 - slot)

    _run_pieces(tot_ref[t], lambda off, size: copy(0, 0, size, slot).wait(), WAIT_CHUNKS)
    lane = lax.broadcasted_iota(I32, (tm, k_stage), 1)
    pg = pg_ref[...]
    p = jnp.zeros((tm, k_stage), F32)
    for k in range(TOP_K):
        p = jnp.where(lane == pg[:, k:k + 1].astype(I32), pg[:, TOP_K + k:TOP_K + k + 1], p)
    ffn = _dot(p.astype(BF16), stage[slot].astype(BF16))
    out = x_ref[...] + _pick_mod(t, n_pt, gp_ref, gs_ref) * ffn

    @pl.when(t < n_pt)
    def _():
        op_ref[...] = out

    @pl.when(t >= n_pt)
    def _():
        os_ref[...] = out


def _moe(geo, h2, pos, pg, nch_pad, soff_pad, x1, mod_p, mod_s, col_gate, layer, w_in, b_in, w_out, b_out):
    bm = MOE_BLOCK
    tm = geo.tm_moe
    n_tiles = geo.nt // tm
    depth, n_e, d, d2 = w_in.shape
    w_in = w_in.reshape(depth * n_e, d, d2)
    b_in = b_in.reshape(depth * n_e, d2)
    w_out = w_out.reshape(depth * n_e, d2 // 2, d)
    b_out = b_out.reshape(depth * n_e, d)
    chunks_per_block = bm // ROW_CHUNK
    max_rows = geo.nt * TOP_K + n_tiles * N_EXPERTS * (ROW_CHUNK - 1) + N_EXPERTS * (bm - 1)
    n_blocks = -(-max_rows // bm)
    nch = nch_pad[:, :N_EXPERTS, 0]
    soff = soff_pad[:, :N_EXPERTS, 0]
    tot = jnp.sum(nch, axis=1)
    per_e = jnp.sum(nch, axis=0)
    padded = (per_e + chunks_per_block - 1) // chunks_per_block * chunks_per_block
    pend = jnp.cumsum(padded)
    pstart = pend - padded
    dst = pstart[None, :] + jnp.cumsum(nch, axis=0) - nch
    first_sub = (pstart // chunks_per_block).astype(I32)
    n_sub = (padded // chunks_per_block).astype(I32)
    fill0 = jnp.concatenate([pstart + per_e, pend[-1:]]).astype(I32)
    filln = jnp.concatenate([padded - per_e, n_blocks * chunks_per_block - pend[-1:]]).astype(I32)

    k_stage = -(-(tm * TOP_K + N_EXPERTS * (ROW_CHUNK - 1)) // LANES) * LANES
    j = jnp.arange(k_stage // ROW_CHUNK, dtype=I32)
    owner = jnp.minimum(jnp.sum(j[None, :, None] >= (soff + nch)[:, None, :], axis=-1), N_EXPERTS - 1)
    shift = jnp.sum(jnp.where(owner[:, :, None] == jnp.arange(N_EXPERTS, dtype=I32), (dst - soff)[:, None, :], 0),
                    axis=-1)
    where = (j[None, :] + shift).astype(I32)
    row = lambda i, *_: (i, 0)
    xs = pl.pallas_call(
        functools.partial(_dispatch_kernel, tm=tm, k_stage=k_stage),
        grid_spec=pltpu.PrefetchScalarGridSpec(
            num_scalar_prefetch=4,
            grid=(n_tiles,),
            in_specs=[pl.BlockSpec((1, SUBLANES, tm), lambda i, *_: (i, 0, 0)), pl.BlockSpec((tm, d), row)],
            out_specs=pl.BlockSpec(memory_space=pl.ANY),
            scratch_shapes=[
                pltpu.VMEM((2, k_stage, d), F32),
                pltpu.VMEM((COPY_CHUNKS * ROW_CHUNK, d), F32),
                pltpu.SemaphoreType.DMA((2,)),
                pltpu.SemaphoreType.DMA((1,)),
            ],
        ),
        out_shape=jax.ShapeDtypeStruct((n_blocks * bm, d), F32),
        compiler_params=_cparams(("arbitrary",)),
        name="moe_dispatch",
    )(where, tot, fill0, filln, pos, h2)

    ys = _ffn(xs, first_sub, n_sub, layer, w_in, b_in, w_out, b_out)

    return pl.pallas_call(
        functools.partial(_combine_kernel, n_pt=geo.n_prompt // tm, tm=tm, k_stage=k_stage),
        grid_spec=pltpu.PrefetchScalarGridSpec(
            num_scalar_prefetch=2,
            grid=(n_tiles,),
            in_specs=[
                pl.BlockSpec((tm, LANES), row),
                pl.BlockSpec((tm, d), row),
                *_mod_specs(geo, col_gate, tm),
                pl.BlockSpec(memory_space=pl.ANY),
            ],
            out_specs=_pair_specs(geo, tm, d),
            scratch_shapes=[
                pltpu.VMEM((2, k_stage, d), F32),
                pltpu.SemaphoreType.DMA((2,)),
            ],
        ),
        out_shape=[jax.ShapeDtypeStruct((geo.n_prompt, d), F32), jax.ShapeDtypeStruct((geo.n_sample, d), F32)],
        compiler_params=_cparams(("arbitrary",)),
        name="moe_combine",
    )(where, tot, pg, x1, mod_p, mod_s, ys)


SW_QW = SW_Q_HEADS * SW_HEAD_DIM
SW_QKW = SW_QW + SW_KV_W
SW_KEYS = 2 * ATT_BLOCK


def _attn_proj_kernel(xp_ref, xs_ref, g_ref, shp_ref, shs_ref, scp_ref, scs_ref, w_ref, e_ref, et_ref, qg_ref,
                      kg_ref, q_ref, k_ref, v_ref, *, n_pt):
    i = pl.program_id(0)
    sh = _pick_mod(i, n_pt, shp_ref, shs_ref)
    sc = _pick_mod(i, n_pt, scp_ref, scs_ref)
    x = jnp.where(i >= n_pt, xs_ref[...], xp_ref[...])
    h = _norm_mod(x, g_ref[...], sc, sh).astype(BF16)
    tm = h.shape[0]
    n_parts = 2 if tm % (2 * SUBLANES) == 0 else 1
    part = tm // n_parts
    for r in range(n_parts):
        rows = slice(r * part, (r + 1) * part)
        z = _dot(h[rows], w_ref[...])
        qk = z[:, :SW_QKW]
        sq = qk * qk
        sq_hi = sq.astype(BF16)
        sq_lo = (sq - sq_hi.astype(F32)).astype(BF16)
        ms = (_dot(sq_hi, e_ref[...]) + _dot(sq_lo, e_ref[...])) * (1.0 / SW_HEAD_DIM)
        inv = lax.rsqrt(ms + NORM_EPS)
        inv_hi = inv.astype(BF16)
        inv_lo = (inv - inv_hi.astype(F32)).astype(BF16)
        qk = qk * (_dot(inv_hi, et_ref[...]) + _dot(inv_lo, et_ref[...]))
        q_ref[rows, :] = (qk[:, :SW_QW] * qg_ref[...] * SW_SCALE).astype(q_ref.dtype)
        k_ref[rows, :] = qk[:, SW_QW:] * kg_ref[...]
        v_ref[rows, :] = z[:, SW_QKW:]


def _attn_proj(geo, x_pair, g, mod_p, mod_s, w_bf16, q_gain, k_gain):
    tm = geo.tm
    n_out = w_bf16.shape[1]
    heads = SW_QKW // SW_HEAD_DIM
    member = (np.arange(SW_QKW)[:, None] // SW_HEAD_DIM == np.arange(LANES)[None, :]).astype(np.float32)
    e = jnp.asarray(member, BF16)
    et = jnp.asarray(member.T, BF16)
    assert heads <= LANES
    qg = jnp.tile(q_gain, SW_Q_HEADS).reshape(1, SW_QW)
    kg = jnp.tile(k_gain, SW_KV_HEADS).reshape(1, SW_KV_W)
    row = lambda i: (i, 0)
    fixed = lambda i: (0, 0)
    return pl.pallas_call(
        functools.partial(_attn_proj_kernel, n_pt=geo.n_pt),
        grid=(geo.n_tiles,),
        in_specs=[
            *_pair_specs(geo, tm, D_MODEL),
            pl.BlockSpec((1, D_MODEL), fixed),
            *_mod_specs(geo, 0, tm),
            *_mod_specs(geo, 1, tm),
            pl.BlockSpec((D_MODEL, n_out), fixed),
            pl.BlockSpec((SW_QKW, LANES), fixed),
            pl.BlockSpec((LANES, SW_QKW), fixed),
            pl.BlockSpec((1, SW_QW), fixed),
            pl.BlockSpec((1, SW_KV_W), fixed),
        ],
        out_specs=[
            pl.BlockSpec((tm, SW_QW), row),
            pl.BlockSpec((tm, SW_KV_W), row),
            pl.BlockSpec((tm, SW_KV_W), row),
        ],
        out_shape=[
            jax.ShapeDtypeStruct((geo.nt, SW_QW), BF16),
            jax.ShapeDtypeStruct((geo.nt, SW_KV_W), F32),
            jax.ShapeDtypeStruct((geo.nt, SW_KV_W), F32),
        ],
        compiler_params=_cparams(("arbitrary",)),
        name="attn_proj_qknorm",
    )(*x_pair, g, mod_p, mod_s, mod_p, mod_s, w_bf16, e, et, qg, kg)


def _rel_bucket_np(dist):
    n = np.maximum(dist, 0)
    max_exact = REL_BUCKETS // 2
    ratio = np.log(np.maximum(n, 1).astype(np.float32) / np.float32(max_exact)) / np.float32(
        math.log(REL_MAX_DIST / max_exact))
    large = max_exact + (ratio * np.float32(REL_BUCKETS - max_exact)).astype(np.int32)
    large = np.minimum(large, REL_BUCKETS - 1)
    return np.where(n < max_exact, n, large).astype(np.int32)


def _bucket_table(qpos, kpos, k_ok):
    dist = qpos[:, None] - kpos[None, :]
    ok = (dist >= 0) & (dist <= WINDOW) & k_ok[None, :]
    return np.where(ok, _rel_bucket_np(dist), -1).astype(np.int32)


def _build_bias(bkt_ref, rb_ref, bias_scr, rows):
    bkt = bkt_ref[...]
    base = jnp.where(bkt < 0, -jnp.inf, 0.0)
    for h in range(SW_Q_HEADS):
        bias_scr[h * rows:(h + 1) * rows, :] = base

    def add_bucket(j, carry):
        hit = bkt == j
        for h in range(SW_Q_HEADS):
            sl = slice(h * rows, (h + 1) * rows)
            bias_scr[sl, :] = bias_scr[sl, :] + jnp.where(hit, rb_ref[j, h], 0.0)
        return carry

    lax.fori_loop(0, REL_BUCKETS, add_bucket, 0)


def _sink_softmax_pv(s, sink, vv_g):
    m = jnp.maximum(jnp.max(s, axis=-1, keepdims=True), sink)
    e = jnp.exp(s - m)
    p = e / (jnp.sum(e, axis=-1, keepdims=True) + jnp.exp(sink - m))
    return _dot(p.astype(BF16), vv_g)


def _build_bias_t(bkt_ref, rb_ref, bias_scr, blk):
    bkt = bkt_ref[...]
    base = jnp.where(bkt < 0, -jnp.inf, 0.0)
    slots = [(g, j) for g in range(SW_KV_HEADS) for j in range(SW_GROUP)]
    for g, j in slots:
        bias_scr[g, :, j * blk:(j + 1) * blk] = base

    def add_bucket(b, carry):
        hit = bkt == b
        for g, j in slots:
            cols = slice(j * blk, (j + 1) * blk)
            bias_scr[g, :, cols] = bias_scr[g, :, cols] + jnp.where(hit, rb_ref[b, g * SW_GROUP + j], 0.0)
        return carry

    lax.fori_loop(0, REL_BUCKETS, add_bucket, 0)


def _swa_prompt_kernel(rb_ref, sink_ref, bkt_ref, q_ref, kp_ref, kc_ref, vp_ref, vc_ref, o_ref, bias_scr, *, qb):
    n = pl.program_id(1)
    blk = ATT_BLOCK
    hd = SW_HEAD_DIM

    @pl.when((pl.program_id(0) == 0) & (n == 0))
    def _():
        _build_bias_t(bkt_ref, rb_ref, bias_scr, blk)

    key_i = lax.broadcasted_iota(I32, (SW_KEYS, SW_GROUP * blk), 0)
    hide_prev = (n == 0) & (key_i < blk)
    zeros_q = jnp.zeros((hd, SW_GROUP * blk), BF16)
    sinks = [jnp.concatenate([jnp.full((1, blk), sink_ref[g * SW_GROUP + j], F32) for j in range(SW_GROUP)], axis=1)
             for g in range(SW_KV_HEADS)]
    for i in range(qb):
        rows = slice(i * blk, (i + 1) * blk)
        if i == 0:
            k2 = jnp.concatenate([kp_ref[...], kc_ref[rows, :]], axis=0)
            v2 = jnp.concatenate([vp_ref[...], vc_ref[rows, :]], axis=0)
        else:
            k2 = kc_ref[(i - 1) * blk:(i + 1) * blk, :]
            v2 = vc_ref[(i - 1) * blk:(i + 1) * blk, :]
        kk = k2.astype(BF16)
        vv_t = v2.T.astype(BF16)
        q_t = q_ref[rows, :].astype(F32).T.astype(BF16)
        outs = []
        for g in range(SW_KV_HEADS):
            heads = [g * SW_GROUP + j for j in range(SW_GROUP)]
            q_cat = jnp.concatenate([q_t[hq * hd:(hq + 1) * hd, :] for hq in heads], axis=1)
            q_full = jnp.concatenate([q_cat if gg == g else zeros_q for gg in range(SW_KV_HEADS)], axis=0)
            s = _dot(kk, q_full) + bias_scr[g]
            if i == 0:
                s = jnp.where(hide_prev, -jnp.inf, s)
            m = jnp.maximum(jnp.max(s, axis=0, keepdims=True), sinks[g])
            e = jnp.exp(s - m)
            den = jnp.sum(e, axis=0, keepdims=True) + jnp.exp(sinks[g] - m)
            o_t = _dot(vv_t[g * hd:(g + 1) * hd, :], e.astype(BF16)) * (1.0 / den)
            outs.extend(o_t[:, j * blk:(j + 1) * blk] for j in range(SW_GROUP))
        o_ref[rows, :] = jnp.concatenate(outs, axis=0).T.astype(o_ref.dtype)


def _swa_prompt(geo, q, k, v, rel_bias, sinks):
    blk = ATT_BLOCK
    assert geo.seq % blk == 0 and WINDOW == blk
    n_blk = geo.seq // blk
    qb = next(c for c in (8, 4, 2, 1) if n_blk % c == 0)
    nb = n_blk // qb
    bkt = jnp.asarray(_bucket_table(blk + np.arange(blk), np.arange(2 * blk), np.ones(2 * blk, bool)).T.copy())
    cur = lambda b, n: (b * nb + n, 0)
    prev = lambda b, n: (b * n_blk + jnp.maximum(n * qb - 1, 0), 0)
    smem = pl.BlockSpec(memory_space=pltpu.SMEM)
    return pl.pallas_call(
        functools.partial(_swa_prompt_kernel, qb=qb),
        grid=(geo.batch, nb),
        in_specs=[
            smem, smem,
            pl.BlockSpec((SW_KEYS, blk), lambda b, n: (0, 0)),
            pl.BlockSpec((qb * blk, SW_QW), cur),
            pl.BlockSpec((blk, SW_KV_W), prev),
            pl.BlockSpec((qb * blk, SW_KV_W), cur),
            pl.BlockSpec((blk, SW_KV_W), prev),
            pl.BlockSpec((qb * blk, SW_KV_W), cur),
        ],
        out_specs=pl.BlockSpec((qb * blk, SW_QW), cur),
        out_shape=jax.ShapeDtypeStruct((geo.n_prompt, SW_QW), BF16),
        scratch_shapes=[pltpu.VMEM((SW_KV_HEADS, SW_KEYS, SW_GROUP * blk), F32)],
        compiler_params=_cparams(("arbitrary", "arbitrary")),
        name="swa_prompt",
    )(rel_bias, sinks, bkt, q, k, k, v, v)


def _swa_step_kernel(rb_ref, sink_ref, bkt_ref, q_ref, kn_ref, vn_ref, ck_ref, cv_ref,
                     o_ref, nk_ref, nv_ref, bias_scr, *, steps, bb):
    pad = SUBLANES
    win = WINDOW

    @pl.when(pl.program_id(0) == 0)
    def _():
        _build_bias(bkt_ref, rb_ref, bias_scr, pad)

    fill = jnp.zeros((SW_KEYS - win - pad, SW_KV_W), F32)
    row8 = lax.broadcasted_iota(I32, (pad, SW_KV_W), 0)

    def shifted(cache, new):
        rolled = pltpu.roll(cache, win - steps, 0)
        tail = jnp.where(row8 < pad - steps, rolled[win - pad:], pltpu.roll(new, pad - steps, 0))
        return rolled[:win - pad], tail

    def body(s, carry):
        ck, cv, kn, vn = ck_ref[s], cv_ref[s], kn_ref[s], vn_ref[s]
        kk = jnp.concatenate([ck, kn, fill], axis=0).astype(BF16)
        vv = jnp.concatenate([cv, vn, fill], axis=0).astype(BF16)
        q = q_ref[s]
        for g in range(SW_KV_HEADS):
            kcols = slice(g * SW_HEAD_DIM, (g + 1) * SW_HEAD_DIM)
            heads = [g * SW_GROUP + j for j in range(SW_GROUP)]
            qs = jnp.concatenate([q[:, hq * SW_HEAD_DIM:(hq + 1) * SW_HEAD_DIM] for hq in heads], axis=0)
            sc = _dot_nt(qs.astype(BF16), kk[:, kcols]) + bias_scr[heads[0] * pad:(heads[-1] + 1) * pad, :]
            sink = jnp.concatenate([jnp.full((pad, 1), sink_ref[hq], F32) for hq in heads], axis=0)
            og = _sink_softmax_pv(sc, sink, vv[:, kcols])
            for j, hq in enumerate(heads):
                o_ref[s, :, hq * SW_HEAD_DIM:(hq + 1) * SW_HEAD_DIM] = og[j * pad:(j + 1) * pad]
        head, tail = shifted(ck, kn)
        nk_ref[s, 0:win - pad] = head
        nk_ref[s, win - pad:win] = tail
        head, tail = shifted(cv, vn)
        nv_ref[s, 0:win - pad] = head
        nv_ref[s, win - pad:win] = tail
        return carry

    lax.fori_loop(0, bb, body, 0, unroll=True)


def _swa_step(geo, q, k, v, cache_k, cache_v, rel_bias, sinks):
    steps, db = geo.dec_seq, geo.dec_batch
    win = cache_k.shape[1]
    assert win == WINDOW and steps <= SUBLANES
    bb = min(STEP_SEQS, db)
    pad = SUBLANES
    q_s = _seq_major(geo, q[geo.n_prompt:].astype(F32))
    k_s = _seq_major(geo, k[geo.n_prompt:])
    v_s = _seq_major(geo, v[geo.n_prompt:])
    kpos = np.arange(SW_KEYS)
    k_ok = kpos < win + steps
    qpos = win + np.arange(pad)
    bkt = _bucket_table(qpos, kpos, k_ok)
    bkt[steps:] = -1
    smem = pl.BlockSpec(memory_space=pltpu.SMEM)
    blk3 = lambda r, w: pl.BlockSpec((bb, r, w), lambda i: (i, 0, 0))
    o_s, nk, nv = pl.pallas_call(
        functools.partial(_swa_step_kernel, steps=steps, bb=bb),
        grid=(db // bb,),
        in_specs=[
            smem, smem,
            pl.BlockSpec((pad, SW_KEYS), lambda i: (0, 0)),
            blk3(pad, SW_QW), blk3(pad, SW_KV_W), blk3(pad, SW_KV_W),
            blk3(win, SW_KV_W), blk3(win, SW_KV_W),
        ],
        out_specs=[blk3(pad, SW_QW), blk3(win, SW_KV_W), blk3(win, SW_KV_W)],
        out_shape=[
            jax.ShapeDtypeStruct((db, pad, SW_QW), F32),
            jax.ShapeDtypeStruct((db, win, SW_KV_W), F32),
            jax.ShapeDtypeStruct((db, win, SW_KV_W), F32),
        ],
        scratch_shapes=[pltpu.VMEM((SW_Q_HEADS * pad, SW_KEYS), F32)],
        compiler_params=_cparams(("arbitrary",)),
        name="swa_step",
    )(rel_bias, sinks, jnp.asarray(bkt), q_s, k_s, v_s, cache_k, cache_v)
    return _time_major(geo, o_s), nk, nv


def kernel(x_prompt, x_sample, state_hgrn, cache_win_k, cache_win_v, c_prompt, c_sample, norm1_g, norm2_g, ada_w, ada_b, hg_w_in, hg_lb_table, hg_onorm_g, hg_w_out, sw_w_in, sw_qnorm_g, sw_knorm_g, sw_sinks, sw_w_out, rel_bias, router_w, router_b, moe_w_in, moe_b_in, moe_w_out, moe_b_out):
    batch, seq, d = x_prompt.shape
    db, steps, _ = x_sample.shape
    assert d == D_MODEL and ada_w.shape[0] == 2 and hg_w_in.shape[0] == 1 and sw_w_in.shape[0] == 1
    geo = _Geom(batch, seq, db, steps)
    x = (x_prompt.reshape(batch * seq, d), x_sample.transpose(1, 0, 2).reshape(steps * db, d))
    n_seq = batch + db
    rows = -(-n_seq // SUBLANES) * SUBLANES
    c_all = jnp.concatenate([c_prompt, c_sample, jnp.zeros((rows - n_seq, d), F32)], axis=0)
    mods = _ada_mods(c_all, ada_w, ada_b)

    def layer_mods(layer):
        mod_p = mods[layer, :batch].reshape(batch, 1, 6 * d)
        mod_s = jnp.tile(mods[layer, batch:n_seq], (steps, 1))
        return mod_p, mod_s

    def moe(layer, a_p, a_s, w_out, x_in, mod_p, mod_s):
        x1, h2, pos, pg, nch, soff = _post(geo, a_p, a_s, w_out.astype(BF16), x_in, norm2_g[layer:layer + 1],
                                           mod_p, mod_s, 0, router_w[layer], router_b[layer])
        return _moe(geo, h2, pos, pg, nch, soff, x1, mod_p, mod_s, 5,
                    layer, moe_w_in, moe_b_in, moe_w_out, moe_b_out)

    mod_p, mod_s = layer_mods(0)
    og = hg_onorm_g[0:1]
    z = _proj(geo, x, norm1_g[0:1], mod_p, mod_s, hg_w_in[0].astype(BF16), 0, 1)
    o_p, st_p = _gla_prompt(geo, z, hg_lb_table, og)
    o_s, st_s = _gla_step(geo, z, state_hgrn[0], hg_lb_table, og)
    x = moe(0, o_p, o_s, hg_w_out[0], x, mod_p, mod_s)

    mod_p, mod_s = layer_mods(1)
    q, k, v = _attn_proj(geo, x, norm1_g[1:2], mod_p, mod_s, sw_w_in[0].astype(BF16), sw_qnorm_g[0], sw_knorm_g[0])
    win = cache_win_k.shape[2]
    a_p = _swa_prompt(geo, q, k, v, rel_bias, sw_sinks[0])
    a_s, nk, nv = _swa_step(geo, q, k, v, cache_win_k[0].reshape(db, win, SW_KV_W),
                            cache_win_v[0].reshape(db, win, SW_KV_W), rel_bias, sw_sinks[0])
    x = moe(1, a_p, a_s, sw_w_out[0], x, mod_p, mod_s)

    y_prompt = x[0].reshape(batch, seq, d)
    y_sample = x[1].reshape(steps, db, d).transpose(1, 0, 2)
    kv_shape = (1, batch, WINDOW, SW_KV_HEADS, SW_HEAD_DIM)
    last_window = lambda a: jnp.stack([a[(b + 1) * seq - WINDOW:(b + 1) * seq] for b in range(batch)])
    k_p = last_window(k).reshape(kv_shape)
    v_p = last_window(v).reshape(kv_shape)
    cache_shape = (1, db, win, SW_KV_HEADS, SW_HEAD_DIM)
    return (y_prompt, y_sample, jnp.swapaxes(st_p, -1, -2)[None], st_s[None], k_p, v_p,
            nk.reshape(cache_shape), nv.reshape(cache_shape))
```
